```python
import numpy as np
import jax, jax.numpy as jnp
from jax import lax

D_MODEL = 1024
BATCH = 2
SEQ = 8192
DEPTH = 2

HEAD_DIM = 64
GM_GROUPS = 4
GM_WIDTH = GM_GROUPS * HEAD_DIM
GM_CHUNK = 128
NSA_HEADS = 8
NSA_KV_GROUPS = 2
NSA_HPG = NSA_HEADS // NSA_KV_GROUPS
NSA_WIDTH = NSA_HEADS * HEAD_DIM
NSA_KV_WIDTH = NSA_KV_GROUPS * HEAD_DIM
CMP_BLOCK = 32
CMP_STRIDE = 16
CMP_HIDDEN = 128
SEL_BLOCK = 64
SEL_TOPK = 16
N_LOCAL_SEL = 2
WINDOW = 512
Q_BLOCK = 128
MEM_HEADS = 4
MEM_WIDTH = MEM_HEADS * HEAD_DIM
MEM_LEN = 256
MIX_WIDTH = GM_WIDTH + NSA_WIDTH + MEM_WIDTH
ROPE_THETA = 10000.0
LN_EPS = 1e-5
ALPHA = (2.0 * DEPTH) ** 0.25
BETA = (8.0 * DEPTH) ** -0.25
NEG_INF = -1e30
FORCE_SCORE = 1e4

IN_SPLITS = (GM_WIDTH, GM_WIDTH, GM_WIDTH,
             NSA_WIDTH,
             NSA_KV_WIDTH, NSA_KV_WIDTH,
             NSA_KV_WIDTH, NSA_KV_WIDTH,
             NSA_KV_WIDTH, NSA_KV_WIDTH,
             NSA_HEADS * 3,
             NSA_WIDTH,
             MEM_WIDTH, MEM_WIDTH)
IN_COLS = int(sum(IN_SPLITS))
SPLIT_POINTS = tuple(int(c) for c in np.cumsum(IN_SPLITS)[:-1])

kernel_name = 'hymba_gmlp_nsa_memory_deepnorm'


def layer_norm(x, g, b):
    xf = x.astype(jnp.float32)
    mu = jnp.mean(xf, axis=-1, keepdims=True)
    var = jnp.mean(jnp.square(xf - mu), axis=-1, keepdims=True)
    y = (xf - mu) * lax.rsqrt(var + LN_EPS)
    return (y * g.astype(jnp.float32) + b.astype(jnp.float32)).astype(x.dtype)


def rope(x, pos):
    half = x.shape[-1] // 2
    inv_freq = ROPE_THETA ** (-jnp.arange(half, dtype=jnp.float32) * 2.0 / x.shape[-1])
    ang = pos.astype(jnp.float32)[:, None] * inv_freq[None, :]
    cos = jnp.cos(ang)[:, None, :].astype(x.dtype)
    sin = jnp.sin(ang)[:, None, :].astype(x.dtype)
    x1, x2 = x[..., :half], x[..., half:]
    return jnp.concatenate([x1 * cos - x2 * sin, x2 * cos + x1 * sin], axis=-1)


def masked_softmax(s, mask):
    p = jax.nn.softmax(jnp.where(mask, s, NEG_INF), axis=-1)
    return p * mask.astype(jnp.float32)


def gmlp_mixer(u, v, z, ln_g, ln_b, ws, bs):
    B, S, _ = u.shape
    u = jax.nn.gelu(u, approximate=False)
    v = jax.nn.gelu(v, approximate=False).reshape(B, S, GM_GROUPS, HEAD_DIM)
    v = layer_norm(v, ln_g, ln_b)
    n_chunk = S // GM_CHUNK
    v = v.reshape(B, n_chunk, GM_CHUNK, GM_GROUPS, HEAD_DIM)
    causal = jnp.tril(jnp.ones((GM_CHUNK, GM_CHUNK), ws.dtype))
    s = jnp.einsum('gij,bnjgc->bnigc', ws * causal[None], v) + bs.T[:, :, None]
    s = s.reshape(B, S, GM_WIDTH)
    return u * s * jax.nn.silu(z)


def nsa_mixer(q, kc, vc, ks, vs, kw, vw, gate_logits, z,
              cmp_pos_k, cmp_k_w1, cmp_k_w2, cmp_pos_v, cmp_v_w1, cmp_v_w2):
    B, S, _ = q.shape
    G, Hg, dh = NSA_KV_GROUPS, NSA_HPG, HEAD_DIM
    dtype = q.dtype
    pos = jnp.arange(S)
    q = rope(q.reshape(B, S, NSA_HEADS, dh), pos)
    kc = rope(kc.reshape(B, S, G, dh), pos)
    ks = rope(ks.reshape(B, S, G, dh), pos)
    kw = rope(kw.reshape(B, S, G, dh), pos)
    vc = vc.reshape(B, S, G, dh)
    vs = vs.reshape(B, S, G, dh)
    vw = vw.reshape(B, S, G, dh)

    n_cmp = (S - CMP_BLOCK) // CMP_STRIDE + 1
    cidx = np.arange(n_cmp)[:, None] * CMP_STRIDE + np.arange(CMP_BLOCK)[None, :]

    def compress(t, pos_emb, w1, w2):
        blk = t[:, cidx] + pos_emb[None, None, :, None, :]
        blk = blk.transpose(0, 3, 1, 2, 4).reshape(B, G, n_cmp, CMP_BLOCK * dh)
        return jax.nn.gelu(blk @ w1) @ w2

    k_cmp = compress(kc, cmp_pos_k, cmp_k_w1, cmp_k_w2)
    v_cmp = compress(vc, cmp_pos_v, cmp_v_w1, cmp_v_w2)

    n_sel = S // SEL_BLOCK
    topk = min(SEL_TOPK, n_sel)
    k_sel = ks.transpose(0, 2, 1, 3).reshape(B, G, n_sel, SEL_BLOCK, dh)
    v_sel = vs.transpose(0, 2, 1, 3).reshape(B, G, n_sel, SEL_BLOCK, dh)
    c_start = np.arange(n_cmp) * CMP_STRIDE
    s_start = np.arange(n_sel) * SEL_BLOCK
    overlap = jnp.asarray(((c_start[:, None] < s_start[None, :] + SEL_BLOCK) &
                           (c_start[:, None] + CMP_BLOCK > s_start[None, :])).astype(np.float32))

    k_win = jnp.pad(kw.transpose(0, 2, 1, 3), ((0, 0), (0, 0), (WINDOW, 0), (0, 0)))
    v_win = jnp.pad(vw.transpose(0, 2, 1, 3), ((0, 0), (0, 0), (WINDOW, 0), (0, 0)))

    n_qb = S // Q_BLOCK
    q_blocks = q.reshape(B, n_qb, Q_BLOCK, G, Hg, dh).transpose(1, 0, 3, 4, 2, 5)
    scale = dh ** -0.5
    gather = jax.vmap(jax.vmap(lambda kb, ix: kb[ix]))

    def block_fn(args):
        qblk, bi = args
        t = bi * Q_BLOCK + jnp.arange(Q_BLOCK)
        s_c = jnp.einsum('bghqd,bgnd->bghqn', qblk, k_cmp).astype(jnp.float32) * scale
        cmask = (jnp.arange(n_cmp) * CMP_STRIDE + CMP_BLOCK - 1)[None, :] <= t[:, None]
        p_c = masked_softmax(s_c, cmask)
        o_c = jnp.einsum('bghqn,bgnd->bghqd', p_c.astype(dtype), v_cmp)
        imp = jnp.einsum('bghqn,nj->bgqj', p_c, overlap)
        blk = jnp.arange(n_sel)[None, :]
        t_blk = (t // SEL_BLOCK)[:, None]
        valid = blk <= t_blk
        forced = (blk == 0) | (valid & (blk > t_blk - N_LOCAL_SEL))
        imp = jnp.where(forced, FORCE_SCORE, jnp.where(valid, imp, -1.0))
        _, top_idx = lax.top_k(imp, topk)
        kg = gather(k_sel, top_idx)
        vg = gather(v_sel, top_idx)
        s_s = jnp.einsum('bghqd,bgqkld->bghqkl', qblk, kg).astype(jnp.float32) * scale
        key_pos = top_idx[..., None] * SEL_BLOCK + jnp.arange(SEL_BLOCK)
        smask = key_pos <= t[None, None, :, None, None]
        n_keys = topk * SEL_BLOCK
        p_s = masked_softmax(s_s.reshape(B, G, Hg, Q_BLOCK, n_keys),
                             smask.reshape(B, G, 1, Q_BLOCK, n_keys))
        o_s = jnp.einsum('bghqm,bgqmd->bghqd', p_s.astype(dtype),
                         vg.reshape(B, G, Q_BLOCK, n_keys, dh))
        start = bi * Q_BLOCK
        kwb = lax.dynamic_slice_in_dim(k_win, start, WINDOW + Q_BLOCK, axis=2)
        vwb = lax.dynamic_slice_in_dim(v_win, start, WINDOW + Q_BLOCK, axis=2)
        s_w = jnp.einsum('bghqd,bgmd->bghqm', qblk, kwb).astype(jnp.float32) * scale
        kpos = start - WINDOW + jnp.arange(WINDOW + Q_BLOCK)
        wmask = ((kpos[None, :] <= t[:, None]) & (kpos[None, :] > t[:, None] - WINDOW)
                 & (kpos[None, :] >= 0))
        p_w = masked_softmax(s_w, wmask)
        o_w = jnp.einsum('bghqm,bgmd->bghqd', p_w.astype(dtype), vwb)
        return jnp.stack([o_c, o_s, o_w], axis=-1)

    outs = lax.map(block_fn, (q_blocks, jnp.arange(n_qb)))
    outs = outs.transpose(1, 0, 4, 2, 3, 5, 6).reshape(B, S, NSA_HEADS, dh, 3)
    gates = jax.nn.sigmoid(gate_logits.astype(jnp.float32)).reshape(B, S, NSA_HEADS, 3)
    o = jnp.einsum('bshdc,bshc->bshd', outs, gates.astype(dtype)).reshape(B, S, NSA_WIDTH)
    return o * jax.nn.silu(z)


def memory_mixer(q, z, mem, w_mem_kv):
    B, S, _ = q.shape
    kv = (mem @ w_mem_kv).reshape(mem.shape[0], mem.shape[1], 2, MEM_HEADS, HEAD_DIM)
    k, v = kv[:, :, 0], kv[:, :, 1]
    q = q.reshape(B, S, MEM_HEADS, HEAD_DIM)
    s = jnp.einsum('bshd,bmhd->bhsm', q, k).astype(jnp.float32) * HEAD_DIM ** -0.5
    p = jax.nn.softmax(s, axis=-1)
    o = jnp.einsum('bhsm,bmhd->bshd', p.astype(q.dtype), v).reshape(B, S, MEM_WIDTH)
    return o * jax.nn.silu(z)


def hybrid_layer(x, mem, w_in, gm_ln_g, gm_ln_b, gm_ws, gm_bs,
                 cmp_pos_k, cmp_k_w1, cmp_k_w2, cmp_pos_v, cmp_v_w1, cmp_v_w2,
                 w_mem_kv, w_out, ln_g, ln_b):
    h = x @ w_in
    (gm_u, gm_v, gm_z, nq, nkc, nvc, nks, nvs, nkw, nvw, ngate, nz,
     mq, mz) = jnp.split(h, SPLIT_POINTS, axis=-1)
    y_gm = gmlp_mixer(gm_u, gm_v, gm_z, gm_ln_g, gm_ln_b, gm_ws, gm_bs)
    y_nsa = nsa_mixer(nq, nkc, nvc, nks, nvs, nkw, nvw, ngate, nz,
                      cmp_pos_k, cmp_k_w1, cmp_k_w2, cmp_pos_v, cmp_v_w1, cmp_v_w2)
    y_mem = memory_mixer(mq, mz, mem, w_mem_kv)
    y = jnp.concatenate([y_gm, y_nsa, y_mem], axis=-1) @ w_out
    return layer_norm(ALPHA * x + y, ln_g, ln_b)


def setup_inputs(seed: int = 0) -> dict:
    key = jax.random.key(seed)
    ks = jax.random.split(key, 20)
    f32 = jnp.float32
    nrm = lambda k, shape, s: jax.random.normal(k, shape, f32) * s
    L = DEPTH
    return {
        'x': nrm(ks[0], (BATCH, SEQ, D_MODEL), 1.0),
        'mem': nrm(ks[1], (BATCH, MEM_LEN, D_MODEL), 1.0),
        'w_in': nrm(ks[2], (L, D_MODEL, IN_COLS), D_MODEL ** -0.5),
        'gm_ln_g': 1.0 + nrm(ks[3], (L, GM_GROUPS, HEAD_DIM), 0.01),
        'gm_ln_b': nrm(ks[4], (L, GM_GROUPS, HEAD_DIM), 0.01),
        'gm_ws': nrm(ks[5], (L, GM_GROUPS, GM_CHUNK, GM_CHUNK), GM_CHUNK ** -0.5),
        'gm_bs': 1.0 + nrm(ks[6], (L, GM_GROUPS, GM_CHUNK), 0.1),
        'cmp_pos_k': nrm(ks[7], (L, CMP_BLOCK, HEAD_DIM), 0.1),
        'cmp_k_w1': nrm(ks[8], (L, CMP_BLOCK * HEAD_DIM, CMP_HIDDEN), (CMP_BLOCK * HEAD_DIM) ** -0.5),
        'cmp_k_w2': nrm(ks[9], (L, CMP_HIDDEN, HEAD_DIM), CMP_HIDDEN ** -0.5),
        'cmp_pos_v': nrm(ks[10], (L, CMP_BLOCK, HEAD_DIM), 0.1),
        'cmp_v_w1': nrm(ks[11], (L, CMP_BLOCK * HEAD_DIM, CMP_HIDDEN), (CMP_BLOCK * HEAD_DIM) ** -0.5),
        'cmp_v_w2': nrm(ks[12], (L, CMP_HIDDEN, HEAD_DIM), CMP_HIDDEN ** -0.5),
        'w_mem_kv': nrm(ks[13], (L, D_MODEL, 2 * MEM_WIDTH), D_MODEL ** -0.5),
        'w_out': nrm(ks[14], (L, MIX_WIDTH, D_MODEL), BETA * MIX_WIDTH ** -0.5),
        'ln_g': 1.0 + nrm(ks[15], (L, D_MODEL), 0.01),
        'ln_b': nrm(ks[16], (L, D_MODEL), 0.01),
    }


def reference(x, mem, w_in, gm_ln_g, gm_ln_b, gm_ws, gm_bs,
              cmp_pos_k, cmp_k_w1, cmp_k_w2, cmp_pos_v, cmp_v_w1, cmp_v_w2,
              w_mem_kv, w_out, ln_g, ln_b):
    for l in range(DEPTH):
        x = hybrid_layer(x, mem, w_in[l], gm_ln_g[l], gm_ln_b[l], gm_ws[l], gm_bs[l],
                         cmp_pos_k[l], cmp_k_w1[l], cmp_k_w2[l],
                         cmp_pos_v[l], cmp_v_w1[l], cmp_v_w2[l],
                         w_mem_kv[l], w_out[l], ln_g[l], ln_b[l])
    return x
```

```python
import functools

import numpy as np
import jax
import jax.numpy as jnp
from jax import lax
from jax.experimental import pallas as pl
from jax.experimental.pallas import tpu as pltpu

HEAD_DIM = 64
GM_GROUPS = 4
GM_WIDTH = GM_GROUPS * HEAD_DIM
GM_CHUNK = 128
NSA_HEADS = 8
NSA_KV_GROUPS = 2
NSA_HPG = NSA_HEADS // NSA_KV_GROUPS
NSA_WIDTH = NSA_HEADS * HEAD_DIM
NSA_KV_WIDTH = NSA_KV_GROUPS * HEAD_DIM
CMP_BLOCK = 32
CMP_STRIDE = 16
CMP_HIDDEN = 128
SEL_BLOCK = 64
SEL_TOPK = 16
N_LOCAL_SEL = 2
WINDOW = 512
Q_BLOCK = 128
MEM_HEADS = 4
MEM_WIDTH = MEM_HEADS * HEAD_DIM
ROPE_THETA = 10000.0
LN_EPS = 1e-5
NEG_INF = -1e30
FORCE_SCORE = 1e4
GATE_COLS = NSA_HEADS * 3

LANES = 128
VMEM_LIMIT_BYTES = 56 * 1024 * 1024

PROJ_ROWS = 256
SEL_KEY_TILE = 512
REMOVED = -3.0e38

PAIR_HEAD_ORDER = tuple(h for i in range(NSA_HPG) for h in (i, i + NSA_HPG))

C_GU, C_GV, C_GZ = 0, 256, 512
C_Q = 768
C_KC, C_VC, C_KS, C_VS, C_KW, C_VW = 1280, 1408, 1536, 1664, 1792, 1920
C_NZ = 2048
C_MQ, C_MZ = 2560, 2816
C_GATE = 3072
N_COLS = 3200


def _dot(a, b):
    return jnp.dot(a, b, preferred_element_type=jnp.float32)


def _dot_nt(a, b):
    return lax.dot_general(a, b, (((1,), (1,)), ((), ())), preferred_element_type=jnp.float32)


def _gelu(x):
    return 0.5 * x * (1.0 + lax.erf(x * np.float32(np.sqrt(0.5))))


def _silu(x):
    return x * jax.nn.sigmoid(x)


def _lane_iota(shape):
    return lax.broadcasted_iota(jnp.int32, shape, len(shape) - 1)


def _low_half(shape):
    return (_lane_iota(shape) % LANES) < HEAD_DIM


def _tile_lanes(x, reps):
    return jnp.concatenate([x] * reps, axis=-1) if reps > 1 else x


def _memkv_kernel(mem_ref, w_ref, k_ref, v_ref):
    kv = _dot(mem_ref[...].astype(jnp.bfloat16), w_ref[0])
    k_ref[0] = kv[:, :MEM_WIDTH].astype(jnp.bfloat16)
    v_ref[0] = kv[:, MEM_WIDTH:].astype(jnp.bfloat16)


def _memkv(mem2d, w_mem_kv_bf16):
    depth = w_mem_kv_bf16.shape[0]
    rows, d_model = mem2d.shape
    out = jax.ShapeDtypeStruct((depth, rows, MEM_WIDTH), jnp.bfloat16)
    return pl.pallas_call(
        _memkv_kernel,
        grid=(depth,),
        in_specs=[pl.BlockSpec((rows, d_model), lambda l: (0, 0)),
                  pl.BlockSpec((1, d_model, 2 * MEM_WIDTH), lambda l: (l, 0, 0))],
        out_specs=[pl.BlockSpec((1, rows, MEM_WIDTH), lambda l: (l, 0, 0)),
                   pl.BlockSpec((1, rows, MEM_WIDTH), lambda l: (l, 0, 0))],
        out_shape=[out, out],
        name="mem_kv_proj",
    )(mem2d, w_mem_kv_bf16)


def _rope(x, cos, sin_signed, low):
    width = x.shape[-1]
    swapped = jnp.where(low, pltpu.roll(x, width - HEAD_DIM // 2, 1), pltpu.roll(x, HEAD_DIM // 2, 1))
    return x * cos + swapped * sin_signed


def _group_layer_norm(v, g, b, low):
    inv = np.float32(1.0 / HEAD_DIM)
    s_lo = jnp.sum(jnp.where(low, v, 0.0), axis=-1, keepdims=True)
    s_hi = jnp.sum(jnp.where(low, 0.0, v), axis=-1, keepdims=True)
    mu = jnp.where(low, s_lo, s_hi) * inv
    d = v - mu
    d2 = d * d
    q_lo = jnp.sum(jnp.where(low, d2, 0.0), axis=-1, keepdims=True)
    q_hi = jnp.sum(jnp.where(low, 0.0, d2), axis=-1, keepdims=True)
    var = jnp.where(low, q_lo, q_hi) * inv
    return d * lax.rsqrt(var + LN_EPS) * g + b


def _inproj_kernel(x_ref, w_ref, cos_ref, sin_ref, rot_low_ref, gws_ref, gbs_ref, glg_ref, glb_ref,
                   mk_ref, mv_ref,
                   ygm_ref, ymem_ref, q_ref, kc_ref, vc_ref, ksa_ref, vs_ref, kw_ref, vw_ref,
                   nz_ref, gate_ref, *, seq_len):
    rows = x_ref.shape[0]
    xb = x_ref[...].astype(jnp.bfloat16)
    low = _low_half((rows, LANES))
    rot_low = rot_low_ref[...] > 0.5
    rot_low = jnp.broadcast_to(rot_low, (rows, LANES))
    cos = cos_ref[...]
    sin = sin_ref[...]

    def proj(c0, width):
        return _dot(xb, w_ref[:, c0:c0 + width])

    u = _gelu(proj(C_GU, GM_WIDTH))
    v = _gelu(proj(C_GV, GM_WIDTH))
    z = proj(C_GZ, GM_WIDTH)
    for pair in range(GM_GROUPS // 2):
        sl = slice(pair * LANES, (pair + 1) * LANES)
        vln = _group_layer_norm(v[:, sl], glg_ref[:, sl], glb_ref[:, sl], low).astype(jnp.bfloat16)
        for c in range(rows // GM_CHUNK):
            rs = slice(c * GM_CHUNK, (c + 1) * GM_CHUNK)
            s_lo = _dot(gws_ref[2 * pair], vln[rs])
            s_hi = _dot(gws_ref[2 * pair + 1], vln[rs])
            s = jnp.where(_low_half((GM_CHUNK, LANES)), s_lo, s_hi) + gbs_ref[:, sl]
            ygm_ref[rs, sl] = (u[rs, sl] * s * _silu(z[rs, sl])).astype(ygm_ref.dtype)

    qscale = np.float32(HEAD_DIM ** -0.5)
    for i in range(NSA_WIDTH // LANES):
        qi = _rope(proj(C_Q + i * LANES, LANES), cos, sin, rot_low) * qscale
        q_ref[:, i * LANES:(i + 1) * LANES] = qi.astype(q_ref.dtype)
    kc_ref[...] = _rope(proj(C_KC, LANES), cos, sin, rot_low)
    vc_ref[...] = proj(C_VC, LANES)
    ks = _rope(proj(C_KS, LANES), cos, sin, rot_low)
    ksa_ref[:, :LANES] = ks.astype(ksa_ref.dtype)
    tok = (pl.program_id(0) * rows) % seq_len + lax.broadcasted_iota(jnp.int32, (rows, LANES), 0)
    onehot = (tok // SEL_BLOCK) == _lane_iota((rows, LANES))
    ksa_ref[:, LANES:] = jnp.where(onehot, 1.0, 0.0).astype(ksa_ref.dtype)
    vs_ref[...] = proj(C_VS, LANES).astype(vs_ref.dtype)
    kw_ref[...] = _rope(proj(C_KW, LANES), cos, sin, rot_low).astype(kw_ref.dtype)
    vw_ref[...] = proj(C_VW, LANES).astype(vw_ref.dtype)
    nz_ref[...] = _silu(proj(C_NZ, NSA_WIDTH))
    gate_ref[...] = jax.nn.sigmoid(proj(C_GATE, LANES))

    mq = proj(C_MQ, MEM_WIDTH) * qscale
    mz = proj(C_MZ, MEM_WIDTH)
    for pair in range(MEM_HEADS // 2):
        sl = slice(pair * LANES, (pair + 1) * LANES)
        kp = mk_ref[0, :, sl]
        vp = mv_ref[0, :, sl]
        outs = []
        for keep_low in (True, False):
            qh = jnp.where(low == keep_low, mq[:, sl], 0.0).astype(jnp.bfloat16)
            s = _dot_nt(qh, kp)
            e = jnp.exp(s - jnp.max(s, axis=-1, keepdims=True))
            p = e / jnp.sum(e, axis=-1, keepdims=True)
            outs.append(_dot(p.astype(jnp.bfloat16), vp))
        o = jnp.where(low, outs[0], outs[1])
        ymem_ref[:, sl] = (o * _silu(mz[:, sl])).astype(ymem_ref.dtype)


def _inproj(x2d, w_cat, cos_t, sin_t, rot_low, gws, gbs, glg, glb, mk, mv, *, batch, seq_len):
    n, d_model = x2d.shape
    rows = PROJ_ROWS
    steps_per_seq = seq_len // rows
    mem_len = mk.shape[0] // batch

    def tok_spec(width):
        return pl.BlockSpec((rows, width), lambda i: (i, 0))

    def const_spec(shape):
        return pl.BlockSpec(shape, lambda i: (0,) * len(shape))

    tab_spec = pl.BlockSpec((rows, LANES), lambda i: (i % steps_per_seq, 0))
    mem_spec = pl.BlockSpec((1, mem_len, MEM_WIDTH), lambda i: (i // steps_per_seq, 0, 0))
    bf16, f32 = jnp.bfloat16, jnp.float32
    outs = [(GM_WIDTH, bf16), (MEM_WIDTH, bf16), (NSA_WIDTH, bf16), (LANES, f32), (LANES, f32),
            (2 * LANES, bf16), (LANES, bf16), (LANES, bf16), (LANES, bf16), (NSA_WIDTH, f32),
            (LANES, f32)]
    return pl.pallas_call(
        functools.partial(_inproj_kernel, seq_len=seq_len),
        grid=(n // rows,),
        in_specs=[tok_spec(d_model), const_spec(w_cat.shape), tab_spec, tab_spec,
                  const_spec(rot_low.shape), const_spec(gws.shape), const_spec(gbs.shape),
                  const_spec(glg.shape), const_spec(glb.shape), mem_spec, mem_spec],
        out_specs=[tok_spec(w) for w, _ in outs],
        out_shape=[jax.ShapeDtypeStruct((n, w), dt) for w, dt in outs],
        compiler_params=pltpu.CompilerParams(dimension_semantics=("arbitrary",),
                                             vmem_limit_bytes=VMEM_LIMIT_BYTES),
        name="in_proj_mixers",
    )(x2d, w_cat, cos_t, sin_t, rot_low, gws, gbs, glg, glb,
      mk.reshape(batch, mem_len, MEM_WIDTH), mv.reshape(batch, mem_len, MEM_WIDTH))


def _compress_kernel(k_ref, v_ref, pos_ref, w1_ref, w2_ref, kcmp_ref, vcmp_ref):
    n_rows = k_ref.shape[1]
    for idx, (src, dst) in enumerate(((k_ref, kcmp_ref), (v_ref, vcmp_ref))):
        xr = src[0]
        top = _dot((xr + pos_ref[idx, 0:1]).astype(jnp.bfloat16), w1_ref[idx, 0])
        bot = _dot((xr + pos_ref[idx, 1:2]).astype(jnp.bfloat16), w1_ref[idx, 1])
        hidden = top + pltpu.roll(bot, n_rows - 1, 0)
        dst[0] = _dot(_gelu(hidden).astype(jnp.bfloat16), w2_ref[idx]).astype(dst.dtype)


def _compress(kc_rows, vc_rows, pos, w1, w2):
    batch, n_rows, width = kc_rows.shape
    row_spec = pl.BlockSpec((1, n_rows, width), lambda b: (b, 0, 0))
    out_spec = pl.BlockSpec((1, n_rows, LANES), lambda b: (b, 0, 0))
    out = jax.ShapeDtypeStruct((batch, n_rows, LANES), jnp.bfloat16)
    return pl.pallas_call(
        _compress_kernel,
        grid=(batch,),
        in_specs=[row_spec, row_spec,
                  pl.BlockSpec(pos.shape, lambda b: (0, 0, 0)),
                  pl.BlockSpec(w1.shape, lambda b: (0, 0, 0, 0)),
                  pl.BlockSpec(w2.shape, lambda b: (0, 0, 0))],
        out_specs=[out_spec, out_spec],
        out_shape=[out, out],
        compiler_params=pltpu.CompilerParams(dimension_semantics=("arbitrary",),
                                             vmem_limit_bytes=VMEM_LIMIT_BYTES),
        name="nsa_compress",
    )(kc_rows, vc_rows, pos, w1, w2)


def _split_bf16(x, parts):
    out = []
    for _ in range(parts):
        hi = x.astype(jnp.bfloat16)
        out.append(hi)
        x = x - hi.astype(jnp.float32)
    return out


def _topk_mask(score):
    lane = _lane_iota(score.shape).astype(jnp.float32)
    picked = jnp.zeros(score.shape, jnp.bool_)
    for _ in range(SEL_TOPK):
        best = jnp.max(score, axis=-1, keepdims=True)
        first = jnp.min(jnp.where(score == best, lane, np.float32(LANES)), axis=-1, keepdims=True)
        hit = lane == first
        picked = picked | hit
        score = jnp.where(hit, REMOVED, score)
    return picked


def _nsa_kernel(q_ref, nz_ref, gate_ref, ksa_ref, vs_ref, kw_ref, vw_ref, kcmp_ref, vcmp_ref,
                ovl_ref, out_ref, qa_ref, m_ref, l_ref, acc_ref, *, seq_len):
    bi = pl.program_id(1)
    start = bi * Q_BLOCK
    n_heads = NSA_HEADS
    rows = n_heads * Q_BLOCK
    n_sel = seq_len // SEL_BLOCK
    n_cmp_rows = kcmp_ref.shape[1]

    low = _low_half((Q_BLOCK, LANES))
    q_stack = []
    for i in range(NSA_HPG):
        qi = q_ref[0, :, i * LANES:(i + 1) * LANES]
        zero = jnp.zeros_like(qi)
        q_stack.append(jnp.where(low, qi, zero))
        q_stack.append(jnp.where(low, zero, qi))
    q_stack = jnp.concatenate(q_stack, axis=0)

    t_q = start + lax.broadcasted_iota(jnp.int32, (Q_BLOCK, 1), 0)
    t_rows = jnp.concatenate([t_q] * n_heads, axis=0)

    s_c = _dot_nt(q_stack, kcmp_ref[0])
    c_end = lax.broadcasted_iota(jnp.int32, (1, n_cmp_rows), 1) * CMP_STRIDE + (CMP_BLOCK - 1)
    cmask = c_end <= t_rows
    s_c = jnp.where(cmask, s_c, NEG_INF)
    e_c = jnp.where(cmask, jnp.exp(s_c - jnp.max(s_c, axis=-1, keepdims=True)), 0.0)
    l_c = jnp.sum(e_c, axis=-1, keepdims=True)
    p_c = e_c * jnp.where(l_c > 0.0, 1.0 / l_c, 0.0)
    o_c = _dot(p_c.astype(jnp.bfloat16), vcmp_ref[0])

    blk = _lane_iota((Q_BLOCK, LANES))
    t_blk = t_q // SEL_BLOCK
    valid = blk <= t_blk
    forced = (blk == 0) | (valid & (blk > t_blk - N_LOCAL_SEL))
    scores = []
    for g in range(NSA_KV_GROUPS):
        p_sum = sum(p_c[(2 * i + g) * Q_BLOCK:(2 * i + g + 1) * Q_BLOCK] for i in range(NSA_HPG))
        imp = sum(_dot(part, ovl_ref[...]) for part in _split_bf16(p_sum, 3))
        sc = jnp.where(forced, FORCE_SCORE, jnp.where(valid, imp, -1.0))
        if n_sel < LANES:
            sc = jnp.where(blk < n_sel, sc, REMOVED)
        scores.append(sc)
    picked = _topk_mask(jnp.concatenate(scores, axis=0))
    bias = jnp.where(picked, 0.0, NEG_INF).astype(jnp.bfloat16)
    bias_rows = jnp.concatenate([bias[(r % 2) * Q_BLOCK:(r % 2 + 1) * Q_BLOCK]
                                 for r in range(n_heads)], axis=0)
    qa_ref[:, :LANES] = q_stack
    qa_ref[:, LANES:] = bias_rows

    m_ref[...] = jnp.full(m_ref.shape, NEG_INF, jnp.float32)
    l_ref[...] = jnp.zeros(l_ref.shape, jnp.float32)
    acc_ref[...] = jnp.zeros(acc_ref.shape, jnp.float32)
    tk = SEL_KEY_TILE
    lane_tiles = tk // LANES

    def sel_step(kt, carry):
        k0 = pl.multiple_of(kt * tk, tk)
        s = _dot_nt(qa_ref[...], ksa_ref[0, pl.ds(k0, tk), :])
        kpos = k0 + lax.broadcasted_iota(jnp.int32, (1, tk), 1)
        s = jnp.where(kpos <= t_rows, s, NEG_INF)
        m_prev = m_ref[...]
        m_next = jnp.maximum(m_prev, jnp.max(s, axis=-1, keepdims=True))
        p = jnp.exp(s - _tile_lanes(m_next, lane_tiles))
        alpha = jnp.exp(m_prev - m_next)
        l_ref[...] = alpha * l_ref[...] + jnp.sum(p, axis=-1, keepdims=True)
        acc_ref[...] = alpha * acc_ref[...] + _dot(p.astype(jnp.bfloat16),
                                                   vs_ref[0, pl.ds(k0, tk), :])
        m_ref[...] = m_next
        return carry

    lax.fori_loop(0, (start + Q_BLOCK + tk - 1) // tk, sel_step, 0)
    o_s = acc_ref[...] / l_ref[...]

    span = WINDOW + Q_BLOCK
    w0 = pl.multiple_of(jnp.maximum(start - WINDOW, 0), Q_BLOCK)
    s_w = _dot_nt(q_stack, kw_ref[0, pl.ds(w0, span), :])
    kpos = w0 + lax.broadcasted_iota(jnp.int32, (1, span), 1)
    wmask = (kpos <= t_rows) & (kpos > t_rows - WINDOW)
    s_w = jnp.where(wmask, s_w, NEG_INF)
    e_w = jnp.where(wmask, jnp.exp(s_w - jnp.max(s_w, axis=-1, keepdims=True)), 0.0)
    p_w = e_w / jnp.sum(e_w, axis=-1, keepdims=True)
    o_w = _dot(p_w.astype(jnp.bfloat16), vw_ref[0, pl.ds(w0, span), :])

    gates = gate_ref[0]
    for i in range(NSA_HPG):
        halves = []
        for g in range(NSA_KV_GROUPS):
            head = i + NSA_HPG * g
            rs = slice((2 * i + g) * Q_BLOCK, (2 * i + g + 1) * Q_BLOCK)
            halves.append(o_c[rs] * gates[:, 3 * head:3 * head + 1]
                          + o_s[rs] * gates[:, 3 * head + 1:3 * head + 2]
                          + o_w[rs] * gates[:, 3 * head + 2:3 * head + 3])
        sl = slice(i * LANES, (i + 1) * LANES)
        out_ref[0, :, sl] = (jnp.where(low, halves[0], halves[1]) * nz_ref[0, :, sl]).astype(out_ref.dtype)


def _nsa(q, nz, gates, ksa, vs, kw, vw, kcmp, vcmp, ovl):
    batch, seq_len, _ = q.shape

    def q_spec(width):
        return pl.BlockSpec((1, Q_BLOCK, width), lambda b, i: (b, i, 0))

    def seq_spec(arr):
        return pl.BlockSpec((1,) + arr.shape[1:], lambda b, i: (b, 0, 0))

    rows = NSA_HEADS * Q_BLOCK
    return pl.pallas_call(
        functools.partial(_nsa_kernel, seq_len=seq_len),
        grid=(batch, seq_len // Q_BLOCK),
        in_specs=[q_spec(NSA_WIDTH), q_spec(NSA_WIDTH), q_spec(LANES),
                  seq_spec(ksa), seq_spec(vs), seq_spec(kw), seq_spec(vw),
                  seq_spec(kcmp), seq_spec(vcmp),
                  pl.BlockSpec(ovl.shape, lambda b, i: (0, 0))],
        out_specs=q_spec(NSA_WIDTH),
        out_shape=jax.ShapeDtypeStruct((batch, seq_len, NSA_WIDTH), jnp.bfloat16),
        scratch_shapes=[pltpu.VMEM((rows, 2 * LANES), jnp.bfloat16),
                        pltpu.VMEM((rows, LANES), jnp.float32),
                        pltpu.VMEM((rows, LANES), jnp.float32),
                        pltpu.VMEM((rows, LANES), jnp.float32)],
        compiler_params=pltpu.CompilerParams(dimension_semantics=("arbitrary", "arbitrary"),
                                             vmem_limit_bytes=VMEM_LIMIT_BYTES),
        name="nsa_attention",
    )(q, nz, gates, ksa, vs, kw, vw, kcmp, vcmp, ovl)


def _outproj_kernel(x_ref, ygm_ref, ynsa_ref, ymem_ref, w_ref, g_ref, b_ref, o_ref, *, alpha):
    y = (_dot(ygm_ref[...], w_ref[:GM_WIDTH])
         + _dot(ynsa_ref[...], w_ref[GM_WIDTH:GM_WIDTH + NSA_WIDTH])
         + _dot(ymem_ref[...], w_ref[GM_WIDTH + NSA_WIDTH:]))
    r = alpha * x_ref[...] + y
    mu = jnp.mean(r, axis=-1, keepdims=True)
    d = r - mu
    var = jnp.mean(d * d, axis=-1, keepdims=True)
    o_ref[...] = d * lax.rsqrt(var + LN_EPS) * g_ref[...] + b_ref[...]


def _outproj(x2d, ygm, ynsa, ymem, w_out, ln_g, ln_b, *, alpha):
    n, d_model = x2d.shape
    rows = PROJ_ROWS

    def tok_spec(width):
        return pl.BlockSpec((rows, width), lambda i: (i, 0))

    def const_spec(shape):
        return pl.BlockSpec(shape, lambda i: (0, 0))

    return pl.pallas_call(
        functools.partial(_outproj_kernel, alpha=alpha),
        grid=(n // rows,),
        in_specs=[tok_spec(d_model), tok_spec(GM_WIDTH), tok_spec(NSA_WIDTH), tok_spec(MEM_WIDTH),
                  const_spec(w_out.shape), const_spec(ln_g.shape), const_spec(ln_b.shape)],
        out_specs=tok_spec(d_model),
        out_shape=jax.ShapeDtypeStruct((n, d_model), jnp.float32),
        compiler_params=pltpu.CompilerParams(dimension_semantics=("arbitrary",),
                                             vmem_limit_bytes=VMEM_LIMIT_BYTES),
        name="out_proj_layernorm",
    )(x2d, ygm, ynsa, ymem, w_out, ln_g, ln_b)


def _pair_heads(w, axis):
    shape = w.shape
    w = w.reshape(shape[:axis] + (NSA_HEADS, HEAD_DIM) + shape[axis + 1:])
    w = jnp.take(w, np.asarray(PAIR_HEAD_ORDER), axis=axis)
    return w.reshape(shape)


def _permute_w_in(w):
    o_gate = 2048
    o_nz = o_gate + GATE_COLS
    o_mq = o_nz + NSA_WIDTH
    pieces = [w[:, :768], _pair_heads(w[:, 768:1280], 1), w[:, 1280:2048],
              _pair_heads(w[:, o_nz:o_mq], 1), w[:, o_mq:],
              w[:, o_gate:o_nz], jnp.zeros((w.shape[0], LANES - GATE_COLS), w.dtype)]
    return jnp.concatenate(pieces, axis=1).astype(jnp.bfloat16)


def _rope_tables(seq_len):
    half = HEAD_DIM // 2
    inv_freq = ROPE_THETA ** (-jnp.arange(half, dtype=jnp.float32) * 2.0 / HEAD_DIM)
    ang = jnp.arange(seq_len).astype(jnp.float32)[:, None] * inv_freq[None, :]
    cos, sin = jnp.cos(ang), jnp.sin(ang)
    reps = LANES // HEAD_DIM
    cos_t = jnp.tile(jnp.concatenate([cos, cos], axis=1), (1, reps))
    sin_t = jnp.tile(jnp.concatenate([-sin, sin], axis=1), (1, reps))
    rot_low = ((np.arange(LANES) % HEAD_DIM) < half).astype(np.float32)[None, :]
    return cos_t, sin_t, jnp.asarray(rot_low)


def _compress_weights(pos_k, w1_k, w2_k, pos_v, w1_v, w2_v):
    half = CMP_BLOCK // 2
    eye = jnp.eye(NSA_KV_GROUPS, dtype=w1_k.dtype)

    def expand_w1(w1):
        w = w1.reshape(2, half, HEAD_DIM, CMP_HIDDEN)
        w = jnp.einsum('aldh,gk->algdkh', w, eye)
        return w.reshape(2, half * NSA_KV_WIDTH, NSA_KV_GROUPS * CMP_HIDDEN)

    def expand_w2(w2):
        w = jnp.einsum('hd,gk->ghkd', w2, eye)
        return w.reshape(NSA_KV_GROUPS * CMP_HIDDEN, NSA_KV_WIDTH)

    def expand_pos(pos):
        p = pos.reshape(2, half, 1, HEAD_DIM)
        p = jnp.broadcast_to(p, (2, half, NSA_KV_GROUPS, HEAD_DIM))
        return p.reshape(2, half * NSA_KV_WIDTH)

    pos = jnp.stack([expand_pos(pos_k), expand_pos(pos_v)])
    w1 = jnp.stack([expand_w1(w1_k), expand_w1(w1_v)]).astype(jnp.bfloat16)
    w2 = jnp.stack([expand_w2(w2_k), expand_w2(w2_v)]).astype(jnp.bfloat16)
    return pos, w1, w2


def _overlap_matrix(n_rows, n_sel):
    c_start = np.arange(n_rows) * CMP_STRIDE
    s_start = np.arange(LANES) * SEL_BLOCK
    ovl = ((c_start[:, None] < s_start[None, :] + SEL_BLOCK)
           & (c_start[:, None] + CMP_BLOCK > s_start[None, :])
           & (np.arange(LANES)[None, :] < n_sel))
    return jnp.asarray(ovl.astype(np.float32), dtype=jnp.bfloat16)


def kernel(x, mem, w_in, gm_ln_g, gm_ln_b, gm_ws, gm_bs, cmp_pos_k, cmp_k_w1, cmp_k_w2,
           cmp_pos_v, cmp_v_w1, cmp_v_w2, w_mem_kv, w_out, ln_g, ln_b):
    batch, seq_len, d_model = x.shape
    depth = w_in.shape[0]
    assert seq_len % SEL_KEY_TILE == 0 and seq_len >= WINDOW + Q_BLOCK
    assert SEL_TOPK <= seq_len // SEL_BLOCK <= LANES
    alpha = (2.0 * depth) ** 0.25
    n_tok = batch * seq_len
    n_rows = seq_len // CMP_STRIDE

    cos_t, sin_t, rot_low = _rope_tables(seq_len)
    ovl = _overlap_matrix(n_rows, seq_len // SEL_BLOCK)
    tril = jnp.tril(jnp.ones((GM_CHUNK, GM_CHUNK), gm_ws.dtype))
    mk_all, mv_all = _memkv(mem.reshape(batch * mem.shape[1], d_model), w_mem_kv.astype(jnp.bfloat16))

    h = x.reshape(n_tok, d_model)
    for l in range(depth):
        w_cat = _permute_w_in(w_in[l])
        gws = (gm_ws[l] * tril[None]).astype(jnp.bfloat16)
        gbs = jnp.repeat(gm_bs[l].T, HEAD_DIM, axis=1)
        glg = gm_ln_g[l].reshape(1, GM_WIDTH)
        glb = gm_ln_b[l].reshape(1, GM_WIDTH)
        (ygm, ymem, q, kc, vc, ksa, vs, kw, vw, nz, gates) = _inproj(
            h, w_cat, cos_t, sin_t, rot_low, gws, gbs, glg, glb, mk_all[l], mv_all[l],
            batch=batch, seq_len=seq_len)

        pos, w1, w2 = _compress_weights(cmp_pos_k[l], cmp_k_w1[l], cmp_k_w2[l],
                                        cmp_pos_v[l], cmp_v_w1[l], cmp_v_w2[l])
        row_shape = (batch, n_rows, CMP_STRIDE * NSA_KV_WIDTH)
        kcmp, vcmp = _compress(kc.reshape(row_shape), vc.reshape(row_shape), pos, w1, w2)

        def per_seq(a):
            return a.reshape(batch, seq_len, a.shape[-1])

        ynsa = _nsa(per_seq(q), per_seq(nz), per_seq(gates), per_seq(ksa), per_seq(vs),
                    per_seq(kw), per_seq(vw), kcmp, vcmp, ovl)

        w_out_p = jnp.concatenate([w_out[l, :GM_WIDTH],
                                   _pair_heads(w_out[l, GM_WIDTH:GM_WIDTH + NSA_WIDTH], 0),
                                   w_out[l, GM_WIDTH + NSA_WIDTH:]], axis=0).astype(jnp.bfloat16)
        h = _outproj(h, ygm, ynsa.reshape(n_tok, NSA_WIDTH), ymem, w_out_p,
                     ln_g[l].reshape(1, d_model), ln_b[l].reshape(1, d_model), alpha=alpha)
    return h.reshape(batch, seq_len, d_model)
```

```python
import functools

import numpy as np
import jax
import jax.numpy as jnp
from jax import lax
from jax.experimental import pallas as pl
from jax.experimental.pallas import tpu as pltpu

HEAD_DIM = 64
GM_GROUPS = 4
GM_WIDTH = GM_GROUPS * HEAD_DIM
GM_CHUNK = 128
NSA_HEADS = 8
NSA_KV_GROUPS = 2
NSA_HPG = NSA_HEADS // NSA_KV_GROUPS
NSA_WIDTH = NSA_HEADS * HEAD_DIM
NSA_KV_WIDTH = NSA_KV_GROUPS * HEAD_DIM
CMP_BLOCK = 32
CMP_STRIDE = 16
CMP_HIDDEN = 128
SEL_BLOCK = 64
SEL_TOPK = 16
N_LOCAL_SEL = 2
WINDOW = 512
Q_BLOCK = 128
MEM_HEADS = 4
MEM_WIDTH = MEM_HEADS * HEAD_DIM
ROPE_THETA = 10000.0
LN_EPS = 1e-5
NEG_INF = -1e30
FORCE_SCORE = 1e4
GATE_COLS = NSA_HEADS * 3

LANES = 128
VMEM_LIMIT_BYTES = 56 * 1024 * 1024

PROJ_ROWS = 256
SEL_KEY_TILE = 512
REMOVED = -3.0e38

PAIR_HEAD_ORDER = tuple(h for i in range(NSA_HPG) for h in (i, i + NSA_HPG))

C_GU, C_GV, C_GZ = 0, 256, 512
C_Q = 768
C_KC, C_VC, C_KS, C_VS, C_KW, C_VW = 1280, 1408, 1536, 1664, 1792, 1920
C_NZ = 2048
C_MQ, C_MZ = 2560, 2816
C_GATE = 3072
N_COLS = 3200


def _dot(a, b):
    return jnp.dot(a, b, preferred_element_type=jnp.float32)


def _dot_nt(a, b):
    return lax.dot_general(a, b, (((1,), (1,)), ((), ())), preferred_element_type=jnp.float32)


def _gelu(x):
    return 0.5 * x * (1.0 + lax.erf(x * np.float32(np.sqrt(0.5))))


def _silu(x):
    return x * jax.nn.sigmoid(x)


def _lane_iota(shape):
    return lax.broadcasted_iota(jnp.int32, shape, len(shape) - 1)


def _low_half(shape):
    return (_lane_iota(shape) % LANES) < HEAD_DIM


def _tile_lanes(x, reps):
    return jnp.concatenate([x] * reps, axis=-1) if reps > 1 else x


def _memkv_kernel(mem_ref, w_ref, k_ref, v_ref):
    kv = _dot(mem_ref[...].astype(jnp.bfloat16), w_ref[0])
    k_ref[0] = kv[:, :MEM_WIDTH].astype(jnp.bfloat16)
    v_ref[0] = kv[:, MEM_WIDTH:].astype(jnp.bfloat16)


def _memkv(mem2d, w_mem_kv_bf16):
    depth = w_mem_kv_bf16.shape[0]
    rows, d_model = mem2d.shape
    out = jax.ShapeDtypeStruct((depth, rows, MEM_WIDTH), jnp.bfloat16)
    return pl.pallas_call(
        _memkv_kernel,
        grid=(depth,),
        in_specs=[pl.BlockSpec((rows, d_model), lambda l: (0, 0)),
                  pl.BlockSpec((1, d_model, 2 * MEM_WIDTH), lambda l: (l, 0, 0))],
        out_specs=[pl.BlockSpec((1, rows, MEM_WIDTH), lambda l: (l, 0, 0)),
                   pl.BlockSpec((1, rows, MEM_WIDTH), lambda l: (l, 0, 0))],
        out_shape=[out, out],
        name="mem_kv_proj",
    )(mem2d, w_mem_kv_bf16)


def _rope(x, cos, sin_signed, low):
    width = x.shape[-1]
    swapped = jnp.where(low, pltpu.roll(x, width - HEAD_DIM // 2, 1), pltpu.roll(x, HEAD_DIM // 2, 1))
    return x * cos + swapped * sin_signed


def _group_layer_norm(v, g, b, low):
    inv = np.float32(1.0 / HEAD_DIM)
    s_lo = jnp.sum(jnp.where(low, v, 0.0), axis=-1, keepdims=True)
    s_hi = jnp.sum(jnp.where(low, 0.0, v), axis=-1, keepdims=True)
    mu = jnp.where(low, s_lo, s_hi) * inv
    d = v - mu
    d2 = d * d
    q_lo = jnp.sum(jnp.where(low, d2, 0.0), axis=-1, keepdims=True)
    q_hi = jnp.sum(jnp.where(low, 0.0, d2), axis=-1, keepdims=True)
    var = jnp.where(low, q_lo, q_hi) * inv
    return d * lax.rsqrt(var + LN_EPS) * g + b


def _inproj_kernel(x_ref, w_ref, cos_ref, sin_ref, rot_low_ref, gws_ref, gbs_ref, glg_ref, glb_ref,
                   mk_ref, mv_ref,
                   ygm_ref, ymem_ref, q_ref, kc_ref, vc_ref, ksa_ref, vs_ref, kw_ref, vw_ref,
                   nz_ref, gate_ref, *, seq_len):
    rows = x_ref.shape[0]
    xb = x_ref[...].astype(jnp.bfloat16)
    low = _low_half((rows, LANES))
    rot_low = rot_low_ref[...] > 0.5
    rot_low = jnp.broadcast_to(rot_low, (rows, LANES))
    cos = cos_ref[...]
    sin = sin_ref[...]

    def proj(c0, width):
        return _dot(xb, w_ref[:, c0:c0 + width])

    u = _gelu(proj(C_GU, GM_WIDTH))
    v = _gelu(proj(C_GV, GM_WIDTH))
    z = proj(C_GZ, GM_WIDTH)
    for pair in range(GM_GROUPS // 2):
        sl = slice(pair * LANES, (pair + 1) * LANES)
        vln = _group_layer_norm(v[:, sl], glg_ref[:, sl], glb_ref[:, sl], low).astype(jnp.bfloat16)
        for c in range(rows // GM_CHUNK):
            rs = slice(c * GM_CHUNK, (c + 1) * GM_CHUNK)
            s_lo = _dot(gws_ref[2 * pair], vln[rs])
            s_hi = _dot(gws_ref[2 * pair + 1], vln[rs])
            s = jnp.where(_low_half((GM_CHUNK, LANES)), s_lo, s_hi) + gbs_ref[:, sl]
            ygm_ref[rs, sl] = (u[rs, sl] * s * _silu(z[rs, sl])).astype(ygm_ref.dtype)

    qscale = np.float32(HEAD_DIM ** -0.5)
    qscale2 = np.float32(HEAD_DIM ** -0.5 * np.log2(np.e))
    ones = jnp.ones((rows, LANES), vs_ref.dtype)
    for i in range(NSA_WIDTH // LANES):
        qi = _rope(proj(C_Q + i * LANES, LANES), cos, sin, rot_low) * qscale2
        q_ref[:, i * LANES:(i + 1) * LANES] = qi.astype(q_ref.dtype)
    kc_ref[...] = _rope(proj(C_KC, LANES), cos, sin, rot_low)
    vc_ref[...] = proj(C_VC, LANES)
    ks = _rope(proj(C_KS, LANES), cos, sin, rot_low)
    ksa_ref[:, :LANES] = ks.astype(ksa_ref.dtype)
    tok = (pl.program_id(0) * rows) % seq_len + lax.broadcasted_iota(jnp.int32, (rows, LANES), 0)
    onehot = (tok // SEL_BLOCK) == _lane_iota((rows, LANES))
    ksa_ref[:, LANES:] = jnp.where(onehot, 1.0, 0.0).astype(ksa_ref.dtype)
    vs_ref[:, :LANES] = proj(C_VS, LANES).astype(vs_ref.dtype)
    vs_ref[:, LANES:] = ones
    kw_ref[...] = _rope(proj(C_KW, LANES), cos, sin, rot_low).astype(kw_ref.dtype)
    vw_ref[:, :LANES] = proj(C_VW, LANES).astype(vw_ref.dtype)
    vw_ref[:, LANES:] = ones
    nz_ref[...] = _silu(proj(C_NZ, NSA_WIDTH))
    gate_ref[...] = jax.nn.sigmoid(proj(C_GATE, LANES))

    mq = proj(C_MQ, MEM_WIDTH) * qscale
    mz = proj(C_MZ, MEM_WIDTH)
    for pair in range(MEM_HEADS // 2):
        sl = slice(pair * LANES, (pair + 1) * LANES)
        kp = mk_ref[0, :, sl]
        vp = mv_ref[0, :, sl]
        outs = []
        for keep_low in (True, False):
            qh = jnp.where(low == keep_low, mq[:, sl], 0.0).astype(jnp.bfloat16)
            s = _dot_nt(qh, kp)
            e = jnp.exp(s - jnp.max(s, axis=-1, keepdims=True))
            p = e / jnp.sum(e, axis=-1, keepdims=True)
            outs.append(_dot(p.astype(jnp.bfloat16), vp))
        o = jnp.where(low, outs[0], outs[1])
        ymem_ref[:, sl] = (o * _silu(mz[:, sl])).astype(ymem_ref.dtype)


def _inproj(x2d, w_cat, cos_t, sin_t, rot_low, gws, gbs, glg, glb, mk, mv, *, batch, seq_len):
    n, d_model = x2d.shape
    rows = PROJ_ROWS
    steps_per_seq = seq_len // rows
    mem_len = mk.shape[0] // batch

    def tok_spec(width):
        return pl.BlockSpec((rows, width), lambda i: (i, 0))

    def const_spec(shape):
        return pl.BlockSpec(shape, lambda i: (0,) * len(shape))

    tab_spec = pl.BlockSpec((rows, LANES), lambda i: (i % steps_per_seq, 0))
    mem_spec = pl.BlockSpec((1, mem_len, MEM_WIDTH), lambda i: (i // steps_per_seq, 0, 0))
    bf16, f32 = jnp.bfloat16, jnp.float32
    outs = [(GM_WIDTH, bf16), (MEM_WIDTH, bf16), (NSA_WIDTH, bf16), (LANES, f32), (LANES, f32),
            (2 * LANES, bf16), (2 * LANES, bf16), (LANES, bf16), (2 * LANES, bf16), (NSA_WIDTH, f32),
            (LANES, f32)]
    return pl.pallas_call(
        functools.partial(_inproj_kernel, seq_len=seq_len),
        grid=(n // rows,),
        in_specs=[tok_spec(d_model), const_spec(w_cat.shape), tab_spec, tab_spec,
                  const_spec(rot_low.shape), const_spec(gws.shape), const_spec(gbs.shape),
                  const_spec(glg.shape), const_spec(glb.shape), mem_spec, mem_spec],
        out_specs=[tok_spec(w) for w, _ in outs],
        out_shape=[jax.ShapeDtypeStruct((n, w), dt) for w, dt in outs],
        compiler_params=pltpu.CompilerParams(dimension_semantics=("arbitrary",),
                                             vmem_limit_bytes=VMEM_LIMIT_BYTES),
        name="in_proj_mixers",
    )(x2d, w_cat, cos_t, sin_t, rot_low, gws, gbs, glg, glb,
      mk.reshape(batch, mem_len, MEM_WIDTH), mv.reshape(batch, mem_len, MEM_WIDTH))


def _compress_kernel(k_ref, v_ref, pos_ref, w1_ref, w2_ref, kcmp_ref, vcmp_ref):
    n_rows = k_ref.shape[1]
    for idx, (src, dst) in enumerate(((k_ref, kcmp_ref), (v_ref, vcmp_ref))):
        xr = src[0]
        top = _dot((xr + pos_ref[idx, 0:1]).astype(jnp.bfloat16), w1_ref[idx, 0])
        bot = _dot((xr + pos_ref[idx, 1:2]).astype(jnp.bfloat16), w1_ref[idx, 1])
        hidden = top + pltpu.roll(bot, n_rows - 1, 0)
        dst[0] = _dot(_gelu(hidden).astype(jnp.bfloat16), w2_ref[idx]).astype(dst.dtype)


def _compress(kc_rows, vc_rows, pos, w1, w2):
    batch, n_rows, width = kc_rows.shape
    row_spec = pl.BlockSpec((1, n_rows, width), lambda b: (b, 0, 0))
    out_spec = pl.BlockSpec((1, n_rows, LANES), lambda b: (b, 0, 0))
    out = jax.ShapeDtypeStruct((batch, n_rows, LANES), jnp.bfloat16)
    return pl.pallas_call(
        _compress_kernel,
        grid=(batch,),
        in_specs=[row_spec, row_spec,
                  pl.BlockSpec(pos.shape, lambda b: (0, 0, 0)),
                  pl.BlockSpec(w1.shape, lambda b: (0, 0, 0, 0)),
                  pl.BlockSpec(w2.shape, lambda b: (0, 0, 0))],
        out_specs=[out_spec, out_spec],
        out_shape=[out, out],
        compiler_params=pltpu.CompilerParams(dimension_semantics=("arbitrary",),
                                             vmem_limit_bytes=VMEM_LIMIT_BYTES),
        name="nsa_compress",
    )(kc_rows, vc_rows, pos, w1, w2)


def _split_bf16(x, parts):
    out = []
    for _ in range(parts):
        hi = x.astype(jnp.bfloat16)
        out.append(hi)
        x = x - hi.astype(jnp.float32)
    return out


def _topk_mask(score):
    lane = _lane_iota(score.shape)
    picked = jnp.zeros(score.shape, jnp.bool_)
    for _ in range(SEL_TOPK):
        first = jnp.argmax(score, axis=-1, keepdims=True).astype(jnp.int32)
        hit = lane == first
        picked = picked | hit
        score = jnp.where(hit, REMOVED, score)
    return picked


def _nsa_kernel(q_ref, nz_ref, gate_ref, ksa_ref, vs_ref, kw_ref, vw_ref, kcmp_ref, vcmp_ref,
                ovl_ref, out_ref, qa_ref, s_ref, p_ref, m_ref, acc_ref, oc_ref, ow_ref,
                sa_ref, sb_ref, *, seq_len):
    bi = pl.program_id(1)
    start = bi * Q_BLOCK
    n_sel = seq_len // SEL_BLOCK
    n_cmp = kcmp_ref.shape[1]
    span = WINDOW + Q_BLOCK
    tk = SEL_KEY_TILE
    bf16 = jnp.bfloat16

    def head_rows(r):
        return slice(r * Q_BLOCK, (r + 1) * Q_BLOCK)

    low = _low_half((Q_BLOCK, LANES))
    for i in range(NSA_HPG):
        qi = q_ref[0, :, i * LANES:(i + 1) * LANES]
        zero = jnp.zeros_like(qi)
        qa_ref[head_rows(2 * i), :LANES] = jnp.where(low, qi, zero)
        qa_ref[head_rows(2 * i + 1), :LANES] = jnp.where(low, zero, qi)

    t_q = start + lax.broadcasted_iota(jnp.int32, (Q_BLOCK, 1), 0)

    s_ref[:, :n_cmp] = _dot_nt(qa_ref[:, :LANES], kcmp_ref[0])
    c_end = lax.broadcasted_iota(jnp.int32, (1, n_cmp), 1) * CMP_STRIDE + (CMP_BLOCK - 1)
    cbias = jnp.where(c_end <= t_q, 0.0, NEG_INF)
    row_seen = t_q >= CMP_BLOCK - 1
    p_sum = [None] * NSA_KV_GROUPS
    for r in range(NSA_HEADS):
        s = s_ref[head_rows(r), :n_cmp] + cbias
        e = jnp.exp2(s - jnp.max(s, axis=-1, keepdims=True))
        inv = jnp.where(row_seen, 1.0 / jnp.sum(e, axis=-1, keepdims=True), 0.0)
        p = e * inv
        g = r % NSA_KV_GROUPS
        p_sum[g] = p if p_sum[g] is None else p_sum[g] + p
        p_ref[head_rows(r), :n_cmp] = p.astype(bf16)
    oc_ref[...] = _dot(p_ref[:, :n_cmp], vcmp_ref[0])

    blk = _lane_iota((Q_BLOCK, LANES))
    t_blk = t_q // SEL_BLOCK
    valid = blk <= t_blk
    forced = (blk == 0) | (valid & (blk > t_blk - N_LOCAL_SEL))
    scores = []
    for g in range(NSA_KV_GROUPS):
        imp = sum(_dot(part, ovl_ref[...]) for part in _split_bf16(p_sum[g], 3))
        sc = jnp.where(forced, FORCE_SCORE, jnp.where(valid, imp, -1.0))
        if n_sel < LANES:
            sc = jnp.where(blk < n_sel, sc, REMOVED)
        scores.append(sc)
    picked = _topk_mask(jnp.concatenate(scores, axis=0))
    bias = jnp.where(picked, 0.0, NEG_INF).astype(bf16)
    for r in range(NSA_HEADS):
        g = r % NSA_KV_GROUPS
        qa_ref[head_rows(r), LANES:] = bias[g * Q_BLOCK:(g + 1) * Q_BLOCK]

    w0 = pl.multiple_of(jnp.maximum(start - WINDOW, 0), Q_BLOCK)
    s_ref[:, n_cmp:n_cmp + span] = _dot_nt(qa_ref[:, :LANES], kw_ref[0, pl.ds(w0, span), :])
    kpos = w0 + lax.broadcasted_iota(jnp.int32, (1, span), 1)
    wbias = jnp.where((kpos <= t_q) & (kpos > t_q - WINDOW), 0.0, NEG_INF)
    for r in range(NSA_HEADS):
        s = s_ref[head_rows(r), n_cmp:n_cmp + span] + wbias
        e = jnp.exp2(s - jnp.max(s, axis=-1, keepdims=True))
        p_ref[head_rows(r), n_cmp:n_cmp + span] = e.astype(bf16)
    o_w = _dot(p_ref[:, n_cmp:n_cmp + span], vw_ref[0, pl.ds(w0, span), :])
    ow_ref[...] = o_w[:, :LANES] / o_w[:, LANES:]

    m_ref[...] = jnp.full(m_ref.shape, NEG_INF, jnp.float32)
    acc_ref[...] = jnp.zeros(acc_ref.shape, jnp.float32)

    def scores(tile):
        k0 = pl.multiple_of(tile * tk, tk)
        return _dot_nt(qa_ref[...], ksa_ref[0, pl.ds(k0, tk), :])

    def consume(buf_ref, tile, causal):
        k0 = pl.multiple_of(tile * tk, tk)
        s = buf_ref[...]
        if causal:
            kpos = k0 + lax.broadcasted_iota(jnp.int32, (1, tk), 1)
            tile_bias = jnp.where(kpos <= t_q, 0.0, NEG_INF)
            s = s + jnp.concatenate([tile_bias] * NSA_HEADS, axis=0)
        m_prev = m_ref[...]
        m_next = jnp.maximum(m_prev, jnp.max(s, axis=-1, keepdims=True))
        p = jnp.exp2(s - _tile_lanes(m_next, tk // LANES))
        alpha = jnp.exp2(m_prev - m_next)
        acc_ref[...] = (_tile_lanes(alpha, 2) * acc_ref[...]
                        + _dot(p.astype(bf16), vs_ref[0, pl.ds(k0, tk), :]))
        m_ref[...] = m_next

    diag = start // tk
    sa_ref[...] = scores(0)

    def tile_pair(j, carry):
        sb_ref[...] = scores(2 * j + 1)
        consume(sa_ref, 2 * j, False)
        sa_ref[...] = scores(2 * j + 2)
        consume(sb_ref, 2 * j + 1, False)
        return carry

    lax.fori_loop(0, diag // 2, tile_pair, 0)

    @pl.when(diag % 2 == 1)
    def _():
        sb_ref[...] = scores(diag)
        consume(sa_ref, diag - 1, False)
        consume(sb_ref, diag, True)

    @pl.when(diag % 2 == 0)
    def _():
        consume(sa_ref, diag, True)

    gates = gate_ref[0]
    for i in range(NSA_HPG):
        halves = []
        for g in range(NSA_KV_GROUPS):
            head = i + NSA_HPG * g
            rs = head_rows(2 * i + g)
            o_s = acc_ref[rs, :LANES] / acc_ref[rs, LANES:]
            halves.append(oc_ref[rs] * gates[:, 3 * head:3 * head + 1]
                          + o_s * gates[:, 3 * head + 1:3 * head + 2]
                          + ow_ref[rs] * gates[:, 3 * head + 2:3 * head + 3])
        sl = slice(i * LANES, (i + 1) * LANES)
        out_ref[0, :, sl] = (jnp.where(low, halves[0], halves[1]) * nz_ref[0, :, sl]).astype(out_ref.dtype)


def _nsa(q, nz, gates, ksa, vs, kw, vw, kcmp, vcmp, ovl):
    batch, seq_len, _ = q.shape

    def q_spec(width):
        return pl.BlockSpec((1, Q_BLOCK, width), lambda b, i: (b, i, 0))

    def seq_spec(arr):
        return pl.BlockSpec((1,) + arr.shape[1:], lambda b, i: (b, 0, 0))

    rows = NSA_HEADS * Q_BLOCK
    cols = kcmp.shape[1] + WINDOW + Q_BLOCK
    return pl.pallas_call(
        functools.partial(_nsa_kernel, seq_len=seq_len),
        grid=(batch, seq_len // Q_BLOCK),
        in_specs=[q_spec(NSA_WIDTH), q_spec(NSA_WIDTH), q_spec(LANES),
                  seq_spec(ksa), seq_spec(vs), seq_spec(kw), seq_spec(vw),
                  seq_spec(kcmp), seq_spec(vcmp),
                  pl.BlockSpec(ovl.shape, lambda b, i: (0, 0))],
        out_specs=q_spec(NSA_WIDTH),
        out_shape=jax.ShapeDtypeStruct((batch, seq_len, NSA_WIDTH), jnp.bfloat16),
        scratch_shapes=[pltpu.VMEM((rows, 2 * LANES), jnp.bfloat16),
                        pltpu.VMEM((rows, cols), jnp.float32),
                        pltpu.VMEM((rows, cols), jnp.bfloat16),
                        pltpu.VMEM((rows, LANES), jnp.float32),
                        pltpu.VMEM((rows, 2 * LANES), jnp.float32),
                        pltpu.VMEM((rows, LANES), jnp.float32),
                        pltpu.VMEM((rows, LANES), jnp.float32),
                        pltpu.VMEM((rows, SEL_KEY_TILE), jnp.float32),
                        pltpu.VMEM((rows, SEL_KEY_TILE), jnp.float32)],
        compiler_params=pltpu.CompilerParams(dimension_semantics=("arbitrary", "arbitrary"),
                                             vmem_limit_bytes=VMEM_LIMIT_BYTES),
        name="nsa_attention",
    )(q, nz, gates, ksa, vs, kw, vw, kcmp, vcmp, ovl)


def _outproj_kernel(x_ref, ygm_ref, ynsa_ref, ymem_ref, w_ref, g_ref, b_ref, o_ref, *, alpha):
    y = (_dot(ygm_ref[...], w_ref[:GM_WIDTH])
         + _dot(ynsa_ref[...], w_ref[GM_WIDTH:GM_WIDTH + NSA_WIDTH])
         + _dot(ymem_ref[...], w_ref[GM_WIDTH + NSA_WIDTH:]))
    r = alpha * x_ref[...] + y
    mu = jnp.mean(r, axis=-1, keepdims=True)
    d = r - mu
    var = jnp.mean(d * d, axis=-1, keepdims=True)
    o_ref[...] = d * lax.rsqrt(var + LN_EPS) * g_ref[...] + b_ref[...]


def _outproj(x2d, ygm, ynsa, ymem, w_out, ln_g, ln_b, *, alpha):
    n, d_model = x2d.shape
    rows = PROJ_ROWS

    def tok_spec(width):
        return pl.BlockSpec((rows, width), lambda i: (i, 0))

    def const_spec(shape):
        return pl.BlockSpec(shape, lambda i: (0, 0))

    return pl.pallas_call(
        functools.partial(_outproj_kernel, alpha=alpha),
        grid=(n // rows,),
        in_specs=[tok_spec(d_model), tok_spec(GM_WIDTH), tok_spec(NSA_WIDTH), tok_spec(MEM_WIDTH),
                  const_spec(w_out.shape), const_spec(ln_g.shape), const_spec(ln_b.shape)],
        out_specs=tok_spec(d_model),
        out_shape=jax.ShapeDtypeStruct((n, d_model), jnp.float32),
        compiler_params=pltpu.CompilerParams(dimension_semantics=("arbitrary",),
                                             vmem_limit_bytes=VMEM_LIMIT_BYTES),
        name="out_proj_layernorm",
    )(x2d, ygm, ynsa, ymem, w_out, ln_g, ln_b)


def _pair_heads(w, axis):
    shape = w.shape
    w = w.reshape(shape[:axis] + (NSA_HEADS, HEAD_DIM) + shape[axis + 1:])
    w = jnp.take(w, np.asarray(PAIR_HEAD_ORDER), axis=axis)
    return w.reshape(shape)


def _permute_w_in(w):
    o_gate = 2048
    o_nz = o_gate + GATE_COLS
    o_mq = o_nz + NSA_WIDTH
    pieces = [w[:, :768], _pair_heads(w[:, 768:1280], 1), w[:, 1280:2048],
              _pair_heads(w[:, o_nz:o_mq], 1), w[:, o_mq:],
              w[:, o_gate:o_nz], jnp.zeros((w.shape[0], LANES - GATE_COLS), w.dtype)]
    return jnp.concatenate(pieces, axis=1).astype(jnp.bfloat16)


def _rope_tables(seq_len):
    half = HEAD_DIM // 2
    inv_freq = ROPE_THETA ** (-jnp.arange(half, dtype=jnp.float32) * 2.0 / HEAD_DIM)
    ang = jnp.arange(seq_len).astype(jnp.float32)[:, None] * inv_freq[None, :]
    cos, sin = jnp.cos(ang), jnp.sin(ang)
    reps = LANES // HEAD_DIM
    cos_t = jnp.tile(jnp.concatenate([cos, cos], axis=1), (1, reps))
    sin_t = jnp.tile(jnp.concatenate([-sin, sin], axis=1), (1, reps))
    rot_low = ((np.arange(LANES) % HEAD_DIM) < half).astype(np.float32)[None, :]
    return cos_t, sin_t, jnp.asarray(rot_low)


def _compress_weights(pos_k, w1_k, w2_k, pos_v, w1_v, w2_v):
    half = CMP_BLOCK // 2
    eye = jnp.eye(NSA_KV_GROUPS, dtype=w1_k.dtype)

    def expand_w1(w1):
        w = w1.reshape(2, half, HEAD_DIM, CMP_HIDDEN)
        w = jnp.einsum('aldh,gk->algdkh', w, eye)
        return w.reshape(2, half * NSA_KV_WIDTH, NSA_KV_GROUPS * CMP_HIDDEN)

    def expand_w2(w2):
        w = jnp.einsum('hd,gk->ghkd', w2, eye)
        return w.reshape(NSA_KV_GROUPS * CMP_HIDDEN, NSA_KV_WIDTH)

    def expand_pos(pos):
        p = pos.reshape(2, half, 1, HEAD_DIM)
        p = jnp.broadcast_to(p, (2, half, NSA_KV_GROUPS, HEAD_DIM))
        return p.reshape(2, half * NSA_KV_WIDTH)

    pos = jnp.stack([expand_pos(pos_k), expand_pos(pos_v)])
    w1 = jnp.stack([expand_w1(w1_k), expand_w1(w1_v)]).astype(jnp.bfloat16)
    w2 = jnp.stack([expand_w2(w2_k), expand_w2(w2_v)]).astype(jnp.bfloat16)
    return pos, w1, w2


def _overlap_matrix(n_rows, n_sel):
    c_start = np.arange(n_rows) * CMP_STRIDE
    s_start = np.arange(LANES) * SEL_BLOCK
    ovl = ((c_start[:, None] < s_start[None, :] + SEL_BLOCK)
           & (c_start[:, None] + CMP_BLOCK > s_start[None, :])
           & (np.arange(LANES)[None, :] < n_sel))
    return jnp.asarray(ovl.astype(np.float32), dtype=jnp.bfloat16)


def kernel(x, mem, w_in, gm_ln_g, gm_ln_b, gm_ws, gm_bs, cmp_pos_k, cmp_k_w1, cmp_k_w2,
           cmp_pos_v, cmp_v_w1, cmp_v_w2, w_mem_kv, w_out, ln_g, ln_b):
    batch, seq_len, d_model = x.shape
    depth = w_in.shape[0]
    assert seq_len % SEL_KEY_TILE == 0 and seq_len >= WINDOW + Q_BLOCK
    assert SEL_TOPK <= seq_len // SEL_BLOCK <= LANES
    alpha = (2.0 * depth) ** 0.25
    n_tok = batch * seq_len
    n_rows = seq_len // CMP_STRIDE

    cos_t, sin_t, rot_low = _rope_tables(seq_len)
    ovl = _overlap_matrix(n_rows, seq_len // SEL_BLOCK)
    tril = jnp.tril(jnp.ones((GM_CHUNK, GM_CHUNK), gm_ws.dtype))
    mk_all, mv_all = _memkv(mem.reshape(batch * mem.shape[1], d_model), w_mem_kv.astype(jnp.bfloat16))

    h = x.reshape(n_tok, d_model)
    for l in range(depth):
        w_cat = _permute_w_in(w_in[l])
        gws = (gm_ws[l] * tril[None]).astype(jnp.bfloat16)
        gbs = jnp.repeat(gm_bs[l].T, HEAD_DIM, axis=1)
        glg = gm_ln_g[l].reshape(1, GM_WIDTH)
        glb = gm_ln_b[l].reshape(1, GM_WIDTH)
        (ygm, ymem, q, kc, vc, ksa, vs, kw, vw, nz, gates) = _inproj(
            h, w_cat, cos_t, sin_t, rot_low, gws, gbs, glg, glb, mk_all[l], mv_all[l],
            batch=batch, seq_len=seq_len)

        pos, w1, w2 = _compress_weights(cmp_pos_k[l], cmp_k_w1[l], cmp_k_w2[l],
                                        cmp_pos_v[l], cmp_v_w1[l], cmp_v_w2[l])
        row_shape = (batch, n_rows, CMP_STRIDE * NSA_KV_WIDTH)
        kcmp, vcmp = _compress(kc.reshape(row_shape), vc.reshape(row_shape), pos, w1, w2)

        def per_seq(a):
            return a.reshape(batch, seq_len, a.shape[-1])

        ynsa = _nsa(per_seq(q), per_seq(nz), per_seq(gates), per_seq(ksa), per_seq(vs),
                    per_seq(kw), per_seq(vw), kcmp, vcmp, ovl)

        w_out_p = jnp.concatenate([w_out[l, :GM_WIDTH],
                                   _pair_heads(w_out[l, GM_WIDTH:GM_WIDTH + NSA_WIDTH], 0),
                                   w_out[l, GM_WIDTH + NSA_WIDTH:]], axis=0).astype(jnp.bfloat16)
        h = _outproj(h, ygm, ynsa.reshape(n_tok, NSA_WIDTH), ymem, w_out_p,
                     ln_g[l].reshape(1, d_model), ln_b[l].reshape(1, d_model), alpha=alpha)
    return h.reshape(batch, seq_len, d_model)
```

```python
import functools

import numpy as np
import jax
import jax.numpy as jnp
from jax import lax
from jax.experimental import pallas as pl
from jax.experimental.pallas import tpu as pltpu

HEAD_DIM = 64
GM_GROUPS = 4
GM_WIDTH = GM_GROUPS * HEAD_DIM
GM_CHUNK = 128
NSA_HEADS = 8
NSA_KV_GROUPS = 2
NSA_HPG = NSA_HEADS // NSA_KV_GROUPS
NSA_WIDTH = NSA_HEADS * HEAD_DIM
NSA_KV_WIDTH = NSA_KV_GROUPS * HEAD_DIM
CMP_BLOCK = 32
CMP_STRIDE = 16
CMP_HIDDEN = 128
SEL_BLOCK = 64
SEL_TOPK = 16
N_LOCAL_SEL = 2
WINDOW = 512
Q_BLOCK = 128
MEM_HEADS = 4
MEM_WIDTH = MEM_HEADS * HEAD_DIM
ROPE_THETA = 10000.0
LN_EPS = 1e-5
NEG_INF = -1e30
FORCE_SCORE = 1e4
GATE_COLS = NSA_HEADS * 3

LANES = 128
VMEM_LIMIT_BYTES = 56 * 1024 * 1024

PROJ_ROWS = 512
PROJ_SUB_ROWS = 256
SEL_KEY_TILE = 512
SOFTMAX_ROWS = 32
REMOVED = -3.0e38

PAIR_HEAD_ORDER = tuple(h for i in range(NSA_HPG) for h in (i, i + NSA_HPG))

C_GU, C_GV, C_GZ = 0, 256, 512
C_Q = 768
C_KC, C_VC, C_KS, C_VS, C_KW, C_VW = 1280, 1408, 1536, 1664, 1792, 1920
C_NZ = 2048
C_MQ, C_MZ = 2560, 2816
C_GATE = 3072
N_COLS = 3200


def _dot(a, b):
    return jnp.dot(a, b, preferred_element_type=jnp.float32)


def _dot_nt(a, b):
    return lax.dot_general(a, b, (((1,), (1,)), ((), ())), preferred_element_type=jnp.float32)


def _gelu(x):
    return 0.5 * x * (1.0 + lax.erf(x * np.float32(np.sqrt(0.5))))


def _silu(x):
    return x * jax.nn.sigmoid(x)


def _lane_iota(shape):
    return lax.broadcasted_iota(jnp.int32, shape, len(shape) - 1)


def _low_half(shape):
    return (_lane_iota(shape) % LANES) < HEAD_DIM


def _tile_lanes(x, reps):
    return jnp.concatenate([x] * reps, axis=-1) if reps > 1 else x


def _memkv_kernel(mem_ref, w_ref, k_ref, v_ref):
    kv = _dot(mem_ref[...].astype(jnp.bfloat16), w_ref[0])
    k_ref[0] = kv[:, :MEM_WIDTH].astype(jnp.bfloat16)
    v_ref[0] = kv[:, MEM_WIDTH:].astype(jnp.bfloat16)


def _memkv(mem2d, w_mem_kv_bf16):
    depth = w_mem_kv_bf16.shape[0]
    rows, d_model = mem2d.shape
    out = jax.ShapeDtypeStruct((depth, rows, MEM_WIDTH), jnp.bfloat16)
    return pl.pallas_call(
        _memkv_kernel,
        grid=(depth,),
        in_specs=[pl.BlockSpec((rows, d_model), lambda l: (0, 0)),
                  pl.BlockSpec((1, d_model, 2 * MEM_WIDTH), lambda l: (l, 0, 0))],
        out_specs=[pl.BlockSpec((1, rows, MEM_WIDTH), lambda l: (l, 0, 0)),
                   pl.BlockSpec((1, rows, MEM_WIDTH), lambda l: (l, 0, 0))],
        out_shape=[out, out],
        name="mem_kv_proj",
    )(mem2d, w_mem_kv_bf16)


def _rope(x, cos, sin_signed, low):
    width = x.shape[-1]
    swapped = jnp.where(low, pltpu.roll(x, width - HEAD_DIM // 2, 1), pltpu.roll(x, HEAD_DIM // 2, 1))
    return x * cos + swapped * sin_signed


def _group_layer_norm(v, g, b, low):
    inv = np.float32(1.0 / HEAD_DIM)
    s_lo = jnp.sum(jnp.where(low, v, 0.0), axis=-1, keepdims=True)
    s_hi = jnp.sum(jnp.where(low, 0.0, v), axis=-1, keepdims=True)
    mu = jnp.where(low, s_lo, s_hi) * inv
    d = v - mu
    d2 = d * d
    q_lo = jnp.sum(jnp.where(low, d2, 0.0), axis=-1, keepdims=True)
    q_hi = jnp.sum(jnp.where(low, 0.0, d2), axis=-1, keepdims=True)
    var = jnp.where(low, q_lo, q_hi) * inv
    return d * lax.rsqrt(var + LN_EPS) * g + b


def _inproj_kernel(x_ref, w_ref, cos_ref, sin_ref, rot_low_ref, gws_ref, gbs_ref, glg_ref, glb_ref,
                   mk_ref, mv_ref,
                   ygm_ref, ymem_ref, q_ref, kc_ref, vc_ref, ksa_ref, vs_ref, kw_ref, vw_ref,
                   nz_ref, gate_ref, *, seq_len):
    for r0 in range(0, x_ref.shape[0], PROJ_SUB_ROWS):
        _inproj_rows(slice(r0, r0 + PROJ_SUB_ROWS), pl.program_id(0) * x_ref.shape[0] + r0,
                     x_ref, w_ref, cos_ref, sin_ref, rot_low_ref, gws_ref, gbs_ref, glg_ref, glb_ref,
                     mk_ref, mv_ref, ygm_ref, ymem_ref, q_ref, kc_ref, vc_ref, ksa_ref, vs_ref,
                     kw_ref, vw_ref, nz_ref, gate_ref, seq_len=seq_len)


def _inproj_rows(rs, row0, x_ref, w_ref, cos_ref, sin_ref, rot_low_ref, gws_ref, gbs_ref, glg_ref,
                 glb_ref, mk_ref, mv_ref, ygm_ref, ymem_ref, q_ref, kc_ref, vc_ref, ksa_ref, vs_ref,
                 kw_ref, vw_ref, nz_ref, gate_ref, *, seq_len):
    rows = rs.stop - rs.start
    xb = x_ref[rs, :].astype(jnp.bfloat16)
    low = _low_half((rows, LANES))
    rot_low = rot_low_ref[...] > 0.5
    rot_low = jnp.broadcast_to(rot_low, (rows, LANES))
    cos = cos_ref[rs, :]
    sin = sin_ref[rs, :]

    def proj(c0, width):
        return _dot(xb, w_ref[:, c0:c0 + width])

    def slab(h, i):
        return h[:, i * LANES:(i + 1) * LANES]

    gm = proj(C_GU, 3 * GM_WIDTH)
    u = _gelu(gm[:, :GM_WIDTH])
    v = _gelu(gm[:, GM_WIDTH:2 * GM_WIDTH])
    z = gm[:, 2 * GM_WIDTH:]
    for pair in range(GM_GROUPS // 2):
        sl = slice(pair * LANES, (pair + 1) * LANES)
        vln = _group_layer_norm(v[:, sl], glg_ref[:, sl], glb_ref[:, sl], low).astype(jnp.bfloat16)
        for c in range(rows // GM_CHUNK):
            cs = slice(c * GM_CHUNK, (c + 1) * GM_CHUNK)
            out_rows = slice(rs.start + c * GM_CHUNK, rs.start + (c + 1) * GM_CHUNK)
            s_lo = _dot(gws_ref[2 * pair], vln[cs])
            s_hi = _dot(gws_ref[2 * pair + 1], vln[cs])
            s = jnp.where(_low_half((GM_CHUNK, LANES)), s_lo, s_hi) + gbs_ref[:, sl]
            ygm_ref[out_rows, sl] = (u[cs, sl] * s * _silu(z[cs, sl])).astype(ygm_ref.dtype)

    qscale = np.float32(HEAD_DIM ** -0.5)
    qscale2 = np.float32(HEAD_DIM ** -0.5 * np.log2(np.e))
    ones = jnp.ones((rows, LANES), vs_ref.dtype)
    qh = proj(C_Q, NSA_WIDTH)
    for i in range(NSA_WIDTH // LANES):
        qi = _rope(slab(qh, i), cos, sin, rot_low) * qscale2
        q_ref[rs, i * LANES:(i + 1) * LANES] = qi.astype(q_ref.dtype)
    kv = proj(C_KC, 6 * NSA_KV_WIDTH)
    kc_ref[rs, :] = _rope(slab(kv, 0), cos, sin, rot_low)
    vc_ref[rs, :] = slab(kv, 1)
    ksa_ref[rs, :LANES] = _rope(slab(kv, 2), cos, sin, rot_low).astype(ksa_ref.dtype)
    tok = row0 % seq_len + lax.broadcasted_iota(jnp.int32, (rows, LANES), 0)
    onehot = (tok // SEL_BLOCK) == _lane_iota((rows, LANES))
    ksa_ref[rs, LANES:] = jnp.where(onehot, 1.0, 0.0).astype(ksa_ref.dtype)
    vs_ref[rs, :LANES] = slab(kv, 3).astype(vs_ref.dtype)
    vs_ref[rs, LANES:] = ones
    kw_ref[rs, :] = _rope(slab(kv, 4), cos, sin, rot_low).astype(kw_ref.dtype)
    vw_ref[rs, :LANES] = slab(kv, 5).astype(vw_ref.dtype)
    vw_ref[rs, LANES:] = ones
    nz_ref[rs, :] = _silu(proj(C_NZ, NSA_WIDTH))
    gate_ref[rs, :] = jax.nn.sigmoid(proj(C_GATE, LANES))

    mem = proj(C_MQ, 2 * MEM_WIDTH)
    mq = mem[:, :MEM_WIDTH] * qscale
    mz = mem[:, MEM_WIDTH:]
    for pair in range(MEM_HEADS // 2):
        sl = slice(pair * LANES, (pair + 1) * LANES)
        kp = mk_ref[0, :, sl]
        vp = mv_ref[0, :, sl]
        outs = []
        for keep_low in (True, False):
            qm = jnp.where(low == keep_low, mq[:, sl], 0.0).astype(jnp.bfloat16)
            s = _dot_nt(qm, kp)
            e = jnp.exp(s - jnp.max(s, axis=-1, keepdims=True))
            p = e / jnp.sum(e, axis=-1, keepdims=True)
            outs.append(_dot(p.astype(jnp.bfloat16), vp))
        o = jnp.where(low, outs[0], outs[1])
        ymem_ref[rs, sl] = (o * _silu(mz[:, sl])).astype(ymem_ref.dtype)


def _inproj(x2d, w_cat, cos_t, sin_t, rot_low, gws, gbs, glg, glb, mk, mv, *, batch, seq_len):
    n, d_model = x2d.shape
    rows = PROJ_ROWS
    steps_per_seq = seq_len // rows
    mem_len = mk.shape[0] // batch

    def tok_spec(width):
        return pl.BlockSpec((rows, width), lambda i: (i, 0))

    def const_spec(shape):
        return pl.BlockSpec(shape, lambda i: (0,) * len(shape))

    tab_spec = pl.BlockSpec((rows, LANES), lambda i: (i % steps_per_seq, 0))
    mem_spec = pl.BlockSpec((1, mem_len, MEM_WIDTH), lambda i: (i // steps_per_seq, 0, 0))
    bf16, f32 = jnp.bfloat16, jnp.float32
    outs = [(GM_WIDTH, bf16), (MEM_WIDTH, bf16), (NSA_WIDTH, bf16), (LANES, f32), (LANES, f32),
            (2 * LANES, bf16), (2 * LANES, bf16), (LANES, bf16), (2 * LANES, bf16), (NSA_WIDTH, f32),
            (LANES, f32)]
    return pl.pallas_call(
        functools.partial(_inproj_kernel, seq_len=seq_len),
        grid=(n // rows,),
        in_specs=[tok_spec(d_model), const_spec(w_cat.shape), tab_spec, tab_spec,
                  const_spec(rot_low.shape), const_spec(gws.shape), const_spec(gbs.shape),
                  const_spec(glg.shape), const_spec(glb.shape), mem_spec, mem_spec],
        out_specs=[tok_spec(w) for w, _ in outs],
        out_shape=[jax.ShapeDtypeStruct((n, w), dt) for w, dt in outs],
        compiler_params=pltpu.CompilerParams(dimension_semantics=("arbitrary",),
                                             vmem_limit_bytes=VMEM_LIMIT_BYTES),
        name="in_proj_mixers",
    )(x2d, w_cat, cos_t, sin_t, rot_low, gws, gbs, glg, glb,
      mk.reshape(batch, mem_len, MEM_WIDTH), mv.reshape(batch, mem_len, MEM_WIDTH))


def _compress_kernel(k_ref, v_ref, pos_ref, w1_ref, w2_ref, kcmp_ref, vcmp_ref):
    n_rows = k_ref.shape[1]
    for idx, (src, dst) in enumerate(((k_ref, kcmp_ref), (v_ref, vcmp_ref))):
        xr = src[0]
        top = _dot((xr + pos_ref[idx, 0:1]).astype(jnp.bfloat16), w1_ref[idx, 0])
        bot = _dot((xr + pos_ref[idx, 1:2]).astype(jnp.bfloat16), w1_ref[idx, 1])
        hidden = top + pltpu.roll(bot, n_rows - 1, 0)
        dst[0] = _dot(_gelu(hidden).astype(jnp.bfloat16), w2_ref[idx]).astype(dst.dtype)


def _compress(kc_rows, vc_rows, pos, w1, w2):
    batch, n_rows, width = kc_rows.shape
    row_spec = pl.BlockSpec((1, n_rows, width), lambda b: (b, 0, 0))
    out_spec = pl.BlockSpec((1, n_rows, LANES), lambda b: (b, 0, 0))
    out = jax.ShapeDtypeStruct((batch, n_rows, LANES), jnp.bfloat16)
    return pl.pallas_call(
        _compress_kernel,
        grid=(batch,),
        in_specs=[row_spec, row_spec,
                  pl.BlockSpec(pos.shape, lambda b: (0, 0, 0)),
                  pl.BlockSpec(w1.shape, lambda b: (0, 0, 0, 0)),
                  pl.BlockSpec(w2.shape, lambda b: (0, 0, 0))],
        out_specs=[out_spec, out_spec],
        out_shape=[out, out],
        compiler_params=pltpu.CompilerParams(dimension_semantics=("arbitrary",),
                                             vmem_limit_bytes=VMEM_LIMIT_BYTES),
        name="nsa_compress",
    )(kc_rows, vc_rows, pos, w1, w2)


def _split_bf16(x, parts):
    out = []
    for _ in range(parts):
        hi = x.astype(jnp.bfloat16)
        out.append(hi)
        x = x - hi.astype(jnp.float32)
    return out


def _topk_columns(score):
    row = lax.broadcasted_iota(jnp.int32, score.shape, 0).astype(jnp.float32)
    picked = jnp.zeros(score.shape, jnp.float32)
    for _ in range(SEL_TOPK):
        best = jnp.max(score, axis=0, keepdims=True)
        first = jnp.min(jnp.where(score == best, row, np.float32(score.shape[0])),
                        axis=0, keepdims=True)
        hit = row == first
        picked = jnp.where(hit, 1.0, picked)
        score = jnp.where(hit, REMOVED, score)
    return picked


def _nsa_kernel(q_ref, nz_ref, gate_ref, ksa_ref, vs_ref, kw_ref, vw_ref, kcmp_ref, vcmp_ref,
                ovl_ref, out_ref, qa_ref, s_ref, p_ref, m_ref, acc_ref, oc_ref, ow_ref,
                sa_ref, sb_ref, bias_ref, ps_ref, *, seq_len):
    bi = pl.program_id(1)
    start = bi * Q_BLOCK
    n_sel = seq_len // SEL_BLOCK
    n_cmp = kcmp_ref.shape[1]
    span = WINDOW + Q_BLOCK
    tk = SEL_KEY_TILE
    bf16 = jnp.bfloat16

    def head_rows(r):
        return slice(r * Q_BLOCK, (r + 1) * Q_BLOCK)

    low = _low_half((Q_BLOCK, LANES))
    for i in range(NSA_HPG):
        qi = q_ref[0, :, i * LANES:(i + 1) * LANES]
        zero = jnp.zeros_like(qi)
        qa_ref[head_rows(2 * i), :LANES] = jnp.where(low, qi, zero)
        qa_ref[head_rows(2 * i + 1), :LANES] = jnp.where(low, zero, qi)

    t_q = start + lax.broadcasted_iota(jnp.int32, (Q_BLOCK, 1), 0)
    cmp_cols = slice(0, n_cmp)
    win_cols = slice(n_cmp, n_cmp + span)
    chunk = SOFTMAX_ROWS
    n_chunks = Q_BLOCK // chunk

    c_end = lax.broadcasted_iota(jnp.int32, (1, n_cmp), 1) * CMP_STRIDE + (CMP_BLOCK - 1)
    bias_ref[:, cmp_cols] = jnp.where(c_end <= t_q, 0.0, NEG_INF)
    w0 = pl.multiple_of(jnp.maximum(start - WINDOW, 0), Q_BLOCK)
    kpos = w0 + lax.broadcasted_iota(jnp.int32, (1, span), 1)
    bias_ref[:, win_cols] = jnp.where((kpos <= t_q) & (kpos > t_q - WINDOW), 0.0, NEG_INF)

    s_ref[:, cmp_cols] = _dot_nt(qa_ref[:, :LANES], kcmp_ref[0])
    s_ref[:, win_cols] = _dot_nt(qa_ref[:, :LANES], kw_ref[0, pl.ds(w0, span), :])

    for c in range(n_chunks):
        crow = slice(c * chunk, (c + 1) * chunk)
        t_c = start + c * chunk + lax.broadcasted_iota(jnp.int32, (chunk, 1), 0)
        seen = t_c >= CMP_BLOCK - 1
        for g in range(NSA_KV_GROUPS):
            p_sum = None
            for i in range(NSA_HPG):
                r = 2 * i + g
                rows = slice(r * Q_BLOCK + c * chunk, r * Q_BLOCK + (c + 1) * chunk)
                s = s_ref[rows, cmp_cols] + bias_ref[crow, cmp_cols]
                e = jnp.exp2(s - jnp.max(s, axis=-1, keepdims=True))
                inv = jnp.where(seen, 1.0 / jnp.sum(e, axis=-1, keepdims=True), 0.0)
                p = e * inv
                p_sum = p if p_sum is None else p_sum + p
                p_ref[rows, cmp_cols] = p.astype(bf16)
            hi, lo = _split_bf16(p_sum, 2)
            ps_ref[2 * g * Q_BLOCK + c * chunk:2 * g * Q_BLOCK + (c + 1) * chunk, :] = hi
            ps_ref[(2 * g + 1) * Q_BLOCK + c * chunk:(2 * g + 1) * Q_BLOCK + (c + 1) * chunk, :] = lo
    oc_ref[...] = _dot(p_ref[:, cmp_cols], vcmp_ref[0])

    imp_parts = _dot_nt(ovl_ref[...], ps_ref[...])
    imp = jnp.concatenate(
        [imp_parts[:, 2 * g * Q_BLOCK:(2 * g + 1) * Q_BLOCK]
         + imp_parts[:, (2 * g + 1) * Q_BLOCK:(2 * g + 2) * Q_BLOCK] for g in range(NSA_KV_GROUPS)],
        axis=1)
    blk = lax.broadcasted_iota(jnp.int32, imp.shape, 0)
    t_blk = (start + _lane_iota((1, imp.shape[1])) % Q_BLOCK) // SEL_BLOCK
    valid = blk <= t_blk
    forced = (blk == 0) | (valid & (blk > t_blk - N_LOCAL_SEL))
    score = jnp.where(forced, FORCE_SCORE, jnp.where(valid, imp, -1.0))
    if n_sel < LANES:
        score = jnp.where(blk < n_sel, score, REMOVED)
    picked = _topk_columns(score).astype(bf16)
    eye = (lax.broadcasted_iota(jnp.int32, (Q_BLOCK, Q_BLOCK), 0)
           == lax.broadcasted_iota(jnp.int32, (Q_BLOCK, Q_BLOCK), 1)).astype(bf16)
    for g in range(NSA_KV_GROUPS):
        picked_q = _dot_nt(eye, picked[:, g * Q_BLOCK:(g + 1) * Q_BLOCK])
        sel_bias = ((1.0 - picked_q) * NEG_INF).astype(bf16)
        for i in range(NSA_HPG):
            qa_ref[head_rows(2 * i + g), LANES:] = sel_bias

    for r in range(NSA_HEADS):
        for c in range(n_chunks):
            crow = slice(c * chunk, (c + 1) * chunk)
            rows = slice(r * Q_BLOCK + c * chunk, r * Q_BLOCK + (c + 1) * chunk)
            s = s_ref[rows, win_cols] + bias_ref[crow, win_cols]
            e = jnp.exp2(s - jnp.max(s, axis=-1, keepdims=True))
            p_ref[rows, win_cols] = e.astype(bf16)
    o_w = _dot(p_ref[:, win_cols], vw_ref[0, pl.ds(w0, span), :])
    ow_ref[...] = o_w[:, :LANES] / o_w[:, LANES:]

    m_ref[...] = jnp.full(m_ref.shape, NEG_INF, jnp.float32)
    acc_ref[...] = jnp.zeros(acc_ref.shape, jnp.float32)

    def scores(tile):
        k0 = pl.multiple_of(tile * tk, tk)
        return _dot_nt(qa_ref[...], ksa_ref[0, pl.ds(k0, tk), :])

    def consume(buf_ref, tile, causal):
        k0 = pl.multiple_of(tile * tk, tk)
        s = buf_ref[...]
        if causal:
            kpos = k0 + lax.broadcasted_iota(jnp.int32, (1, tk), 1)
            tile_bias = jnp.where(kpos <= t_q, 0.0, NEG_INF)
            s = s + jnp.concatenate([tile_bias] * NSA_HEADS, axis=0)
        m_prev = m_ref[...]
        m_next = jnp.maximum(m_prev, jnp.max(s, axis=-1, keepdims=True))
        p = jnp.exp2(s - _tile_lanes(m_next, tk // LANES))
        alpha = jnp.exp2(m_prev - m_next)
        acc_ref[...] = (_tile_lanes(alpha, 2) * acc_ref[...]
                        + _dot(p.astype(bf16), vs_ref[0, pl.ds(k0, tk), :]))
        m_ref[...] = m_next

    diag = start // tk
    sa_ref[...] = scores(0)

    def tile_pair(j, carry):
        sb_ref[...] = scores(2 * j + 1)
        consume(sa_ref, 2 * j, False)
        sa_ref[...] = scores(2 * j + 2)
        consume(sb_ref, 2 * j + 1, False)
        return carry

    lax.fori_loop(0, diag // 2, tile_pair, 0)

    @pl.when(diag % 2 == 1)
    def _():
        sb_ref[...] = scores(diag)
        consume(sa_ref, diag - 1, False)
        consume(sb_ref, diag, True)

    @pl.when(diag % 2 == 0)
    def _():
        consume(sa_ref, diag, True)

    gates = gate_ref[0]
    for i in range(NSA_HPG):
        halves = []
        for g in range(NSA_KV_GROUPS):
            head = i + NSA_HPG * g
            rs = head_rows(2 * i + g)
            o_s = acc_ref[rs, :LANES] / acc_ref[rs, LANES:]
            halves.append(oc_ref[rs] * gates[:, 3 * head:3 * head + 1]
                          + o_s * gates[:, 3 * head + 1:3 * head + 2]
                          + ow_ref[rs] * gates[:, 3 * head + 2:3 * head + 3])
        sl = slice(i * LANES, (i + 1) * LANES)
        out_ref[0, :, sl] = (jnp.where(low, halves[0], halves[1]) * nz_ref[0, :, sl]).astype(out_ref.dtype)


def _nsa(q, nz, gates, ksa, vs, kw, vw, kcmp, vcmp, ovl):
    batch, seq_len, _ = q.shape

    def q_spec(width):
        return pl.BlockSpec((1, Q_BLOCK, width), lambda b, i: (b, i, 0))

    def seq_spec(arr):
        return pl.BlockSpec((1,) + arr.shape[1:], lambda b, i: (b, 0, 0))

    rows = NSA_HEADS * Q_BLOCK
    cols = kcmp.shape[1] + WINDOW + Q_BLOCK
    return pl.pallas_call(
        functools.partial(_nsa_kernel, seq_len=seq_len),
        grid=(batch, seq_len // Q_BLOCK),
        in_specs=[q_spec(NSA_WIDTH), q_spec(NSA_WIDTH), q_spec(LANES),
                  seq_spec(ksa), seq_spec(vs), seq_spec(kw), seq_spec(vw),
                  seq_spec(kcmp), seq_spec(vcmp),
                  pl.BlockSpec(ovl.shape, lambda b, i: (0, 0))],
        out_specs=q_spec(NSA_WIDTH),
        out_shape=jax.ShapeDtypeStruct((batch, seq_len, NSA_WIDTH), jnp.bfloat16),
        scratch_shapes=[pltpu.VMEM((rows, 2 * LANES), jnp.bfloat16),
                        pltpu.VMEM((rows, cols), jnp.float32),
                        pltpu.VMEM((rows, cols), jnp.bfloat16),
                        pltpu.VMEM((rows, LANES), jnp.float32),
                        pltpu.VMEM((rows, 2 * LANES), jnp.float32),
                        pltpu.VMEM((rows, LANES), jnp.float32),
                        pltpu.VMEM((rows, LANES), jnp.float32),
                        pltpu.VMEM((rows, SEL_KEY_TILE), jnp.float32),
                        pltpu.VMEM((rows, SEL_KEY_TILE), jnp.float32),
                        pltpu.VMEM((Q_BLOCK, cols), jnp.float32),
                        pltpu.VMEM((2 * NSA_KV_GROUPS * Q_BLOCK, kcmp.shape[1]), jnp.bfloat16)],
        compiler_params=pltpu.CompilerParams(dimension_semantics=("arbitrary", "arbitrary"),
                                             vmem_limit_bytes=VMEM_LIMIT_BYTES),
        name="nsa_attention",
    )(q, nz, gates, ksa, vs, kw, vw, kcmp, vcmp, ovl)


def _outproj_kernel(x_ref, ygm_ref, ynsa_ref, ymem_ref, w_ref, g_ref, b_ref, o_ref, *, alpha):
    for r0 in range(0, x_ref.shape[0], PROJ_SUB_ROWS):
        rs = slice(r0, r0 + PROJ_SUB_ROWS)
        y = (_dot(ygm_ref[rs, :], w_ref[:GM_WIDTH])
             + _dot(ynsa_ref[rs, :], w_ref[GM_WIDTH:GM_WIDTH + NSA_WIDTH])
             + _dot(ymem_ref[rs, :], w_ref[GM_WIDTH + NSA_WIDTH:]))
        r = alpha * x_ref[rs, :] + y
        mu = jnp.mean(r, axis=-1, keepdims=True)
        d = r - mu
        var = jnp.mean(d * d, axis=-1, keepdims=True)
        o_ref[rs, :] = d * lax.rsqrt(var + LN_EPS) * g_ref[...] + b_ref[...]


def _outproj(x2d, ygm, ynsa, ymem, w_out, ln_g, ln_b, *, alpha):
    n, d_model = x2d.shape
    rows = PROJ_ROWS

    def tok_spec(width):
        return pl.BlockSpec((rows, width), lambda i: (i, 0))

    def const_spec(shape):
        return pl.BlockSpec(shape, lambda i: (0, 0))

    return pl.pallas_call(
        functools.partial(_outproj_kernel, alpha=alpha),
        grid=(n // rows,),
        in_specs=[tok_spec(d_model), tok_spec(GM_WIDTH), tok_spec(NSA_WIDTH), tok_spec(MEM_WIDTH),
                  const_spec(w_out.shape), const_spec(ln_g.shape), const_spec(ln_b.shape)],
        out_specs=tok_spec(d_model),
        out_shape=jax.ShapeDtypeStruct((n, d_model), jnp.float32),
        compiler_params=pltpu.CompilerParams(dimension_semantics=("arbitrary",),
                                             vmem_limit_bytes=VMEM_LIMIT_BYTES),
        name="out_proj_layernorm",
    )(x2d, ygm, ynsa, ymem, w_out, ln_g, ln_b)


def _pair_heads(w, axis):
    shape = w.shape
    w = w.reshape(shape[:axis] + (NSA_HEADS, HEAD_DIM) + shape[axis + 1:])
    w = jnp.take(w, np.asarray(PAIR_HEAD_ORDER), axis=axis)
    return w.reshape(shape)


def _permute_w_in(w):
    o_gate = 2048
    o_nz = o_gate + GATE_COLS
    o_mq = o_nz + NSA_WIDTH
    pieces = [w[:, :768], _pair_heads(w[:, 768:1280], 1), w[:, 1280:2048],
              _pair_heads(w[:, o_nz:o_mq], 1), w[:, o_mq:],
              w[:, o_gate:o_nz], jnp.zeros((w.shape[0], LANES - GATE_COLS), w.dtype)]
    return jnp.concatenate(pieces, axis=1).astype(jnp.bfloat16)


def _rope_tables(seq_len):
    half = HEAD_DIM // 2
    inv_freq = ROPE_THETA ** (-jnp.arange(half, dtype=jnp.float32) * 2.0 / HEAD_DIM)
    ang = jnp.arange(seq_len).astype(jnp.float32)[:, None] * inv_freq[None, :]
    cos, sin = jnp.cos(ang), jnp.sin(ang)
    reps = LANES // HEAD_DIM
    cos_t = jnp.tile(jnp.concatenate([cos, cos], axis=1), (1, reps))
    sin_t = jnp.tile(jnp.concatenate([-sin, sin], axis=1), (1, reps))
    rot_low = ((np.arange(LANES) % HEAD_DIM) < half).astype(np.float32)[None, :]
    return cos_t, sin_t, jnp.asarray(rot_low)


def _compress_weights(pos_k, w1_k, w2_k, pos_v, w1_v, w2_v):
    half = CMP_BLOCK // 2
    eye = jnp.eye(NSA_KV_GROUPS, dtype=w1_k.dtype)

    def expand_w1(w1):
        w = w1.reshape(2, half, HEAD_DIM, CMP_HIDDEN)
        w = jnp.einsum('aldh,gk->algdkh', w, eye)
        return w.reshape(2, half * NSA_KV_WIDTH, NSA_KV_GROUPS * CMP_HIDDEN)

    def expand_w2(w2):
        w = jnp.einsum('hd,gk->ghkd', w2, eye)
        return w.reshape(NSA_KV_GROUPS * CMP_HIDDEN, NSA_KV_WIDTH)

    def expand_pos(pos):
        p = pos.reshape(2, half, 1, HEAD_DIM)
        p = jnp.broadcast_to(p, (2, half, NSA_KV_GROUPS, HEAD_DIM))
        return p.reshape(2, half * NSA_KV_WIDTH)

    pos = jnp.stack([expand_pos(pos_k), expand_pos(pos_v)])
    w1 = jnp.stack([expand_w1(w1_k), expand_w1(w1_v)]).astype(jnp.bfloat16)
    w2 = jnp.stack([expand_w2(w2_k), expand_w2(w2_v)]).astype(jnp.bfloat16)
    return pos, w1, w2


def _overlap_matrix(n_rows, n_sel):
    c_start = np.arange(n_rows) * CMP_STRIDE
    s_start = np.arange(LANES) * SEL_BLOCK
    ovl = ((c_start[:, None] < s_start[None, :] + SEL_BLOCK)
           & (c_start[:, None] + CMP_BLOCK > s_start[None, :])
           & (np.arange(LANES)[None, :] < n_sel))
    return jnp.asarray(ovl.T.astype(np.float32), dtype=jnp.bfloat16)


def kernel(x, mem, w_in, gm_ln_g, gm_ln_b, gm_ws, gm_bs, cmp_pos_k, cmp_k_w1, cmp_k_w2,
           cmp_pos_v, cmp_v_w1, cmp_v_w2, w_mem_kv, w_out, ln_g, ln_b):
    batch, seq_len, d_model = x.shape
    depth = w_in.shape[0]
    assert seq_len % SEL_KEY_TILE == 0 and seq_len >= WINDOW + Q_BLOCK
    assert SEL_TOPK <= seq_len // SEL_BLOCK <= LANES
    alpha = (2.0 * depth) ** 0.25
    n_tok = batch * seq_len
    n_rows = seq_len // CMP_STRIDE

    cos_t, sin_t, rot_low = _rope_tables(seq_len)
    ovl = _overlap_matrix(n_rows, seq_len // SEL_BLOCK)
    tril = jnp.tril(jnp.ones((GM_CHUNK, GM_CHUNK), gm_ws.dtype))
    mk_all, mv_all = _memkv(mem.reshape(batch * mem.shape[1], d_model), w_mem_kv.astype(jnp.bfloat16))

    h = x.reshape(n_tok, d_model)
    for l in range(depth):
        w_cat = _permute_w_in(w_in[l])
        gws = (gm_ws[l] * tril[None]).astype(jnp.bfloat16)
        gbs = jnp.repeat(gm_bs[l].T, HEAD_DIM, axis=1)
        glg = gm_ln_g[l].reshape(1, GM_WIDTH)
        glb = gm_ln_b[l].reshape(1, GM_WIDTH)
        (ygm, ymem, q, kc, vc, ksa, vs, kw, vw, nz, gates) = _inproj(
            h, w_cat, cos_t, sin_t, rot_low, gws, gbs, glg, glb, mk_all[l], mv_all[l],
            batch=batch, seq_len=seq_len)

        pos, w1, w2 = _compress_weights(cmp_pos_k[l], cmp_k_w1[l], cmp_k_w2[l],
                                        cmp_pos_v[l], cmp_v_w1[l], cmp_v_w2[l])
        row_shape = (batch, n_rows, CMP_STRIDE * NSA_KV_WIDTH)
        kcmp, vcmp = _compress(kc.reshape(row_shape), vc.reshape(row_shape), pos, w1, w2)

        def per_seq(a):
            return a.reshape(batch, seq_len, a.shape[-1])

        ynsa = _nsa(per_seq(q), per_seq(nz), per_seq(gates), per_seq(ksa), per_seq(vs),
                    per_seq(kw), per_seq(vw), kcmp, vcmp, ovl)

        w_out_p = jnp.concatenate([w_out[l, :GM_WIDTH],
                                   _pair_heads(w_out[l, GM_WIDTH:GM_WIDTH + NSA_WIDTH], 0),
                                   w_out[l, GM_WIDTH + NSA_WIDTH:]], axis=0).astype(jnp.bfloat16)
        h = _outproj(h, ygm, ynsa.reshape(n_tok, NSA_WIDTH), ymem, w_out_p,
                     ln_g[l].reshape(1, d_model), ln_b[l].reshape(1, d_model), alpha=alpha)
    return h.reshape(batch, seq_len, d_model)
```

```python
import functools

import numpy as np
import jax
import jax.numpy as jnp
from jax import lax
from jax.experimental import pallas as pl
from jax.experimental.pallas import tpu as pltpu

HEAD_DIM = 64
GM_GROUPS = 4
GM_WIDTH = GM_GROUPS * HEAD_DIM
GM_CHUNK = 128
NSA_HEADS = 8
NSA_KV_GROUPS = 2
NSA_HPG = NSA_HEADS // NSA_KV_GROUPS
NSA_WIDTH = NSA_HEADS * HEAD_DIM
NSA_KV_WIDTH = NSA_KV_GROUPS * HEAD_DIM
CMP_BLOCK = 32
CMP_STRIDE = 16
CMP_HIDDEN = 128
SEL_BLOCK = 64
SEL_TOPK = 16
N_LOCAL_SEL = 2
WINDOW = 512
Q_BLOCK = 128
MEM_HEADS = 4
MEM_WIDTH = MEM_HEADS * HEAD_DIM
ROPE_THETA = 10000.0
LN_EPS = 1e-5
NEG_INF = -1e30
FORCE_SCORE = 1e4
GATE_COLS = NSA_HEADS * 3

LANES = 128
VMEM_LIMIT_BYTES = 56 * 1024 * 1024

PROJ_ROWS = 512
PROJ_SUB_ROWS = 256
SEL_KEY_TILE = 512
SOFTMAX_ROWS = 32
REMOVED = -3.0e38

PAIR_HEAD_ORDER = tuple(h for i in range(NSA_HPG) for h in (i, i + NSA_HPG))

C_GU, C_GV, C_GZ = 0, 256, 512
C_Q = 768
C_KC, C_VC, C_KS, C_VS, C_KW, C_VW = 1280, 1408, 1536, 1664, 1792, 1920
C_NZ = 2048
C_MQ, C_MZ = 2560, 2816
C_GATE = 3072
N_COLS = 3200


def _dot(a, b):
    return jnp.dot(a, b, preferred_element_type=jnp.float32)


def _dot_nt(a, b):
    return lax.dot_general(a, b, (((1,), (1,)), ((), ())), preferred_element_type=jnp.float32)


def _gelu(x):
    return 0.5 * x * (1.0 + lax.erf(x * np.float32(np.sqrt(0.5))))


def _silu(x):
    return x * jax.nn.sigmoid(x)


def _lane_iota(shape):
    return lax.broadcasted_iota(jnp.int32, shape, len(shape) - 1)


def _low_half(shape):
    return (_lane_iota(shape) % LANES) < HEAD_DIM


def _tile_lanes(x, reps):
    return jnp.concatenate([x] * reps, axis=-1) if reps > 1 else x


def _memkv_kernel(mem_ref, w_ref, k_ref, v_ref):
    kv = _dot(mem_ref[...].astype(jnp.bfloat16), w_ref[0])
    k_ref[0] = kv[:, :MEM_WIDTH].astype(jnp.bfloat16)
    v_ref[0] = kv[:, MEM_WIDTH:].astype(jnp.bfloat16)


def _memkv(mem2d, w_mem_kv_bf16):
    depth = w_mem_kv_bf16.shape[0]
    rows, d_model = mem2d.shape
    out = jax.ShapeDtypeStruct((depth, rows, MEM_WIDTH), jnp.bfloat16)
    return pl.pallas_call(
        _memkv_kernel,
        grid=(depth,),
        in_specs=[pl.BlockSpec((rows, d_model), lambda l: (0, 0)),
                  pl.BlockSpec((1, d_model, 2 * MEM_WIDTH), lambda l: (l, 0, 0))],
        out_specs=[pl.BlockSpec((1, rows, MEM_WIDTH), lambda l: (l, 0, 0)),
                   pl.BlockSpec((1, rows, MEM_WIDTH), lambda l: (l, 0, 0))],
        out_shape=[out, out],
        name="mem_kv_proj",
    )(mem2d, w_mem_kv_bf16)


def _rope(x, cos, sin_signed, low):
    width = x.shape[-1]
    swapped = jnp.where(low, pltpu.roll(x, width - HEAD_DIM // 2, 1), pltpu.roll(x, HEAD_DIM // 2, 1))
    return x * cos + swapped * sin_signed


def _group_layer_norm(v, g, b, low):
    inv = np.float32(1.0 / HEAD_DIM)
    s_lo = jnp.sum(jnp.where(low, v, 0.0), axis=-1, keepdims=True)
    s_hi = jnp.sum(jnp.where(low, 0.0, v), axis=-1, keepdims=True)
    mu = jnp.where(low, s_lo, s_hi) * inv
    d = v - mu
    d2 = d * d
    q_lo = jnp.sum(jnp.where(low, d2, 0.0), axis=-1, keepdims=True)
    q_hi = jnp.sum(jnp.where(low, 0.0, d2), axis=-1, keepdims=True)
    var = jnp.where(low, q_lo, q_hi) * inv
    return d * lax.rsqrt(var + LN_EPS) * g + b


def _inproj_kernel(x_ref, w_ref, cos_ref, sin_ref, rot_low_ref, gws_ref, gbs_ref, glg_ref, glb_ref,
                   mk_ref, mv_ref,
                   ygm_ref, ymem_ref, q_ref, kc_ref, vc_ref, ksa_ref, vs_ref, kw_ref, vw_ref,
                   nz_ref, gate_ref, stage_ref, *, seq_len):
    for r0 in range(0, x_ref.shape[0], PROJ_SUB_ROWS):
        _inproj_rows(slice(r0, r0 + PROJ_SUB_ROWS), pl.program_id(0) * x_ref.shape[0] + r0,
                     x_ref, w_ref, cos_ref, sin_ref, rot_low_ref, gws_ref, gbs_ref, glg_ref, glb_ref,
                     mk_ref, mv_ref, ygm_ref, ymem_ref, q_ref, kc_ref, vc_ref, ksa_ref, vs_ref,
                     kw_ref, vw_ref, nz_ref, gate_ref, stage_ref, seq_len=seq_len)


def _inproj_rows(rs, row0, x_ref, w_ref, cos_ref, sin_ref, rot_low_ref, gws_ref, gbs_ref, glg_ref,
                 glb_ref, mk_ref, mv_ref, ygm_ref, ymem_ref, q_ref, kc_ref, vc_ref, ksa_ref, vs_ref,
                 kw_ref, vw_ref, nz_ref, gate_ref, stage_ref, *, seq_len):
    rows = rs.stop - rs.start
    xb = x_ref[rs, :].astype(jnp.bfloat16)
    low = _low_half((rows, LANES))
    rot_low = rot_low_ref[...] > 0.5
    rot_low = jnp.broadcast_to(rot_low, (rows, LANES))
    cos = cos_ref[rs, :]
    sin = sin_ref[rs, :]

    def proj(c0, width):
        return _dot(xb, w_ref[:, c0:c0 + width])

    def slab(h, i):
        return h[:, i * LANES:(i + 1) * LANES]

    gm = proj(C_GU, 3 * GM_WIDTH)
    u = _gelu(gm[:, :GM_WIDTH])
    v = _gelu(gm[:, GM_WIDTH:2 * GM_WIDTH])
    z = gm[:, 2 * GM_WIDTH:]
    for pair in range(GM_GROUPS // 2):
        sl = slice(pair * LANES, (pair + 1) * LANES)
        vln = _group_layer_norm(v[:, sl], glg_ref[:, sl], glb_ref[:, sl], low).astype(jnp.bfloat16)
        for c in range(rows // GM_CHUNK):
            cs = slice(c * GM_CHUNK, (c + 1) * GM_CHUNK)
            out_rows = slice(rs.start + c * GM_CHUNK, rs.start + (c + 1) * GM_CHUNK)
            s_lo = _dot(gws_ref[2 * pair], vln[cs])
            s_hi = _dot(gws_ref[2 * pair + 1], vln[cs])
            s = jnp.where(_low_half((GM_CHUNK, LANES)), s_lo, s_hi) + gbs_ref[:, sl]
            ygm_ref[out_rows, sl] = (u[cs, sl] * s * _silu(z[cs, sl])).astype(ygm_ref.dtype)

    qscale = np.float32(HEAD_DIM ** -0.5)
    qscale2 = np.float32(HEAD_DIM ** -0.5 * np.log2(np.e))
    ones = jnp.ones((rows, LANES), vs_ref.dtype)
    qh = proj(C_Q, NSA_WIDTH)
    for i in range(NSA_WIDTH // LANES):
        qi = _rope(slab(qh, i), cos, sin, rot_low) * qscale2
        q_ref[rs, i * LANES:(i + 1) * LANES] = qi.astype(q_ref.dtype)
    kv = proj(C_KC, 6 * NSA_KV_WIDTH)
    stage_ref[0, rs, :] = _rope(slab(kv, 0), cos, sin, rot_low)
    stage_ref[1, rs, :] = slab(kv, 1)
    out_rows = slice(rs.start // CMP_STRIDE, rs.stop // CMP_STRIDE)
    for j, dst in enumerate((kc_ref, vc_ref)):
        for l in range(CMP_STRIDE):
            token_l = stage_ref[j, pl.ds(rs.start + l, rows // CMP_STRIDE, stride=CMP_STRIDE), :]
            dst[out_rows, l * LANES:(l + 1) * LANES] = token_l
    ksa_ref[rs, :LANES] = _rope(slab(kv, 2), cos, sin, rot_low).astype(ksa_ref.dtype)
    tok = row0 % seq_len + lax.broadcasted_iota(jnp.int32, (rows, LANES), 0)
    onehot = (tok // SEL_BLOCK) == _lane_iota((rows, LANES))
    ksa_ref[rs, LANES:] = jnp.where(onehot, 1.0, 0.0).astype(ksa_ref.dtype)
    vs_ref[rs, :LANES] = slab(kv, 3).astype(vs_ref.dtype)
    vs_ref[rs, LANES:] = ones
    kw_ref[rs, :] = _rope(slab(kv, 4), cos, sin, rot_low).astype(kw_ref.dtype)
    vw_ref[rs, :LANES] = slab(kv, 5).astype(vw_ref.dtype)
    vw_ref[rs, LANES:] = ones
    nz_ref[rs, :] = _silu(proj(C_NZ, NSA_WIDTH))
    gate_ref[rs, :] = jax.nn.sigmoid(proj(C_GATE, LANES))

    mem = proj(C_MQ, 2 * MEM_WIDTH)
    mq = mem[:, :MEM_WIDTH] * qscale
    mz = mem[:, MEM_WIDTH:]
    for pair in range(MEM_HEADS // 2):
        sl = slice(pair * LANES, (pair + 1) * LANES)
        kp = mk_ref[0, :, sl]
        vp = mv_ref[0, :, sl]
        outs = []
        for keep_low in (True, False):
            qm = jnp.where(low == keep_low, mq[:, sl], 0.0).astype(jnp.bfloat16)
            s = _dot_nt(qm, kp)
            e = jnp.exp(s - jnp.max(s, axis=-1, keepdims=True))
            p = e / jnp.sum(e, axis=-1, keepdims=True)
            outs.append(_dot(p.astype(jnp.bfloat16), vp))
        o = jnp.where(low, outs[0], outs[1])
        ymem_ref[rs, sl] = (o * _silu(mz[:, sl])).astype(ymem_ref.dtype)


def _inproj(x2d, w_cat, cos_t, sin_t, rot_low, gws, gbs, glg, glb, mk, mv, *, batch, seq_len):
    n, d_model = x2d.shape
    rows = PROJ_ROWS
    steps_per_seq = seq_len // rows
    mem_len = mk.shape[0] // batch

    def tok_spec(width):
        return pl.BlockSpec((rows, width), lambda i: (i, 0))

    def const_spec(shape):
        return pl.BlockSpec(shape, lambda i: (0,) * len(shape))

    tab_spec = pl.BlockSpec((rows, LANES), lambda i: (i % steps_per_seq, 0))
    mem_spec = pl.BlockSpec((1, mem_len, MEM_WIDTH), lambda i: (i // steps_per_seq, 0, 0))
    bf16, f32 = jnp.bfloat16, jnp.float32
    outs = [(1, GM_WIDTH, bf16), (1, MEM_WIDTH, bf16), (1, NSA_WIDTH, bf16),
            (CMP_STRIDE, CMP_STRIDE * LANES, f32), (CMP_STRIDE, CMP_STRIDE * LANES, f32),
            (1, 2 * LANES, bf16), (1, 2 * LANES, bf16), (1, LANES, bf16), (1, 2 * LANES, bf16),
            (1, NSA_WIDTH, f32), (1, LANES, f32)]
    return pl.pallas_call(
        functools.partial(_inproj_kernel, seq_len=seq_len),
        grid=(n // rows,),
        in_specs=[tok_spec(d_model), const_spec(w_cat.shape), tab_spec, tab_spec,
                  const_spec(rot_low.shape), const_spec(gws.shape), const_spec(gbs.shape),
                  const_spec(glg.shape), const_spec(glb.shape), mem_spec, mem_spec],
        out_specs=[pl.BlockSpec((rows // d, w), lambda i: (i, 0)) for d, w, _ in outs],
        out_shape=[jax.ShapeDtypeStruct((n // d, w), dt) for d, w, dt in outs],
        scratch_shapes=[pltpu.VMEM((2, rows, LANES), f32)],
        compiler_params=pltpu.CompilerParams(dimension_semantics=("arbitrary",),
                                             vmem_limit_bytes=VMEM_LIMIT_BYTES),
        name="in_proj_mixers",
    )(x2d, w_cat, cos_t, sin_t, rot_low, gws, gbs, glg, glb,
      mk.reshape(batch, mem_len, MEM_WIDTH), mv.reshape(batch, mem_len, MEM_WIDTH))


def _compress_kernel(k_ref, v_ref, pos_ref, w1_ref, w2_ref, kcmp_ref, vcmp_ref):
    n_rows = k_ref.shape[1]
    for idx, (src, dst) in enumerate(((k_ref, kcmp_ref), (v_ref, vcmp_ref))):
        xr = src[0]
        top = _dot((xr + pos_ref[idx, 0:1]).astype(jnp.bfloat16), w1_ref[idx, 0])
        bot = _dot((xr + pos_ref[idx, 1:2]).astype(jnp.bfloat16), w1_ref[idx, 1])
        hidden = top + pltpu.roll(bot, n_rows - 1, 0)
        act = jax.nn.gelu(hidden, approximate=True)
        dst[0, :, :LANES] = _dot(act.astype(jnp.bfloat16), w2_ref[idx]).astype(dst.dtype)
    vcmp_ref[0, :, LANES:] = jnp.ones((n_rows, LANES), vcmp_ref.dtype)


def _compress(kc_rows, vc_rows, pos, w1, w2):
    batch, n_rows, width = kc_rows.shape
    row_spec = pl.BlockSpec((1, n_rows, width), lambda b: (b, 0, 0))
    def out_spec(width):
        return pl.BlockSpec((1, n_rows, width), lambda b: (b, 0, 0))

    def out(width):
        return jax.ShapeDtypeStruct((batch, n_rows, width), jnp.bfloat16)

    return pl.pallas_call(
        _compress_kernel,
        grid=(batch,),
        in_specs=[row_spec, row_spec,
                  pl.BlockSpec(pos.shape, lambda b: (0, 0, 0)),
                  pl.BlockSpec(w1.shape, lambda b: (0, 0, 0, 0)),
                  pl.BlockSpec(w2.shape, lambda b: (0, 0, 0))],
        out_specs=[out_spec(LANES), out_spec(2 * LANES)],
        out_shape=[out(LANES), out(2 * LANES)],
        compiler_params=pltpu.CompilerParams(dimension_semantics=("arbitrary",),
                                             vmem_limit_bytes=VMEM_LIMIT_BYTES),
        name="nsa_compress",
    )(kc_rows, vc_rows, pos, w1, w2)


def _split_bf16(x, parts):
    out = []
    for _ in range(parts):
        hi = x.astype(jnp.bfloat16)
        out.append(hi)
        x = x - hi.astype(jnp.float32)
    return out


def _topk_columns(score):
    row = lax.broadcasted_iota(jnp.int32, score.shape, 0).astype(jnp.float32)
    picked = jnp.zeros(score.shape, jnp.float32)
    for _ in range(SEL_TOPK):
        best = jnp.max(score, axis=0, keepdims=True)
        first = jnp.min(jnp.where(score == best, row, np.float32(score.shape[0])),
                        axis=0, keepdims=True)
        hit = row == first
        picked = jnp.where(hit, 1.0, picked)
        score = jnp.where(hit, REMOVED, score)
    return picked


def _nsa_kernel(q_ref, qnext_ref, nz_ref, gate_ref, ksa_ref, vs_ref, kw_ref, vw_ref, kcmp_ref,
                vcmp_ref, ovl_ref, out_ref, qa_ref, sc_ref, sw_ref, pc_ref, pw_ref, bc_ref, bw_ref,
                m_ref, acc_ref, oc_ref, ow_ref, sa_ref, sb_ref, mc_ref, mw_ref, qn_ref, selb_ref,
                ocn_ref, *, seq_len):
    bi = pl.program_id(1)
    start = bi * Q_BLOCK
    n_sel = seq_len // SEL_BLOCK
    n_cmp = kcmp_ref.shape[1]
    span = WINDOW + Q_BLOCK
    tk = SEL_KEY_TILE
    bf16 = jnp.bfloat16

    def head_rows(r):
        return slice(r * Q_BLOCK, (r + 1) * Q_BLOCK)

    low = _low_half((Q_BLOCK, LANES))
    chunk = SOFTMAX_ROWS
    n_chunks = Q_BLOCK // chunk

    def stack_queries(src_ref, dst_ref):
        for i in range(NSA_HPG):
            qi = src_ref[0, :, i * LANES:(i + 1) * LANES]
            zero = jnp.zeros_like(qi)
            dst_ref[head_rows(2 * i), :LANES] = jnp.where(low, qi, zero)
            dst_ref[head_rows(2 * i + 1), :LANES] = jnp.where(low, zero, qi)

    def compressed_scores(qs_ref, blk_start):
        t_b = blk_start + lax.broadcasted_iota(jnp.int32, (Q_BLOCK, 1), 0)
        c_end = lax.broadcasted_iota(jnp.int32, (1, n_cmp), 1) * CMP_STRIDE + (CMP_BLOCK - 1)
        bc_ref[...] = jnp.where(c_end <= t_b, 0.0, NEG_INF)
        sc_ref[...] = _dot_nt(qs_ref[:, :LANES], kcmp_ref[0])

    def masked_exp(s_ref, b_ref, m_ref_, p_ref):
        width_tiles = s_ref.shape[1] // LANES
        for r in range(NSA_HEADS):
            for c in range(n_chunks):
                crow = slice(c * chunk, (c + 1) * chunk)
                rows = slice(r * Q_BLOCK + c * chunk, r * Q_BLOCK + (c + 1) * chunk)
                row_max = jnp.max(s_ref[rows, :] + b_ref[crow, :], axis=-1, keepdims=True)
                m_ref_[rows, :] = jnp.broadcast_to(row_max, (chunk, LANES))
        for r in range(NSA_HEADS):
            for c in range(n_chunks):
                crow = slice(c * chunk, (c + 1) * chunk)
                rows = slice(r * Q_BLOCK + c * chunk, r * Q_BLOCK + (c + 1) * chunk)
                s = s_ref[rows, :] + b_ref[crow, :]
                p_ref[rows, :] = jnp.exp2(s - _tile_lanes(m_ref_[rows, :], width_tiles)).astype(bf16)

    def compressed_out(blk_start):
        t_col = blk_start + lax.broadcasted_iota(jnp.int32, (Q_BLOCK, 1), 0)
        seen_col = jnp.concatenate([t_col >= CMP_BLOCK - 1] * NSA_HEADS, axis=0)
        o_c = _dot(pc_ref[...], vcmp_ref[0])
        ocn_ref[...] = jnp.where(seen_col, o_c[:, :LANES] / o_c[:, LANES:], 0.0)

    def candidate_scores(blk_start):
        parts = _dot_nt(ovl_ref[...], pc_ref[...])
        t_lane = blk_start + _lane_iota((1, parts.shape[1])) % Q_BLOCK
        inv = jnp.where(t_lane >= CMP_BLOCK - 1, 1.0 / parts[LANES:LANES + 1, :], 0.0)
        weighted = parts[:LANES, :] * inv
        imp = jnp.concatenate(
            [sum(weighted[:, (2 * i + g) * Q_BLOCK:(2 * i + g + 1) * Q_BLOCK] for i in range(NSA_HPG))
             for g in range(NSA_KV_GROUPS)], axis=1)
        blk = lax.broadcasted_iota(jnp.int32, imp.shape, 0)
        t_blk = (blk_start + _lane_iota((1, imp.shape[1])) % Q_BLOCK) // SEL_BLOCK
        valid = blk <= t_blk
        forced = (blk == 0) | (valid & (blk > t_blk - N_LOCAL_SEL))
        score = jnp.where(forced, FORCE_SCORE, jnp.where(valid, imp, -1.0))
        if n_sel < LANES:
            score = jnp.where(blk < n_sel, score, REMOVED)
        return score

    def select_blocks(score):
        picked = _topk_columns(score).astype(bf16)
        eye = (lax.broadcasted_iota(jnp.int32, (Q_BLOCK, Q_BLOCK), 0)
               == lax.broadcasted_iota(jnp.int32, (Q_BLOCK, Q_BLOCK), 1)).astype(bf16)
        for g in range(NSA_KV_GROUPS):
            picked_q = _dot_nt(eye, picked[:, g * Q_BLOCK:(g + 1) * Q_BLOCK])
            selb_ref[g * Q_BLOCK:(g + 1) * Q_BLOCK, :] = ((1.0 - picked_q) * NEG_INF).astype(bf16)

    t_q = start + lax.broadcasted_iota(jnp.int32, (Q_BLOCK, 1), 0)
    w0 = pl.multiple_of(jnp.maximum(start - WINDOW, 0), Q_BLOCK)

    def window_scores():
        kpos = w0 + lax.broadcasted_iota(jnp.int32, (1, span), 1)
        bw_ref[...] = jnp.where((kpos <= t_q) & (kpos > t_q - WINDOW), 0.0, NEG_INF)
        sw_ref[...] = _dot_nt(qa_ref[:, :LANES], kw_ref[0, pl.ds(w0, span), :])

    def window_out():
        o_w = _dot(pw_ref[...], vw_ref[0, pl.ds(w0, span), :])
        ow_ref[...] = o_w[:, :LANES] / o_w[:, LANES:]

    def scores(tile):
        k0 = pl.multiple_of(tile * tk, tk)
        return _dot_nt(qa_ref[...], ksa_ref[0, pl.ds(k0, tk), :])

    @pl.when(bi == 0)
    def _():
        stack_queries(q_ref, qn_ref)
        compressed_scores(qn_ref, start)
        masked_exp(sc_ref, bc_ref, mc_ref, pc_ref)
        compressed_out(start)
        select_blocks(candidate_scores(start))

    stack_queries(q_ref, qa_ref)
    for r in range(NSA_HEADS):
        g = r % NSA_KV_GROUPS
        qa_ref[head_rows(r), LANES:] = selb_ref[g * Q_BLOCK:(g + 1) * Q_BLOCK, :]
    oc_ref[...] = ocn_ref[...]
    stack_queries(qnext_ref, qn_ref)

    nxt = start + Q_BLOCK
    compressed_scores(qn_ref, nxt)
    window_scores()
    sa_ref[...] = scores(0)
    masked_exp(sc_ref, bc_ref, mc_ref, pc_ref)
    next_score = candidate_scores(nxt)
    masked_exp(sw_ref, bw_ref, mw_ref, pw_ref)
    compressed_out(nxt)
    window_out()
    select_blocks(next_score)

    m_ref[...] = jnp.full(m_ref.shape, NEG_INF, jnp.float32)
    acc_ref[...] = jnp.zeros(acc_ref.shape, jnp.float32)

    def consume(buf_ref, tile, causal):
        k0 = pl.multiple_of(tile * tk, tk)
        s = buf_ref[...]
        if causal:
            kpos = k0 + lax.broadcasted_iota(jnp.int32, (1, tk), 1)
            tile_bias = jnp.where(kpos <= t_q, 0.0, NEG_INF)
            s = s + jnp.concatenate([tile_bias] * NSA_HEADS, axis=0)
        m_prev = m_ref[...]
        m_next = jnp.maximum(m_prev, jnp.max(s, axis=-1, keepdims=True))
        p = jnp.exp2(s - _tile_lanes(m_next, tk // LANES))
        alpha = jnp.exp2(m_prev - m_next)
        acc_ref[...] = (_tile_lanes(alpha, 2) * acc_ref[...]
                        + _dot(p.astype(bf16), vs_ref[0, pl.ds(k0, tk), :]))
        m_ref[...] = m_next

    diag = start // tk

    def tile_pair(j, carry):
        sb_ref[...] = scores(2 * j + 1)
        consume(sa_ref, 2 * j, False)
        sa_ref[...] = scores(2 * j + 2)
        consume(sb_ref, 2 * j + 1, False)
        return carry

    lax.fori_loop(0, diag // 2, tile_pair, 0)

    @pl.when(diag % 2 == 1)
    def _():
        sb_ref[...] = scores(diag)
        consume(sa_ref, diag - 1, False)
        consume(sb_ref, diag, True)

    @pl.when(diag % 2 == 0)
    def _():
        consume(sa_ref, diag, True)

    gates = gate_ref[0]
    for i in range(NSA_HPG):
        halves = []
        for g in range(NSA_KV_GROUPS):
            head = i + NSA_HPG * g
            rs = head_rows(2 * i + g)
            o_s = acc_ref[rs, :LANES] / acc_ref[rs, LANES:]
            halves.append(oc_ref[rs] * gates[:, 3 * head:3 * head + 1]
                          + o_s * gates[:, 3 * head + 1:3 * head + 2]
                          + ow_ref[rs] * gates[:, 3 * head + 2:3 * head + 3])
        sl = slice(i * LANES, (i + 1) * LANES)
        out_ref[0, :, sl] = (jnp.where(low, halves[0], halves[1]) * nz_ref[0, :, sl]).astype(out_ref.dtype)


def _nsa(q, nz, gates, ksa, vs, kw, vw, kcmp, vcmp, ovl):
    batch, seq_len, _ = q.shape

    def q_spec(width):
        return pl.BlockSpec((1, Q_BLOCK, width), lambda b, i: (b, i, 0))

    def seq_spec(arr):
        return pl.BlockSpec((1,) + arr.shape[1:], lambda b, i: (b, 0, 0))

    rows = NSA_HEADS * Q_BLOCK
    n_cmp = kcmp.shape[1]
    span = WINDOW + Q_BLOCK
    last = seq_len // Q_BLOCK - 1
    next_q_spec = pl.BlockSpec((1, Q_BLOCK, NSA_WIDTH), lambda b, i: (b, jnp.minimum(i + 1, last), 0))
    return pl.pallas_call(
        functools.partial(_nsa_kernel, seq_len=seq_len),
        grid=(batch, seq_len // Q_BLOCK),
        in_specs=[q_spec(NSA_WIDTH), next_q_spec, q_spec(NSA_WIDTH), q_spec(LANES),
                  seq_spec(ksa), seq_spec(vs), seq_spec(kw), seq_spec(vw),
                  seq_spec(kcmp), seq_spec(vcmp),
                  pl.BlockSpec(ovl.shape, lambda b, i: (0, 0))],
        out_specs=q_spec(NSA_WIDTH),
        out_shape=jax.ShapeDtypeStruct((batch, seq_len, NSA_WIDTH), jnp.bfloat16),
        scratch_shapes=[pltpu.VMEM((rows, 2 * LANES), jnp.bfloat16),
                        pltpu.VMEM((rows, n_cmp), jnp.float32),
                        pltpu.VMEM((rows, span), jnp.float32),
                        pltpu.VMEM((rows, n_cmp), jnp.bfloat16),
                        pltpu.VMEM((rows, span), jnp.bfloat16),
                        pltpu.VMEM((Q_BLOCK, n_cmp), jnp.float32),
                        pltpu.VMEM((Q_BLOCK, span), jnp.float32),
                        pltpu.VMEM((rows, LANES), jnp.float32),
                        pltpu.VMEM((rows, 2 * LANES), jnp.float32),
                        pltpu.VMEM((rows, LANES), jnp.float32),
                        pltpu.VMEM((rows, LANES), jnp.float32),
                        pltpu.VMEM((rows, SEL_KEY_TILE), jnp.float32),
                        pltpu.VMEM((rows, SEL_KEY_TILE), jnp.float32),
                        pltpu.VMEM((rows, LANES), jnp.float32),
                        pltpu.VMEM((rows, LANES), jnp.float32),
                        pltpu.VMEM((rows, LANES), jnp.bfloat16),
                        pltpu.VMEM((NSA_KV_GROUPS * Q_BLOCK, LANES), jnp.bfloat16),
                        pltpu.VMEM((rows, LANES), jnp.float32)],
        compiler_params=pltpu.CompilerParams(dimension_semantics=("arbitrary", "arbitrary"),
                                             vmem_limit_bytes=VMEM_LIMIT_BYTES),
        name="nsa_attention",
    )(q, q, nz, gates, ksa, vs, kw, vw, kcmp, vcmp, ovl)


def _outproj_kernel(x_ref, ygm_ref, ynsa_ref, ymem_ref, w_ref, g_ref, b_ref, o_ref, *, alpha):
    for r0 in range(0, x_ref.shape[0], PROJ_SUB_ROWS):
        rs = slice(r0, r0 + PROJ_SUB_ROWS)
        y = (_dot(ygm_ref[rs, :], w_ref[:GM_WIDTH])
             + _dot(ynsa_ref[rs, :], w_ref[GM_WIDTH:GM_WIDTH + NSA_WIDTH])
             + _dot(ymem_ref[rs, :], w_ref[GM_WIDTH + NSA_WIDTH:]))
        r = alpha * x_ref[rs, :] + y
        mu = jnp.mean(r, axis=-1, keepdims=True)
        d = r - mu
        var = jnp.mean(d * d, axis=-1, keepdims=True)
        o_ref[rs, :] = d * lax.rsqrt(var + LN_EPS) * g_ref[...] + b_ref[...]


def _outproj(x2d, ygm, ynsa, ymem, w_out, ln_g, ln_b, *, alpha):
    n, d_model = x2d.shape
    rows = PROJ_ROWS

    def tok_spec(width):
        return pl.BlockSpec((rows, width), lambda i: (i, 0))

    def const_spec(shape):
        return pl.BlockSpec(shape, lambda i: (0, 0))

    return pl.pallas_call(
        functools.partial(_outproj_kernel, alpha=alpha),
        grid=(n // rows,),
        in_specs=[tok_spec(d_model), tok_spec(GM_WIDTH), tok_spec(NSA_WIDTH), tok_spec(MEM_WIDTH),
                  const_spec(w_out.shape), const_spec(ln_g.shape), const_spec(ln_b.shape)],
        out_specs=tok_spec(d_model),
        out_shape=jax.ShapeDtypeStruct((n, d_model), jnp.float32),
        compiler_params=pltpu.CompilerParams(dimension_semantics=("arbitrary",),
                                             vmem_limit_bytes=VMEM_LIMIT_BYTES),
        name="out_proj_layernorm",
    )(x2d, ygm, ynsa, ymem, w_out, ln_g, ln_b)


def _pair_head_slices(w, start, axis):
    return [lax.slice_in_dim(w, start + h * HEAD_DIM, start + (h + 1) * HEAD_DIM, axis=axis)
            for h in PAIR_HEAD_ORDER]


def _permute_w_in(w):
    w = w.astype(jnp.bfloat16)
    o_gate = 2048
    o_nz = o_gate + GATE_COLS
    o_mq = o_nz + NSA_WIDTH
    pieces = ([w[..., :768]] + _pair_head_slices(w, 768, 2) + [w[..., 1280:2048]]
              + _pair_head_slices(w, o_nz, 2)
              + [w[..., o_mq:], w[..., o_gate:o_nz],
                 jnp.zeros(w.shape[:2] + (LANES - GATE_COLS,), w.dtype)])
    return jnp.concatenate(pieces, axis=2)


def _permute_w_out(w):
    w = w.astype(jnp.bfloat16)
    pieces = ([w[:, :GM_WIDTH]] + _pair_head_slices(w, GM_WIDTH, 1) + [w[:, GM_WIDTH + NSA_WIDTH:]])
    return jnp.concatenate(pieces, axis=1)


def _rope_tables(seq_len):
    half = HEAD_DIM // 2
    inv_freq = ROPE_THETA ** (-jnp.arange(half, dtype=jnp.float32) * 2.0 / HEAD_DIM)
    ang = jnp.arange(seq_len).astype(jnp.float32)[:, None] * inv_freq[None, :]
    cos, sin = jnp.cos(ang), jnp.sin(ang)
    reps = LANES // HEAD_DIM
    cos_t = jnp.tile(jnp.concatenate([cos, cos], axis=1), (1, reps))
    sin_t = jnp.tile(jnp.concatenate([-sin, sin], axis=1), (1, reps))
    rot_low = ((np.arange(LANES) % HEAD_DIM) < half).astype(np.float32)[None, :]
    return cos_t, sin_t, jnp.asarray(rot_low)


def _compress_weights(pos_k, w1_k, w2_k, pos_v, w1_v, w2_v):
    half = CMP_BLOCK // 2
    eye = jnp.eye(NSA_KV_GROUPS, dtype=w1_k.dtype)

    def expand_w1(w1):
        w = w1.reshape(2, half, HEAD_DIM, CMP_HIDDEN)
        w = jnp.einsum('aldh,gk->algdkh', w, eye)
        return w.reshape(2, half * NSA_KV_WIDTH, NSA_KV_GROUPS * CMP_HIDDEN)

    def expand_w2(w2):
        w = jnp.einsum('hd,gk->ghkd', w2, eye)
        return w.reshape(NSA_KV_GROUPS * CMP_HIDDEN, NSA_KV_WIDTH)

    def expand_pos(pos):
        p = pos.reshape(2, half, 1, HEAD_DIM)
        p = jnp.broadcast_to(p, (2, half, NSA_KV_GROUPS, HEAD_DIM))
        return p.reshape(2, half * NSA_KV_WIDTH)

    pos = jnp.stack([expand_pos(pos_k), expand_pos(pos_v)])
    w1 = jnp.stack([expand_w1(w1_k), expand_w1(w1_v)]).astype(jnp.bfloat16)
    w2 = jnp.stack([expand_w2(w2_k), expand_w2(w2_v)]).astype(jnp.bfloat16)
    return pos, w1, w2


def _overlap_matrix(n_rows, n_sel):
    c_start = np.arange(n_rows) * CMP_STRIDE
    s_start = np.arange(LANES) * SEL_BLOCK
    ovl = ((c_start[:, None] < s_start[None, :] + SEL_BLOCK)
           & (c_start[:, None] + CMP_BLOCK > s_start[None, :])
           & (np.arange(LANES)[None, :] < n_sel))
    out = np.zeros((LANES + 16, n_rows), np.float32)
    out[:LANES] = ovl.T
    out[LANES] = 1.0
    return jnp.asarray(out, dtype=jnp.bfloat16)


def kernel(x, mem, w_in, gm_ln_g, gm_ln_b, gm_ws, gm_bs, cmp_pos_k, cmp_k_w1, cmp_k_w2,
           cmp_pos_v, cmp_v_w1, cmp_v_w2, w_mem_kv, w_out, ln_g, ln_b):
    batch, seq_len, d_model = x.shape
    depth = w_in.shape[0]
    assert seq_len % SEL_KEY_TILE == 0 and seq_len >= WINDOW + Q_BLOCK
    assert SEL_TOPK <= seq_len // SEL_BLOCK <= LANES
    alpha = (2.0 * depth) ** 0.25
    n_tok = batch * seq_len
    n_rows = seq_len // CMP_STRIDE

    cos_t, sin_t, rot_low = _rope_tables(seq_len)
    ovl = _overlap_matrix(n_rows, seq_len // SEL_BLOCK)
    tril = jnp.tril(jnp.ones((GM_CHUNK, GM_CHUNK), gm_ws.dtype))
    mk_all, mv_all = _memkv(mem.reshape(batch * mem.shape[1], d_model), w_mem_kv.astype(jnp.bfloat16))
    w_cat_all = _permute_w_in(w_in)
    w_out_all = _permute_w_out(w_out)

    h = x.reshape(n_tok, d_model)
    for l in range(depth):
        w_cat = w_cat_all[l]
        gws = (gm_ws[l] * tril[None]).astype(jnp.bfloat16)
        gbs = jnp.repeat(gm_bs[l].T, HEAD_DIM, axis=1)
        glg = gm_ln_g[l].reshape(1, GM_WIDTH)
        glb = gm_ln_b[l].reshape(1, GM_WIDTH)
        (ygm, ymem, q, kc, vc, ksa, vs, kw, vw, nz, gates) = _inproj(
            h, w_cat, cos_t, sin_t, rot_low, gws, gbs, glg, glb, mk_all[l], mv_all[l],
            batch=batch, seq_len=seq_len)

        pos, w1, w2 = _compress_weights(cmp_pos_k[l], cmp_k_w1[l], cmp_k_w2[l],
                                        cmp_pos_v[l], cmp_v_w1[l], cmp_v_w2[l])
        row_shape = (batch, n_rows, CMP_STRIDE * NSA_KV_WIDTH)
        kcmp, vcmp = _compress(kc.reshape(row_shape), vc.reshape(row_shape), pos, w1, w2)

        def per_seq(a):
            return a.reshape(batch, seq_len, a.shape[-1])

        ynsa = _nsa(per_seq(q), per_seq(nz), per_seq(gates), per_seq(ksa), per_seq(vs),
                    per_seq(kw), per_seq(vw), kcmp, vcmp, ovl)

        h = _outproj(h, ygm, ynsa.reshape(n_tok, NSA_WIDTH), ymem, w_out_all[l],
                     ln_g[l].reshape(1, d_model), ln_b[l].reshape(1, d_model), alpha=alpha)
    return h.reshape(batch, seq_len, d_model)
```

```python
import functools

import numpy as np
import jax
import jax.numpy as jnp
from jax import lax
from jax.experimental import pallas as pl
from jax.experimental.pallas import tpu as pltpu

HEAD_DIM = 64
GM_GROUPS = 4
GM_WIDTH = GM_GROUPS * HEAD_DIM
GM_CHUNK = 128
NSA_HEADS = 8
NSA_KV_GROUPS = 2
NSA_HPG = NSA_HEADS // NSA_KV_GROUPS
NSA_WIDTH = NSA_HEADS * HEAD_DIM
NSA_KV_WIDTH = NSA_KV_GROUPS * HEAD_DIM
CMP_BLOCK = 32
CMP_STRIDE = 16
CMP_HIDDEN = 128
SEL_BLOCK = 64
SEL_TOPK = 16
N_LOCAL_SEL = 2
WINDOW = 512
Q_BLOCK = 128
MEM_HEADS = 4
MEM_WIDTH = MEM_HEADS * HEAD_DIM
ROPE_THETA = 10000.0
LN_EPS = 1e-5
NEG_INF = -1e30
FORCE_SCORE = 1e4
GATE_COLS = NSA_HEADS * 3

LANES = 128
VMEM_LIMIT_BYTES = 56 * 1024 * 1024

PROJ_ROWS = 1024
PROJ_SUB_ROWS = 256
SEL_KEY_TILE = 512
SOFTMAX_ROWS = 32
REMOVED = -3.0e38

PAIR_HEAD_ORDER = tuple(h for i in range(NSA_HPG) for h in (i, i + NSA_HPG))

C_GU, C_GV, C_GZ = 0, 256, 512
C_Q = 768
C_KC, C_VC, C_KS, C_VS, C_KW, C_VW = 1280, 1408, 1536, 1664, 1792, 1920
C_NZ = 2048
C_MQ, C_MZ = 2560, 2816
C_GATE = 3072
N_COLS = 3200


def _dot(a, b):
    return jnp.dot(a, b, preferred_element_type=jnp.float32)


def _dot_nt(a, b):
    return lax.dot_general(a, b, (((1,), (1,)), ((), ())), preferred_element_type=jnp.float32)


def _gelu(x):
    return 0.5 * x * (1.0 + lax.erf(x * np.float32(np.sqrt(0.5))))


def _silu(x):
    return x * jax.nn.sigmoid(x)


def _lane_iota(shape):
    return lax.broadcasted_iota(jnp.int32, shape, len(shape) - 1)


def _low_half(shape):
    return (_lane_iota(shape) % LANES) < HEAD_DIM


def _tile_lanes(x, reps):
    return jnp.concatenate([x] * reps, axis=-1) if reps > 1 else x


def _memkv_kernel(mem_ref, w_ref, k_ref, v_ref):
    kv = _dot(mem_ref[...].astype(jnp.bfloat16), w_ref[0])
    k_ref[0] = kv[:, :MEM_WIDTH].astype(jnp.bfloat16)
    v_ref[0] = kv[:, MEM_WIDTH:].astype(jnp.bfloat16)


def _memkv(mem2d, w_mem_kv_bf16):
    depth = w_mem_kv_bf16.shape[0]
    rows, d_model = mem2d.shape
    out = jax.ShapeDtypeStruct((depth, rows, MEM_WIDTH), jnp.bfloat16)
    return pl.pallas_call(
        _memkv_kernel,
        grid=(depth,),
        in_specs=[pl.BlockSpec((rows, d_model), lambda l: (0, 0)),
                  pl.BlockSpec((1, d_model, 2 * MEM_WIDTH), lambda l: (l, 0, 0))],
        out_specs=[pl.BlockSpec((1, rows, MEM_WIDTH), lambda l: (l, 0, 0)),
                   pl.BlockSpec((1, rows, MEM_WIDTH), lambda l: (l, 0, 0))],
        out_shape=[out, out],
        name="mem_kv_proj",
    )(mem2d, w_mem_kv_bf16)


def _rope(x, cos, sin_signed, low):
    width = x.shape[-1]
    swapped = jnp.where(low, pltpu.roll(x, width - HEAD_DIM // 2, 1), pltpu.roll(x, HEAD_DIM // 2, 1))
    return x * cos + swapped * sin_signed


def _group_layer_norm(v, g, b, low):
    inv = np.float32(1.0 / HEAD_DIM)
    s_lo = jnp.sum(jnp.where(low, v, 0.0), axis=-1, keepdims=True)
    s_hi = jnp.sum(jnp.where(low, 0.0, v), axis=-1, keepdims=True)
    mu = jnp.where(low, s_lo, s_hi) * inv
    d = v - mu
    d2 = d * d
    q_lo = jnp.sum(jnp.where(low, d2, 0.0), axis=-1, keepdims=True)
    q_hi = jnp.sum(jnp.where(low, 0.0, d2), axis=-1, keepdims=True)
    var = jnp.where(low, q_lo, q_hi) * inv
    return d * lax.rsqrt(var + LN_EPS) * g + b


def _inproj_kernel(x_ref, w_ref, cos_ref, sin_ref, rot_low_ref, gws_ref, gbs_ref, glg_ref, glb_ref,
                   mk_ref, mv_ref,
                   ygm_ref, ymem_ref, q_ref, kc_ref, vc_ref, ksa_ref, vs_ref, kw_ref, vw_ref,
                   nz_ref, gate_ref, stage_ref, *, seq_len):
    subs = [slice(r0, r0 + PROJ_SUB_ROWS) for r0 in range(0, x_ref.shape[0], PROJ_SUB_ROWS)]
    projected = _project(x_ref, w_ref, subs[0])
    for j, rs in enumerate(subs):
        upcoming = _project(x_ref, w_ref, subs[j + 1]) if j + 1 < len(subs) else None
        _inproj_rows(rs, pl.program_id(0) * x_ref.shape[0] + rs.start, projected,
                     cos_ref, sin_ref, rot_low_ref, gws_ref, gbs_ref, glg_ref, glb_ref,
                     mk_ref, mv_ref, ygm_ref, ymem_ref, q_ref, kc_ref, vc_ref, ksa_ref, vs_ref,
                     kw_ref, vw_ref, nz_ref, gate_ref, stage_ref, seq_len=seq_len)
        projected = upcoming


def _project(x_ref, w_ref, rs):
    xb = x_ref[rs, :].astype(jnp.bfloat16)
    sections = ((C_GU, 3 * GM_WIDTH), (C_Q, NSA_WIDTH), (C_KC, 6 * NSA_KV_WIDTH),
                (C_NZ, NSA_WIDTH), (C_GATE, LANES), (C_MQ, 2 * MEM_WIDTH))
    return tuple(_dot(xb, w_ref[:, c0:c0 + width]) for c0, width in sections)


def _inproj_rows(rs, row0, projected, cos_ref, sin_ref, rot_low_ref, gws_ref, gbs_ref, glg_ref,
                 glb_ref, mk_ref, mv_ref, ygm_ref, ymem_ref, q_ref, kc_ref, vc_ref, ksa_ref, vs_ref,
                 kw_ref, vw_ref, nz_ref, gate_ref, stage_ref, *, seq_len):
    rows = rs.stop - rs.start
    gm, qh, kv, nz_raw, gate_raw, mem = projected
    low = _low_half((rows, LANES))
    rot_low = rot_low_ref[...] > 0.5
    rot_low = jnp.broadcast_to(rot_low, (rows, LANES))
    cos = cos_ref[rs, :]
    sin = sin_ref[rs, :]

    def slab(h, i):
        return h[:, i * LANES:(i + 1) * LANES]

    u = _gelu(gm[:, :GM_WIDTH])
    v = _gelu(gm[:, GM_WIDTH:2 * GM_WIDTH])
    z = gm[:, 2 * GM_WIDTH:]
    for pair in range(GM_GROUPS // 2):
        sl = slice(pair * LANES, (pair + 1) * LANES)
        vln = _group_layer_norm(v[:, sl], glg_ref[:, sl], glb_ref[:, sl], low).astype(jnp.bfloat16)
        for c in range(rows // GM_CHUNK):
            cs = slice(c * GM_CHUNK, (c + 1) * GM_CHUNK)
            out_rows = slice(rs.start + c * GM_CHUNK, rs.start + (c + 1) * GM_CHUNK)
            s_lo = _dot(gws_ref[2 * pair], vln[cs])
            s_hi = _dot(gws_ref[2 * pair + 1], vln[cs])
            s = jnp.where(_low_half((GM_CHUNK, LANES)), s_lo, s_hi) + gbs_ref[:, sl]
            ygm_ref[out_rows, sl] = (u[cs, sl] * s * _silu(z[cs, sl])).astype(ygm_ref.dtype)

    qscale = np.float32(HEAD_DIM ** -0.5)
    qscale2 = np.float32(HEAD_DIM ** -0.5 * np.log2(np.e))
    ones = jnp.ones((rows, LANES), vs_ref.dtype)
    for i in range(NSA_WIDTH // LANES):
        qi = _rope(slab(qh, i), cos, sin, rot_low) * qscale2
        q_ref[rs, i * LANES:(i + 1) * LANES] = qi.astype(q_ref.dtype)
    stage_ref[0, rs, :] = _rope(slab(kv, 0), cos, sin, rot_low)
    stage_ref[1, rs, :] = slab(kv, 1)
    out_rows = slice(rs.start // CMP_STRIDE, rs.stop // CMP_STRIDE)
    for j, dst in enumerate((kc_ref, vc_ref)):
        for l in range(CMP_STRIDE):
            token_l = stage_ref[j, pl.ds(rs.start + l, rows // CMP_STRIDE, stride=CMP_STRIDE), :]
            dst[out_rows, l * LANES:(l + 1) * LANES] = token_l
    ksa_ref[rs, :LANES] = _rope(slab(kv, 2), cos, sin, rot_low).astype(ksa_ref.dtype)
    tok = row0 % seq_len + lax.broadcasted_iota(jnp.int32, (rows, LANES), 0)
    onehot = (tok // SEL_BLOCK) == _lane_iota((rows, LANES))
    ksa_ref[rs, LANES:] = jnp.where(onehot, 1.0, 0.0).astype(ksa_ref.dtype)
    vs_ref[rs, :LANES] = slab(kv, 3).astype(vs_ref.dtype)
    vs_ref[rs, LANES:] = ones
    kw_ref[rs, :] = _rope(slab(kv, 4), cos, sin, rot_low).astype(kw_ref.dtype)
    vw_ref[rs, :LANES] = slab(kv, 5).astype(vw_ref.dtype)
    vw_ref[rs, LANES:] = ones
    nz_ref[rs, :] = _silu(nz_raw)
    gate_ref[rs, :] = jax.nn.sigmoid(gate_raw)

    mq = mem[:, :MEM_WIDTH] * qscale
    mz = mem[:, MEM_WIDTH:]
    for pair in range(MEM_HEADS // 2):
        sl = slice(pair * LANES, (pair + 1) * LANES)
        kp = mk_ref[0, :, sl]
        vp = mv_ref[0, :, sl]
        outs = []
        for keep_low in (True, False):
            qm = jnp.where(low == keep_low, mq[:, sl], 0.0).astype(jnp.bfloat16)
            s = _dot_nt(qm, kp)
            e = jnp.exp(s - jnp.max(s, axis=-1, keepdims=True))
            p = e / jnp.sum(e, axis=-1, keepdims=True)
            outs.append(_dot(p.astype(jnp.bfloat16), vp))
        o = jnp.where(low, outs[0], outs[1])
        ymem_ref[rs, sl] = (o * _silu(mz[:, sl])).astype(ymem_ref.dtype)


def _inproj(x2d, w_cat, cos_t, sin_t, rot_low, gws, gbs, glg, glb, mk, mv, *, layer, batch, seq_len):
    n, d_model = x2d.shape
    rows = PROJ_ROWS
    steps_per_seq = seq_len // rows
    mem_len = mk.shape[1] // batch

    def tok_spec(width):
        return pl.BlockSpec((rows, width), lambda i: (i, 0))

    def const_spec(shape):
        return pl.BlockSpec(shape, lambda i: (0,) * len(shape))

    def layer_spec(shape):
        return pl.BlockSpec((None,) + shape[1:], lambda i: (layer,) + (0,) * (len(shape) - 1))

    tab_spec = pl.BlockSpec((rows, LANES), lambda i: (i % steps_per_seq, 0))
    mem_spec = pl.BlockSpec((None, 1, mem_len, MEM_WIDTH), lambda i: (layer, i // steps_per_seq, 0, 0))
    bf16, f32 = jnp.bfloat16, jnp.float32
    outs = [(1, GM_WIDTH, bf16), (1, MEM_WIDTH, bf16), (1, NSA_WIDTH, bf16),
            (CMP_STRIDE, CMP_STRIDE * LANES, f32), (CMP_STRIDE, CMP_STRIDE * LANES, f32),
            (1, 2 * LANES, bf16), (1, 2 * LANES, bf16), (1, LANES, bf16), (1, 2 * LANES, bf16),
            (1, NSA_WIDTH, f32), (1, LANES, f32)]
    return pl.pallas_call(
        functools.partial(_inproj_kernel, seq_len=seq_len),
        grid=(n // rows,),
        in_specs=[tok_spec(d_model), layer_spec(w_cat.shape), tab_spec, tab_spec,
                  const_spec(rot_low.shape), const_spec(gws.shape), const_spec(gbs.shape),
                  const_spec(glg.shape), const_spec(glb.shape), mem_spec, mem_spec],
        out_specs=[pl.BlockSpec((rows // d, w), lambda i: (i, 0)) for d, w, _ in outs],
        out_shape=[jax.ShapeDtypeStruct((n // d, w), dt) for d, w, dt in outs],
        scratch_shapes=[pltpu.VMEM((2, rows, LANES), f32)],
        compiler_params=pltpu.CompilerParams(dimension_semantics=("arbitrary",),
                                             vmem_limit_bytes=VMEM_LIMIT_BYTES),
        name="in_proj_mixers",
    )(x2d, w_cat, cos_t, sin_t, rot_low, gws, gbs, glg, glb,
      mk.reshape(-1, batch, mem_len, MEM_WIDTH), mv.reshape(-1, batch, mem_len, MEM_WIDTH))


def _compress_kernel(k_ref, v_ref, pos_ref, w1_ref, w2_ref, kcmp_ref, vcmp_ref):
    n_rows = k_ref.shape[1]
    for idx, (src, dst) in enumerate(((k_ref, kcmp_ref), (v_ref, vcmp_ref))):
        xr = src[0]
        top = _dot((xr + pos_ref[idx, 0:1]).astype(jnp.bfloat16), w1_ref[idx, 0])
        bot = _dot((xr + pos_ref[idx, 1:2]).astype(jnp.bfloat16), w1_ref[idx, 1])
        hidden = top + pltpu.roll(bot, n_rows - 1, 0)
        act = jax.nn.gelu(hidden, approximate=True)
        dst[0, :, :LANES] = _dot(act.astype(jnp.bfloat16), w2_ref[idx]).astype(dst.dtype)
    vcmp_ref[0, :, LANES:] = jnp.ones((n_rows, LANES), vcmp_ref.dtype)


def _compress(kc_rows, vc_rows, pos, w1, w2):
    batch, n_rows, width = kc_rows.shape
    row_spec = pl.BlockSpec((1, n_rows, width), lambda b: (b, 0, 0))
    def out_spec(width):
        return pl.BlockSpec((1, n_rows, width), lambda b: (b, 0, 0))

    def out(width):
        return jax.ShapeDtypeStruct((batch, n_rows, width), jnp.bfloat16)

    return pl.pallas_call(
        _compress_kernel,
        grid=(batch,),
        in_specs=[row_spec, row_spec,
                  pl.BlockSpec(pos.shape, lambda b: (0, 0, 0)),
                  pl.BlockSpec(w1.shape, lambda b: (0, 0, 0, 0)),
                  pl.BlockSpec(w2.shape, lambda b: (0, 0, 0))],
        out_specs=[out_spec(LANES), out_spec(2 * LANES)],
        out_shape=[out(LANES), out(2 * LANES)],
        compiler_params=pltpu.CompilerParams(dimension_semantics=("arbitrary",),
                                             vmem_limit_bytes=VMEM_LIMIT_BYTES),
        name="nsa_compress",
    )(kc_rows, vc_rows, pos, w1, w2)


def _split_bf16(x, parts):
    out = []
    for _ in range(parts):
        hi = x.astype(jnp.bfloat16)
        out.append(hi)
        x = x - hi.astype(jnp.float32)
    return out


def _topk_columns(score):
    row = lax.broadcasted_iota(jnp.int32, score.shape, 0).astype(jnp.float32)
    picked = jnp.zeros(score.shape, jnp.float32)
    for _ in range(SEL_TOPK):
        best = jnp.max(score, axis=0, keepdims=True)
        first = jnp.min(jnp.where(score == best, row, np.float32(score.shape[0])),
                        axis=0, keepdims=True)
        hit = row == first
        picked = jnp.where(hit, 1.0, picked)
        score = jnp.where(hit, REMOVED, score)
    return picked


def _nsa_kernel(q_ref, qnext_ref, nz_ref, gate_ref, ksa_ref, vs_ref, kw_ref, vw_ref, kcmp_ref,
                vcmp_ref, ovl_ref, out_ref, qa_ref, sc_ref, sw_ref, pc_ref, pw_ref, bc_ref, bw_ref,
                m_ref, acc_ref, oc_ref, ow_ref, sa_ref, sb_ref, mc_ref, mw_ref, qn_ref, selb_ref,
                ocn_ref, *, seq_len):
    bi = pl.program_id(1)
    start = bi * Q_BLOCK
    n_sel = seq_len // SEL_BLOCK
    n_cmp = kcmp_ref.shape[1]
    span = WINDOW + Q_BLOCK
    tk = SEL_KEY_TILE
    bf16 = jnp.bfloat16

    def head_rows(r):
        return slice(r * Q_BLOCK, (r + 1) * Q_BLOCK)

    low = _low_half((Q_BLOCK, LANES))
    chunk = SOFTMAX_ROWS
    n_chunks = Q_BLOCK // chunk

    def stack_queries(src_ref, dst_ref):
        for i in range(NSA_HPG):
            qi = src_ref[0, :, i * LANES:(i + 1) * LANES]
            zero = jnp.zeros_like(qi)
            dst_ref[head_rows(2 * i), :LANES] = jnp.where(low, qi, zero)
            dst_ref[head_rows(2 * i + 1), :LANES] = jnp.where(low, zero, qi)

    def compressed_scores(qs_ref, blk_start):
        t_b = blk_start + lax.broadcasted_iota(jnp.int32, (Q_BLOCK, 1), 0)
        c_end = lax.broadcasted_iota(jnp.int32, (1, n_cmp), 1) * CMP_STRIDE + (CMP_BLOCK - 1)
        bc_ref[...] = jnp.where(c_end <= t_b, 0.0, NEG_INF)
        sc_ref[...] = _dot_nt(qs_ref[:, :LANES], kcmp_ref[0])

    def masked_exp(s_ref, b_ref, m_ref_, p_ref):
        width_tiles = s_ref.shape[1] // LANES
        for r in range(NSA_HEADS):
            for c in range(n_chunks):
                crow = slice(c * chunk, (c + 1) * chunk)
                rows = slice(r * Q_BLOCK + c * chunk, r * Q_BLOCK + (c + 1) * chunk)
                row_max = jnp.max(s_ref[rows, :] + b_ref[crow, :], axis=-1, keepdims=True)
                m_ref_[rows, :] = jnp.broadcast_to(row_max, (chunk, LANES))
        for r in range(NSA_HEADS):
            for c in range(n_chunks):
                crow = slice(c * chunk, (c + 1) * chunk)
                rows = slice(r * Q_BLOCK + c * chunk, r * Q_BLOCK + (c + 1) * chunk)
                s = s_ref[rows, :] + b_ref[crow, :]
                p_ref[rows, :] = jnp.exp2(s - _tile_lanes(m_ref_[rows, :], width_tiles)).astype(bf16)

    def compressed_out(blk_start):
        t_col = blk_start + lax.broadcasted_iota(jnp.int32, (Q_BLOCK, 1), 0)
        seen_col = jnp.concatenate([t_col >= CMP_BLOCK - 1] * NSA_HEADS, axis=0)
        o_c = _dot(pc_ref[...], vcmp_ref[0])
        ocn_ref[...] = jnp.where(seen_col, o_c[:, :LANES] / o_c[:, LANES:], 0.0)

    def candidate_scores(blk_start):
        parts = _dot_nt(ovl_ref[...], pc_ref[...])
        t_lane = blk_start + _lane_iota((1, parts.shape[1])) % Q_BLOCK
        inv = jnp.where(t_lane >= CMP_BLOCK - 1, 1.0 / parts[LANES:LANES + 1, :], 0.0)
        weighted = parts[:LANES, :] * inv
        imp = jnp.concatenate(
            [sum(weighted[:, (2 * i + g) * Q_BLOCK:(2 * i + g + 1) * Q_BLOCK] for i in range(NSA_HPG))
             for g in range(NSA_KV_GROUPS)], axis=1)
        blk = lax.broadcasted_iota(jnp.int32, imp.shape, 0)
        t_blk = (blk_start + _lane_iota((1, imp.shape[1])) % Q_BLOCK) // SEL_BLOCK
        valid = blk <= t_blk
        forced = (blk == 0) | (valid & (blk > t_blk - N_LOCAL_SEL))
        score = jnp.where(forced, FORCE_SCORE, jnp.where(valid, imp, -1.0))
        if n_sel < LANES:
            score = jnp.where(blk < n_sel, score, REMOVED)
        return score

    def select_blocks(score):
        picked = _topk_columns(score).astype(bf16)
        eye = (lax.broadcasted_iota(jnp.int32, (Q_BLOCK, Q_BLOCK), 0)
               == lax.broadcasted_iota(jnp.int32, (Q_BLOCK, Q_BLOCK), 1)).astype(bf16)
        for g in range(NSA_KV_GROUPS):
            picked_q = _dot_nt(eye, picked[:, g * Q_BLOCK:(g + 1) * Q_BLOCK])
            selb_ref[g * Q_BLOCK:(g + 1) * Q_BLOCK, :] = ((1.0 - picked_q) * NEG_INF).astype(bf16)

    t_q = start + lax.broadcasted_iota(jnp.int32, (Q_BLOCK, 1), 0)
    w0 = pl.multiple_of(jnp.maximum(start - WINDOW, 0), Q_BLOCK)

    def window_scores():
        kpos = w0 + lax.broadcasted_iota(jnp.int32, (1, span), 1)
        bw_ref[...] = jnp.where((kpos <= t_q) & (kpos > t_q - WINDOW), 0.0, NEG_INF)
        sw_ref[...] = _dot_nt(qa_ref[:, :LANES], kw_ref[0, pl.ds(w0, span), :])

    def window_out():
        o_w = _dot(pw_ref[...], vw_ref[0, pl.ds(w0, span), :])
        ow_ref[...] = o_w[:, :LANES] / o_w[:, LANES:]

    def scores(tile):
        k0 = pl.multiple_of(tile * tk, tk)
        return _dot_nt(qa_ref[...], ksa_ref[0, pl.ds(k0, tk), :])

    @pl.when(bi == 0)
    def _():
        stack_queries(q_ref, qn_ref)
        compressed_scores(qn_ref, start)
        masked_exp(sc_ref, bc_ref, mc_ref, pc_ref)
        compressed_out(start)
        select_blocks(candidate_scores(start))

    stack_queries(q_ref, qa_ref)
    for r in range(NSA_HEADS):
        g = r % NSA_KV_GROUPS
        qa_ref[head_rows(r), LANES:] = selb_ref[g * Q_BLOCK:(g + 1) * Q_BLOCK, :]
    oc_ref[...] = ocn_ref[...]
    stack_queries(qnext_ref, qn_ref)

    nxt = start + Q_BLOCK
    compressed_scores(qn_ref, nxt)
    window_scores()
    masked_exp(sc_ref, bc_ref, mc_ref, pc_ref)
    next_score = candidate_scores(nxt)
    sa_ref[...] = scores(0)
    masked_exp(sw_ref, bw_ref, mw_ref, pw_ref)
    compressed_out(nxt)
    window_out()
    select_blocks(next_score)

    m_ref[...] = jnp.full(m_ref.shape, NEG_INF, jnp.float32)
    acc_ref[...] = jnp.zeros(acc_ref.shape, jnp.float32)

    def consume(buf_ref, tile, causal):
        k0 = pl.multiple_of(tile * tk, tk)
        s = buf_ref[...]
        if causal:
            kpos = k0 + lax.broadcasted_iota(jnp.int32, (1, tk), 1)
            tile_bias = jnp.where(kpos <= t_q, 0.0, NEG_INF)
            s = s + jnp.concatenate([tile_bias] * NSA_HEADS, axis=0)
        m_prev = m_ref[...]
        m_next = jnp.maximum(m_prev, jnp.max(s, axis=-1, keepdims=True))
        p = jnp.exp2(s - _tile_lanes(m_next, tk // LANES))
        alpha = jnp.exp2(m_prev - m_next)
        acc_ref[...] = (_tile_lanes(alpha, 2) * acc_ref[...]
                        + _dot(p.astype(bf16), vs_ref[0, pl.ds(k0, tk), :]))
        m_ref[...] = m_next

    diag = start // tk

    def tile_pair(j, carry):
        sb_ref[...] = scores(2 * j + 1)
        consume(sa_ref, 2 * j, False)
        sa_ref[...] = scores(2 * j + 2)
        consume(sb_ref, 2 * j + 1, False)
        return carry

    lax.fori_loop(0, diag // 2, tile_pair, 0)

    @pl.when(diag % 2 == 1)
    def _():
        sb_ref[...] = scores(diag)
        consume(sa_ref, diag - 1, False)
        consume(sb_ref, diag, True)

    @pl.when(diag % 2 == 0)
    def _():
        consume(sa_ref, diag, True)

    gates = gate_ref[0]
    for i in range(NSA_HPG):
        halves = []
        for g in range(NSA_KV_GROUPS):
            head = i + NSA_HPG * g
            rs = head_rows(2 * i + g)
            o_s = acc_ref[rs, :LANES] / acc_ref[rs, LANES:]
            halves.append(oc_ref[rs] * gates[:, 3 * head:3 * head + 1]
                          + o_s * gates[:, 3 * head + 1:3 * head + 2]
                          + ow_ref[rs] * gates[:, 3 * head + 2:3 * head + 3])
        sl = slice(i * LANES, (i + 1) * LANES)
        out_ref[0, :, sl] = (jnp.where(low, halves[0], halves[1]) * nz_ref[0, :, sl]).astype(out_ref.dtype)


def _nsa(q, nz, gates, ksa, vs, kw, vw, kcmp, vcmp, ovl):
    batch, seq_len, _ = q.shape

    def q_spec(width):
        return pl.BlockSpec((1, Q_BLOCK, width), lambda b, i: (b, i, 0))

    def seq_spec(arr):
        return pl.BlockSpec((1,) + arr.shape[1:], lambda b, i: (b, 0, 0))

    rows = NSA_HEADS * Q_BLOCK
    n_cmp = kcmp.shape[1]
    span = WINDOW + Q_BLOCK
    last = seq_len // Q_BLOCK - 1
    next_q_spec = pl.BlockSpec((1, Q_BLOCK, NSA_WIDTH), lambda b, i: (b, jnp.minimum(i + 1, last), 0))
    return pl.pallas_call(
        functools.partial(_nsa_kernel, seq_len=seq_len),
        grid=(batch, seq_len // Q_BLOCK),
        in_specs=[q_spec(NSA_WIDTH), next_q_spec, q_spec(NSA_WIDTH), q_spec(LANES),
                  seq_spec(ksa), seq_spec(vs), seq_spec(kw), seq_spec(vw),
                  seq_spec(kcmp), seq_spec(vcmp),
                  pl.BlockSpec(ovl.shape, lambda b, i: (0, 0))],
        out_specs=q_spec(NSA_WIDTH),
        out_shape=jax.ShapeDtypeStruct((batch, seq_len, NSA_WIDTH), jnp.bfloat16),
        scratch_shapes=[pltpu.VMEM((rows, 2 * LANES), jnp.bfloat16),
                        pltpu.VMEM((rows, n_cmp), jnp.float32),
                        pltpu.VMEM((rows, span), jnp.float32),
                        pltpu.VMEM((rows, n_cmp), jnp.bfloat16),
                        pltpu.VMEM((rows, span), jnp.bfloat16),
                        pltpu.VMEM((Q_BLOCK, n_cmp), jnp.float32),
                        pltpu.VMEM((Q_BLOCK, span), jnp.float32),
                        pltpu.VMEM((rows, LANES), jnp.float32),
                        pltpu.VMEM((rows, 2 * LANES), jnp.float32),
                        pltpu.VMEM((rows, LANES), jnp.float32),
                        pltpu.VMEM((rows, LANES), jnp.float32),
                        pltpu.VMEM((rows, SEL_KEY_TILE), jnp.float32),
                        pltpu.VMEM((rows, SEL_KEY_TILE), jnp.float32),
                        pltpu.VMEM((rows, LANES), jnp.float32),
                        pltpu.VMEM((rows, LANES), jnp.float32),
                        pltpu.VMEM((rows, LANES), jnp.bfloat16),
                        pltpu.VMEM((NSA_KV_GROUPS * Q_BLOCK, LANES), jnp.bfloat16),
                        pltpu.VMEM((rows, LANES), jnp.float32)],
        compiler_params=pltpu.CompilerParams(dimension_semantics=("arbitrary", "arbitrary"),
                                             vmem_limit_bytes=VMEM_LIMIT_BYTES),
        name="nsa_attention",
    )(q, q, nz, gates, ksa, vs, kw, vw, kcmp, vcmp, ovl)


def _outproj_kernel(x_ref, ygm_ref, ynsa_ref, ymem_ref, w_ref, g_ref, b_ref, o_ref, *, alpha):
    def mix_proj(rs):
        return (_dot(ygm_ref[rs, :], w_ref[:GM_WIDTH])
                + _dot(ynsa_ref[rs, :], w_ref[GM_WIDTH:GM_WIDTH + NSA_WIDTH])
                + _dot(ymem_ref[rs, :], w_ref[GM_WIDTH + NSA_WIDTH:]))

    subs = [slice(r0, r0 + PROJ_SUB_ROWS) for r0 in range(0, x_ref.shape[0], PROJ_SUB_ROWS)]
    y_next = mix_proj(subs[0])
    for j, rs in enumerate(subs):
        y = y_next
        y_next = mix_proj(subs[j + 1]) if j + 1 < len(subs) else None
        r = alpha * x_ref[rs, :] + y
        mu = jnp.mean(r, axis=-1, keepdims=True)
        d = r - mu
        var = jnp.mean(d * d, axis=-1, keepdims=True)
        o_ref[rs, :] = d * lax.rsqrt(var + LN_EPS) * g_ref[...] + b_ref[...]


def _outproj(x2d, ygm, ynsa, ymem, w_out, ln_g, ln_b, *, layer, alpha):
    n, d_model = x2d.shape
    rows = PROJ_ROWS

    def tok_spec(width):
        return pl.BlockSpec((rows, width), lambda i: (i, 0))

    def layer_spec(shape):
        return pl.BlockSpec((None,) + shape[1:], lambda i: (layer, 0, 0))

    return pl.pallas_call(
        functools.partial(_outproj_kernel, alpha=alpha),
        grid=(n // rows,),
        in_specs=[tok_spec(d_model), tok_spec(GM_WIDTH), tok_spec(NSA_WIDTH), tok_spec(MEM_WIDTH),
                  layer_spec(w_out.shape), layer_spec(ln_g.shape), layer_spec(ln_b.shape)],
        out_specs=tok_spec(d_model),
        out_shape=jax.ShapeDtypeStruct((n, d_model), jnp.float32),
        compiler_params=pltpu.CompilerParams(dimension_semantics=("arbitrary",),
                                             vmem_limit_bytes=VMEM_LIMIT_BYTES),
        name="out_proj_layernorm",
    )(x2d, ygm, ynsa, ymem, w_out, ln_g, ln_b)


def _pair_head_slices(w, start, axis):
    return [lax.slice_in_dim(w, start + h * HEAD_DIM, start + (h + 1) * HEAD_DIM, axis=axis)
            for h in PAIR_HEAD_ORDER]


def _permute_w_in(w):
    w = w.astype(jnp.bfloat16)
    o_gate = 2048
    o_nz = o_gate + GATE_COLS
    o_mq = o_nz + NSA_WIDTH
    pieces = ([w[..., :768]] + _pair_head_slices(w, 768, 2) + [w[..., 1280:2048]]
              + _pair_head_slices(w, o_nz, 2)
              + [w[..., o_mq:], w[..., o_gate:o_nz],
                 jnp.zeros(w.shape[:2] + (LANES - GATE_COLS,), w.dtype)])
    return jnp.concatenate(pieces, axis=2)


def _permute_w_out(w):
    w = w.astype(jnp.bfloat16)
    pieces = ([w[:, :GM_WIDTH]] + _pair_head_slices(w, GM_WIDTH, 1) + [w[:, GM_WIDTH + NSA_WIDTH:]])
    return jnp.concatenate(pieces, axis=1)


def _rope_tables(seq_len):
    half = HEAD_DIM // 2
    inv_freq = ROPE_THETA ** (-jnp.arange(half, dtype=jnp.float32) * 2.0 / HEAD_DIM)
    ang = jnp.arange(seq_len).astype(jnp.float32)[:, None] * inv_freq[None, :]
    cos, sin = jnp.cos(ang), jnp.sin(ang)
    reps = LANES // HEAD_DIM
    cos_t = jnp.tile(jnp.concatenate([cos, cos], axis=1), (1, reps))
    sin_t = jnp.tile(jnp.concatenate([-sin, sin], axis=1), (1, reps))
    rot_low = ((np.arange(LANES) % HEAD_DIM) < half).astype(np.float32)[None, :]
    return cos_t, sin_t, jnp.asarray(rot_low)


def _compress_weights(pos_k, w1_k, w2_k, pos_v, w1_v, w2_v):
    half = CMP_BLOCK // 2
    eye = jnp.eye(NSA_KV_GROUPS, dtype=w1_k.dtype)

    def expand_w1(w1):
        w = w1.reshape(2, half, HEAD_DIM, CMP_HIDDEN)
        w = jnp.einsum('aldh,gk->algdkh', w, eye)
        return w.reshape(2, half * NSA_KV_WIDTH, NSA_KV_GROUPS * CMP_HIDDEN)

    def expand_w2(w2):
        w = jnp.einsum('hd,gk->ghkd', w2, eye)
        return w.reshape(NSA_KV_GROUPS * CMP_HIDDEN, NSA_KV_WIDTH)

    def expand_pos(pos):
        p = pos.reshape(2, half, 1, HEAD_DIM)
        p = jnp.broadcast_to(p, (2, half, NSA_KV_GROUPS, HEAD_DIM))
        return p.reshape(2, half * NSA_KV_WIDTH)

    pos = jnp.stack([expand_pos(pos_k), expand_pos(pos_v)])
    w1 = jnp.stack([expand_w1(w1_k), expand_w1(w1_v)]).astype(jnp.bfloat16)
    w2 = jnp.stack([expand_w2(w2_k), expand_w2(w2_v)]).astype(jnp.bfloat16)
    return pos, w1, w2


def _overlap_matrix(n_rows, n_sel):
    c_start = np.arange(n_rows) * CMP_STRIDE
    s_start = np.arange(LANES) * SEL_BLOCK
    ovl = ((c_start[:, None] < s_start[None, :] + SEL_BLOCK)
           & (c_start[:, None] + CMP_BLOCK > s_start[None, :])
           & (np.arange(LANES)[None, :] < n_sel))
    out = np.zeros((LANES + 16, n_rows), np.float32)
    out[:LANES] = ovl.T
    out[LANES] = 1.0
    return jnp.asarray(out, dtype=jnp.bfloat16)


def kernel(x, mem, w_in, gm_ln_g, gm_ln_b, gm_ws, gm_bs, cmp_pos_k, cmp_k_w1, cmp_k_w2,
           cmp_pos_v, cmp_v_w1, cmp_v_w2, w_mem_kv, w_out, ln_g, ln_b):
    batch, seq_len, d_model = x.shape
    depth = w_in.shape[0]
    assert seq_len % SEL_KEY_TILE == 0 and seq_len >= WINDOW + Q_BLOCK
    assert SEL_TOPK <= seq_len // SEL_BLOCK <= LANES
    alpha = (2.0 * depth) ** 0.25
    n_tok = batch * seq_len
    n_rows = seq_len // CMP_STRIDE

    cos_t, sin_t, rot_low = _rope_tables(seq_len)
    ovl = _overlap_matrix(n_rows, seq_len // SEL_BLOCK)
    tril = jnp.tril(jnp.ones((GM_CHUNK, GM_CHUNK), gm_ws.dtype))
    mk_all, mv_all = _memkv(mem.reshape(batch * mem.shape[1], d_model), w_mem_kv.astype(jnp.bfloat16))
    w_cat_all = _permute_w_in(w_in)
    w_out_all = _permute_w_out(w_out)

    h = x.reshape(n_tok, d_model)
    for l in range(depth):
        gws = (gm_ws[l] * tril[None]).astype(jnp.bfloat16)
        gbs = jnp.repeat(gm_bs[l].T, HEAD_DIM, axis=1)
        glg = gm_ln_g[l].reshape(1, GM_WIDTH)
        glb = gm_ln_b[l].reshape(1, GM_WIDTH)
        (ygm, ymem, q, kc, vc, ksa, vs, kw, vw, nz, gates) = _inproj(
            h, w_cat_all, cos_t, sin_t, rot_low, gws, gbs, glg, glb, mk_all, mv_all,
            layer=l, batch=batch, seq_len=seq_len)

        pos, w1, w2 = _compress_weights(cmp_pos_k[l], cmp_k_w1[l], cmp_k_w2[l],
                                        cmp_pos_v[l], cmp_v_w1[l], cmp_v_w2[l])
        row_shape = (batch, n_rows, CMP_STRIDE * NSA_KV_WIDTH)
        kcmp, vcmp = _compress(kc.reshape(row_shape), vc.reshape(row_shape), pos, w1, w2)

        def per_seq(a):
            return a.reshape(batch, seq_len, a.shape[-1])

        ynsa = _nsa(per_seq(q), per_seq(nz), per_seq(gates), per_seq(ksa), per_seq(vs),
                    per_seq(kw), per_seq(vw), kcmp, vcmp, ovl)

        h = _outproj(h, ygm, ynsa.reshape(n_tok, NSA_WIDTH), ymem, w_out_all,
                     ln_g.reshape(depth, 1, d_model), ln_b.reshape(depth, 1, d_model),
                     layer=l, alpha=alpha)
    return h.reshape(batch, seq_len, d_model)
```

```python
import functools

import numpy as np
import jax
import jax.numpy as jnp
from jax import lax
from jax.experimental import pallas as pl
from jax.experimental.pallas import tpu as pltpu

HEAD_DIM = 64
GM_GROUPS = 4
GM_WIDTH = GM_GROUPS * HEAD_DIM
GM_CHUNK = 128
NSA_HEADS = 8
NSA_KV_GROUPS = 2
NSA_HPG = NSA_HEADS // NSA_KV_GROUPS
NSA_WIDTH = NSA_HEADS * HEAD_DIM
NSA_KV_WIDTH = NSA_KV_GROUPS * HEAD_DIM
CMP_BLOCK = 32
CMP_STRIDE = 16
CMP_HIDDEN = 128
SEL_BLOCK = 64
SEL_TOPK = 16
N_LOCAL_SEL = 2
WINDOW = 512
Q_BLOCK = 128
MEM_HEADS = 4
MEM_WIDTH = MEM_HEADS * HEAD_DIM
ROPE_THETA = 10000.0
LN_EPS = 1e-5
NEG_INF = -1e30
FORCE_SCORE = 1e4
GATE_COLS = NSA_HEADS * 3

LANES = 128
VMEM_LIMIT_BYTES = 56 * 1024 * 1024

PROJ_ROWS = 1024
PROJ_SUB_ROWS = 256
SEL_KEY_TILE = 512
SOFTMAX_ROWS = 32
REMOVED = -3.0e38

PAIR_HEAD_ORDER = tuple(h for i in range(NSA_HPG) for h in (i, i + NSA_HPG))

C_GU, C_GV, C_GZ = 0, 256, 512
C_Q = 768
C_KC, C_VC, C_KS, C_VS, C_KW, C_VW = 1280, 1408, 1536, 1664, 1792, 1920
C_NZ = 2048
C_MQ, C_MZ = 2560, 2816
C_GATE = 3072
N_COLS = 3200


def _dot(a, b):
    return jnp.dot(a, b, preferred_element_type=jnp.float32)


def _dot_nt(a, b):
    return lax.dot_general(a, b, (((1,), (1,)), ((), ())), preferred_element_type=jnp.float32)


def _gelu(x):
    return 0.5 * x * (1.0 + lax.erf(x * np.float32(np.sqrt(0.5))))


def _silu(x):
    return x * jax.nn.sigmoid(x)


def _lane_iota(shape):
    return lax.broadcasted_iota(jnp.int32, shape, len(shape) - 1)


def _low_half(shape):
    return (_lane_iota(shape) % LANES) < HEAD_DIM


def _tile_lanes(x, reps):
    return jnp.concatenate([x] * reps, axis=-1) if reps > 1 else x


def _memkv_kernel(mem_ref, w_ref, k_ref, v_ref):
    kv = _dot(mem_ref[...].astype(jnp.bfloat16), w_ref[0])
    k_ref[0] = kv[:, :MEM_WIDTH].astype(jnp.bfloat16)
    v_ref[0] = kv[:, MEM_WIDTH:].astype(jnp.bfloat16)


def _memkv(mem2d, w_mem_kv_bf16):
    depth = w_mem_kv_bf16.shape[0]
    rows, d_model = mem2d.shape
    out = jax.ShapeDtypeStruct((depth, rows, MEM_WIDTH), jnp.bfloat16)
    return pl.pallas_call(
        _memkv_kernel,
        grid=(depth,),
        in_specs=[pl.BlockSpec((rows, d_model), lambda l: (0, 0)),
                  pl.BlockSpec((1, d_model, 2 * MEM_WIDTH), lambda l: (l, 0, 0))],
        out_specs=[pl.BlockSpec((1, rows, MEM_WIDTH), lambda l: (l, 0, 0)),
                   pl.BlockSpec((1, rows, MEM_WIDTH), lambda l: (l, 0, 0))],
        out_shape=[out, out],
        name="mem_kv_proj",
    )(mem2d, w_mem_kv_bf16)


def _rope(x, cos, sin_signed, low):
    width = x.shape[-1]
    swapped = jnp.where(low, pltpu.roll(x, width - HEAD_DIM // 2, 1), pltpu.roll(x, HEAD_DIM // 2, 1))
    return x * cos + swapped * sin_signed


def _group_layer_norm(v, g, b, low):
    inv = np.float32(1.0 / HEAD_DIM)
    s_lo = jnp.sum(jnp.where(low, v, 0.0), axis=-1, keepdims=True)
    s_hi = jnp.sum(jnp.where(low, 0.0, v), axis=-1, keepdims=True)
    mu = jnp.where(low, s_lo, s_hi) * inv
    d = v - mu
    d2 = d * d
    q_lo = jnp.sum(jnp.where(low, d2, 0.0), axis=-1, keepdims=True)
    q_hi = jnp.sum(jnp.where(low, 0.0, d2), axis=-1, keepdims=True)
    var = jnp.where(low, q_lo, q_hi) * inv
    return d * lax.rsqrt(var + LN_EPS) * g + b


def _inproj_kernel(x_ref, w_ref, cos_ref, sin_ref, rot_low_ref, gws_ref, gbs_ref, glg_ref, glb_ref,
                   mk_ref, mv_ref,
                   ygm_ref, ymem_ref, q_ref, kc_ref, vc_ref, ksa_ref, vs_ref, kw_ref, vw_ref,
                   nz_ref, gate_ref, stage_ref, *, seq_len):
    subs = [slice(r0, r0 + PROJ_SUB_ROWS) for r0 in range(0, x_ref.shape[0], PROJ_SUB_ROWS)]
    projected = _project(x_ref, w_ref, subs[0])
    for j, rs in enumerate(subs):
        upcoming = _project(x_ref, w_ref, subs[j + 1]) if j + 1 < len(subs) else None
        _inproj_rows(rs, pl.program_id(0) * x_ref.shape[0] + rs.start, projected,
                     cos_ref, sin_ref, rot_low_ref, gws_ref, gbs_ref, glg_ref, glb_ref,
                     mk_ref, mv_ref, ygm_ref, ymem_ref, q_ref, kc_ref, vc_ref, ksa_ref, vs_ref,
                     kw_ref, vw_ref, nz_ref, gate_ref, stage_ref, seq_len=seq_len)
        projected = upcoming


def _project(x_ref, w_ref, rs):
    xb = x_ref[rs, :].astype(jnp.bfloat16)
    sections = ((C_GU, 3 * GM_WIDTH), (C_Q, NSA_WIDTH), (C_KC, 6 * NSA_KV_WIDTH),
                (C_NZ, NSA_WIDTH), (C_GATE, LANES), (C_MQ, 2 * MEM_WIDTH))
    return tuple(_dot(xb, w_ref[:, c0:c0 + width]) for c0, width in sections)


def _inproj_rows(rs, row0, projected, cos_ref, sin_ref, rot_low_ref, gws_ref, gbs_ref, glg_ref,
                 glb_ref, mk_ref, mv_ref, ygm_ref, ymem_ref, q_ref, kc_ref, vc_ref, ksa_ref, vs_ref,
                 kw_ref, vw_ref, nz_ref, gate_ref, stage_ref, *, seq_len):
    rows = rs.stop - rs.start
    gm, qh, kv, nz_raw, gate_raw, mem = projected
    low = _low_half((rows, LANES))
    rot_low = rot_low_ref[...] > 0.5
    rot_low = jnp.broadcast_to(rot_low, (rows, LANES))
    cos = cos_ref[rs, :]
    sin = sin_ref[rs, :]

    def slab(h, i):
        return h[:, i * LANES:(i + 1) * LANES]

    u = _gelu(gm[:, :GM_WIDTH])
    v = _gelu(gm[:, GM_WIDTH:2 * GM_WIDTH])
    z = gm[:, 2 * GM_WIDTH:]
    for pair in range(GM_GROUPS // 2):
        sl = slice(pair * LANES, (pair + 1) * LANES)
        vln = _group_layer_norm(v[:, sl], glg_ref[:, sl], glb_ref[:, sl], low).astype(jnp.bfloat16)
        for c in range(rows // GM_CHUNK):
            cs = slice(c * GM_CHUNK, (c + 1) * GM_CHUNK)
            out_rows = slice(rs.start + c * GM_CHUNK, rs.start + (c + 1) * GM_CHUNK)
            s_lo = _dot(gws_ref[2 * pair], vln[cs])
            s_hi = _dot(gws_ref[2 * pair + 1], vln[cs])
            s = jnp.where(_low_half((GM_CHUNK, LANES)), s_lo, s_hi) + gbs_ref[:, sl]
            ygm_ref[out_rows, sl] = (u[cs, sl] * s * _silu(z[cs, sl])).astype(ygm_ref.dtype)

    qscale = np.float32(HEAD_DIM ** -0.5)
    qscale2 = np.float32(HEAD_DIM ** -0.5 * np.log2(np.e))
    ones = jnp.ones((rows, LANES), vs_ref.dtype)
    for i in range(NSA_WIDTH // LANES):
        qi = _rope(slab(qh, i), cos, sin, rot_low) * qscale2
        q_ref[rs, i * LANES:(i + 1) * LANES] = qi.astype(q_ref.dtype)
    stage_ref[0, rs, :] = _rope(slab(kv, 0), cos, sin, rot_low)
    stage_ref[1, rs, :] = slab(kv, 1)
    out_rows = slice(rs.start // CMP_STRIDE, rs.stop // CMP_STRIDE)
    for j, dst in enumerate((kc_ref, vc_ref)):
        for l in range(CMP_STRIDE):
            token_l = stage_ref[j, pl.ds(rs.start + l, rows // CMP_STRIDE, stride=CMP_STRIDE), :]
            dst[out_rows, l * LANES:(l + 1) * LANES] = token_l
    ksa_ref[rs, :LANES] = _rope(slab(kv, 2), cos, sin, rot_low).astype(ksa_ref.dtype)
    tok = row0 % seq_len + lax.broadcasted_iota(jnp.int32, (rows, LANES), 0)
    onehot = (tok // SEL_BLOCK) == _lane_iota((rows, LANES))
    ksa_ref[rs, LANES:] = jnp.where(onehot, 1.0, 0.0).astype(ksa_ref.dtype)
    vs_ref[rs, :LANES] = slab(kv, 3).astype(vs_ref.dtype)
    vs_ref[rs, LANES:] = ones
    kw_ref[rs, :] = _rope(slab(kv, 4), cos, sin, rot_low).astype(kw_ref.dtype)
    vw_ref[rs, :LANES] = slab(kv, 5).astype(vw_ref.dtype)
    vw_ref[rs, LANES:] = ones
    nz_ref[rs, :] = _silu(nz_raw)
    gate_ref[rs, :] = jax.nn.sigmoid(gate_raw)

    mq = mem[:, :MEM_WIDTH] * qscale
    mz = mem[:, MEM_WIDTH:]
    for pair in range(MEM_HEADS // 2):
        sl = slice(pair * LANES, (pair + 1) * LANES)
        kp = mk_ref[0, :, sl]
        vp = mv_ref[0, :, sl]
        outs = []
        for keep_low in (True, False):
            qm = jnp.where(low == keep_low, mq[:, sl], 0.0).astype(jnp.bfloat16)
            s = _dot_nt(qm, kp)
            e = jnp.exp(s - jnp.max(s, axis=-1, keepdims=True))
            p = e / jnp.sum(e, axis=-1, keepdims=True)
            outs.append(_dot(p.astype(jnp.bfloat16), vp))
        o = jnp.where(low, outs[0], outs[1])
        ymem_ref[rs, sl] = (o * _silu(mz[:, sl])).astype(ymem_ref.dtype)


def _inproj(x2d, w_cat, cos_t, sin_t, rot_low, gws, gbs, glg, glb, mk, mv, *, layer, batch, seq_len):
    n, d_model = x2d.shape
    rows = PROJ_ROWS
    steps_per_seq = seq_len // rows
    mem_len = mk.shape[1] // batch

    def tok_spec(width):
        return pl.BlockSpec((rows, width), lambda i: (i, 0))

    def const_spec(shape):
        return pl.BlockSpec(shape, lambda i: (0,) * len(shape))

    def layer_spec(shape):
        return pl.BlockSpec((None,) + shape[1:], lambda i: (layer,) + (0,) * (len(shape) - 1))

    tab_spec = pl.BlockSpec((rows, LANES), lambda i: (i % steps_per_seq, 0))
    mem_spec = pl.BlockSpec((None, 1, mem_len, MEM_WIDTH), lambda i: (layer, i // steps_per_seq, 0, 0))
    bf16, f32 = jnp.bfloat16, jnp.float32
    outs = [(1, GM_WIDTH, bf16), (1, MEM_WIDTH, bf16), (1, NSA_WIDTH, bf16),
            (CMP_STRIDE, CMP_STRIDE * LANES, f32), (CMP_STRIDE, CMP_STRIDE * LANES, f32),
            (1, 2 * LANES, bf16), (1, 2 * LANES, bf16), (1, LANES, bf16), (1, 2 * LANES, bf16),
            (1, NSA_WIDTH, f32), (1, LANES, f32)]
    return pl.pallas_call(
        functools.partial(_inproj_kernel, seq_len=seq_len),
        grid=(n // rows,),
        in_specs=[tok_spec(d_model), layer_spec(w_cat.shape), tab_spec, tab_spec,
                  const_spec(rot_low.shape), const_spec(gws.shape), const_spec(gbs.shape),
                  const_spec(glg.shape), const_spec(glb.shape), mem_spec, mem_spec],
        out_specs=[pl.BlockSpec((rows // d, w), lambda i: (i, 0)) for d, w, _ in outs],
        out_shape=[jax.ShapeDtypeStruct((n // d, w), dt) for d, w, dt in outs],
        scratch_shapes=[pltpu.VMEM((2, rows, LANES), f32)],
        compiler_params=pltpu.CompilerParams(dimension_semantics=("arbitrary",),
                                             vmem_limit_bytes=VMEM_LIMIT_BYTES),
        name="in_proj_mixers",
    )(x2d, w_cat, cos_t, sin_t, rot_low, gws, gbs, glg, glb,
      mk.reshape(-1, batch, mem_len, MEM_WIDTH), mv.reshape(-1, batch, mem_len, MEM_WIDTH))


def _compress_kernel(k_ref, v_ref, pos_ref, w1_ref, w2_ref, kcmp_ref, vcmp_ref):
    n_rows = k_ref.shape[1]
    for idx, (src, dst) in enumerate(((k_ref, kcmp_ref), (v_ref, vcmp_ref))):
        xr = src[0]
        top = _dot((xr + pos_ref[idx, 0:1]).astype(jnp.bfloat16), w1_ref[idx, 0])
        bot = _dot((xr + pos_ref[idx, 1:2]).astype(jnp.bfloat16), w1_ref[idx, 1])
        hidden = top + pltpu.roll(bot, n_rows - 1, 0)
        act = jax.nn.gelu(hidden, approximate=True)
        dst[0, :, :LANES] = _dot(act.astype(jnp.bfloat16), w2_ref[idx]).astype(dst.dtype)
    vcmp_ref[0, :, LANES:] = jnp.ones((n_rows, LANES), vcmp_ref.dtype)


def _compress(kc_rows, vc_rows, pos, w1, w2):
    batch, n_rows, width = kc_rows.shape
    row_spec = pl.BlockSpec((1, n_rows, width), lambda b: (b, 0, 0))
    def out_spec(width):
        return pl.BlockSpec((1, n_rows, width), lambda b: (b, 0, 0))

    def out(width):
        return jax.ShapeDtypeStruct((batch, n_rows, width), jnp.bfloat16)

    return pl.pallas_call(
        _compress_kernel,
        grid=(batch,),
        in_specs=[row_spec, row_spec,
                  pl.BlockSpec(pos.shape, lambda b: (0, 0, 0)),
                  pl.BlockSpec(w1.shape, lambda b: (0, 0, 0, 0)),
                  pl.BlockSpec(w2.shape, lambda b: (0, 0, 0))],
        out_specs=[out_spec(LANES), out_spec(2 * LANES)],
        out_shape=[out(LANES), out(2 * LANES)],
        compiler_params=pltpu.CompilerParams(dimension_semantics=("arbitrary",),
                                             vmem_limit_bytes=VMEM_LIMIT_BYTES),
        name="nsa_compress",
    )(kc_rows, vc_rows, pos, w1, w2)


def _split_bf16(x, parts):
    out = []
    for _ in range(parts):
        hi = x.astype(jnp.bfloat16)
        out.append(hi)
        x = x - hi.astype(jnp.float32)
    return out


def _topk_columns(score):
    row = lax.broadcasted_iota(jnp.int32, score.shape, 0).astype(jnp.float32)
    picked = jnp.zeros(score.shape, jnp.float32)
    for _ in range(SEL_TOPK):
        best = jnp.max(score, axis=0, keepdims=True)
        first = jnp.min(jnp.where(score == best, row, np.float32(score.shape[0])),
                        axis=0, keepdims=True)
        hit = row == first
        picked = jnp.where(hit, 1.0, picked)
        score = jnp.where(hit, REMOVED, score)
    return picked


def _nsa_kernel(q_ref, qnext_ref, nz_ref, gate_ref, ksa_ref, vs_ref, kw_ref, vw_ref, kcmp_ref,
                vcmp_ref, ovl_ref, gexp_ref, out_ref, qa_ref, sc_ref, sw_ref, pc_ref, pw_ref, bc_ref,
                bw_ref, m_ref, acc_ref, oc_ref, ow_ref, sa_ref, sb_ref, mc_ref, mw_ref, qn_ref,
                selb_ref, ocn_ref, gx_ref, *, seq_len):
    bi = pl.program_id(1)
    start = bi * Q_BLOCK
    n_sel = seq_len // SEL_BLOCK
    n_cmp = kcmp_ref.shape[1]
    span = WINDOW + Q_BLOCK
    tk = SEL_KEY_TILE
    bf16 = jnp.bfloat16

    def head_rows(r):
        return slice(r * Q_BLOCK, (r + 1) * Q_BLOCK)

    low = _low_half((Q_BLOCK, LANES))
    chunk = SOFTMAX_ROWS
    n_chunks = Q_BLOCK // chunk

    def stack_queries(src_ref, dst_ref):
        for i in range(NSA_HPG):
            qi = src_ref[0, :, i * LANES:(i + 1) * LANES]
            zero = jnp.zeros_like(qi)
            dst_ref[head_rows(2 * i), :LANES] = jnp.where(low, qi, zero)
            dst_ref[head_rows(2 * i + 1), :LANES] = jnp.where(low, zero, qi)

    def compressed_scores(qs_ref, blk_start):
        t_b = blk_start + lax.broadcasted_iota(jnp.int32, (Q_BLOCK, 1), 0)
        c_end = lax.broadcasted_iota(jnp.int32, (1, n_cmp), 1) * CMP_STRIDE + (CMP_BLOCK - 1)
        bc_ref[...] = jnp.where(c_end <= t_b, 0.0, NEG_INF)
        sc_ref[...] = _dot_nt(qs_ref[:, :LANES], kcmp_ref[0])

    def masked_exp(s_ref, b_ref, m_ref_, p_ref):
        width_tiles = s_ref.shape[1] // LANES
        for r in range(NSA_HEADS):
            for c in range(n_chunks):
                crow = slice(c * chunk, (c + 1) * chunk)
                rows = slice(r * Q_BLOCK + c * chunk, r * Q_BLOCK + (c + 1) * chunk)
                row_max = jnp.max(s_ref[rows, :] + b_ref[crow, :], axis=-1, keepdims=True)
                m_ref_[rows, :] = jnp.broadcast_to(row_max, (chunk, LANES))
        for r in range(NSA_HEADS):
            for c in range(n_chunks):
                crow = slice(c * chunk, (c + 1) * chunk)
                rows = slice(r * Q_BLOCK + c * chunk, r * Q_BLOCK + (c + 1) * chunk)
                s = s_ref[rows, :] + b_ref[crow, :]
                p_ref[rows, :] = jnp.exp2(s - _tile_lanes(m_ref_[rows, :], width_tiles)).astype(bf16)

    def compressed_out(blk_start):
        t_col = blk_start + lax.broadcasted_iota(jnp.int32, (Q_BLOCK, 1), 0)
        seen_col = jnp.concatenate([t_col >= CMP_BLOCK - 1] * NSA_HEADS, axis=0)
        o_c = _dot(pc_ref[...], vcmp_ref[0])
        ocn_ref[...] = jnp.where(seen_col, o_c[:, :LANES] / o_c[:, LANES:], 0.0)

    def candidate_scores(blk_start):
        parts = _dot_nt(ovl_ref[...], pc_ref[...])
        t_lane = blk_start + _lane_iota((1, parts.shape[1])) % Q_BLOCK
        inv = jnp.where(t_lane >= CMP_BLOCK - 1, 1.0 / parts[LANES:LANES + 1, :], 0.0)
        weighted = parts[:LANES, :] * inv
        imp = jnp.concatenate(
            [sum(weighted[:, (2 * i + g) * Q_BLOCK:(2 * i + g + 1) * Q_BLOCK] for i in range(NSA_HPG))
             for g in range(NSA_KV_GROUPS)], axis=1)
        blk = lax.broadcasted_iota(jnp.int32, imp.shape, 0)
        t_blk = (blk_start + _lane_iota((1, imp.shape[1])) % Q_BLOCK) // SEL_BLOCK
        valid = blk <= t_blk
        forced = (blk == 0) | (valid & (blk > t_blk - N_LOCAL_SEL))
        score = jnp.where(forced, FORCE_SCORE, jnp.where(valid, imp, -1.0))
        if n_sel < LANES:
            score = jnp.where(blk < n_sel, score, REMOVED)
        return score

    def select_blocks(score):
        picked = _topk_columns(score).astype(bf16)
        eye = (lax.broadcasted_iota(jnp.int32, (Q_BLOCK, Q_BLOCK), 0)
               == lax.broadcasted_iota(jnp.int32, (Q_BLOCK, Q_BLOCK), 1)).astype(bf16)
        for g in range(NSA_KV_GROUPS):
            picked_q = _dot_nt(eye, picked[:, g * Q_BLOCK:(g + 1) * Q_BLOCK])
            selb_ref[g * Q_BLOCK:(g + 1) * Q_BLOCK, :] = ((1.0 - picked_q) * NEG_INF).astype(bf16)

    t_q = start + lax.broadcasted_iota(jnp.int32, (Q_BLOCK, 1), 0)
    w0 = pl.multiple_of(jnp.maximum(start - WINDOW, 0), Q_BLOCK)

    def window_scores():
        kpos = w0 + lax.broadcasted_iota(jnp.int32, (1, span), 1)
        bw_ref[...] = jnp.where((kpos <= t_q) & (kpos > t_q - WINDOW), 0.0, NEG_INF)
        sw_ref[...] = _dot_nt(qa_ref[:, :LANES], kw_ref[0, pl.ds(w0, span), :])

    def window_out():
        o_w = _dot(pw_ref[...], vw_ref[0, pl.ds(w0, span), :])
        ow_ref[...] = o_w[:, :LANES] / o_w[:, LANES:]

    def scores(tile):
        k0 = pl.multiple_of(tile * tk, tk)
        return _dot_nt(qa_ref[...], ksa_ref[0, pl.ds(k0, tk), :])

    @pl.when(bi == 0)
    def _():
        stack_queries(q_ref, qn_ref)
        compressed_scores(qn_ref, start)
        masked_exp(sc_ref, bc_ref, mc_ref, pc_ref)
        compressed_out(start)
        select_blocks(candidate_scores(start))

    stack_queries(q_ref, qa_ref)
    for r in range(NSA_HEADS):
        g = r % NSA_KV_GROUPS
        qa_ref[head_rows(r), LANES:] = selb_ref[g * Q_BLOCK:(g + 1) * Q_BLOCK, :]
    oc_ref[...] = ocn_ref[...]
    stack_queries(qnext_ref, qn_ref)

    nxt = start + Q_BLOCK
    compressed_scores(qn_ref, nxt)
    window_scores()
    masked_exp(sc_ref, bc_ref, mc_ref, pc_ref)
    next_score = candidate_scores(nxt)
    sa_ref[...] = scores(0)
    masked_exp(sw_ref, bw_ref, mw_ref, pw_ref)
    compressed_out(nxt)
    window_out()
    g_hi, g_lo = _split_bf16(gate_ref[0], 2)
    gx_ref[...] = _dot(g_hi, gexp_ref[...]) + _dot(g_lo, gexp_ref[...])
    select_blocks(next_score)

    m_ref[...] = jnp.full(m_ref.shape, NEG_INF, jnp.float32)
    acc_ref[...] = jnp.zeros(acc_ref.shape, jnp.float32)

    def consume(buf_ref, tile, causal):
        k0 = pl.multiple_of(tile * tk, tk)
        s = buf_ref[...]
        if causal:
            kpos = k0 + lax.broadcasted_iota(jnp.int32, (1, tk), 1)
            tile_bias = jnp.where(kpos <= t_q, 0.0, NEG_INF)
            s = s + jnp.concatenate([tile_bias] * NSA_HEADS, axis=0)
        m_prev = m_ref[...]
        m_next = jnp.maximum(m_prev, jnp.max(s, axis=-1, keepdims=True))
        p = jnp.exp2(s - _tile_lanes(m_next, tk // LANES))
        alpha = jnp.exp2(m_prev - m_next)
        acc_ref[...] = (_tile_lanes(alpha, 2) * acc_ref[...]
                        + _dot(p.astype(bf16), vs_ref[0, pl.ds(k0, tk), :]))
        m_ref[...] = m_next

    diag = start // tk

    def tile_pair(first):
        sb_ref[...] = scores(first + 1)
        consume(sa_ref, first, False)
        sa_ref[...] = scores(first + 2)
        consume(sb_ref, first + 1, False)

    def tile_quad(j, carry):
        tile_pair(4 * j)
        tile_pair(4 * j + 2)
        return carry

    lax.fori_loop(0, diag // 4, tile_quad, 0)

    @pl.when(diag % 4 >= 2)
    def _():
        tile_pair((diag // 4) * 4)

    @pl.when(diag % 2 == 1)
    def _():
        sb_ref[...] = scores(diag)
        consume(sa_ref, diag - 1, False)
        consume(sb_ref, diag, True)

    @pl.when(diag % 2 == 0)
    def _():
        consume(sa_ref, diag, True)

    for i in range(NSA_HPG):
        lo_rows, hi_rows = head_rows(2 * i), head_rows(2 * i + 1)
        o_s = jnp.where(low, acc_ref[lo_rows, :LANES] / acc_ref[lo_rows, LANES:],
                        acc_ref[hi_rows, :LANES] / acc_ref[hi_rows, LANES:])
        branches = (jnp.where(low, oc_ref[lo_rows], oc_ref[hi_rows]), o_s,
                    jnp.where(low, ow_ref[lo_rows], ow_ref[hi_rows]))
        mixed = sum(o * gx_ref[:, (3 * i + c) * LANES:(3 * i + c + 1) * LANES]
                    for c, o in enumerate(branches))
        sl = slice(i * LANES, (i + 1) * LANES)
        out_ref[0, :, sl] = (mixed * nz_ref[0, :, sl]).astype(out_ref.dtype)


def _nsa(q, nz, gates, ksa, vs, kw, vw, kcmp, vcmp, ovl, gexp):
    batch, seq_len, _ = q.shape

    def q_spec(width):
        return pl.BlockSpec((1, Q_BLOCK, width), lambda b, i: (b, i, 0))

    def seq_spec(arr):
        return pl.BlockSpec((1,) + arr.shape[1:], lambda b, i: (b, 0, 0))

    rows = NSA_HEADS * Q_BLOCK
    n_cmp = kcmp.shape[1]
    span = WINDOW + Q_BLOCK
    last = seq_len // Q_BLOCK - 1
    next_q_spec = pl.BlockSpec((1, Q_BLOCK, NSA_WIDTH), lambda b, i: (b, jnp.minimum(i + 1, last), 0))
    return pl.pallas_call(
        functools.partial(_nsa_kernel, seq_len=seq_len),
        grid=(batch, seq_len // Q_BLOCK),
        in_specs=[q_spec(NSA_WIDTH), next_q_spec, q_spec(NSA_WIDTH), q_spec(LANES),
                  seq_spec(ksa), seq_spec(vs), seq_spec(kw), seq_spec(vw),
                  seq_spec(kcmp), seq_spec(vcmp),
                  pl.BlockSpec(ovl.shape, lambda b, i: (0, 0)),
                  pl.BlockSpec(gexp.shape, lambda b, i: (0, 0))],
        out_specs=q_spec(NSA_WIDTH),
        out_shape=jax.ShapeDtypeStruct((batch, seq_len, NSA_WIDTH), jnp.bfloat16),
        scratch_shapes=[pltpu.VMEM((rows, 2 * LANES), jnp.bfloat16),
                        pltpu.VMEM((rows, n_cmp), jnp.float32),
                        pltpu.VMEM((rows, span), jnp.float32),
                        pltpu.VMEM((rows, n_cmp), jnp.bfloat16),
                        pltpu.VMEM((rows, span), jnp.bfloat16),
                        pltpu.VMEM((Q_BLOCK, n_cmp), jnp.float32),
                        pltpu.VMEM((Q_BLOCK, span), jnp.float32),
                        pltpu.VMEM((rows, LANES), jnp.float32),
                        pltpu.VMEM((rows, 2 * LANES), jnp.float32),
                        pltpu.VMEM((rows, LANES), jnp.float32),
                        pltpu.VMEM((rows, LANES), jnp.float32),
                        pltpu.VMEM((rows, SEL_KEY_TILE), jnp.float32),
                        pltpu.VMEM((rows, SEL_KEY_TILE), jnp.float32),
                        pltpu.VMEM((rows, LANES), jnp.float32),
                        pltpu.VMEM((rows, LANES), jnp.float32),
                        pltpu.VMEM((rows, LANES), jnp.bfloat16),
                        pltpu.VMEM((NSA_KV_GROUPS * Q_BLOCK, LANES), jnp.bfloat16),
                        pltpu.VMEM((rows, LANES), jnp.float32),
                        pltpu.VMEM((Q_BLOCK, gexp.shape[1]), jnp.float32)],
        compiler_params=pltpu.CompilerParams(dimension_semantics=("arbitrary", "arbitrary"),
                                             vmem_limit_bytes=VMEM_LIMIT_BYTES),
        name="nsa_attention",
    )(q, q, nz, gates, ksa, vs, kw, vw, kcmp, vcmp, ovl, gexp)


def _outproj_kernel(x_ref, ygm_ref, ynsa_ref, ymem_ref, w_ref, g_ref, b_ref, o_ref, *, alpha):
    def mix_proj(rs):
        return (_dot(ygm_ref[rs, :], w_ref[:GM_WIDTH])
                + _dot(ynsa_ref[rs, :], w_ref[GM_WIDTH:GM_WIDTH + NSA_WIDTH])
                + _dot(ymem_ref[rs, :], w_ref[GM_WIDTH + NSA_WIDTH:]))

    subs = [slice(r0, r0 + PROJ_SUB_ROWS) for r0 in range(0, x_ref.shape[0], PROJ_SUB_ROWS)]
    y_next = mix_proj(subs[0])
    for j, rs in enumerate(subs):
        y = y_next
        y_next = mix_proj(subs[j + 1]) if j + 1 < len(subs) else None
        r = alpha * x_ref[rs, :] + y
        mu = jnp.mean(r, axis=-1, keepdims=True)
        d = r - mu
        var = jnp.mean(d * d, axis=-1, keepdims=True)
        o_ref[rs, :] = d * lax.rsqrt(var + LN_EPS) * g_ref[...] + b_ref[...]


def _outproj(x2d, ygm, ynsa, ymem, w_out, ln_g, ln_b, *, layer, alpha):
    n, d_model = x2d.shape
    rows = PROJ_ROWS

    def tok_spec(width):
        return pl.BlockSpec((rows, width), lambda i: (i, 0))

    def layer_spec(shape):
        return pl.BlockSpec((None,) + shape[1:], lambda i: (layer, 0, 0))

    return pl.pallas_call(
        functools.partial(_outproj_kernel, alpha=alpha),
        grid=(n // rows,),
        in_specs=[tok_spec(d_model), tok_spec(GM_WIDTH), tok_spec(NSA_WIDTH), tok_spec(MEM_WIDTH),
                  layer_spec(w_out.shape), layer_spec(ln_g.shape), layer_spec(ln_b.shape)],
        out_specs=tok_spec(d_model),
        out_shape=jax.ShapeDtypeStruct((n, d_model), jnp.float32),
        compiler_params=pltpu.CompilerParams(dimension_semantics=("arbitrary",),
                                             vmem_limit_bytes=VMEM_LIMIT_BYTES),
        name="out_proj_layernorm",
    )(x2d, ygm, ynsa, ymem, w_out, ln_g, ln_b)


def _pair_head_slices(w, start, axis):
    return [lax.slice_in_dim(w, start + h * HEAD_DIM, start + (h + 1) * HEAD_DIM, axis=axis)
            for h in PAIR_HEAD_ORDER]


def _permute_w_in(w):
    w = w.astype(jnp.bfloat16)
    o_gate = 2048
    o_nz = o_gate + GATE_COLS
    o_mq = o_nz + NSA_WIDTH
    pieces = ([w[..., :768]] + _pair_head_slices(w, 768, 2) + [w[..., 1280:2048]]
              + _pair_head_slices(w, o_nz, 2)
              + [w[..., o_mq:], w[..., o_gate:o_nz],
                 jnp.zeros(w.shape[:2] + (LANES - GATE_COLS,), w.dtype)])
    return jnp.concatenate(pieces, axis=2)


def _permute_w_out(w):
    w = w.astype(jnp.bfloat16)
    pieces = ([w[:, :GM_WIDTH]] + _pair_head_slices(w, GM_WIDTH, 1) + [w[:, GM_WIDTH + NSA_WIDTH:]])
    return jnp.concatenate(pieces, axis=1)


def _rope_tables(seq_len):
    half = HEAD_DIM // 2
    inv_freq = ROPE_THETA ** (-jnp.arange(half, dtype=jnp.float32) * 2.0 / HEAD_DIM)
    ang = jnp.arange(seq_len).astype(jnp.float32)[:, None] * inv_freq[None, :]
    cos, sin = jnp.cos(ang), jnp.sin(ang)
    reps = LANES // HEAD_DIM
    cos_t = jnp.tile(jnp.concatenate([cos, cos], axis=1), (1, reps))
    sin_t = jnp.tile(jnp.concatenate([-sin, sin], axis=1), (1, reps))
    rot_low = ((np.arange(LANES) % HEAD_DIM) < half).astype(np.float32)[None, :]
    return cos_t, sin_t, jnp.asarray(rot_low)


def _compress_weights(pos_k, w1_k, w2_k, pos_v, w1_v, w2_v):
    half = CMP_BLOCK // 2
    eye = jnp.eye(NSA_KV_GROUPS, dtype=w1_k.dtype)

    def expand_w1(w1):
        w = w1.reshape(2, half, HEAD_DIM, CMP_HIDDEN)
        w = jnp.einsum('aldh,gk->algdkh', w, eye)
        return w.reshape(2, half * NSA_KV_WIDTH, NSA_KV_GROUPS * CMP_HIDDEN)

    def expand_w2(w2):
        w = jnp.einsum('hd,gk->ghkd', w2, eye)
        return w.reshape(NSA_KV_GROUPS * CMP_HIDDEN, NSA_KV_WIDTH)

    def expand_pos(pos):
        p = pos.reshape(2, half, 1, HEAD_DIM)
        p = jnp.broadcast_to(p, (2, half, NSA_KV_GROUPS, HEAD_DIM))
        return p.reshape(2, half * NSA_KV_WIDTH)

    pos = jnp.stack([expand_pos(pos_k), expand_pos(pos_v)])
    w1 = jnp.stack([expand_w1(w1_k), expand_w1(w1_v)]).astype(jnp.bfloat16)
    w2 = jnp.stack([expand_w2(w2_k), expand_w2(w2_v)]).astype(jnp.bfloat16)
    return pos, w1, w2


def _gate_expansion():
    out = np.zeros((LANES, NSA_HPG * 3 * LANES), np.float32)
    for i in range(NSA_HPG):
        for c in range(3):
            base = (3 * i + c) * LANES
            out[3 * i + c, base:base + HEAD_DIM] = 1.0
            out[3 * (i + NSA_HPG) + c, base + HEAD_DIM:base + LANES] = 1.0
    return jnp.asarray(out, dtype=jnp.bfloat16)


def _overlap_matrix(n_rows, n_sel):
    c_start = np.arange(n_rows) * CMP_STRIDE
    s_start = np.arange(LANES) * SEL_BLOCK
    ovl = ((c_start[:, None] < s_start[None, :] + SEL_BLOCK)
           & (c_start[:, None] + CMP_BLOCK > s_start[None, :])
           & (np.arange(LANES)[None, :] < n_sel))
    out = np.zeros((LANES + 16, n_rows), np.float32)
    out[:LANES] = ovl.T
    out[LANES] = 1.0
    return jnp.asarray(out, dtype=jnp.bfloat16)


def kernel(x, mem, w_in, gm_ln_g, gm_ln_b, gm_ws, gm_bs, cmp_pos_k, cmp_k_w1, cmp_k_w2,
           cmp_pos_v, cmp_v_w1, cmp_v_w2, w_mem_kv, w_out, ln_g, ln_b):
    batch, seq_len, d_model = x.shape
    depth = w_in.shape[0]
    assert seq_len % SEL_KEY_TILE == 0 and seq_len >= WINDOW + Q_BLOCK
    assert SEL_TOPK <= seq_len // SEL_BLOCK <= LANES
    alpha = (2.0 * depth) ** 0.25
    n_tok = batch * seq_len
    n_rows = seq_len // CMP_STRIDE

    cos_t, sin_t, rot_low = _rope_tables(seq_len)
    ovl = _overlap_matrix(n_rows, seq_len // SEL_BLOCK)
    gexp = _gate_expansion()
    tril = jnp.tril(jnp.ones((GM_CHUNK, GM_CHUNK), gm_ws.dtype))
    mk_all, mv_all = _memkv(mem.reshape(batch * mem.shape[1], d_model), w_mem_kv.astype(jnp.bfloat16))
    w_cat_all = _permute_w_in(w_in)
    w_out_all = _permute_w_out(w_out)

    h = x.reshape(n_tok, d_model)
    for l in range(depth):
        gws = (gm_ws[l] * tril[None]).astype(jnp.bfloat16)
        gbs = jnp.repeat(gm_bs[l].T, HEAD_DIM, axis=1)
        glg = gm_ln_g[l].reshape(1, GM_WIDTH)
        glb = gm_ln_b[l].reshape(1, GM_WIDTH)
        (ygm, ymem, q, kc, vc, ksa, vs, kw, vw, nz, gates) = _inproj(
            h, w_cat_all, cos_t, sin_t, rot_low, gws, gbs, glg, glb, mk_all, mv_all,
            layer=l, batch=batch, seq_len=seq_len)

        pos, w1, w2 = _compress_weights(cmp_pos_k[l], cmp_k_w1[l], cmp_k_w2[l],
                                        cmp_pos_v[l], cmp_v_w1[l], cmp_v_w2[l])
        row_shape = (batch, n_rows, CMP_STRIDE * NSA_KV_WIDTH)
        kcmp, vcmp = _compress(kc.reshape(row_shape), vc.reshape(row_shape), pos, w1, w2)

        def per_seq(a):
            return a.reshape(batch, seq_len, a.shape[-1])

        ynsa = _nsa(per_seq(q), per_seq(nz), per_seq(gates), per_seq(ksa), per_seq(vs),
                    per_seq(kw), per_seq(vw), kcmp, vcmp, ovl, gexp)

        h = _outproj(h, ygm, ynsa.reshape(n_tok, NSA_WIDTH), ymem, w_out_all,
                     ln_g.reshape(depth, 1, d_model), ln_b.reshape(depth, 1, d_model),
                     layer=l, alpha=alpha)
    return h.reshape(batch, seq_len, d_model)
```

```python
import functools

import numpy as np
import jax
import jax.numpy as jnp
from jax import lax
from jax.experimental import pallas as pl
from jax.experimental.pallas import tpu as pltpu

HEAD_DIM = 64
GM_GROUPS = 4
GM_WIDTH = GM_GROUPS * HEAD_DIM
GM_CHUNK = 128
NSA_HEADS = 8
NSA_KV_GROUPS = 2
NSA_HPG = NSA_HEADS // NSA_KV_GROUPS
NSA_WIDTH = NSA_HEADS * HEAD_DIM
NSA_KV_WIDTH = NSA_KV_GROUPS * HEAD_DIM
CMP_BLOCK = 32
CMP_STRIDE = 16
CMP_HIDDEN = 128
SEL_BLOCK = 64
SEL_TOPK = 16
N_LOCAL_SEL = 2
WINDOW = 512
Q_BLOCK = 128
MEM_HEADS = 4
MEM_WIDTH = MEM_HEADS * HEAD_DIM
ROPE_THETA = 10000.0
LN_EPS = 1e-5
NEG_INF = -1e30
FORCE_SCORE = 1e4
GATE_COLS = NSA_HEADS * 3

LANES = 128
VMEM_LIMIT_BYTES = 56 * 1024 * 1024

PROJ_ROWS = 1024
PROJ_SUB_ROWS = 256
SEL_KEY_TILE = 512
SOFTMAX_ROWS = 32
REMOVED = -3.0e38

PAIR_HEAD_ORDER = tuple(h for i in range(NSA_HPG) for h in (i, i + NSA_HPG))

C_GU, C_GV, C_GZ = 0, 256, 512
C_Q = 768
C_KC, C_VC, C_KS, C_VS, C_KW, C_VW = 1280, 1408, 1536, 1664, 1792, 1920
C_NZ = 2048
C_MQ, C_MZ = 2560, 2816
C_GATE = 3072
N_COLS = 3200


def _dot(a, b):
    return jnp.dot(a, b, preferred_element_type=jnp.float32)


def _dot_nt(a, b):
    return lax.dot_general(a, b, (((1,), (1,)), ((), ())), preferred_element_type=jnp.float32)


def _gelu(x):
    return 0.5 * x * (1.0 + lax.erf(x * np.float32(np.sqrt(0.5))))


def _silu(x):
    return x * jax.nn.sigmoid(x)


def _lane_iota(shape):
    return lax.broadcasted_iota(jnp.int32, shape, len(shape) - 1)


def _low_half(shape):
    return (_lane_iota(shape) % LANES) < HEAD_DIM


def _tile_lanes(x, reps):
    return jnp.concatenate([x] * reps, axis=-1) if reps > 1 else x


def _memkv_kernel(mem_ref, w_ref, k_ref, v_ref):
    kv = _dot(mem_ref[...].astype(jnp.bfloat16), w_ref[0])
    k_ref[0] = kv[:, :MEM_WIDTH].astype(jnp.bfloat16)
    v_ref[0] = kv[:, MEM_WIDTH:].astype(jnp.bfloat16)


def _memkv(mem2d, w_mem_kv_bf16):
    depth = w_mem_kv_bf16.shape[0]
    rows, d_model = mem2d.shape
    out = jax.ShapeDtypeStruct((depth, rows, MEM_WIDTH), jnp.bfloat16)
    return pl.pallas_call(
        _memkv_kernel,
        grid=(depth,),
        in_specs=[pl.BlockSpec((rows, d_model), lambda l: (0, 0)),
                  pl.BlockSpec((1, d_model, 2 * MEM_WIDTH), lambda l: (l, 0, 0))],
        out_specs=[pl.BlockSpec((1, rows, MEM_WIDTH), lambda l: (l, 0, 0)),
                   pl.BlockSpec((1, rows, MEM_WIDTH), lambda l: (l, 0, 0))],
        out_shape=[out, out],
        name="mem_kv_proj",
    )(mem2d, w_mem_kv_bf16)


def _rope(x, cos, sin_signed, low):
    width = x.shape[-1]
    swapped = jnp.where(low, pltpu.roll(x, width - HEAD_DIM // 2, 1), pltpu.roll(x, HEAD_DIM // 2, 1))
    return x * cos + swapped * sin_signed


def _group_layer_norm(v, g, b, low):
    inv = np.float32(1.0 / HEAD_DIM)
    s_lo = jnp.sum(jnp.where(low, v, 0.0), axis=-1, keepdims=True)
    s_hi = jnp.sum(jnp.where(low, 0.0, v), axis=-1, keepdims=True)
    mu = jnp.where(low, s_lo, s_hi) * inv
    d = v - mu
    d2 = d * d
    q_lo = jnp.sum(jnp.where(low, d2, 0.0), axis=-1, keepdims=True)
    q_hi = jnp.sum(jnp.where(low, 0.0, d2), axis=-1, keepdims=True)
    var = jnp.where(low, q_lo, q_hi) * inv
    return d * lax.rsqrt(var + LN_EPS) * g + b


def _inproj_kernel(x_ref, w_ref, cos_ref, sin_ref, rot_low_ref, gws_ref, gbs_ref, glg_ref, glb_ref,
                   mk_ref, mv_ref,
                   ygm_ref, ymem_ref, q_ref, kc_ref, vc_ref, ksa_ref, vs_ref, kw_ref, vw_ref,
                   nz_ref, gate_ref, stage_ref, *, seq_len):
    subs = [slice(r0, r0 + PROJ_SUB_ROWS) for r0 in range(0, x_ref.shape[0], PROJ_SUB_ROWS)]
    mixers = _project(x_ref, w_ref, subs[0], MIXER_SECTIONS)
    others = _project(x_ref, w_ref, subs[0], OTHER_SECTIONS)
    for j, rs in enumerate(subs):
        upcoming = subs[j + 1] if j + 1 < len(subs) else None
        next_mixers = _project(x_ref, w_ref, upcoming, MIXER_SECTIONS) if upcoming else None
        epilogue = _inproj_rows(rs, pl.program_id(0) * x_ref.shape[0] + rs.start, mixers, others,
                                cos_ref, sin_ref, rot_low_ref, gws_ref, gbs_ref, glg_ref, glb_ref,
                                mk_ref, mv_ref, ygm_ref, ymem_ref, q_ref, kc_ref, vc_ref, ksa_ref,
                                vs_ref, kw_ref, vw_ref, nz_ref, gate_ref, stage_ref, seq_len=seq_len)
        next(epilogue)
        next_others = _project(x_ref, w_ref, upcoming, OTHER_SECTIONS) if upcoming else None
        for _ in epilogue:
            pass
        mixers, others = next_mixers, next_others


MIXER_SECTIONS = ((C_GU, 3 * GM_WIDTH), (C_MQ, 2 * MEM_WIDTH))
OTHER_SECTIONS = ((C_Q, NSA_WIDTH), (C_KC, 6 * NSA_KV_WIDTH), (C_NZ, NSA_WIDTH), (C_GATE, LANES))


def _project(x_ref, w_ref, rs, sections):
    xb = x_ref[rs, :].astype(jnp.bfloat16)
    return tuple(_dot(xb, w_ref[:, c0:c0 + width]) for c0, width in sections)


def _inproj_rows(rs, row0, mixers, others, cos_ref, sin_ref, rot_low_ref, gws_ref, gbs_ref, glg_ref,
                 glb_ref, mk_ref, mv_ref, ygm_ref, ymem_ref, q_ref, kc_ref, vc_ref, ksa_ref, vs_ref,
                 kw_ref, vw_ref, nz_ref, gate_ref, stage_ref, *, seq_len):
    rows = rs.stop - rs.start
    gm, mem = mixers
    qh, kv, nz_raw, gate_raw = others
    low = _low_half((rows, LANES))
    rot_low = rot_low_ref[...] > 0.5
    rot_low = jnp.broadcast_to(rot_low, (rows, LANES))
    cos = cos_ref[rs, :]
    sin = sin_ref[rs, :]

    def slab(h, i):
        return h[:, i * LANES:(i + 1) * LANES]

    qscale = np.float32(HEAD_DIM ** -0.5)

    u = _gelu(gm[:, :GM_WIDTH])
    v = _gelu(gm[:, GM_WIDTH:2 * GM_WIDTH])
    z = gm[:, 2 * GM_WIDTH:]
    spatial = {}
    for pair in range(GM_GROUPS // 2):
        sl = slice(pair * LANES, (pair + 1) * LANES)
        vln = _group_layer_norm(v[:, sl], glg_ref[:, sl], glb_ref[:, sl], low).astype(jnp.bfloat16)
        for c in range(rows // GM_CHUNK):
            cs = slice(c * GM_CHUNK, (c + 1) * GM_CHUNK)
            spatial[pair, c] = (_dot(gws_ref[2 * pair], vln[cs]), _dot(gws_ref[2 * pair + 1], vln[cs]))
    mq = mem[:, :MEM_WIDTH] * qscale
    mz = mem[:, MEM_WIDTH:]
    mem_scores = {}
    for pair in range(MEM_HEADS // 2):
        sl = slice(pair * LANES, (pair + 1) * LANES)
        for keep_low in (True, False):
            qm = jnp.where(low == keep_low, mq[:, sl], 0.0).astype(jnp.bfloat16)
            mem_scores[pair, keep_low] = _dot_nt(qm, mk_ref[0, :, sl])
    yield

    for pair in range(GM_GROUPS // 2):
        sl = slice(pair * LANES, (pair + 1) * LANES)
        for c in range(rows // GM_CHUNK):
            cs = slice(c * GM_CHUNK, (c + 1) * GM_CHUNK)
            out_rows = slice(rs.start + c * GM_CHUNK, rs.start + (c + 1) * GM_CHUNK)
            s_lo, s_hi = spatial[pair, c]
            s = jnp.where(_low_half((GM_CHUNK, LANES)), s_lo, s_hi) + gbs_ref[:, sl]
            ygm_ref[out_rows, sl] = (u[cs, sl] * s * _silu(z[cs, sl])).astype(ygm_ref.dtype)

    qscale2 = np.float32(HEAD_DIM ** -0.5 * np.log2(np.e))
    ones = jnp.ones((rows, LANES), vs_ref.dtype)
    for i in range(NSA_WIDTH // LANES):
        qi = _rope(slab(qh, i), cos, sin, rot_low) * qscale2
        q_ref[rs, i * LANES:(i + 1) * LANES] = qi.astype(q_ref.dtype)
    stage_ref[0, rs, :] = _rope(slab(kv, 0), cos, sin, rot_low)
    stage_ref[1, rs, :] = slab(kv, 1)
    out_rows = slice(rs.start // CMP_STRIDE, rs.stop // CMP_STRIDE)
    for j, dst in enumerate((kc_ref, vc_ref)):
        for l in range(CMP_STRIDE):
            token_l = stage_ref[j, pl.ds(rs.start + l, rows // CMP_STRIDE, stride=CMP_STRIDE), :]
            dst[out_rows, l * LANES:(l + 1) * LANES] = token_l
    ksa_ref[rs, :LANES] = _rope(slab(kv, 2), cos, sin, rot_low).astype(ksa_ref.dtype)
    tok = row0 % seq_len + lax.broadcasted_iota(jnp.int32, (rows, LANES), 0)
    onehot = (tok // SEL_BLOCK) == _lane_iota((rows, LANES))
    ksa_ref[rs, LANES:] = jnp.where(onehot, 1.0, 0.0).astype(ksa_ref.dtype)
    vs_ref[rs, :LANES] = slab(kv, 3).astype(vs_ref.dtype)
    vs_ref[rs, LANES:] = ones
    kw_ref[rs, :] = _rope(slab(kv, 4), cos, sin, rot_low).astype(kw_ref.dtype)
    vw_ref[rs, :LANES] = slab(kv, 5).astype(vw_ref.dtype)
    vw_ref[rs, LANES:] = ones
    nz_ref[rs, :] = _silu(nz_raw)
    gate_ref[rs, :] = jax.nn.sigmoid(gate_raw)

    for pair in range(MEM_HEADS // 2):
        sl = slice(pair * LANES, (pair + 1) * LANES)
        vp = mv_ref[0, :, sl]
        outs = []
        for keep_low in (True, False):
            s = mem_scores[pair, keep_low]
            e = jnp.exp(s - jnp.max(s, axis=-1, keepdims=True))
            p = e / jnp.sum(e, axis=-1, keepdims=True)
            outs.append(_dot(p.astype(jnp.bfloat16), vp))
        o = jnp.where(low, outs[0], outs[1])
        ymem_ref[rs, sl] = (o * _silu(mz[:, sl])).astype(ymem_ref.dtype)


def _inproj(x2d, w_cat, cos_t, sin_t, rot_low, gws, gbs, glg, glb, mk, mv, *, layer, batch, seq_len):
    n, d_model = x2d.shape
    rows = PROJ_ROWS
    steps_per_seq = seq_len // rows
    mem_len = mk.shape[1] // batch

    def tok_spec(width):
        return pl.BlockSpec((rows, width), lambda i: (i, 0))

    def const_spec(shape):
        return pl.BlockSpec(shape, lambda i: (0,) * len(shape))

    def layer_spec(shape):
        return pl.BlockSpec((None,) + shape[1:], lambda i: (layer,) + (0,) * (len(shape) - 1))

    tab_spec = pl.BlockSpec((rows, LANES), lambda i: (i % steps_per_seq, 0))
    mem_spec = pl.BlockSpec((None, 1, mem_len, MEM_WIDTH), lambda i: (layer, i // steps_per_seq, 0, 0))
    bf16, f32 = jnp.bfloat16, jnp.float32
    outs = [(1, GM_WIDTH, bf16), (1, MEM_WIDTH, bf16), (1, NSA_WIDTH, bf16),
            (CMP_STRIDE, CMP_STRIDE * LANES, f32), (CMP_STRIDE, CMP_STRIDE * LANES, f32),
            (1, 2 * LANES, bf16), (1, 2 * LANES, bf16), (1, LANES, bf16), (1, 2 * LANES, bf16),
            (1, NSA_WIDTH, f32), (1, LANES, f32)]
    return pl.pallas_call(
        functools.partial(_inproj_kernel, seq_len=seq_len),
        grid=(n // rows,),
        in_specs=[tok_spec(d_model), layer_spec(w_cat.shape), tab_spec, tab_spec,
                  const_spec(rot_low.shape), const_spec(gws.shape), const_spec(gbs.shape),
                  const_spec(glg.shape), const_spec(glb.shape), mem_spec, mem_spec],
        out_specs=[pl.BlockSpec((rows // d, w), lambda i: (i, 0)) for d, w, _ in outs],
        out_shape=[jax.ShapeDtypeStruct((n // d, w), dt) for d, w, dt in outs],
        scratch_shapes=[pltpu.VMEM((2, rows, LANES), f32)],
        compiler_params=pltpu.CompilerParams(dimension_semantics=("arbitrary",),
                                             vmem_limit_bytes=VMEM_LIMIT_BYTES),
        name="in_proj_mixers",
    )(x2d, w_cat, cos_t, sin_t, rot_low, gws, gbs, glg, glb,
      mk.reshape(-1, batch, mem_len, MEM_WIDTH), mv.reshape(-1, batch, mem_len, MEM_WIDTH))


def _compress_kernel(k_ref, v_ref, pos_ref, w1_ref, w2_ref, kcmp_ref, vcmp_ref):
    n_rows = k_ref.shape[1]
    for idx, (src, dst) in enumerate(((k_ref, kcmp_ref), (v_ref, vcmp_ref))):
        xr = src[0]
        top = _dot((xr + pos_ref[idx, 0:1]).astype(jnp.bfloat16), w1_ref[idx, 0])
        bot = _dot((xr + pos_ref[idx, 1:2]).astype(jnp.bfloat16), w1_ref[idx, 1])
        hidden = top + pltpu.roll(bot, n_rows - 1, 0)
        act = jax.nn.gelu(hidden, approximate=True)
        dst[0, :, :LANES] = _dot(act.astype(jnp.bfloat16), w2_ref[idx]).astype(dst.dtype)
    vcmp_ref[0, :, LANES:] = jnp.ones((n_rows, LANES), vcmp_ref.dtype)


def _compress(kc_rows, vc_rows, pos, w1, w2):
    batch, n_rows, width = kc_rows.shape
    row_spec = pl.BlockSpec((1, n_rows, width), lambda b: (b, 0, 0))
    def out_spec(width):
        return pl.BlockSpec((1, n_rows, width), lambda b: (b, 0, 0))

    def out(width):
        return jax.ShapeDtypeStruct((batch, n_rows, width), jnp.bfloat16)

    return pl.pallas_call(
        _compress_kernel,
        grid=(batch,),
        in_specs=[row_spec, row_spec,
                  pl.BlockSpec(pos.shape, lambda b: (0, 0, 0)),
                  pl.BlockSpec(w1.shape, lambda b: (0, 0, 0, 0)),
                  pl.BlockSpec(w2.shape, lambda b: (0, 0, 0))],
        out_specs=[out_spec(LANES), out_spec(2 * LANES)],
        out_shape=[out(LANES), out(2 * LANES)],
        compiler_params=pltpu.CompilerParams(dimension_semantics=("arbitrary",),
                                             vmem_limit_bytes=VMEM_LIMIT_BYTES),
        name="nsa_compress",
    )(kc_rows, vc_rows, pos, w1, w2)


def _split_bf16(x, parts):
    out = []
    for _ in range(parts):
        hi = x.astype(jnp.bfloat16)
        out.append(hi)
        x = x - hi.astype(jnp.float32)
    return out


def _topk_columns(score):
    row = lax.broadcasted_iota(jnp.int32, score.shape, 0).astype(jnp.float32)
    picked = jnp.zeros(score.shape, jnp.float32)
    for _ in range(SEL_TOPK):
        best = jnp.max(score, axis=0, keepdims=True)
        first = jnp.min(jnp.where(score == best, row, np.float32(score.shape[0])),
                        axis=0, keepdims=True)
        hit = row == first
        picked = jnp.where(hit, 1.0, picked)
        score = jnp.where(hit, REMOVED, score)
    return picked


def _nsa_kernel(q_ref, qnext_ref, nz_ref, gate_ref, ksa_ref, vs_ref, kw_ref, vw_ref, kcmp_ref,
                vcmp_ref, ovl_ref, gexp_ref, out_ref, qa_ref, sc_ref, sw_ref, pc_ref, pw_ref, bc_ref,
                bw_ref, m_ref, acc_ref, oc_ref, ow_ref, sa_ref, sb_ref, mc_ref, mw_ref, qn_ref,
                selb_ref, ocn_ref, gx_ref, *, seq_len):
    bi = pl.program_id(1)
    start = bi * Q_BLOCK
    n_sel = seq_len // SEL_BLOCK
    n_cmp = kcmp_ref.shape[1]
    span = WINDOW + Q_BLOCK
    tk = SEL_KEY_TILE
    bf16 = jnp.bfloat16

    def head_rows(r):
        return slice(r * Q_BLOCK, (r + 1) * Q_BLOCK)

    low = _low_half((Q_BLOCK, LANES))
    chunk = SOFTMAX_ROWS
    n_chunks = Q_BLOCK // chunk

    def stack_queries(src_ref, dst_ref):
        for i in range(NSA_HPG):
            qi = src_ref[0, :, i * LANES:(i + 1) * LANES]
            zero = jnp.zeros_like(qi)
            dst_ref[head_rows(2 * i), :LANES] = jnp.where(low, qi, zero)
            dst_ref[head_rows(2 * i + 1), :LANES] = jnp.where(low, zero, qi)

    def compressed_scores(qs_ref, blk_start):
        t_b = blk_start + lax.broadcasted_iota(jnp.int32, (Q_BLOCK, 1), 0)
        c_end = lax.broadcasted_iota(jnp.int32, (1, n_cmp), 1) * CMP_STRIDE + (CMP_BLOCK - 1)
        bc_ref[...] = jnp.where(c_end <= t_b, 0.0, NEG_INF)
        sc_ref[...] = _dot_nt(qs_ref[:, :LANES], kcmp_ref[0])

    def masked_exp(s_ref, b_ref, m_ref_, p_ref):
        width_tiles = s_ref.shape[1] // LANES
        for r in range(NSA_HEADS):
            for c in range(n_chunks):
                crow = slice(c * chunk, (c + 1) * chunk)
                rows = slice(r * Q_BLOCK + c * chunk, r * Q_BLOCK + (c + 1) * chunk)
                row_max = jnp.max(s_ref[rows, :] + b_ref[crow, :], axis=-1, keepdims=True)
                m_ref_[rows, :] = jnp.broadcast_to(row_max, (chunk, LANES))
        for r in range(NSA_HEADS):
            for c in range(n_chunks):
                crow = slice(c * chunk, (c + 1) * chunk)
                rows = slice(r * Q_BLOCK + c * chunk, r * Q_BLOCK + (c + 1) * chunk)
                s = s_ref[rows, :] + b_ref[crow, :]
                p_ref[rows, :] = jnp.exp2(s - _tile_lanes(m_ref_[rows, :], width_tiles)).astype(bf16)

    def compressed_out(blk_start):
        t_col = blk_start + lax.broadcasted_iota(jnp.int32, (Q_BLOCK, 1), 0)
        seen_col = jnp.concatenate([t_col >= CMP_BLOCK - 1] * NSA_HEADS, axis=0)
        o_c = _dot(pc_ref[...], vcmp_ref[0])
        ocn_ref[...] = jnp.where(seen_col, o_c[:, :LANES] / o_c[:, LANES:], 0.0)

    def candidate_scores(blk_start):
        parts = _dot_nt(ovl_ref[...], pc_ref[...])
        t_lane = blk_start + _lane_iota((1, parts.shape[1])) % Q_BLOCK
        inv = jnp.where(t_lane >= CMP_BLOCK - 1, 1.0 / parts[LANES:LANES + 1, :], 0.0)
        weighted = parts[:LANES, :] * inv
        imp = jnp.concatenate(
            [sum(weighted[:, (2 * i + g) * Q_BLOCK:(2 * i + g + 1) * Q_BLOCK] for i in range(NSA_HPG))
             for g in range(NSA_KV_GROUPS)], axis=1)
        blk = lax.broadcasted_iota(jnp.int32, imp.shape, 0)
        t_blk = (blk_start + _lane_iota((1, imp.shape[1])) % Q_BLOCK) // SEL_BLOCK
        valid = blk <= t_blk
        forced = (blk == 0) | (valid & (blk > t_blk - N_LOCAL_SEL))
        score = jnp.where(forced, FORCE_SCORE, jnp.where(valid, imp, -1.0))
        if n_sel < LANES:
            score = jnp.where(blk < n_sel, score, REMOVED)
        return score

    def select_blocks(score):
        picked = _topk_columns(score).astype(bf16)
        eye = (lax.broadcasted_iota(jnp.int32, (Q_BLOCK, Q_BLOCK), 0)
               == lax.broadcasted_iota(jnp.int32, (Q_BLOCK, Q_BLOCK), 1)).astype(bf16)
        for g in range(NSA_KV_GROUPS):
            picked_q = _dot_nt(eye, picked[:, g * Q_BLOCK:(g + 1) * Q_BLOCK])
            selb_ref[g * Q_BLOCK:(g + 1) * Q_BLOCK, :] = ((1.0 - picked_q) * NEG_INF).astype(bf16)

    t_q = start + lax.broadcasted_iota(jnp.int32, (Q_BLOCK, 1), 0)
    w0 = pl.multiple_of(jnp.maximum(start - WINDOW, 0), Q_BLOCK)

    def window_scores():
        kpos = w0 + lax.broadcasted_iota(jnp.int32, (1, span), 1)
        bw_ref[...] = jnp.where((kpos <= t_q) & (kpos > t_q - WINDOW), 0.0, NEG_INF)
        sw_ref[...] = _dot_nt(qa_ref[:, :LANES], kw_ref[0, pl.ds(w0, span), :])

    def window_out():
        o_w = _dot(pw_ref[...], vw_ref[0, pl.ds(w0, span), :])
        ow_ref[...] = o_w[:, :LANES] / o_w[:, LANES:]

    def scores(tile):
        k0 = pl.multiple_of(tile * tk, tk)
        return _dot_nt(qa_ref[...], ksa_ref[0, pl.ds(k0, tk), :])

    @pl.when(bi == 0)
    def _():
        stack_queries(q_ref, qn_ref)
        compressed_scores(qn_ref, start)
        masked_exp(sc_ref, bc_ref, mc_ref, pc_ref)
        compressed_out(start)
        select_blocks(candidate_scores(start))

    stack_queries(q_ref, qa_ref)
    for r in range(NSA_HEADS):
        g = r % NSA_KV_GROUPS
        qa_ref[head_rows(r), LANES:] = selb_ref[g * Q_BLOCK:(g + 1) * Q_BLOCK, :]
    oc_ref[...] = ocn_ref[...]
    stack_queries(qnext_ref, qn_ref)

    nxt = start + Q_BLOCK
    compressed_scores(qn_ref, nxt)
    window_scores()
    masked_exp(sc_ref, bc_ref, mc_ref, pc_ref)
    next_score = candidate_scores(nxt)
    sa_ref[...] = scores(0)
    masked_exp(sw_ref, bw_ref, mw_ref, pw_ref)
    compressed_out(nxt)
    window_out()
    g_hi, g_lo = _split_bf16(gate_ref[0], 2)
    gx_ref[...] = _dot(g_hi, gexp_ref[...]) + _dot(g_lo, gexp_ref[...])
    select_blocks(next_score)

    m_ref[...] = jnp.full(m_ref.shape, NEG_INF, jnp.float32)
    acc_ref[...] = jnp.zeros(acc_ref.shape, jnp.float32)

    def consume(buf_ref, tile, causal):
        k0 = pl.multiple_of(tile * tk, tk)
        s = buf_ref[...]
        if causal:
            kpos = k0 + lax.broadcasted_iota(jnp.int32, (1, tk), 1)
            tile_bias = jnp.where(kpos <= t_q, 0.0, NEG_INF)
            s = s + jnp.concatenate([tile_bias] * NSA_HEADS, axis=0)
        m_prev = m_ref[...]
        m_next = jnp.maximum(m_prev, jnp.max(s, axis=-1, keepdims=True))
        p = jnp.exp2(s - _tile_lanes(m_next, tk // LANES))
        alpha = jnp.exp2(m_prev - m_next)
        acc_ref[...] = (_tile_lanes(alpha, 2) * acc_ref[...]
                        + _dot(p.astype(bf16), vs_ref[0, pl.ds(k0, tk), :]))
        m_ref[...] = m_next

    diag = start // tk

    def tile_pair(first):
        sb_ref[...] = scores(first + 1)
        consume(sa_ref, first, False)
        sa_ref[...] = scores(first + 2)
        consume(sb_ref, first + 1, False)

    def tile_quad(j, carry):
        tile_pair(4 * j)
        tile_pair(4 * j + 2)
        return carry

    lax.fori_loop(0, diag // 4, tile_quad, 0)

    @pl.when(diag % 4 >= 2)
    def _():
        tile_pair((diag // 4) * 4)

    @pl.when(diag % 2 == 1)
    def _():
        sb_ref[...] = scores(diag)
        consume(sa_ref, diag - 1, False)
        consume(sb_ref, diag, True)

    @pl.when(diag % 2 == 0)
    def _():
        consume(sa_ref, diag, True)

    for i in range(NSA_HPG):
        lo_rows, hi_rows = head_rows(2 * i), head_rows(2 * i + 1)
        o_s = jnp.where(low, acc_ref[lo_rows, :LANES] / acc_ref[lo_rows, LANES:],
                        acc_ref[hi_rows, :LANES] / acc_ref[hi_rows, LANES:])
        branches = (jnp.where(low, oc_ref[lo_rows], oc_ref[hi_rows]), o_s,
                    jnp.where(low, ow_ref[lo_rows], ow_ref[hi_rows]))
        mixed = sum(o * gx_ref[:, (3 * i + c) * LANES:(3 * i + c + 1) * LANES]
                    for c, o in enumerate(branches))
        sl = slice(i * LANES, (i + 1) * LANES)
        out_ref[0, :, sl] = (mixed * nz_ref[0, :, sl]).astype(out_ref.dtype)


def _nsa(q, nz, gates, ksa, vs, kw, vw, kcmp, vcmp, ovl, gexp):
    batch, seq_len, _ = q.shape

    def q_spec(width):
        return pl.BlockSpec((1, Q_BLOCK, width), lambda b, i: (b, i, 0))

    def seq_spec(arr):
        return pl.BlockSpec((1,) + arr.shape[1:], lambda b, i: (b, 0, 0))

    rows = NSA_HEADS * Q_BLOCK
    n_cmp = kcmp.shape[1]
    span = WINDOW + Q_BLOCK
    last = seq_len // Q_BLOCK - 1
    next_q_spec = pl.BlockSpec((1, Q_BLOCK, NSA_WIDTH), lambda b, i: (b, jnp.minimum(i + 1, last), 0))
    return pl.pallas_call(
        functools.partial(_nsa_kernel, seq_len=seq_len),
        grid=(batch, seq_len // Q_BLOCK),
        in_specs=[q_spec(NSA_WIDTH), next_q_spec, q_spec(NSA_WIDTH), q_spec(LANES),
                  seq_spec(ksa), seq_spec(vs), seq_spec(kw), seq_spec(vw),
                  seq_spec(kcmp), seq_spec(vcmp),
                  pl.BlockSpec(ovl.shape, lambda b, i: (0, 0)),
                  pl.BlockSpec(gexp.shape, lambda b, i: (0, 0))],
        out_specs=q_spec(NSA_WIDTH),
        out_shape=jax.ShapeDtypeStruct((batch, seq_len, NSA_WIDTH), jnp.bfloat16),
        scratch_shapes=[pltpu.VMEM((rows, 2 * LANES), jnp.bfloat16),
                        pltpu.VMEM((rows, n_cmp), jnp.float32),
                        pltpu.VMEM((rows, span), jnp.float32),
                        pltpu.VMEM((rows, n_cmp), jnp.bfloat16),
                        pltpu.VMEM((rows, span), jnp.bfloat16),
                        pltpu.VMEM((Q_BLOCK, n_cmp), jnp.float32),
                        pltpu.VMEM((Q_BLOCK, span), jnp.float32),
                        pltpu.VMEM((rows, LANES), jnp.float32),
                        pltpu.VMEM((rows, 2 * LANES), jnp.float32),
                        pltpu.VMEM((rows, LANES), jnp.float32),
                        pltpu.VMEM((rows, LANES), jnp.float32),
                        pltpu.VMEM((rows, SEL_KEY_TILE), jnp.float32),
                        pltpu.VMEM((rows, SEL_KEY_TILE), jnp.float32),
                        pltpu.VMEM((rows, LANES), jnp.float32),
                        pltpu.VMEM((rows, LANES), jnp.float32),
                        pltpu.VMEM((rows, LANES), jnp.bfloat16),
                        pltpu.VMEM((NSA_KV_GROUPS * Q_BLOCK, LANES), jnp.bfloat16),
                        pltpu.VMEM((rows, LANES), jnp.float32),
                        pltpu.VMEM((Q_BLOCK, gexp.shape[1]), jnp.float32)],
        compiler_params=pltpu.CompilerParams(dimension_semantics=("arbitrary", "arbitrary"),
                                             vmem_limit_bytes=VMEM_LIMIT_BYTES),
        name="nsa_attention",
    )(q, q, nz, gates, ksa, vs, kw, vw, kcmp, vcmp, ovl, gexp)


def _outproj_kernel(x_ref, ygm_ref, ynsa_ref, ymem_ref, w_ref, g_ref, b_ref, o_ref, *, alpha):
    def mix_proj(rs):
        return (_dot(ygm_ref[rs, :], w_ref[:GM_WIDTH])
                + _dot(ynsa_ref[rs, :], w_ref[GM_WIDTH:GM_WIDTH + NSA_WIDTH])
                + _dot(ymem_ref[rs, :], w_ref[GM_WIDTH + NSA_WIDTH:]))

    subs = [slice(r0, r0 + PROJ_SUB_ROWS) for r0 in range(0, x_ref.shape[0], PROJ_SUB_ROWS)]
    y_next = mix_proj(subs[0])
    for j, rs in enumerate(subs):
        y = y_next
        y_next = mix_proj(subs[j + 1]) if j + 1 < len(subs) else None
        r = alpha * x_ref[rs, :] + y
        mu = jnp.mean(r, axis=-1, keepdims=True)
        d = r - mu
        var = jnp.mean(d * d, axis=-1, keepdims=True)
        o_ref[rs, :] = d * lax.rsqrt(var + LN_EPS) * g_ref[...] + b_ref[...]


def _outproj(x2d, ygm, ynsa, ymem, w_out, ln_g, ln_b, *, layer, alpha):
    n, d_model = x2d.shape
    rows = PROJ_ROWS

    def tok_spec(width):
        return pl.BlockSpec((rows, width), lambda i: (i, 0))

    def layer_spec(shape):
        return pl.BlockSpec((None,) + shape[1:], lambda i: (layer, 0, 0))

    return pl.pallas_call(
        functools.partial(_outproj_kernel, alpha=alpha),
        grid=(n // rows,),
        in_specs=[tok_spec(d_model), tok_spec(GM_WIDTH), tok_spec(NSA_WIDTH), tok_spec(MEM_WIDTH),
                  layer_spec(w_out.shape), layer_spec(ln_g.shape), layer_spec(ln_b.shape)],
        out_specs=tok_spec(d_model),
        out_shape=jax.ShapeDtypeStruct((n, d_model), jnp.float32),
        compiler_params=pltpu.CompilerParams(dimension_semantics=("arbitrary",),
                                             vmem_limit_bytes=VMEM_LIMIT_BYTES),
        name="out_proj_layernorm",
    )(x2d, ygm, ynsa, ymem, w_out, ln_g, ln_b)


def _pair_head_slices(w, start, axis):
    return [lax.slice_in_dim(w, start + h * HEAD_DIM, start + (h + 1) * HEAD_DIM, axis=axis)
            for h in PAIR_HEAD_ORDER]


def _permute_w_in(w):
    w = w.astype(jnp.bfloat16)
    o_gate = 2048
    o_nz = o_gate + GATE_COLS
    o_mq = o_nz + NSA_WIDTH
    pieces = ([w[..., :768]] + _pair_head_slices(w, 768, 2) + [w[..., 1280:2048]]
              + _pair_head_slices(w, o_nz, 2)
              + [w[..., o_mq:], w[..., o_gate:o_nz],
                 jnp.zeros(w.shape[:2] + (LANES - GATE_COLS,), w.dtype)])
    return jnp.concatenate(pieces, axis=2)


def _permute_w_out(w):
    w = w.astype(jnp.bfloat16)
    pieces = ([w[:, :GM_WIDTH]] + _pair_head_slices(w, GM_WIDTH, 1) + [w[:, GM_WIDTH + NSA_WIDTH:]])
    return jnp.concatenate(pieces, axis=1)


def _rope_tables(seq_len):
    half = HEAD_DIM // 2
    inv_freq = ROPE_THETA ** (-jnp.arange(half, dtype=jnp.float32) * 2.0 / HEAD_DIM)
    ang = jnp.arange(seq_len).astype(jnp.float32)[:, None] * inv_freq[None, :]
    cos, sin = jnp.cos(ang), jnp.sin(ang)
    reps = LANES // HEAD_DIM
    cos_t = jnp.tile(jnp.concatenate([cos, cos], axis=1), (1, reps))
    sin_t = jnp.tile(jnp.concatenate([-sin, sin], axis=1), (1, reps))
    rot_low = ((np.arange(LANES) % HEAD_DIM) < half).astype(np.float32)[None, :]
    return cos_t, sin_t, jnp.asarray(rot_low)


def _compress_weights(pos_k, w1_k, w2_k, pos_v, w1_v, w2_v):
    half = CMP_BLOCK // 2
    eye = jnp.eye(NSA_KV_GROUPS, dtype=w1_k.dtype)

    def expand_w1(w1):
        w = w1.reshape(2, half, HEAD_DIM, CMP_HIDDEN)
        w = jnp.einsum('aldh,gk->algdkh', w, eye)
        return w.reshape(2, half * NSA_KV_WIDTH, NSA_KV_GROUPS * CMP_HIDDEN)

    def expand_w2(w2):
        w = jnp.einsum('hd,gk->ghkd', w2, eye)
        return w.reshape(NSA_KV_GROUPS * CMP_HIDDEN, NSA_KV_WIDTH)

    def expand_pos(pos):
        p = pos.reshape(2, half, 1, HEAD_DIM)
        p = jnp.broadcast_to(p, (2, half, NSA_KV_GROUPS, HEAD_DIM))
        return p.reshape(2, half * NSA_KV_WIDTH)

    pos = jnp.stack([expand_pos(pos_k), expand_pos(pos_v)])
    w1 = jnp.stack([expand_w1(w1_k), expand_w1(w1_v)]).astype(jnp.bfloat16)
    w2 = jnp.stack([expand_w2(w2_k), expand_w2(w2_v)]).astype(jnp.bfloat16)
    return pos, w1, w2


def _gate_expansion():
    out = np.zeros((LANES, NSA_HPG * 3 * LANES), np.float32)
    for i in range(NSA_HPG):
        for c in range(3):
            base = (3 * i + c) * LANES
            out[3 * i + c, base:base + HEAD_DIM] = 1.0
            out[3 * (i + NSA_HPG) + c, base + HEAD_DIM:base + LANES] = 1.0
    return jnp.asarray(out, dtype=jnp.bfloat16)


def _overlap_matrix(n_rows, n_sel):
    c_start = np.arange(n_rows) * CMP_STRIDE
    s_start = np.arange(LANES) * SEL_BLOCK
    ovl = ((c_start[:, None] < s_start[None, :] + SEL_BLOCK)
           & (c_start[:, None] + CMP_BLOCK > s_start[None, :])
           & (np.arange(LANES)[None, :] < n_sel))
    out = np.zeros((LANES + 16, n_rows), np.float32)
    out[:LANES] = ovl.T
    out[LANES] = 1.0
    return jnp.asarray(out, dtype=jnp.bfloat16)


def kernel(x, mem, w_in, gm_ln_g, gm_ln_b, gm_ws, gm_bs, cmp_pos_k, cmp_k_w1, cmp_k_w2,
           cmp_pos_v, cmp_v_w1, cmp_v_w2, w_mem_kv, w_out, ln_g, ln_b):
    batch, seq_len, d_model = x.shape
    depth = w_in.shape[0]
    assert seq_len % SEL_KEY_TILE == 0 and seq_len >= WINDOW + Q_BLOCK
    assert SEL_TOPK <= seq_len // SEL_BLOCK <= LANES
    alpha = (2.0 * depth) ** 0.25
    n_tok = batch * seq_len
    n_rows = seq_len // CMP_STRIDE

    cos_t, sin_t, rot_low = _rope_tables(seq_len)
    ovl = _overlap_matrix(n_rows, seq_len // SEL_BLOCK)
    gexp = _gate_expansion()
    tril = jnp.tril(jnp.ones((GM_CHUNK, GM_CHUNK), gm_ws.dtype))
    mk_all, mv_all = _memkv(mem.reshape(batch * mem.shape[1], d_model), w_mem_kv.astype(jnp.bfloat16))
    w_cat_all = _permute_w_in(w_in)
    w_out_all = _permute_w_out(w_out)

    h = x.reshape(n_tok, d_model)
    for l in range(depth):
        gws = (gm_ws[l] * tril[None]).astype(jnp.bfloat16)
        gbs = jnp.repeat(gm_bs[l].T, HEAD_DIM, axis=1)
        glg = gm_ln_g[l].reshape(1, GM_WIDTH)
        glb = gm_ln_b[l].reshape(1, GM_WIDTH)
        (ygm, ymem, q, kc, vc, ksa, vs, kw, vw, nz, gates) = _inproj(
            h, w_cat_all, cos_t, sin_t, rot_low, gws, gbs, glg, glb, mk_all, mv_all,
            layer=l, batch=batch, seq_len=seq_len)

        pos, w1, w2 = _compress_weights(cmp_pos_k[l], cmp_k_w1[l], cmp_k_w2[l],
                                        cmp_pos_v[l], cmp_v_w1[l], cmp_v_w2[l])
        row_shape = (batch, n_rows, CMP_STRIDE * NSA_KV_WIDTH)
        kcmp, vcmp = _compress(kc.reshape(row_shape), vc.reshape(row_shape), pos, w1, w2)

        def per_seq(a):
            return a.reshape(batch, seq_len, a.shape[-1])

        ynsa = _nsa(per_seq(q), per_seq(nz), per_seq(gates), per_seq(ksa), per_seq(vs),
                    per_seq(kw), per_seq(vw), kcmp, vcmp, ovl, gexp)

        h = _outproj(h, ygm, ynsa.reshape(n_tok, NSA_WIDTH), ymem, w_out_all,
                     ln_g.reshape(depth, 1, d_model), ln_b.reshape(depth, 1, d_model),
                     layer=l, alpha=alpha)
    return h.reshape(batch, seq_len, d_model)
```

```python
import functools

import numpy as np
import jax
import jax.numpy as jnp
from jax import lax
from jax.experimental import pallas as pl
from jax.experimental.pallas import tpu as pltpu

HEAD_DIM = 64
GM_GROUPS = 4
GM_WIDTH = GM_GROUPS * HEAD_DIM
GM_CHUNK = 128
NSA_HEADS = 8
NSA_KV_GROUPS = 2
NSA_HPG = NSA_HEADS // NSA_KV_GROUPS
NSA_WIDTH = NSA_HEADS * HEAD_DIM
NSA_KV_WIDTH = NSA_KV_GROUPS * HEAD_DIM
CMP_BLOCK = 32
CMP_STRIDE = 16
CMP_HIDDEN = 128
SEL_BLOCK = 64
SEL_TOPK = 16
N_LOCAL_SEL = 2
WINDOW = 512
Q_BLOCK = 128
MEM_HEADS = 4
MEM_WIDTH = MEM_HEADS * HEAD_DIM
ROPE_THETA = 10000.0
LN_EPS = 1e-5
NEG_INF = -1e30
FORCE_SCORE = 1e4
GATE_COLS = NSA_HEADS * 3

LANES = 128
VMEM_LIMIT_BYTES = 56 * 1024 * 1024

PROJ_ROWS = 1024
PROJ_SUB_ROWS = 256
SEL_KEY_TILE = 512
SOFTMAX_ROWS = 32
REMOVED = -3.0e38

PAIR_HEAD_ORDER = tuple(h for i in range(NSA_HPG) for h in (i, i + NSA_HPG))

C_GU, C_GV, C_GZ = 0, 256, 512
C_Q = 768
C_KC, C_VC, C_KS, C_VS, C_KW, C_VW = 1280, 1408, 1536, 1664, 1792, 1920
C_NZ = 2048
C_MQ, C_MZ = 2560, 2816
C_GATE = 3072
N_COLS = 3200


def _dot(a, b):
    return jnp.dot(a, b, preferred_element_type=jnp.float32)


def _dot_nt(a, b):
    return lax.dot_general(a, b, (((1,), (1,)), ((), ())), preferred_element_type=jnp.float32)


def _gelu(x):
    return 0.5 * x * (1.0 + lax.erf(x * np.float32(np.sqrt(0.5))))


def _silu(x):
    return x * jax.nn.sigmoid(x)


def _lane_iota(shape):
    return lax.broadcasted_iota(jnp.int32, shape, len(shape) - 1)


def _low_half(shape):
    return (_lane_iota(shape) % LANES) < HEAD_DIM


def _tile_lanes(x, reps):
    return jnp.concatenate([x] * reps, axis=-1) if reps > 1 else x


def _memkv_kernel(mem_ref, w_ref, k_ref, v_ref):
    kv = _dot(mem_ref[...].astype(jnp.bfloat16), w_ref[0])
    k_ref[0] = kv[:, :MEM_WIDTH].astype(jnp.bfloat16)
    v_ref[0] = kv[:, MEM_WIDTH:].astype(jnp.bfloat16)


def _memkv(mem2d, w_mem_kv_bf16):
    depth = w_mem_kv_bf16.shape[0]
    rows, d_model = mem2d.shape
    out = jax.ShapeDtypeStruct((depth, rows, MEM_WIDTH), jnp.bfloat16)
    return pl.pallas_call(
        _memkv_kernel,
        grid=(depth,),
        in_specs=[pl.BlockSpec((rows, d_model), lambda l: (0, 0)),
                  pl.BlockSpec((1, d_model, 2 * MEM_WIDTH), lambda l: (l, 0, 0))],
        out_specs=[pl.BlockSpec((1, rows, MEM_WIDTH), lambda l: (l, 0, 0)),
                   pl.BlockSpec((1, rows, MEM_WIDTH), lambda l: (l, 0, 0))],
        out_shape=[out, out],
        name="mem_kv_proj",
    )(mem2d, w_mem_kv_bf16)


def _rope(x, cos, sin_signed, low):
    width = x.shape[-1]
    swapped = jnp.where(low, pltpu.roll(x, width - HEAD_DIM // 2, 1), pltpu.roll(x, HEAD_DIM // 2, 1))
    return x * cos + swapped * sin_signed


def _group_layer_norm(v, g, b, low):
    inv = np.float32(1.0 / HEAD_DIM)
    s_lo = jnp.sum(jnp.where(low, v, 0.0), axis=-1, keepdims=True)
    s_hi = jnp.sum(jnp.where(low, 0.0, v), axis=-1, keepdims=True)
    mu = jnp.where(low, s_lo, s_hi) * inv
    d = v - mu
    d2 = d * d
    q_lo = jnp.sum(jnp.where(low, d2, 0.0), axis=-1, keepdims=True)
    q_hi = jnp.sum(jnp.where(low, 0.0, d2), axis=-1, keepdims=True)
    var = jnp.where(low, q_lo, q_hi) * inv
    return d * lax.rsqrt(var + LN_EPS) * g + b


def _inproj_kernel(x_ref, w_ref, cos_ref, sin_ref, rot_low_ref, gws_ref, gbs_ref, glg_ref, glb_ref,
                   mk_ref, mv_ref,
                   ygm_ref, ymem_ref, q_ref, kc_ref, vc_ref, ksa_ref, vs_ref, kw_ref, vw_ref,
                   nz_ref, gate_ref, stage_ref, *, seq_len):
    subs = [slice(r0, r0 + PROJ_SUB_ROWS) for r0 in range(0, x_ref.shape[0], PROJ_SUB_ROWS)]
    mixers = _project(x_ref, w_ref, subs[0], MIXER_SECTIONS)
    others = _project(x_ref, w_ref, subs[0], OTHER_SECTIONS)
    for j, rs in enumerate(subs):
        upcoming = subs[j + 1] if j + 1 < len(subs) else None
        next_mixers = _project(x_ref, w_ref, upcoming, MIXER_SECTIONS) if upcoming else None
        epilogue = _inproj_rows(rs, pl.program_id(0) * x_ref.shape[0] + rs.start, mixers, others,
                                cos_ref, sin_ref, rot_low_ref, gws_ref, gbs_ref, glg_ref, glb_ref,
                                mk_ref, mv_ref, ygm_ref, ymem_ref, q_ref, kc_ref, vc_ref, ksa_ref,
                                vs_ref, kw_ref, vw_ref, nz_ref, gate_ref, stage_ref, seq_len=seq_len)
        next(epilogue)
        next_others = _project(x_ref, w_ref, upcoming, OTHER_SECTIONS) if upcoming else None
        for _ in epilogue:
            pass
        mixers, others = next_mixers, next_others


MIXER_SECTIONS = ((C_GU, 3 * GM_WIDTH), (C_MQ, 2 * MEM_WIDTH))
OTHER_SECTIONS = ((C_Q, NSA_WIDTH), (C_KC, 6 * NSA_KV_WIDTH), (C_NZ, NSA_WIDTH), (C_GATE, LANES))


def _project(x_ref, w_ref, rs, sections):
    xb = x_ref[rs, :].astype(jnp.bfloat16)
    return tuple(_dot(xb, w_ref[:, c0:c0 + width]) for c0, width in sections)


def _inproj_rows(rs, row0, mixers, others, cos_ref, sin_ref, rot_low_ref, gws_ref, gbs_ref, glg_ref,
                 glb_ref, mk_ref, mv_ref, ygm_ref, ymem_ref, q_ref, kc_ref, vc_ref, ksa_ref, vs_ref,
                 kw_ref, vw_ref, nz_ref, gate_ref, stage_ref, *, seq_len):
    rows = rs.stop - rs.start
    gm, mem = mixers
    qh, kv, nz_raw, gate_raw = others
    low = _low_half((rows, LANES))
    rot_low = rot_low_ref[...] > 0.5
    rot_low = jnp.broadcast_to(rot_low, (rows, LANES))
    cos = cos_ref[rs, :]
    sin = sin_ref[rs, :]

    def slab(h, i):
        return h[:, i * LANES:(i + 1) * LANES]

    qscale = np.float32(HEAD_DIM ** -0.5)

    u = _gelu(gm[:, :GM_WIDTH])
    v = _gelu(gm[:, GM_WIDTH:2 * GM_WIDTH])
    z = gm[:, 2 * GM_WIDTH:]
    spatial = {}
    for pair in range(GM_GROUPS // 2):
        sl = slice(pair * LANES, (pair + 1) * LANES)
        vln = _group_layer_norm(v[:, sl], glg_ref[:, sl], glb_ref[:, sl], low).astype(jnp.bfloat16)
        for c in range(rows // GM_CHUNK):
            cs = slice(c * GM_CHUNK, (c + 1) * GM_CHUNK)
            spatial[pair, c] = (_dot(gws_ref[2 * pair], vln[cs]), _dot(gws_ref[2 * pair + 1], vln[cs]))
    mq = mem[:, :MEM_WIDTH] * qscale
    mz = mem[:, MEM_WIDTH:]
    mem_scores = {}
    for pair in range(MEM_HEADS // 2):
        sl = slice(pair * LANES, (pair + 1) * LANES)
        for keep_low in (True, False):
            qm = jnp.where(low == keep_low, mq[:, sl], 0.0).astype(jnp.bfloat16)
            mem_scores[pair, keep_low] = _dot_nt(qm, mk_ref[0, :, sl])
    yield

    for pair in range(GM_GROUPS // 2):
        sl = slice(pair * LANES, (pair + 1) * LANES)
        for c in range(rows // GM_CHUNK):
            cs = slice(c * GM_CHUNK, (c + 1) * GM_CHUNK)
            out_rows = slice(rs.start + c * GM_CHUNK, rs.start + (c + 1) * GM_CHUNK)
            s_lo, s_hi = spatial[pair, c]
            s = jnp.where(_low_half((GM_CHUNK, LANES)), s_lo, s_hi) + gbs_ref[:, sl]
            ygm_ref[out_rows, sl] = (u[cs, sl] * s * _silu(z[cs, sl])).astype(ygm_ref.dtype)

    qscale2 = np.float32(HEAD_DIM ** -0.5 * np.log2(np.e))
    ones = jnp.ones((rows, LANES), vs_ref.dtype)
    for i in range(NSA_WIDTH // LANES):
        qi = _rope(slab(qh, i), cos, sin, rot_low) * qscale2
        q_ref[rs, i * LANES:(i + 1) * LANES] = qi.astype(q_ref.dtype)
    stage_ref[0, rs, :] = _rope(slab(kv, 0), cos, sin, rot_low)
    stage_ref[1, rs, :] = slab(kv, 1)
    out_rows = slice(rs.start // CMP_STRIDE, rs.stop // CMP_STRIDE)
    for j, dst in enumerate((kc_ref, vc_ref)):
        for l in range(CMP_STRIDE):
            token_l = stage_ref[j, pl.ds(rs.start + l, rows // CMP_STRIDE, stride=CMP_STRIDE), :]
            dst[out_rows, l * LANES:(l + 1) * LANES] = token_l
    ksa_ref[rs, :LANES] = _rope(slab(kv, 2), cos, sin, rot_low).astype(ksa_ref.dtype)
    tok = row0 % seq_len + lax.broadcasted_iota(jnp.int32, (rows, LANES), 0)
    onehot = (tok // SEL_BLOCK) == _lane_iota((rows, LANES))
    ksa_ref[rs, LANES:] = jnp.where(onehot, 1.0, 0.0).astype(ksa_ref.dtype)
    vs_ref[rs, :LANES] = slab(kv, 3).astype(vs_ref.dtype)
    vs_ref[rs, LANES:] = ones
    kw_ref[rs, :] = _rope(slab(kv, 4), cos, sin, rot_low).astype(kw_ref.dtype)
    vw_ref[rs, :LANES] = slab(kv, 5).astype(vw_ref.dtype)
    vw_ref[rs, LANES:] = ones
    nz_ref[rs, :] = _silu(nz_raw)
    gate_ref[rs, :] = jax.nn.sigmoid(gate_raw)

    for pair in range(MEM_HEADS // 2):
        sl = slice(pair * LANES, (pair + 1) * LANES)
        vp = mv_ref[0, :, sl]
        outs = []
        for keep_low in (True, False):
            s = mem_scores[pair, keep_low]
            e = jnp.exp(s - jnp.max(s, axis=-1, keepdims=True))
            p = e / jnp.sum(e, axis=-1, keepdims=True)
            outs.append(_dot(p.astype(jnp.bfloat16), vp))
        o = jnp.where(low, outs[0], outs[1])
        ymem_ref[rs, sl] = (o * _silu(mz[:, sl])).astype(ymem_ref.dtype)


def _inproj(x2d, w_cat, cos_t, sin_t, rot_low, gws, gbs, glg, glb, mk, mv, *, layer, batch, seq_len):
    n, d_model = x2d.shape
    rows = PROJ_ROWS
    steps_per_seq = seq_len // rows
    mem_len = mk.shape[1] // batch

    def tok_spec(width):
        return pl.BlockSpec((rows, width), lambda i: (i, 0))

    def const_spec(shape):
        return pl.BlockSpec(shape, lambda i: (0,) * len(shape))

    def layer_spec(shape):
        return pl.BlockSpec((None,) + shape[1:], lambda i: (layer,) + (0,) * (len(shape) - 1))

    tab_spec = pl.BlockSpec((rows, LANES), lambda i: (i % steps_per_seq, 0))
    mem_spec = pl.BlockSpec((None, 1, mem_len, MEM_WIDTH), lambda i: (layer, i // steps_per_seq, 0, 0))
    bf16, f32 = jnp.bfloat16, jnp.float32
    outs = [(1, GM_WIDTH, bf16), (1, MEM_WIDTH, bf16), (1, NSA_WIDTH, bf16),
            (CMP_STRIDE, CMP_STRIDE * LANES, f32), (CMP_STRIDE, CMP_STRIDE * LANES, f32),
            (1, 2 * LANES, bf16), (1, 2 * LANES, bf16), (1, LANES, bf16), (1, 2 * LANES, bf16),
            (1, NSA_WIDTH, f32), (1, LANES, f32)]
    return pl.pallas_call(
        functools.partial(_inproj_kernel, seq_len=seq_len),
        grid=(n // rows,),
        in_specs=[tok_spec(d_model), layer_spec(w_cat.shape), tab_spec, tab_spec,
                  const_spec(rot_low.shape), const_spec(gws.shape), const_spec(gbs.shape),
                  const_spec(glg.shape), const_spec(glb.shape), mem_spec, mem_spec],
        out_specs=[pl.BlockSpec((rows // d, w), lambda i: (i, 0)) for d, w, _ in outs],
        out_shape=[jax.ShapeDtypeStruct((n // d, w), dt) for d, w, dt in outs],
        scratch_shapes=[pltpu.VMEM((2, rows, LANES), f32)],
        compiler_params=pltpu.CompilerParams(dimension_semantics=("arbitrary",),
                                             vmem_limit_bytes=VMEM_LIMIT_BYTES),
        name="in_proj_mixers",
    )(x2d, w_cat, cos_t, sin_t, rot_low, gws, gbs, glg, glb,
      mk.reshape(-1, batch, mem_len, MEM_WIDTH), mv.reshape(-1, batch, mem_len, MEM_WIDTH))


def _compress_kernel(k_ref, v_ref, pos_ref, w1_ref, w2_ref, kcmp_ref, vcmp_ref):
    n_rows = k_ref.shape[1]
    for idx, (src, dst) in enumerate(((k_ref, kcmp_ref), (v_ref, vcmp_ref))):
        xr = src[0]
        top = _dot((xr + pos_ref[idx, 0:1]).astype(jnp.bfloat16), w1_ref[idx, 0])
        bot = _dot((xr + pos_ref[idx, 1:2]).astype(jnp.bfloat16), w1_ref[idx, 1])
        hidden = top + pltpu.roll(bot, n_rows - 1, 0)
        act = jax.nn.gelu(hidden, approximate=True)
        dst[0, :, :LANES] = _dot(act.astype(jnp.bfloat16), w2_ref[idx]).astype(dst.dtype)
    vcmp_ref[0, :, LANES:] = jnp.ones((n_rows, LANES), vcmp_ref.dtype)


def _compress(kc_rows, vc_rows, pos, w1, w2):
    batch, n_rows, width = kc_rows.shape
    row_spec = pl.BlockSpec((1, n_rows, width), lambda b: (b, 0, 0))
    def out_spec(width):
        return pl.BlockSpec((1, n_rows, width), lambda b: (b, 0, 0))

    def out(width):
        return jax.ShapeDtypeStruct((batch, n_rows, width), jnp.bfloat16)

    return pl.pallas_call(
        _compress_kernel,
        grid=(batch,),
        in_specs=[row_spec, row_spec,
                  pl.BlockSpec(pos.shape, lambda b: (0, 0, 0)),
                  pl.BlockSpec(w1.shape, lambda b: (0, 0, 0, 0)),
                  pl.BlockSpec(w2.shape, lambda b: (0, 0, 0))],
        out_specs=[out_spec(LANES), out_spec(2 * LANES)],
        out_shape=[out(LANES), out(2 * LANES)],
        compiler_params=pltpu.CompilerParams(dimension_semantics=("arbitrary",),
                                             vmem_limit_bytes=VMEM_LIMIT_BYTES),
        name="nsa_compress",
    )(kc_rows, vc_rows, pos, w1, w2)


def _split_bf16(x, parts):
    out = []
    for _ in range(parts):
        hi = x.astype(jnp.bfloat16)
        out.append(hi)
        x = x - hi.astype(jnp.float32)
    return out


def _topk_columns(score):
    row = lax.broadcasted_iota(jnp.int32, score.shape, 0).astype(jnp.float32)
    picked = jnp.zeros(score.shape, jnp.float32)
    for _ in range(SEL_TOPK):
        best = jnp.max(score, axis=0, keepdims=True)
        first = jnp.min(jnp.where(score == best, row, np.float32(score.shape[0])),
                        axis=0, keepdims=True)
        hit = row == first
        picked = jnp.where(hit, 1.0, picked)
        score = jnp.where(hit, REMOVED, score)
    return picked


def _nsa_kernel(q_ref, qnext_ref, nz_ref, gate_ref, ksa_ref, vs_ref, kw_ref, vw_ref, kcmp_ref,
                vcmp_ref, ovl_ref, gexp_ref, out_ref, qa_ref, sc_ref, sw_ref, pc_ref, pw_ref, bc_ref,
                bw_ref, m_ref, acc_ref, oc_ref, ow_ref, sa_ref, sb_ref, mc_ref, mw_ref, qn_ref,
                selb_ref, ocn_ref, gx_ref, *, seq_len):
    bi = pl.program_id(1)
    start = bi * Q_BLOCK
    n_sel = seq_len // SEL_BLOCK
    n_cmp = kcmp_ref.shape[1]
    span = WINDOW + Q_BLOCK
    tk = SEL_KEY_TILE
    bf16 = jnp.bfloat16

    def head_rows(r):
        return slice(r * Q_BLOCK, (r + 1) * Q_BLOCK)

    low = _low_half((Q_BLOCK, LANES))
    chunk = SOFTMAX_ROWS
    n_chunks = Q_BLOCK // chunk

    def stack_queries(src_ref, dst_ref):
        for i in range(NSA_HPG):
            qi = src_ref[0, :, i * LANES:(i + 1) * LANES]
            zero = jnp.zeros_like(qi)
            dst_ref[head_rows(2 * i), :LANES] = jnp.where(low, qi, zero)
            dst_ref[head_rows(2 * i + 1), :LANES] = jnp.where(low, zero, qi)

    def compressed_scores(qs_ref, blk_start):
        t_b = blk_start + lax.broadcasted_iota(jnp.int32, (Q_BLOCK, 1), 0)
        c_end = lax.broadcasted_iota(jnp.int32, (1, n_cmp), 1) * CMP_STRIDE + (CMP_BLOCK - 1)
        bc_ref[...] = jnp.where(c_end <= t_b, 0.0, NEG_INF)
        sc_ref[...] = _dot_nt(qs_ref[:, :LANES], kcmp_ref[0])

    def masked_exp(s_ref, b_ref, m_ref_, p_ref):
        width_tiles = s_ref.shape[1] // LANES
        for r in range(NSA_HEADS):
            for c in range(n_chunks):
                crow = slice(c * chunk, (c + 1) * chunk)
                rows = slice(r * Q_BLOCK + c * chunk, r * Q_BLOCK + (c + 1) * chunk)
                row_max = jnp.max(s_ref[rows, :] + b_ref[crow, :], axis=-1, keepdims=True)
                m_ref_[rows, :] = jnp.broadcast_to(row_max, (chunk, LANES))
        for r in range(NSA_HEADS):
            for c in range(n_chunks):
                crow = slice(c * chunk, (c + 1) * chunk)
                rows = slice(r * Q_BLOCK + c * chunk, r * Q_BLOCK + (c + 1) * chunk)
                s = s_ref[rows, :] + b_ref[crow, :]
                p_ref[rows, :] = jnp.exp2(s - _tile_lanes(m_ref_[rows, :], width_tiles)).astype(bf16)

    def compressed_out(blk_start):
        t_col = blk_start + lax.broadcasted_iota(jnp.int32, (Q_BLOCK, 1), 0)
        seen_col = jnp.concatenate([t_col >= CMP_BLOCK - 1] * NSA_HEADS, axis=0)
        o_c = _dot(pc_ref[...], vcmp_ref[0])
        ocn_ref[...] = jnp.where(seen_col, o_c[:, :LANES] / o_c[:, LANES:], 0.0)

    def candidate_scores(blk_start):
        parts = _dot_nt(ovl_ref[...], pc_ref[...])
        t_lane = blk_start + _lane_iota((1, parts.shape[1])) % Q_BLOCK
        inv = jnp.where(t_lane >= CMP_BLOCK - 1, 1.0 / parts[LANES:LANES + 1, :], 0.0)
        weighted = parts[:LANES, :] * inv
        imp = jnp.concatenate(
            [sum(weighted[:, (2 * i + g) * Q_BLOCK:(2 * i + g + 1) * Q_BLOCK] for i in range(NSA_HPG))
             for g in range(NSA_KV_GROUPS)], axis=1)
        blk = lax.broadcasted_iota(jnp.int32, imp.shape, 0)
        t_blk = (blk_start + _lane_iota((1, imp.shape[1])) % Q_BLOCK) // SEL_BLOCK
        valid = blk <= t_blk
        forced = (blk == 0) | (valid & (blk > t_blk - N_LOCAL_SEL))
        score = jnp.where(forced, FORCE_SCORE, jnp.where(valid, imp, -1.0))
        if n_sel < LANES:
            score = jnp.where(blk < n_sel, score, REMOVED)
        return score

    def select_blocks(score):
        picked = _topk_columns(score).astype(bf16)
        eye = (lax.broadcasted_iota(jnp.int32, (Q_BLOCK, Q_BLOCK), 0)
               == lax.broadcasted_iota(jnp.int32, (Q_BLOCK, Q_BLOCK), 1)).astype(bf16)
        for g in range(NSA_KV_GROUPS):
            picked_q = _dot_nt(eye, picked[:, g * Q_BLOCK:(g + 1) * Q_BLOCK])
            selb_ref[g * Q_BLOCK:(g + 1) * Q_BLOCK, :] = ((1.0 - picked_q) * NEG_INF).astype(bf16)

    t_q = start + lax.broadcasted_iota(jnp.int32, (Q_BLOCK, 1), 0)
    w0 = pl.multiple_of(jnp.maximum(start - WINDOW, 0), Q_BLOCK)

    def window_scores():
        kpos = w0 + lax.broadcasted_iota(jnp.int32, (1, span), 1)
        bw_ref[...] = jnp.where((kpos <= t_q) & (kpos > t_q - WINDOW), 0.0, NEG_INF)
        sw_ref[...] = _dot_nt(qa_ref[:, :LANES], kw_ref[0, pl.ds(w0, span), :])

    def window_out():
        o_w = _dot(pw_ref[...], vw_ref[0, pl.ds(w0, span), :])
        ow_ref[...] = o_w[:, :LANES] / o_w[:, LANES:]

    def scores(tile):
        k0 = pl.multiple_of(tile * tk, tk)
        return _dot_nt(qa_ref[...], ksa_ref[0, pl.ds(k0, tk), :])

    @pl.when(bi == 0)
    def _():
        stack_queries(q_ref, qn_ref)
        compressed_scores(qn_ref, start)
        masked_exp(sc_ref, bc_ref, mc_ref, pc_ref)
        compressed_out(start)
        select_blocks(candidate_scores(start))

    stack_queries(q_ref, qa_ref)
    for r in range(NSA_HEADS):
        g = r % NSA_KV_GROUPS
        qa_ref[head_rows(r), LANES:] = selb_ref[g * Q_BLOCK:(g + 1) * Q_BLOCK, :]
    oc_ref[...] = ocn_ref[...]
    stack_queries(qnext_ref, qn_ref)

    nxt = start + Q_BLOCK
    compressed_scores(qn_ref, nxt)
    window_scores()
    masked_exp(sc_ref, bc_ref, mc_ref, pc_ref)
    next_score = candidate_scores(nxt)
    sa_ref[...] = scores(0)
    masked_exp(sw_ref, bw_ref, mw_ref, pw_ref)
    compressed_out(nxt)
    window_out()
    g_hi, g_lo = _split_bf16(gate_ref[0], 2)
    gx_ref[...] = _dot(g_hi, gexp_ref[...]) + _dot(g_lo, gexp_ref[...])
    select_blocks(next_score)

    m_ref[...] = jnp.full(m_ref.shape, NEG_INF, jnp.float32)
    acc_ref[...] = jnp.zeros(acc_ref.shape, jnp.float32)

    def consume(buf_ref, tile, causal):
        k0 = pl.multiple_of(tile * tk, tk)
        s = buf_ref[...]
        if causal:
            kpos = k0 + lax.broadcasted_iota(jnp.int32, (1, tk), 1)
            tile_bias = jnp.where(kpos <= t_q, 0.0, NEG_INF)
            s = s + jnp.concatenate([tile_bias] * NSA_HEADS, axis=0)
        m_prev = m_ref[...]
        m_next = jnp.maximum(m_prev, jnp.max(s, axis=-1, keepdims=True))
        p = jnp.exp2(s - _tile_lanes(m_next, tk // LANES))
        alpha = jnp.exp2(m_prev - m_next)
        acc_ref[...] = (_tile_lanes(alpha, 2) * acc_ref[...]
                        + _dot(p.astype(bf16), vs_ref[0, pl.ds(k0, tk), :]))
        m_ref[...] = m_next

    diag = start // tk

    def tile_pair(first):
        sb_ref[...] = scores(first + 1)
        consume(sa_ref, first, False)
        sa_ref[...] = scores(first + 2)
        consume(sb_ref, first + 1, False)

    def tile_quad(j, carry):
        tile_pair(4 * j)
        tile_pair(4 * j + 2)
        return carry

    lax.fori_loop(0, diag // 4, tile_quad, 0)

    @pl.when(diag % 4 >= 2)
    def _():
        tile_pair((diag // 4) * 4)

    @pl.when(diag % 2 == 1)
    def _():
        sb_ref[...] = scores(diag)
        consume(sa_ref, diag - 1, False)
        consume(sb_ref, diag, True)

    @pl.when(diag % 2 == 0)
    def _():
        consume(sa_ref, diag, True)

    for i in range(NSA_HPG):
        lo_rows, hi_rows = head_rows(2 * i), head_rows(2 * i + 1)
        o_s = jnp.where(low, acc_ref[lo_rows, :LANES] / acc_ref[lo_rows, LANES:],
                        acc_ref[hi_rows, :LANES] / acc_ref[hi_rows, LANES:])
        branches = (jnp.where(low, oc_ref[lo_rows], oc_ref[hi_rows]), o_s,
                    jnp.where(low, ow_ref[lo_rows], ow_ref[hi_rows]))
        mixed = sum(o * gx_ref[:, (3 * i + c) * LANES:(3 * i + c + 1) * LANES]
                    for c, o in enumerate(branches))
        sl = slice(i * LANES, (i + 1) * LANES)
        out_ref[0, :, sl] = (mixed * nz_ref[0, :, sl]).astype(out_ref.dtype)


def _nsa(q, nz, gates, ksa, vs, kw, vw, kcmp, vcmp, ovl, gexp):
    batch, seq_len, _ = q.shape

    def q_spec(width):
        return pl.BlockSpec((1, Q_BLOCK, width), lambda b, i: (b, i, 0))

    def seq_spec(arr):
        return pl.BlockSpec((1,) + arr.shape[1:], lambda b, i: (b, 0, 0))

    rows = NSA_HEADS * Q_BLOCK
    n_cmp = kcmp.shape[1]
    span = WINDOW + Q_BLOCK
    last = seq_len // Q_BLOCK - 1
    next_q_spec = pl.BlockSpec((1, Q_BLOCK, NSA_WIDTH), lambda b, i: (b, jnp.minimum(i + 1, last), 0))
    return pl.pallas_call(
        functools.partial(_nsa_kernel, seq_len=seq_len),
        grid=(batch, seq_len // Q_BLOCK),
        in_specs=[q_spec(NSA_WIDTH), next_q_spec, q_spec(NSA_WIDTH), q_spec(LANES),
                  seq_spec(ksa), seq_spec(vs), seq_spec(kw), seq_spec(vw),
                  seq_spec(kcmp), seq_spec(vcmp),
                  pl.BlockSpec(ovl.shape, lambda b, i: (0, 0)),
                  pl.BlockSpec(gexp.shape, lambda b, i: (0, 0))],
        out_specs=q_spec(NSA_WIDTH),
        out_shape=jax.ShapeDtypeStruct((batch, seq_len, NSA_WIDTH), jnp.bfloat16),
        scratch_shapes=[pltpu.VMEM((rows, 2 * LANES), jnp.bfloat16),
                        pltpu.VMEM((rows, n_cmp), jnp.float32),
                        pltpu.VMEM((rows, span), jnp.float32),
                        pltpu.VMEM((rows, n_cmp), jnp.bfloat16),
                        pltpu.VMEM((rows, span), jnp.bfloat16),
                        pltpu.VMEM((Q_BLOCK, n_cmp), jnp.float32),
                        pltpu.VMEM((Q_BLOCK, span), jnp.float32),
                        pltpu.VMEM((rows, LANES), jnp.float32),
                        pltpu.VMEM((rows, 2 * LANES), jnp.float32),
                        pltpu.VMEM((rows, LANES), jnp.float32),
                        pltpu.VMEM((rows, LANES), jnp.float32),
                        pltpu.VMEM((rows, SEL_KEY_TILE), jnp.float32),
                        pltpu.VMEM((rows, SEL_KEY_TILE), jnp.float32),
                        pltpu.VMEM((rows, LANES), jnp.float32),
                        pltpu.VMEM((rows, LANES), jnp.float32),
                        pltpu.VMEM((rows, LANES), jnp.bfloat16),
                        pltpu.VMEM((NSA_KV_GROUPS * Q_BLOCK, LANES), jnp.bfloat16),
                        pltpu.VMEM((rows, LANES), jnp.float32),
                        pltpu.VMEM((Q_BLOCK, gexp.shape[1]), jnp.float32)],
        compiler_params=pltpu.CompilerParams(dimension_semantics=("arbitrary", "arbitrary"),
                                             vmem_limit_bytes=VMEM_LIMIT_BYTES),
        name="nsa_attention",
    )(q, q, nz, gates, ksa, vs, kw, vw, kcmp, vcmp, ovl, gexp)


def _outproj_kernel(x_ref, ygm_ref, ynsa_ref, ymem_ref, w_ref, g_ref, b_ref, o_ref, *, alpha):
    def mix_proj(rs):
        return (_dot(ygm_ref[rs, :], w_ref[:GM_WIDTH])
                + _dot(ynsa_ref[rs, :], w_ref[GM_WIDTH:GM_WIDTH + NSA_WIDTH])
                + _dot(ymem_ref[rs, :], w_ref[GM_WIDTH + NSA_WIDTH:]))

    subs = [slice(r0, r0 + PROJ_SUB_ROWS) for r0 in range(0, x_ref.shape[0], PROJ_SUB_ROWS)]
    y_next = mix_proj(subs[0])
    for j, rs in enumerate(subs):
        y = y_next
        y_next = mix_proj(subs[j + 1]) if j + 1 < len(subs) else None
        r = alpha * x_ref[rs, :] + y
        mu = jnp.mean(r, axis=-1, keepdims=True)
        d = r - mu
        var = jnp.mean(d * d, axis=-1, keepdims=True)
        o_ref[rs, :] = d * lax.rsqrt(var + LN_EPS) * g_ref[...] + b_ref[...]


def _outproj(x2d, ygm, ynsa, ymem, w_out, ln_g, ln_b, *, layer, alpha):
    n, d_model = x2d.shape
    rows = PROJ_ROWS

    def tok_spec(width):
        return pl.BlockSpec((rows, width), lambda i: (i, 0))

    def layer_spec(shape):
        return pl.BlockSpec((None,) + shape[1:], lambda i: (layer, 0, 0))

    return pl.pallas_call(
        functools.partial(_outproj_kernel, alpha=alpha),
        grid=(n // rows,),
        in_specs=[tok_spec(d_model), tok_spec(GM_WIDTH), tok_spec(NSA_WIDTH), tok_spec(MEM_WIDTH),
                  layer_spec(w_out.shape), layer_spec(ln_g.shape), layer_spec(ln_b.shape)],
        out_specs=tok_spec(d_model),
        out_shape=jax.ShapeDtypeStruct((n, d_model), jnp.float32),
        compiler_params=pltpu.CompilerParams(dimension_semantics=("arbitrary",),
                                             vmem_limit_bytes=VMEM_LIMIT_BYTES),
        name="out_proj_layernorm",
    )(x2d, ygm, ynsa, ymem, w_out, ln_g, ln_b)


def _pair_head_slices(w, start, axis):
    return [lax.slice_in_dim(w, start + h * HEAD_DIM, start + (h + 1) * HEAD_DIM, axis=axis)
            for h in PAIR_HEAD_ORDER]


def _permute_w_in(w):
    o_gate = 2048
    o_nz = o_gate + GATE_COLS
    o_mq = o_nz + NSA_WIDTH
    pieces = ([w[..., :768]] + _pair_head_slices(w, 768, 2) + [w[..., 1280:2048]]
              + _pair_head_slices(w, o_nz, 2)
              + [w[..., o_mq:], w[..., o_gate:o_nz],
                 jnp.zeros(w.shape[:2] + (LANES - GATE_COLS,), w.dtype)])
    return jnp.concatenate([p.astype(jnp.bfloat16) for p in pieces], axis=2)


def _permute_w_out(w):
    pieces = ([w[:, :GM_WIDTH]] + _pair_head_slices(w, GM_WIDTH, 1) + [w[:, GM_WIDTH + NSA_WIDTH:]])
    return jnp.concatenate([p.astype(jnp.bfloat16) for p in pieces], axis=1)


def _rope_tables(seq_len):
    half = HEAD_DIM // 2
    inv_freq = ROPE_THETA ** (-jnp.arange(half, dtype=jnp.float32) * 2.0 / HEAD_DIM)
    ang = jnp.arange(seq_len).astype(jnp.float32)[:, None] * inv_freq[None, :]
    cos, sin = jnp.cos(ang), jnp.sin(ang)
    reps = LANES // HEAD_DIM
    cos_t = jnp.tile(jnp.concatenate([cos, cos], axis=1), (1, reps))
    sin_t = jnp.tile(jnp.concatenate([-sin, sin], axis=1), (1, reps))
    rot_low = ((np.arange(LANES) % HEAD_DIM) < half).astype(np.float32)[None, :]
    return cos_t, sin_t, jnp.asarray(rot_low)


def _compress_weights(pos_k, w1_k, w2_k, pos_v, w1_v, w2_v):
    half = CMP_BLOCK // 2

    def block_diag2(w):
        z = jnp.zeros_like(w)
        return jnp.concatenate([jnp.concatenate([w, z], axis=-1),
                                jnp.concatenate([z, w], axis=-1)], axis=-2)

    def expand_w1(w1):
        w = w1.reshape(2, half, HEAD_DIM, CMP_HIDDEN)
        w = block_diag2(w)
        return w.reshape(2, half * NSA_KV_WIDTH, NSA_KV_GROUPS * CMP_HIDDEN)

    def expand_w2(w2):
        return block_diag2(w2)

    def expand_pos(pos):
        p = pos.reshape(2, half, 1, HEAD_DIM)
        p = jnp.broadcast_to(p, (2, half, NSA_KV_GROUPS, HEAD_DIM))
        return p.reshape(2, half * NSA_KV_WIDTH)

    pos = jnp.stack([expand_pos(pos_k), expand_pos(pos_v)])
    w1 = jnp.stack([expand_w1(w1_k), expand_w1(w1_v)]).astype(jnp.bfloat16)
    w2 = jnp.stack([expand_w2(w2_k), expand_w2(w2_v)]).astype(jnp.bfloat16)
    return pos, w1, w2


def _gate_expansion():
    out = np.zeros((LANES, NSA_HPG * 3 * LANES), np.float32)
    for i in range(NSA_HPG):
        for c in range(3):
            base = (3 * i + c) * LANES
            out[3 * i + c, base:base + HEAD_DIM] = 1.0
            out[3 * (i + NSA_HPG) + c, base + HEAD_DIM:base + LANES] = 1.0
    return jnp.asarray(out, dtype=jnp.bfloat16)


def _overlap_matrix(n_rows, n_sel):
    c_start = np.arange(n_rows) * CMP_STRIDE
    s_start = np.arange(LANES) * SEL_BLOCK
    ovl = ((c_start[:, None] < s_start[None, :] + SEL_BLOCK)
           & (c_start[:, None] + CMP_BLOCK > s_start[None, :])
           & (np.arange(LANES)[None, :] < n_sel))
    out = np.zeros((LANES + 16, n_rows), np.float32)
    out[:LANES] = ovl.T
    out[LANES] = 1.0
    return jnp.asarray(out, dtype=jnp.bfloat16)


def kernel(x, mem, w_in, gm_ln_g, gm_ln_b, gm_ws, gm_bs, cmp_pos_k, cmp_k_w1, cmp_k_w2,
           cmp_pos_v, cmp_v_w1, cmp_v_w2, w_mem_kv, w_out, ln_g, ln_b):
    batch, seq_len, d_model = x.shape
    depth = w_in.shape[0]
    assert seq_len % SEL_KEY_TILE == 0 and seq_len >= WINDOW + Q_BLOCK
    assert SEL_TOPK <= seq_len // SEL_BLOCK <= LANES
    alpha = (2.0 * depth) ** 0.25
    n_tok = batch * seq_len
    n_rows = seq_len // CMP_STRIDE

    cos_t, sin_t, rot_low = _rope_tables(seq_len)
    ovl = _overlap_matrix(n_rows, seq_len // SEL_BLOCK)
    gexp = _gate_expansion()
    tril = jnp.tril(jnp.ones((GM_CHUNK, GM_CHUNK), gm_ws.dtype))
    mk_all, mv_all = _memkv(mem.reshape(batch * mem.shape[1], d_model), w_mem_kv.astype(jnp.bfloat16))
    w_cat_all = _permute_w_in(w_in)
    w_out_all = _permute_w_out(w_out)

    h = x.reshape(n_tok, d_model)
    for l in range(depth):
        gws = (gm_ws[l] * tril[None]).astype(jnp.bfloat16)
        gbs = jnp.repeat(gm_bs[l].T, HEAD_DIM, axis=1)
        glg = gm_ln_g[l].reshape(1, GM_WIDTH)
        glb = gm_ln_b[l].reshape(1, GM_WIDTH)
        (ygm, ymem, q, kc, vc, ksa, vs, kw, vw, nz, gates) = _inproj(
            h, w_cat_all, cos_t, sin_t, rot_low, gws, gbs, glg, glb, mk_all, mv_all,
            layer=l, batch=batch, seq_len=seq_len)

        pos, w1, w2 = _compress_weights(cmp_pos_k[l], cmp_k_w1[l], cmp_k_w2[l],
                                        cmp_pos_v[l], cmp_v_w1[l], cmp_v_w2[l])
        row_shape = (batch, n_rows, CMP_STRIDE * NSA_KV_WIDTH)
        kcmp, vcmp = _compress(kc.reshape(row_shape), vc.reshape(row_shape), pos, w1, w2)

        def per_seq(a):
            return a.reshape(batch, seq_len, a.shape[-1])

        ynsa = _nsa(per_seq(q), per_seq(nz), per_seq(gates), per_seq(ksa), per_seq(vs),
                    per_seq(kw), per_seq(vw), kcmp, vcmp, ovl, gexp)

        h = _outproj(h, ygm, ynsa.reshape(n_tok, NSA_WIDTH), ymem, w_out_all,
                     ln_g.reshape(depth, 1, d_model), ln_b.reshape(depth, 1, d_model),
                     layer=l, alpha=alpha)
    return h.reshape(batch, seq_len, d_model)
```

```python
import functools

import numpy as np
import jax
import jax.numpy as jnp
from jax import lax
from jax.experimental import pallas as pl
from jax.experimental.pallas import tpu as pltpu

HEAD_DIM = 64
GM_GROUPS = 4
GM_WIDTH = GM_GROUPS * HEAD_DIM
GM_CHUNK = 128
NSA_HEADS = 8
NSA_KV_GROUPS = 2
NSA_HPG = NSA_HEADS // NSA_KV_GROUPS
NSA_WIDTH = NSA_HEADS * HEAD_DIM
NSA_KV_WIDTH = NSA_KV_GROUPS * HEAD_DIM
CMP_BLOCK = 32
CMP_STRIDE = 16
CMP_HIDDEN = 128
SEL_BLOCK = 64
SEL_TOPK = 16
N_LOCAL_SEL = 2
WINDOW = 512
Q_BLOCK = 128
MEM_HEADS = 4
MEM_WIDTH = MEM_HEADS * HEAD_DIM
ROPE_THETA = 10000.0
LN_EPS = 1e-5
NEG_INF = -1e30
FORCE_SCORE = 1e4
GATE_COLS = NSA_HEADS * 3

LANES = 128
VMEM_LIMIT_BYTES = 56 * 1024 * 1024

PROJ_ROWS = 1024
PROJ_SUB_ROWS = 256
SEL_KEY_TILE = 512
SOFTMAX_ROWS = 32
REMOVED = -3.0e38

PAIR_HEAD_ORDER = tuple(h for i in range(NSA_HPG) for h in (i, i + NSA_HPG))

C_GU, C_GV, C_GZ = 0, 256, 512
C_Q = 768
C_KC, C_VC, C_KS, C_VS, C_KW, C_VW = 1280, 1408, 1536, 1664, 1792, 1920
C_NZ = 2048
C_MQ, C_MZ = 2560, 2816
C_GATE = 3072
N_COLS = 3200


def _dot(a, b):
    return jnp.dot(a, b, preferred_element_type=jnp.float32)


def _dot_nt(a, b):
    return lax.dot_general(a, b, (((1,), (1,)), ((), ())), preferred_element_type=jnp.float32)


def _gelu(x):
    return 0.5 * x * (1.0 + lax.erf(x * np.float32(np.sqrt(0.5))))


def _silu(x):
    return x * jax.nn.sigmoid(x)


def _lane_iota(shape):
    return lax.broadcasted_iota(jnp.int32, shape, len(shape) - 1)


def _low_half(shape):
    return (_lane_iota(shape) % LANES) < HEAD_DIM


def _tile_lanes(x, reps):
    return jnp.concatenate([x] * reps, axis=-1) if reps > 1 else x


def _memkv_kernel(mem_ref, w_ref, k_ref, v_ref):
    kv = _dot(mem_ref[...].astype(jnp.bfloat16), w_ref[0])
    k_ref[0] = kv[:, :MEM_WIDTH].astype(jnp.bfloat16)
    v_ref[0] = kv[:, MEM_WIDTH:].astype(jnp.bfloat16)


def _memkv(mem2d, w_mem_kv_bf16):
    depth = w_mem_kv_bf16.shape[0]
    rows, d_model = mem2d.shape
    out = jax.ShapeDtypeStruct((depth, rows, MEM_WIDTH), jnp.bfloat16)
    return pl.pallas_call(
        _memkv_kernel,
        grid=(depth,),
        in_specs=[pl.BlockSpec((rows, d_model), lambda l: (0, 0)),
                  pl.BlockSpec((1, d_model, 2 * MEM_WIDTH), lambda l: (l, 0, 0))],
        out_specs=[pl.BlockSpec((1, rows, MEM_WIDTH), lambda l: (l, 0, 0)),
                   pl.BlockSpec((1, rows, MEM_WIDTH), lambda l: (l, 0, 0))],
        out_shape=[out, out],
        name="mem_kv_proj",
    )(mem2d, w_mem_kv_bf16)


def _rope(x, cos, sin_signed, low):
    width = x.shape[-1]
    swapped = jnp.where(low, pltpu.roll(x, width - HEAD_DIM // 2, 1), pltpu.roll(x, HEAD_DIM // 2, 1))
    return x * cos + swapped * sin_signed


def _group_layer_norm(v, g, b, low):
    inv = np.float32(1.0 / HEAD_DIM)
    s_lo = jnp.sum(jnp.where(low, v, 0.0), axis=-1, keepdims=True)
    s_hi = jnp.sum(jnp.where(low, 0.0, v), axis=-1, keepdims=True)
    mu = jnp.where(low, s_lo, s_hi) * inv
    d = v - mu
    d2 = d * d
    q_lo = jnp.sum(jnp.where(low, d2, 0.0), axis=-1, keepdims=True)
    q_hi = jnp.sum(jnp.where(low, 0.0, d2), axis=-1, keepdims=True)
    var = jnp.where(low, q_lo, q_hi) * inv
    return d * lax.rsqrt(var + LN_EPS) * g + b


def _inproj_kernel(x_ref, w_ref, cos_ref, sin_ref, rot_low_ref, gws_ref, gbs_ref, glg_ref, glb_ref,
                   mk_ref, mv_ref,
                   ygm_ref, ymem_ref, q_ref, kc_ref, vc_ref, ksa_ref, vs_ref, kw_ref, vw_ref,
                   nz_ref, gate_ref, stage_ref, *, seq_len):
    subs = [slice(r0, r0 + PROJ_SUB_ROWS) for r0 in range(0, x_ref.shape[0], PROJ_SUB_ROWS)]
    mixers = _project(x_ref, w_ref, subs[0], MIXER_SECTIONS)
    others = _project(x_ref, w_ref, subs[0], OTHER_SECTIONS)
    for j, rs in enumerate(subs):
        upcoming = subs[j + 1] if j + 1 < len(subs) else None
        next_mixers = _project(x_ref, w_ref, upcoming, MIXER_SECTIONS) if upcoming else None
        epilogue = _inproj_rows(rs, pl.program_id(0) * x_ref.shape[0] + rs.start, mixers, others,
                                cos_ref, sin_ref, rot_low_ref, gws_ref, gbs_ref, glg_ref, glb_ref,
                                mk_ref, mv_ref, ygm_ref, ymem_ref, q_ref, kc_ref, vc_ref, ksa_ref,
                                vs_ref, kw_ref, vw_ref, nz_ref, gate_ref, stage_ref, seq_len=seq_len)
        next(epilogue)
        next_others = _project(x_ref, w_ref, upcoming, OTHER_SECTIONS) if upcoming else None
        for _ in epilogue:
            pass
        mixers, others = next_mixers, next_others


MIXER_SECTIONS = ((C_GU, 3 * GM_WIDTH), (C_MQ, 2 * MEM_WIDTH))
OTHER_SECTIONS = ((C_Q, NSA_WIDTH), (C_KC, 6 * NSA_KV_WIDTH), (C_NZ, NSA_WIDTH), (C_GATE, LANES))


def _project(x_ref, w_ref, rs, sections):
    xb = x_ref[rs, :].astype(jnp.bfloat16)
    return tuple(_dot(xb, w_ref[:, c0:c0 + width]) for c0, width in sections)


def _inproj_rows(rs, row0, mixers, others, cos_ref, sin_ref, rot_low_ref, gws_ref, gbs_ref, glg_ref,
                 glb_ref, mk_ref, mv_ref, ygm_ref, ymem_ref, q_ref, kc_ref, vc_ref, ksa_ref, vs_ref,
                 kw_ref, vw_ref, nz_ref, gate_ref, stage_ref, *, seq_len):
    rows = rs.stop - rs.start
    gm, mem = mixers
    qh, kv, nz_raw, gate_raw = others
    low = _low_half((rows, LANES))
    rot_low = rot_low_ref[...] > 0.5
    rot_low = jnp.broadcast_to(rot_low, (rows, LANES))
    cos = cos_ref[rs, :]
    sin = sin_ref[rs, :]

    def slab(h, i):
        return h[:, i * LANES:(i + 1) * LANES]

    qscale = np.float32(HEAD_DIM ** -0.5)

    u = _gelu(gm[:, :GM_WIDTH])
    v = _gelu(gm[:, GM_WIDTH:2 * GM_WIDTH])
    z = gm[:, 2 * GM_WIDTH:]
    spatial = {}
    for pair in range(GM_GROUPS // 2):
        sl = slice(pair * LANES, (pair + 1) * LANES)
        vln = _group_layer_norm(v[:, sl], glg_ref[:, sl], glb_ref[:, sl], low).astype(jnp.bfloat16)
        for c in range(rows // GM_CHUNK):
            cs = slice(c * GM_CHUNK, (c + 1) * GM_CHUNK)
            spatial[pair, c] = (_dot(gws_ref[2 * pair], vln[cs]), _dot(gws_ref[2 * pair + 1], vln[cs]))
    mq = mem[:, :MEM_WIDTH] * qscale
    mz = mem[:, MEM_WIDTH:]
    mem_scores = {}
    for pair in range(MEM_HEADS // 2):
        sl = slice(pair * LANES, (pair + 1) * LANES)
        for keep_low in (True, False):
            qm = jnp.where(low == keep_low, mq[:, sl], 0.0).astype(jnp.bfloat16)
            mem_scores[pair, keep_low] = _dot_nt(qm, mk_ref[0, :, sl])
    yield

    for pair in range(GM_GROUPS // 2):
        sl = slice(pair * LANES, (pair + 1) * LANES)
        for c in range(rows // GM_CHUNK):
            cs = slice(c * GM_CHUNK, (c + 1) * GM_CHUNK)
            out_rows = slice(rs.start + c * GM_CHUNK, rs.start + (c + 1) * GM_CHUNK)
            s_lo, s_hi = spatial[pair, c]
            s = jnp.where(_low_half((GM_CHUNK, LANES)), s_lo, s_hi) + gbs_ref[:, sl]
            ygm_ref[out_rows, sl] = (u[cs, sl] * s * _silu(z[cs, sl])).astype(ygm_ref.dtype)

    qscale2 = np.float32(HEAD_DIM ** -0.5 * np.log2(np.e))
    ones = jnp.ones((rows, LANES), vs_ref.dtype)
    for i in range(NSA_WIDTH // LANES):
        qi = _rope(slab(qh, i), cos, sin, rot_low) * qscale2
        q_ref[rs, i * LANES:(i + 1) * LANES] = qi.astype(q_ref.dtype)
    stage_ref[0, rs, :] = _rope(slab(kv, 0), cos, sin, rot_low)
    stage_ref[1, rs, :] = slab(kv, 1)
    out_rows = slice(rs.start // CMP_STRIDE, rs.stop // CMP_STRIDE)
    for j, dst in enumerate((kc_ref, vc_ref)):
        for l in range(CMP_STRIDE):
            token_l = stage_ref[j, pl.ds(rs.start + l, rows // CMP_STRIDE, stride=CMP_STRIDE), :]
            dst[out_rows, l * LANES:(l + 1) * LANES] = token_l
    ksa_ref[rs, :LANES] = _rope(slab(kv, 2), cos, sin, rot_low).astype(ksa_ref.dtype)
    tok = row0 % seq_len + lax.broadcasted_iota(jnp.int32, (rows, LANES), 0)
    onehot = (tok // SEL_BLOCK) == _lane_iota((rows, LANES))
    ksa_ref[rs, LANES:] = jnp.where(onehot, 1.0, 0.0).astype(ksa_ref.dtype)
    vs_ref[rs, :LANES] = slab(kv, 3).astype(vs_ref.dtype)
    vs_ref[rs, LANES:] = ones
    kw_ref[rs, :] = _rope(slab(kv, 4), cos, sin, rot_low).astype(kw_ref.dtype)
    vw_ref[rs, :LANES] = slab(kv, 5).astype(vw_ref.dtype)
    vw_ref[rs, LANES:] = ones
    nz_ref[rs, :] = _silu(nz_raw)
    gate_ref[rs, :] = jax.nn.sigmoid(gate_raw)

    for pair in range(MEM_HEADS // 2):
        sl = slice(pair * LANES, (pair + 1) * LANES)
        vp = mv_ref[0, :, sl]
        outs = []
        for keep_low in (True, False):
            s = mem_scores[pair, keep_low]
            e = jnp.exp(s - jnp.max(s, axis=-1, keepdims=True))
            p = e / jnp.sum(e, axis=-1, keepdims=True)
            outs.append(_dot(p.astype(jnp.bfloat16), vp))
        o = jnp.where(low, outs[0], outs[1])
        ymem_ref[rs, sl] = (o * _silu(mz[:, sl])).astype(ymem_ref.dtype)


def _inproj(x2d, w_cat, cos_t, sin_t, rot_low, gws, gbs, glg, glb, mk, mv, *, layer, batch, seq_len):
    n, d_model = x2d.shape
    rows = PROJ_ROWS
    steps_per_seq = seq_len // rows
    mem_len = mk.shape[1] // batch

    def tok_spec(width):
        return pl.BlockSpec((rows, width), lambda i: (i, 0))

    def const_spec(shape):
        return pl.BlockSpec(shape, lambda i: (0,) * len(shape))

    def layer_spec(shape):
        return pl.BlockSpec((None,) + shape[1:], lambda i: (layer,) + (0,) * (len(shape) - 1))

    tab_spec = pl.BlockSpec((rows, LANES), lambda i: (i % steps_per_seq, 0))
    mem_spec = pl.BlockSpec((None, 1, mem_len, MEM_WIDTH), lambda i: (layer, i // steps_per_seq, 0, 0))
    bf16, f32 = jnp.bfloat16, jnp.float32
    outs = [(1, GM_WIDTH, bf16), (1, MEM_WIDTH, bf16), (1, NSA_WIDTH, bf16),
            (CMP_STRIDE, CMP_STRIDE * LANES, f32), (CMP_STRIDE, CMP_STRIDE * LANES, f32),
            (1, 2 * LANES, bf16), (1, 2 * LANES, bf16), (1, LANES, bf16), (1, 2 * LANES, bf16),
            (1, NSA_WIDTH, f32), (1, LANES, f32)]
    return pl.pallas_call(
        functools.partial(_inproj_kernel, seq_len=seq_len),
        grid=(n // rows,),
        in_specs=[tok_spec(d_model), layer_spec(w_cat.shape), tab_spec, tab_spec,
                  const_spec(rot_low.shape), const_spec(gws.shape), const_spec(gbs.shape),
                  const_spec(glg.shape), const_spec(glb.shape), mem_spec, mem_spec],
        out_specs=[pl.BlockSpec((rows // d, w), lambda i: (i, 0)) for d, w, _ in outs],
        out_shape=[jax.ShapeDtypeStruct((n // d, w), dt) for d, w, dt in outs],
        scratch_shapes=[pltpu.VMEM((2, rows, LANES), f32)],
        compiler_params=pltpu.CompilerParams(dimension_semantics=("arbitrary",),
                                             vmem_limit_bytes=VMEM_LIMIT_BYTES),
        name="in_proj_mixers",
    )(x2d, w_cat, cos_t, sin_t, rot_low, gws, gbs, glg, glb,
      mk.reshape(-1, batch, mem_len, MEM_WIDTH), mv.reshape(-1, batch, mem_len, MEM_WIDTH))


def _compress_kernel(k_ref, v_ref, pos_ref, w1_ref, w2_ref, kcmp_ref, vcmp_ref):
    n_rows = k_ref.shape[1]
    for idx, (src, dst) in enumerate(((k_ref, kcmp_ref), (v_ref, vcmp_ref))):
        xr = src[0]
        top = _dot((xr + pos_ref[idx, 0:1]).astype(jnp.bfloat16), w1_ref[idx, 0])
        bot = _dot((xr + pos_ref[idx, 1:2]).astype(jnp.bfloat16), w1_ref[idx, 1])
        hidden = top + pltpu.roll(bot, n_rows - 1, 0)
        act = jax.nn.gelu(hidden, approximate=True)
        dst[0, :, :LANES] = _dot(act.astype(jnp.bfloat16), w2_ref[idx]).astype(dst.dtype)
    vcmp_ref[0, :, LANES:] = jnp.ones((n_rows, LANES), vcmp_ref.dtype)


def _compress(kc_rows, vc_rows, pos, w1, w2):
    batch, n_rows, width = kc_rows.shape
    row_spec = pl.BlockSpec((1, n_rows, width), lambda b: (b, 0, 0))
    def out_spec(width):
        return pl.BlockSpec((1, n_rows, width), lambda b: (b, 0, 0))

    def out(width):
        return jax.ShapeDtypeStruct((batch, n_rows, width), jnp.bfloat16)

    return pl.pallas_call(
        _compress_kernel,
        grid=(batch,),
        in_specs=[row_spec, row_spec,
                  pl.BlockSpec(pos.shape, lambda b: (0, 0, 0)),
                  pl.BlockSpec(w1.shape, lambda b: (0, 0, 0, 0)),
                  pl.BlockSpec(w2.shape, lambda b: (0, 0, 0))],
        out_specs=[out_spec(LANES), out_spec(2 * LANES)],
        out_shape=[out(LANES), out(2 * LANES)],
        compiler_params=pltpu.CompilerParams(dimension_semantics=("arbitrary",),
                                             vmem_limit_bytes=VMEM_LIMIT_BYTES),
        name="nsa_compress",
    )(kc_rows, vc_rows, pos, w1, w2)


def _split_bf16(x, parts):
    out = []
    for _ in range(parts):
        hi = x.astype(jnp.bfloat16)
        out.append(hi)
        x = x - hi.astype(jnp.float32)
    return out


def _topk_columns(score):
    row = lax.broadcasted_iota(jnp.int32, score.shape, 0).astype(jnp.float32)
    picked = jnp.zeros(score.shape, jnp.float32)
    for _ in range(SEL_TOPK):
        best = jnp.max(score, axis=0, keepdims=True)
        first = jnp.min(jnp.where(score == best, row, np.float32(score.shape[0])),
                        axis=0, keepdims=True)
        hit = row == first
        picked = jnp.where(hit, 1.0, picked)
        score = jnp.where(hit, REMOVED, score)
    return picked


def _nsa_kernel(q_ref, qnext_ref, nz_ref, gate_ref, ksa_ref, vs_ref, kw_ref, vw_ref, kcmp_ref,
                vcmp_ref, ovl_ref, gexp_ref, out_ref, qa_ref, sc_ref, sw_ref, pc_ref, pw_ref, bc_ref,
                bw_ref, m_ref, acc_ref, oc_ref, ow_ref, sa_ref, sb_ref, mc_ref, mw_ref, qn_ref,
                selb_ref, ocn_ref, gx_ref, *, seq_len):
    bi = pl.program_id(1)
    start = bi * Q_BLOCK
    n_sel = seq_len // SEL_BLOCK
    n_cmp = kcmp_ref.shape[1]
    span = WINDOW + Q_BLOCK
    tk = SEL_KEY_TILE
    bf16 = jnp.bfloat16

    def head_rows(r):
        return slice(r * Q_BLOCK, (r + 1) * Q_BLOCK)

    low = _low_half((Q_BLOCK, LANES))
    chunk = SOFTMAX_ROWS
    n_chunks = Q_BLOCK // chunk

    def stack_queries(src_ref, dst_ref):
        for i in range(NSA_HPG):
            qi = src_ref[0, :, i * LANES:(i + 1) * LANES]
            zero = jnp.zeros_like(qi)
            dst_ref[head_rows(2 * i), :LANES] = jnp.where(low, qi, zero)
            dst_ref[head_rows(2 * i + 1), :LANES] = jnp.where(low, zero, qi)

    def compressed_scores(qs_ref, blk_start):
        t_b = blk_start + lax.broadcasted_iota(jnp.int32, (Q_BLOCK, 1), 0)
        c_end = lax.broadcasted_iota(jnp.int32, (1, n_cmp), 1) * CMP_STRIDE + (CMP_BLOCK - 1)
        bc_ref[...] = jnp.where(c_end <= t_b, 0.0, NEG_INF)
        sc_ref[...] = _dot_nt(qs_ref[:, :LANES], kcmp_ref[0])

    def masked_exp(s_ref, b_ref, m_ref_, p_ref):
        width_tiles = s_ref.shape[1] // LANES
        for r in range(NSA_HEADS):
            for c in range(n_chunks):
                crow = slice(c * chunk, (c + 1) * chunk)
                rows = slice(r * Q_BLOCK + c * chunk, r * Q_BLOCK + (c + 1) * chunk)
                row_max = jnp.max(s_ref[rows, :] + b_ref[crow, :], axis=-1, keepdims=True)
                m_ref_[rows, :] = jnp.broadcast_to(row_max, (chunk, LANES))
        for r in range(NSA_HEADS):
            for c in range(n_chunks):
                crow = slice(c * chunk, (c + 1) * chunk)
                rows = slice(r * Q_BLOCK + c * chunk, r * Q_BLOCK + (c + 1) * chunk)
                s = s_ref[rows, :] + b_ref[crow, :]
                p_ref[rows, :] = jnp.exp2(s - _tile_lanes(m_ref_[rows, :], width_tiles)).astype(bf16)

    def compressed_out(blk_start):
        t_col = blk_start + lax.broadcasted_iota(jnp.int32, (Q_BLOCK, 1), 0)
        seen_col = jnp.concatenate([t_col >= CMP_BLOCK - 1] * NSA_HEADS, axis=0)
        o_c = _dot(pc_ref[...], vcmp_ref[0])
        ocn_ref[...] = jnp.where(seen_col, o_c[:, :LANES] / o_c[:, LANES:], 0.0)

    def candidate_scores(blk_start):
        parts = _dot_nt(ovl_ref[...], pc_ref[...])
        t_lane = blk_start + _lane_iota((1, parts.shape[1])) % Q_BLOCK
        inv = jnp.where(t_lane >= CMP_BLOCK - 1, 1.0 / parts[LANES:LANES + 1, :], 0.0)
        weighted = parts[:LANES, :] * inv
        imp = jnp.concatenate(
            [sum(weighted[:, (2 * i + g) * Q_BLOCK:(2 * i + g + 1) * Q_BLOCK] for i in range(NSA_HPG))
             for g in range(NSA_KV_GROUPS)], axis=1)
        blk = lax.broadcasted_iota(jnp.int32, imp.shape, 0)
        t_blk = (blk_start + _lane_iota((1, imp.shape[1])) % Q_BLOCK) // SEL_BLOCK
        valid = blk <= t_blk
        forced = (blk == 0) | (valid & (blk > t_blk - N_LOCAL_SEL))
        score = jnp.where(forced, FORCE_SCORE, jnp.where(valid, imp, -1.0))
        if n_sel < LANES:
            score = jnp.where(blk < n_sel, score, REMOVED)
        return score

    def select_blocks(score):
        picked = _topk_columns(score).astype(bf16)
        eye = (lax.broadcasted_iota(jnp.int32, (Q_BLOCK, Q_BLOCK), 0)
               == lax.broadcasted_iota(jnp.int32, (Q_BLOCK, Q_BLOCK), 1)).astype(bf16)
        for g in range(NSA_KV_GROUPS):
            picked_q = _dot_nt(eye, picked[:, g * Q_BLOCK:(g + 1) * Q_BLOCK])
            selb_ref[g * Q_BLOCK:(g + 1) * Q_BLOCK, :] = ((1.0 - picked_q) * NEG_INF).astype(bf16)

    t_q = start + lax.broadcasted_iota(jnp.int32, (Q_BLOCK, 1), 0)
    w0 = pl.multiple_of(jnp.maximum(start - WINDOW, 0), Q_BLOCK)

    def window_scores():
        kpos = w0 + lax.broadcasted_iota(jnp.int32, (1, span), 1)
        bw_ref[...] = jnp.where((kpos <= t_q) & (kpos > t_q - WINDOW), 0.0, NEG_INF)
        sw_ref[...] = _dot_nt(qa_ref[:, :LANES], kw_ref[0, pl.ds(w0, span), :])

    def window_out():
        o_w = _dot(pw_ref[...], vw_ref[0, pl.ds(w0, span), :])
        ow_ref[...] = o_w[:, :LANES] / o_w[:, LANES:]

    def scores(tile):
        k0 = pl.multiple_of(tile * tk, tk)
        return _dot_nt(qa_ref[...], ksa_ref[0, pl.ds(k0, tk), :])

    @pl.when(bi == 0)
    def _():
        stack_queries(q_ref, qn_ref)
        compressed_scores(qn_ref, start)
        masked_exp(sc_ref, bc_ref, mc_ref, pc_ref)
        compressed_out(start)
        select_blocks(candidate_scores(start))

    stack_queries(q_ref, qa_ref)
    for r in range(NSA_HEADS):
        g = r % NSA_KV_GROUPS
        qa_ref[head_rows(r), LANES:] = selb_ref[g * Q_BLOCK:(g + 1) * Q_BLOCK, :]
    oc_ref[...] = ocn_ref[...]
    stack_queries(qnext_ref, qn_ref)

    nxt = start + Q_BLOCK
    compressed_scores(qn_ref, nxt)
    window_scores()
    masked_exp(sc_ref, bc_ref, mc_ref, pc_ref)
    next_score = candidate_scores(nxt)
    compressed_out(nxt)
    masked_exp(sw_ref, bw_ref, mw_ref, pw_ref)
    window_out()
    sa_ref[...] = scores(0)
    g_hi, g_lo = _split_bf16(gate_ref[0], 2)
    gx_ref[...] = _dot(g_hi, gexp_ref[...]) + _dot(g_lo, gexp_ref[...])
    select_blocks(next_score)

    m_ref[...] = jnp.full(m_ref.shape, NEG_INF, jnp.float32)
    acc_ref[...] = jnp.zeros(acc_ref.shape, jnp.float32)

    def consume(buf_ref, tile, causal):
        k0 = pl.multiple_of(tile * tk, tk)
        s = buf_ref[...]
        if causal:
            kpos = k0 + lax.broadcasted_iota(jnp.int32, (1, tk), 1)
            tile_bias = jnp.where(kpos <= t_q, 0.0, NEG_INF)
            s = s + jnp.concatenate([tile_bias] * NSA_HEADS, axis=0)
        m_prev = m_ref[...]
        m_next = jnp.maximum(m_prev, jnp.max(s, axis=-1, keepdims=True))
        p = jnp.exp2(s - _tile_lanes(m_next, tk // LANES))
        alpha = jnp.exp2(m_prev - m_next)
        acc_ref[...] = (_tile_lanes(alpha, 2) * acc_ref[...]
                        + _dot(p.astype(bf16), vs_ref[0, pl.ds(k0, tk), :]))
        m_ref[...] = m_next

    diag = start // tk

    def tile_pair(first):
        sb_ref[...] = scores(first + 1)
        consume(sa_ref, first, False)
        sa_ref[...] = scores(first + 2)
        consume(sb_ref, first + 1, False)

    def tile_quad(j, carry):
        tile_pair(4 * j)
        tile_pair(4 * j + 2)
        return carry

    lax.fori_loop(0, diag // 4, tile_quad, 0)

    @pl.when(diag % 4 >= 2)
    def _():
        tile_pair((diag // 4) * 4)

    @pl.when(diag % 2 == 1)
    def _():
        sb_ref[...] = scores(diag)
        consume(sa_ref, diag - 1, False)
        consume(sb_ref, diag, True)

    @pl.when(diag % 2 == 0)
    def _():
        consume(sa_ref, diag, True)

    for i in range(NSA_HPG):
        lo_rows, hi_rows = head_rows(2 * i), head_rows(2 * i + 1)
        o_s = jnp.where(low, acc_ref[lo_rows, :LANES] / acc_ref[lo_rows, LANES:],
                        acc_ref[hi_rows, :LANES] / acc_ref[hi_rows, LANES:])
        branches = (jnp.where(low, oc_ref[lo_rows], oc_ref[hi_rows]), o_s,
                    jnp.where(low, ow_ref[lo_rows], ow_ref[hi_rows]))
        mixed = sum(o * gx_ref[:, (3 * i + c) * LANES:(3 * i + c + 1) * LANES]
                    for c, o in enumerate(branches))
        sl = slice(i * LANES, (i + 1) * LANES)
        out_ref[0, :, sl] = (mixed * nz_ref[0, :, sl]).astype(out_ref.dtype)


def _nsa(q, nz, gates, ksa, vs, kw, vw, kcmp, vcmp, ovl, gexp):
    batch, seq_len, _ = q.shape

    def q_spec(width):
        return pl.BlockSpec((1, Q_BLOCK, width), lambda b, i: (b, i, 0))

    def seq_spec(arr):
        return pl.BlockSpec((1,) + arr.shape[1:], lambda b, i: (b, 0, 0))

    rows = NSA_HEADS * Q_BLOCK
    n_cmp = kcmp.shape[1]
    span = WINDOW + Q_BLOCK
    last = seq_len // Q_BLOCK - 1
    next_q_spec = pl.BlockSpec((1, Q_BLOCK, NSA_WIDTH), lambda b, i: (b, jnp.minimum(i + 1, last), 0))
    return pl.pallas_call(
        functools.partial(_nsa_kernel, seq_len=seq_len),
        grid=(batch, seq_len // Q_BLOCK),
        in_specs=[q_spec(NSA_WIDTH), next_q_spec, q_spec(NSA_WIDTH), q_spec(LANES),
                  seq_spec(ksa), seq_spec(vs), seq_spec(kw), seq_spec(vw),
                  seq_spec(kcmp), seq_spec(vcmp),
                  pl.BlockSpec(ovl.shape, lambda b, i: (0, 0)),
                  pl.BlockSpec(gexp.shape, lambda b, i: (0, 0))],
        out_specs=q_spec(NSA_WIDTH),
        out_shape=jax.ShapeDtypeStruct((batch, seq_len, NSA_WIDTH), jnp.bfloat16),
        scratch_shapes=[pltpu.VMEM((rows, 2 * LANES), jnp.bfloat16),
                        pltpu.VMEM((rows, n_cmp), jnp.float32),
                        pltpu.VMEM((rows, span), jnp.float32),
                        pltpu.VMEM((rows, n_cmp), jnp.bfloat16),
                        pltpu.VMEM((rows, span), jnp.bfloat16),
                        pltpu.VMEM((Q_BLOCK, n_cmp), jnp.float32),
                        pltpu.VMEM((Q_BLOCK, span), jnp.float32),
                        pltpu.VMEM((rows, LANES), jnp.float32),
                        pltpu.VMEM((rows, 2 * LANES), jnp.float32),
                        pltpu.VMEM((rows, LANES), jnp.float32),
                        pltpu.VMEM((rows, LANES), jnp.float32),
                        pltpu.VMEM((rows, SEL_KEY_TILE), jnp.float32),
                        pltpu.VMEM((rows, SEL_KEY_TILE), jnp.float32),
                        pltpu.VMEM((rows, LANES), jnp.float32),
                        pltpu.VMEM((rows, LANES), jnp.float32),
                        pltpu.VMEM((rows, LANES), jnp.bfloat16),
                        pltpu.VMEM((NSA_KV_GROUPS * Q_BLOCK, LANES), jnp.bfloat16),
                        pltpu.VMEM((rows, LANES), jnp.float32),
                        pltpu.VMEM((Q_BLOCK, gexp.shape[1]), jnp.float32)],
        compiler_params=pltpu.CompilerParams(dimension_semantics=("arbitrary", "arbitrary"),
                                             vmem_limit_bytes=VMEM_LIMIT_BYTES),
        name="nsa_attention",
    )(q, q, nz, gates, ksa, vs, kw, vw, kcmp, vcmp, ovl, gexp)


def _outproj_kernel(x_ref, ygm_ref, ynsa_ref, ymem_ref, w_ref, g_ref, b_ref, o_ref, *, alpha):
    def mix_proj(rs):
        return (_dot(ygm_ref[rs, :], w_ref[:GM_WIDTH])
                + _dot(ynsa_ref[rs, :], w_ref[GM_WIDTH:GM_WIDTH + NSA_WIDTH])
                + _dot(ymem_ref[rs, :], w_ref[GM_WIDTH + NSA_WIDTH:]))

    subs = [slice(r0, r0 + PROJ_SUB_ROWS) for r0 in range(0, x_ref.shape[0], PROJ_SUB_ROWS)]
    y_next = mix_proj(subs[0])
    for j, rs in enumerate(subs):
        y = y_next
        y_next = mix_proj(subs[j + 1]) if j + 1 < len(subs) else None
        r = alpha * x_ref[rs, :] + y
        mu = jnp.mean(r, axis=-1, keepdims=True)
        d = r - mu
        var = jnp.mean(d * d, axis=-1, keepdims=True)
        o_ref[rs, :] = d * lax.rsqrt(var + LN_EPS) * g_ref[...] + b_ref[...]


def _outproj(x2d, ygm, ynsa, ymem, w_out, ln_g, ln_b, *, layer, alpha):
    n, d_model = x2d.shape
    rows = PROJ_ROWS

    def tok_spec(width):
        return pl.BlockSpec((rows, width), lambda i: (i, 0))

    def layer_spec(shape):
        return pl.BlockSpec((None,) + shape[1:], lambda i: (layer, 0, 0))

    return pl.pallas_call(
        functools.partial(_outproj_kernel, alpha=alpha),
        grid=(n // rows,),
        in_specs=[tok_spec(d_model), tok_spec(GM_WIDTH), tok_spec(NSA_WIDTH), tok_spec(MEM_WIDTH),
                  layer_spec(w_out.shape), layer_spec(ln_g.shape), layer_spec(ln_b.shape)],
        out_specs=tok_spec(d_model),
        out_shape=jax.ShapeDtypeStruct((n, d_model), jnp.float32),
        compiler_params=pltpu.CompilerParams(dimension_semantics=("arbitrary",),
                                             vmem_limit_bytes=VMEM_LIMIT_BYTES),
        name="out_proj_layernorm",
    )(x2d, ygm, ynsa, ymem, w_out, ln_g, ln_b)


def _pair_head_slices(w, start, axis):
    return [lax.slice_in_dim(w, start + h * HEAD_DIM, start + (h + 1) * HEAD_DIM, axis=axis)
            for h in PAIR_HEAD_ORDER]


def _permute_w_in(w):
    o_gate = 2048
    o_nz = o_gate + GATE_COLS
    o_mq = o_nz + NSA_WIDTH
    pieces = ([w[..., :768]] + _pair_head_slices(w, 768, 2) + [w[..., 1280:2048]]
              + _pair_head_slices(w, o_nz, 2)
              + [w[..., o_mq:], w[..., o_gate:o_nz],
                 jnp.zeros(w.shape[:2] + (LANES - GATE_COLS,), w.dtype)])
    return jnp.concatenate([p.astype(jnp.bfloat16) for p in pieces], axis=2)


def _permute_w_out(w):
    pieces = ([w[:, :GM_WIDTH]] + _pair_head_slices(w, GM_WIDTH, 1) + [w[:, GM_WIDTH + NSA_WIDTH:]])
    return jnp.concatenate([p.astype(jnp.bfloat16) for p in pieces], axis=1)


def _rope_tables(seq_len):
    half = HEAD_DIM // 2
    inv_freq = ROPE_THETA ** (-jnp.arange(half, dtype=jnp.float32) * 2.0 / HEAD_DIM)
    ang = jnp.arange(seq_len).astype(jnp.float32)[:, None] * inv_freq[None, :]
    cos, sin = jnp.cos(ang), jnp.sin(ang)
    reps = LANES // HEAD_DIM
    cos_t = jnp.tile(jnp.concatenate([cos, cos], axis=1), (1, reps))
    sin_t = jnp.tile(jnp.concatenate([-sin, sin], axis=1), (1, reps))
    rot_low = ((np.arange(LANES) % HEAD_DIM) < half).astype(np.float32)[None, :]
    return cos_t, sin_t, jnp.asarray(rot_low)


def _compress_weights(pos_k, w1_k, w2_k, pos_v, w1_v, w2_v):
    half = CMP_BLOCK // 2

    def block_diag2(w):
        z = jnp.zeros_like(w)
        return jnp.concatenate([jnp.concatenate([w, z], axis=-1),
                                jnp.concatenate([z, w], axis=-1)], axis=-2)

    def expand_w1(w1):
        w = w1.reshape(2, half, HEAD_DIM, CMP_HIDDEN)
        w = block_diag2(w)
        return w.reshape(2, half * NSA_KV_WIDTH, NSA_KV_GROUPS * CMP_HIDDEN)

    def expand_w2(w2):
        return block_diag2(w2)

    def expand_pos(pos):
        p = pos.reshape(2, half, 1, HEAD_DIM)
        p = jnp.broadcast_to(p, (2, half, NSA_KV_GROUPS, HEAD_DIM))
        return p.reshape(2, half * NSA_KV_WIDTH)

    pos = jnp.stack([expand_pos(pos_k), expand_pos(pos_v)])
    w1 = jnp.stack([expand_w1(w1_k), expand_w1(w1_v)]).astype(jnp.bfloat16)
    w2 = jnp.stack([expand_w2(w2_k), expand_w2(w2_v)]).astype(jnp.bfloat16)
    return pos, w1, w2


def _gate_expansion():
    out = np.zeros((LANES, NSA_HPG * 3 * LANES), np.float32)
    for i in range(NSA_HPG):
        for c in range(3):
            base = (3 * i + c) * LANES
            out[3 * i + c, base:base + HEAD_DIM] = 1.0
            out[3 * (i + NSA_HPG) + c, base + HEAD_DIM:base + LANES] = 1.0
    return jnp.asarray(out, dtype=jnp.bfloat16)


def _overlap_matrix(n_rows, n_sel):
    c_start = np.arange(n_rows) * CMP_STRIDE
    s_start = np.arange(LANES) * SEL_BLOCK
    ovl = ((c_start[:, None] < s_start[None, :] + SEL_BLOCK)
           & (c_start[:, None] + CMP_BLOCK > s_start[None, :])
           & (np.arange(LANES)[None, :] < n_sel))
    out = np.zeros((LANES + 16, n_rows), np.float32)
    out[:LANES] = ovl.T
    out[LANES] = 1.0
    return jnp.asarray(out, dtype=jnp.bfloat16)


def kernel(x, mem, w_in, gm_ln_g, gm_ln_b, gm_ws, gm_bs, cmp_pos_k, cmp_k_w1, cmp_k_w2,
           cmp_pos_v, cmp_v_w1, cmp_v_w2, w_mem_kv, w_out, ln_g, ln_b):
    batch, seq_len, d_model = x.shape
    depth = w_in.shape[0]
    assert seq_len % SEL_KEY_TILE == 0 and seq_len >= WINDOW + Q_BLOCK
    assert SEL_TOPK <= seq_len // SEL_BLOCK <= LANES
    alpha = (2.0 * depth) ** 0.25
    n_tok = batch * seq_len
    n_rows = seq_len // CMP_STRIDE

    cos_t, sin_t, rot_low = _rope_tables(seq_len)
    ovl = _overlap_matrix(n_rows, seq_len // SEL_BLOCK)
    gexp = _gate_expansion()
    tril = jnp.tril(jnp.ones((GM_CHUNK, GM_CHUNK), gm_ws.dtype))
    mk_all, mv_all = _memkv(mem.reshape(batch * mem.shape[1], d_model), w_mem_kv.astype(jnp.bfloat16))
    w_cat_all = _permute_w_in(w_in)
    w_out_all = _permute_w_out(w_out)

    h = x.reshape(n_tok, d_model)
    for l in range(depth):
        gws = (gm_ws[l] * tril[None]).astype(jnp.bfloat16)
        gbs = jnp.repeat(gm_bs[l].T, HEAD_DIM, axis=1)
        glg = gm_ln_g[l].reshape(1, GM_WIDTH)
        glb = gm_ln_b[l].reshape(1, GM_WIDTH)
        (ygm, ymem, q, kc, vc, ksa, vs, kw, vw, nz, gates) = _inproj(
            h, w_cat_all, cos_t, sin_t, rot_low, gws, gbs, glg, glb, mk_all, mv_all,
            layer=l, batch=batch, seq_len=seq_len)

        pos, w1, w2 = _compress_weights(cmp_pos_k[l], cmp_k_w1[l], cmp_k_w2[l],
                                        cmp_pos_v[l], cmp_v_w1[l], cmp_v_w2[l])
        row_shape = (batch, n_rows, CMP_STRIDE * NSA_KV_WIDTH)
        kcmp, vcmp = _compress(kc.reshape(row_shape), vc.reshape(row_shape), pos, w1, w2)

        def per_seq(a):
            return a.reshape(batch, seq_len, a.shape[-1])

        ynsa = _nsa(per_seq(q), per_seq(nz), per_seq(gates), per_seq(ksa), per_seq(vs),
                    per_seq(kw), per_seq(vw), kcmp, vcmp, ovl, gexp)

        h = _outproj(h, ygm, ynsa.reshape(n_tok, NSA_WIDTH), ymem, w_out_all,
                     ln_g.reshape(depth, 1, d_model), ln_b.reshape(depth, 1, d_model),
                     layer=l, alpha=alpha)
    return h.reshape(batch, seq_len, d_model)
```

```python
import functools

import numpy as np
import jax
import jax.numpy as jnp
from jax import lax
from jax.experimental import pallas as pl
from jax.experimental.pallas import tpu as pltpu

HEAD_DIM = 64
GM_GROUPS = 4
GM_WIDTH = GM_GROUPS * HEAD_DIM
GM_CHUNK = 128
NSA_HEADS = 8
NSA_KV_GROUPS = 2
NSA_HPG = NSA_HEADS // NSA_KV_GROUPS
NSA_WIDTH = NSA_HEADS * HEAD_DIM
NSA_KV_WIDTH = NSA_KV_GROUPS * HEAD_DIM
CMP_BLOCK = 32
CMP_STRIDE = 16
CMP_HIDDEN = 128
SEL_BLOCK = 64
SEL_TOPK = 16
N_LOCAL_SEL = 2
WINDOW = 512
Q_BLOCK = 128
MEM_HEADS = 4
MEM_WIDTH = MEM_HEADS * HEAD_DIM
ROPE_THETA = 10000.0
LN_EPS = 1e-5
NEG_INF = -1e30
FORCE_SCORE = 1e4
GATE_COLS = NSA_HEADS * 3

LANES = 128
VMEM_LIMIT_BYTES = 56 * 1024 * 1024

PROJ_ROWS = 1024
PROJ_SUB_ROWS = 256
SEL_KEY_TILE = 512
SOFTMAX_ROWS = 32
REMOVED = -3.0e38

PAIR_HEAD_ORDER = tuple(h for i in range(NSA_HPG) for h in (i, i + NSA_HPG))

C_GU, C_GV, C_GZ = 0, 256, 512
C_Q = 768
C_KC, C_VC, C_KS, C_VS, C_KW, C_VW = 1280, 1408, 1536, 1664, 1792, 1920
C_NZ = 2048
C_MQ, C_MZ = 2560, 2816
C_GATE = 3072
N_COLS = 3200


def _dot(a, b):
    return jnp.dot(a, b, preferred_element_type=jnp.float32)


def _dot_nt(a, b):
    return lax.dot_general(a, b, (((1,), (1,)), ((), ())), preferred_element_type=jnp.float32)


def _gelu(x):
    return 0.5 * x * (1.0 + lax.erf(x * np.float32(np.sqrt(0.5))))


def _silu(x):
    return x * jax.nn.sigmoid(x)


def _lane_iota(shape):
    return lax.broadcasted_iota(jnp.int32, shape, len(shape) - 1)


def _low_half(shape):
    return (_lane_iota(shape) % LANES) < HEAD_DIM


def _tile_lanes(x, reps):
    return jnp.concatenate([x] * reps, axis=-1) if reps > 1 else x


def _memkv_kernel(mem_ref, w_ref, k_ref, v_ref):
    kv = _dot(mem_ref[...].astype(jnp.bfloat16), w_ref[0])
    k_ref[0] = kv[:, :MEM_WIDTH].astype(jnp.bfloat16)
    v_ref[0] = kv[:, MEM_WIDTH:].astype(jnp.bfloat16)


def _memkv(mem2d, w_mem_kv_bf16):
    depth = w_mem_kv_bf16.shape[0]
    rows, d_model = mem2d.shape
    out = jax.ShapeDtypeStruct((depth, rows, MEM_WIDTH), jnp.bfloat16)
    return pl.pallas_call(
        _memkv_kernel,
        grid=(depth,),
        in_specs=[pl.BlockSpec((rows, d_model), lambda l: (0, 0)),
                  pl.BlockSpec((1, d_model, 2 * MEM_WIDTH), lambda l: (l, 0, 0))],
        out_specs=[pl.BlockSpec((1, rows, MEM_WIDTH), lambda l: (l, 0, 0)),
                   pl.BlockSpec((1, rows, MEM_WIDTH), lambda l: (l, 0, 0))],
        out_shape=[out, out],
        name="mem_kv_proj",
    )(mem2d, w_mem_kv_bf16)


def _rope(x, cos, sin_signed, low):
    width = x.shape[-1]
    swapped = jnp.where(low, pltpu.roll(x, width - HEAD_DIM // 2, 1), pltpu.roll(x, HEAD_DIM // 2, 1))
    return x * cos + swapped * sin_signed


def _group_layer_norm(v, g, b, low):
    inv = np.float32(1.0 / HEAD_DIM)
    s_lo = jnp.sum(jnp.where(low, v, 0.0), axis=-1, keepdims=True)
    s_hi = jnp.sum(jnp.where(low, 0.0, v), axis=-1, keepdims=True)
    mu = jnp.where(low, s_lo, s_hi) * inv
    d = v - mu
    d2 = d * d
    q_lo = jnp.sum(jnp.where(low, d2, 0.0), axis=-1, keepdims=True)
    q_hi = jnp.sum(jnp.where(low, 0.0, d2), axis=-1, keepdims=True)
    var = jnp.where(low, q_lo, q_hi) * inv
    return d * lax.rsqrt(var + LN_EPS) * g + b


def _inproj_kernel(x_ref, w_ref, cos_ref, sin_ref, rot_low_ref, gws_ref, gbs_ref, glg_ref, glb_ref,
                   mk_ref, mv_ref,
                   ygm_ref, ymem_ref, q_ref, kc_ref, vc_ref, ksa_ref, vs_ref, kw_ref, vw_ref,
                   nz_ref, gate_ref, stage_ref, *, seq_len):
    subs = [slice(r0, r0 + PROJ_SUB_ROWS) for r0 in range(0, x_ref.shape[0], PROJ_SUB_ROWS)]
    mixers = _project(x_ref, w_ref, subs[0], MIXER_SECTIONS)
    others = _project(x_ref, w_ref, subs[0], OTHER_SECTIONS)
    for j, rs in enumerate(subs):
        upcoming = subs[j + 1] if j + 1 < len(subs) else None
        next_mixers = _project(x_ref, w_ref, upcoming, MIXER_SECTIONS) if upcoming else None
        epilogue = _inproj_rows(rs, pl.program_id(0) * x_ref.shape[0] + rs.start, mixers, others,
                                cos_ref, sin_ref, rot_low_ref, gws_ref, gbs_ref, glg_ref, glb_ref,
                                mk_ref, mv_ref, ygm_ref, ymem_ref, q_ref, kc_ref, vc_ref, ksa_ref,
                                vs_ref, kw_ref, vw_ref, nz_ref, gate_ref, stage_ref, seq_len=seq_len)
        next(epilogue)
        next_others = _project(x_ref, w_ref, upcoming, OTHER_SECTIONS) if upcoming else None
        for _ in epilogue:
            pass
        mixers, others = next_mixers, next_others


MIXER_SECTIONS = ((C_GU, 3 * GM_WIDTH), (C_MQ, 2 * MEM_WIDTH))
OTHER_SECTIONS = ((C_Q, NSA_WIDTH), (C_KC, 6 * NSA_KV_WIDTH), (C_NZ, NSA_WIDTH), (C_GATE, LANES))


def _project(x_ref, w_ref, rs, sections):
    xb = x_ref[rs, :].astype(jnp.bfloat16)
    return tuple(_dot(xb, w_ref[:, c0:c0 + width]) for c0, width in sections)


def _inproj_rows(rs, row0, mixers, others, cos_ref, sin_ref, rot_low_ref, gws_ref, gbs_ref, glg_ref,
                 glb_ref, mk_ref, mv_ref, ygm_ref, ymem_ref, q_ref, kc_ref, vc_ref, ksa_ref, vs_ref,
                 kw_ref, vw_ref, nz_ref, gate_ref, stage_ref, *, seq_len):
    rows = rs.stop - rs.start
    gm, mem = mixers
    qh, kv, nz_raw, gate_raw = others
    low = _low_half((rows, LANES))
    rot_low = rot_low_ref[...] > 0.5
    rot_low = jnp.broadcast_to(rot_low, (rows, LANES))
    cos = cos_ref[rs, :]
    sin = sin_ref[rs, :]

    def slab(h, i):
        return h[:, i * LANES:(i + 1) * LANES]

    qscale = np.float32(HEAD_DIM ** -0.5)

    u = _gelu(gm[:, :GM_WIDTH])
    v = _gelu(gm[:, GM_WIDTH:2 * GM_WIDTH])
    z = gm[:, 2 * GM_WIDTH:]
    spatial = {}
    for pair in range(GM_GROUPS // 2):
        sl = slice(pair * LANES, (pair + 1) * LANES)
        vln = _group_layer_norm(v[:, sl], glg_ref[:, sl], glb_ref[:, sl], low).astype(jnp.bfloat16)
        for c in range(rows // GM_CHUNK):
            cs = slice(c * GM_CHUNK, (c + 1) * GM_CHUNK)
            spatial[pair, c] = (_dot(gws_ref[2 * pair], vln[cs]), _dot(gws_ref[2 * pair + 1], vln[cs]))
    mq = mem[:, :MEM_WIDTH] * qscale
    mz = mem[:, MEM_WIDTH:]
    mem_scores = {}
    for pair in range(MEM_HEADS // 2):
        sl = slice(pair * LANES, (pair + 1) * LANES)
        for keep_low in (True, False):
            qm = jnp.where(low == keep_low, mq[:, sl], 0.0).astype(jnp.bfloat16)
            mem_scores[pair, keep_low] = _dot_nt(qm, mk_ref[0, :, sl])
    yield

    for pair in range(GM_GROUPS // 2):
        sl = slice(pair * LANES, (pair + 1) * LANES)
        for c in range(rows // GM_CHUNK):
            cs = slice(c * GM_CHUNK, (c + 1) * GM_CHUNK)
            out_rows = slice(rs.start + c * GM_CHUNK, rs.start + (c + 1) * GM_CHUNK)
            s_lo, s_hi = spatial[pair, c]
            s = jnp.where(_low_half((GM_CHUNK, LANES)), s_lo, s_hi) + gbs_ref[:, sl]
            ygm_ref[out_rows, sl] = (u[cs, sl] * s * _silu(z[cs, sl])).astype(ygm_ref.dtype)

    qscale2 = np.float32(HEAD_DIM ** -0.5 * np.log2(np.e))
    ones = jnp.ones((rows, LANES), vs_ref.dtype)
    for i in range(NSA_WIDTH // LANES):
        qi = _rope(slab(qh, i), cos, sin, rot_low) * qscale2
        q_ref[rs, i * LANES:(i + 1) * LANES] = qi.astype(q_ref.dtype)
    stage_ref[0, rs, :] = _rope(slab(kv, 0), cos, sin, rot_low)
    stage_ref[1, rs, :] = slab(kv, 1)
    out_rows = slice(rs.start // CMP_STRIDE, rs.stop // CMP_STRIDE)
    for j, dst in enumerate((kc_ref, vc_ref)):
        for l in range(CMP_STRIDE):
            token_l = stage_ref[j, pl.ds(rs.start + l, rows // CMP_STRIDE, stride=CMP_STRIDE), :]
            dst[out_rows, l * LANES:(l + 1) * LANES] = token_l
    ksa_ref[rs, :LANES] = _rope(slab(kv, 2), cos, sin, rot_low).astype(ksa_ref.dtype)
    tok = row0 % seq_len + lax.broadcasted_iota(jnp.int32, (rows, LANES), 0)
    onehot = (tok // SEL_BLOCK) == _lane_iota((rows, LANES))
    ksa_ref[rs, LANES:] = jnp.where(onehot, 1.0, 0.0).astype(ksa_ref.dtype)
    vs_ref[rs, :LANES] = slab(kv, 3).astype(vs_ref.dtype)
    vs_ref[rs, LANES:] = ones
    kw_ref[rs, :] = _rope(slab(kv, 4), cos, sin, rot_low).astype(kw_ref.dtype)
    vw_ref[rs, :LANES] = slab(kv, 5).astype(vw_ref.dtype)
    vw_ref[rs, LANES:] = ones
    nz_ref[rs, :] = _silu(nz_raw)
    gate_ref[rs, :] = jax.nn.sigmoid(gate_raw)

    for pair in range(MEM_HEADS // 2):
        sl = slice(pair * LANES, (pair + 1) * LANES)
        vp = mv_ref[0, :, sl]
        outs = []
        for keep_low in (True, False):
            s = mem_scores[pair, keep_low]
            e = jnp.exp(s - jnp.max(s, axis=-1, keepdims=True))
            p = e / jnp.sum(e, axis=-1, keepdims=True)
            outs.append(_dot(p.astype(jnp.bfloat16), vp))
        o = jnp.where(low, outs[0], outs[1])
        ymem_ref[rs, sl] = (o * _silu(mz[:, sl])).astype(ymem_ref.dtype)


def _inproj(x2d, w_cat, cos_t, sin_t, rot_low, gws, gbs, glg, glb, mk, mv, *, layer, batch, seq_len):
    n, d_model = x2d.shape
    rows = PROJ_ROWS
    steps_per_seq = seq_len // rows
    mem_len = mk.shape[1] // batch

    def tok_spec(width):
        return pl.BlockSpec((rows, width), lambda i: (i, 0))

    def const_spec(shape):
        return pl.BlockSpec(shape, lambda i: (0,) * len(shape))

    def layer_spec(shape):
        return pl.BlockSpec((None,) + shape[1:], lambda i: (layer,) + (0,) * (len(shape) - 1))

    tab_spec = pl.BlockSpec((rows, LANES), lambda i: (i % steps_per_seq, 0))
    mem_spec = pl.BlockSpec((None, 1, mem_len, MEM_WIDTH), lambda i: (layer, i // steps_per_seq, 0, 0))
    bf16, f32 = jnp.bfloat16, jnp.float32
    outs = [(1, GM_WIDTH, bf16), (1, MEM_WIDTH, bf16), (1, NSA_WIDTH, bf16),
            (CMP_STRIDE, CMP_STRIDE * LANES, f32), (CMP_STRIDE, CMP_STRIDE * LANES, f32),
            (1, 2 * LANES, bf16), (1, 2 * LANES, bf16), (1, LANES, bf16), (1, 2 * LANES, bf16),
            (1, NSA_WIDTH, f32), (1, LANES, f32)]
    return pl.pallas_call(
        functools.partial(_inproj_kernel, seq_len=seq_len),
        grid=(n // rows,),
        in_specs=[tok_spec(d_model), layer_spec(w_cat.shape), tab_spec, tab_spec,
                  const_spec(rot_low.shape), const_spec(gws.shape), const_spec(gbs.shape),
                  const_spec(glg.shape), const_spec(glb.shape), mem_spec, mem_spec],
        out_specs=[pl.BlockSpec((rows // d, w), lambda i: (i, 0)) for d, w, _ in outs],
        out_shape=[jax.ShapeDtypeStruct((n // d, w), dt) for d, w, dt in outs],
        scratch_shapes=[pltpu.VMEM((2, rows, LANES), f32)],
        compiler_params=pltpu.CompilerParams(dimension_semantics=("arbitrary",),
                                             vmem_limit_bytes=VMEM_LIMIT_BYTES),
        name="in_proj_mixers",
    )(x2d, w_cat, cos_t, sin_t, rot_low, gws, gbs, glg, glb,
      mk.reshape(-1, batch, mem_len, MEM_WIDTH), mv.reshape(-1, batch, mem_len, MEM_WIDTH))


def _compress_kernel(k_ref, v_ref, pos_ref, w1_ref, w2_ref, kcmp_ref, vcmp_ref):
    n_rows = k_ref.shape[1]
    for idx, (src, dst) in enumerate(((k_ref, kcmp_ref), (v_ref, vcmp_ref))):
        xr = src[0]
        top = _dot((xr + pos_ref[idx, 0:1]).astype(jnp.bfloat16), w1_ref[idx, 0])
        bot = _dot((xr + pos_ref[idx, 1:2]).astype(jnp.bfloat16), w1_ref[idx, 1])
        hidden = top + pltpu.roll(bot, n_rows - 1, 0)
        act = jax.nn.gelu(hidden, approximate=True)
        dst[0, :, :LANES] = _dot(act.astype(jnp.bfloat16), w2_ref[idx]).astype(dst.dtype)
    vcmp_ref[0, :, LANES:] = jnp.ones((n_rows, LANES), vcmp_ref.dtype)


def _compress(kc_rows, vc_rows, pos, w1, w2):
    batch, n_rows, width = kc_rows.shape
    row_spec = pl.BlockSpec((1, n_rows, width), lambda b: (b, 0, 0))
    def out_spec(width):
        return pl.BlockSpec((1, n_rows, width), lambda b: (b, 0, 0))

    def out(width):
        return jax.ShapeDtypeStruct((batch, n_rows, width), jnp.bfloat16)

    return pl.pallas_call(
        _compress_kernel,
        grid=(batch,),
        in_specs=[row_spec, row_spec,
                  pl.BlockSpec(pos.shape, lambda b: (0, 0, 0)),
                  pl.BlockSpec(w1.shape, lambda b: (0, 0, 0, 0)),
                  pl.BlockSpec(w2.shape, lambda b: (0, 0, 0))],
        out_specs=[out_spec(LANES), out_spec(2 * LANES)],
        out_shape=[out(LANES), out(2 * LANES)],
        compiler_params=pltpu.CompilerParams(dimension_semantics=("arbitrary",),
                                             vmem_limit_bytes=VMEM_LIMIT_BYTES),
        name="nsa_compress",
    )(kc_rows, vc_rows, pos, w1, w2)


def _split_bf16(x, parts):
    out = []
    for _ in range(parts):
        hi = x.astype(jnp.bfloat16)
        out.append(hi)
        x = x - hi.astype(jnp.float32)
    return out


def _topk_columns(score):
    row = lax.broadcasted_iota(jnp.int32, score.shape, 0).astype(jnp.float32)
    picked = jnp.zeros(score.shape, jnp.float32)
    for _ in range(SEL_TOPK):
        best = jnp.max(score, axis=0, keepdims=True)
        first = jnp.min(jnp.where(score == best, row, np.float32(score.shape[0])),
                        axis=0, keepdims=True)
        hit = row == first
        picked = jnp.where(hit, 1.0, picked)
        score = jnp.where(hit, REMOVED, score)
    return picked


def _nsa_kernel(q_ref, qnext_ref, nz_ref, gate_ref, ksa_ref, vs_ref, kw_ref, vw_ref, kcmp_ref,
                vcmp_ref, ovl_ref, gexp_ref, out_ref, qa_ref, sc_ref, sw_ref, pc_ref, pw_ref, bc_ref,
                bw_ref, m_ref, acc_ref, oc_ref, ow_ref, sa_ref, sb_ref, mc_ref, mw_ref, qn_ref,
                selb_ref, ocn_ref, gx_ref, *, seq_len):
    bi = pl.program_id(1)
    start = bi * Q_BLOCK
    n_sel = seq_len // SEL_BLOCK
    n_cmp = kcmp_ref.shape[1]
    span = WINDOW + Q_BLOCK
    tk = SEL_KEY_TILE
    bf16 = jnp.bfloat16

    def head_rows(r):
        return slice(r * Q_BLOCK, (r + 1) * Q_BLOCK)

    low = _low_half((Q_BLOCK, LANES))
    chunk = SOFTMAX_ROWS
    n_chunks = Q_BLOCK // chunk

    def stack_queries(src_ref, dst_ref):
        for i in range(NSA_HPG):
            qi = src_ref[0, :, i * LANES:(i + 1) * LANES]
            zero = jnp.zeros_like(qi)
            dst_ref[head_rows(2 * i), :LANES] = jnp.where(low, qi, zero)
            dst_ref[head_rows(2 * i + 1), :LANES] = jnp.where(low, zero, qi)

    def compressed_scores(qs_ref, blk_start):
        t_b = blk_start + lax.broadcasted_iota(jnp.int32, (Q_BLOCK, 1), 0)
        c_end = lax.broadcasted_iota(jnp.int32, (1, n_cmp), 1) * CMP_STRIDE + (CMP_BLOCK - 1)
        bc_ref[...] = jnp.where(c_end <= t_b, 0.0, NEG_INF)
        sc_ref[...] = _dot_nt(qs_ref[:, :LANES], kcmp_ref[0])

    def masked_exp(s_ref, b_ref, m_ref_, p_ref):
        width_tiles = s_ref.shape[1] // LANES
        for r in range(NSA_HEADS):
            for c in range(n_chunks):
                crow = slice(c * chunk, (c + 1) * chunk)
                rows = slice(r * Q_BLOCK + c * chunk, r * Q_BLOCK + (c + 1) * chunk)
                row_max = jnp.max(s_ref[rows, :] + b_ref[crow, :], axis=-1, keepdims=True)
                m_ref_[rows, :] = jnp.broadcast_to(row_max, (chunk, LANES))
        for r in range(NSA_HEADS):
            for c in range(n_chunks):
                crow = slice(c * chunk, (c + 1) * chunk)
                rows = slice(r * Q_BLOCK + c * chunk, r * Q_BLOCK + (c + 1) * chunk)
                s = s_ref[rows, :] + b_ref[crow, :]
                p_ref[rows, :] = jnp.exp2(s - _tile_lanes(m_ref_[rows, :], width_tiles)).astype(bf16)

    def compressed_out(blk_start):
        t_col = blk_start + lax.broadcasted_iota(jnp.int32, (Q_BLOCK, 1), 0)
        seen_col = jnp.concatenate([t_col >= CMP_BLOCK - 1] * NSA_HEADS, axis=0)
        o_c = _dot(pc_ref[...], vcmp_ref[0])
        ocn_ref[...] = jnp.where(seen_col, o_c[:, :LANES] / o_c[:, LANES:], 0.0)

    def candidate_scores(blk_start):
        parts = _dot_nt(ovl_ref[...], pc_ref[...])
        t_lane = blk_start + _lane_iota((1, parts.shape[1])) % Q_BLOCK
        inv = jnp.where(t_lane >= CMP_BLOCK - 1, 1.0 / parts[LANES:LANES + 1, :], 0.0)
        weighted = parts[:LANES, :] * inv
        imp = jnp.concatenate(
            [sum(weighted[:, (2 * i + g) * Q_BLOCK:(2 * i + g + 1) * Q_BLOCK] for i in range(NSA_HPG))
             for g in range(NSA_KV_GROUPS)], axis=1)
        blk = lax.broadcasted_iota(jnp.int32, imp.shape, 0)
        t_blk = (blk_start + _lane_iota((1, imp.shape[1])) % Q_BLOCK) // SEL_BLOCK
        valid = blk <= t_blk
        forced = (blk == 0) | (valid & (blk > t_blk - N_LOCAL_SEL))
        score = jnp.where(forced, FORCE_SCORE, jnp.where(valid, imp, -1.0))
        if n_sel < LANES:
            score = jnp.where(blk < n_sel, score, REMOVED)
        return score

    def select_blocks(score):
        picked = _topk_columns(score).astype(bf16)
        eye = (lax.broadcasted_iota(jnp.int32, (Q_BLOCK, Q_BLOCK), 0)
               == lax.broadcasted_iota(jnp.int32, (Q_BLOCK, Q_BLOCK), 1)).astype(bf16)
        for g in range(NSA_KV_GROUPS):
            picked_q = _dot_nt(eye, picked[:, g * Q_BLOCK:(g + 1) * Q_BLOCK])
            selb_ref[g * Q_BLOCK:(g + 1) * Q_BLOCK, :] = ((1.0 - picked_q) * NEG_INF).astype(bf16)

    t_q = start + lax.broadcasted_iota(jnp.int32, (Q_BLOCK, 1), 0)
    w0 = pl.multiple_of(jnp.maximum(start - WINDOW, 0), Q_BLOCK)

    def window_scores():
        kpos = w0 + lax.broadcasted_iota(jnp.int32, (1, span), 1)
        bw_ref[...] = jnp.where((kpos <= t_q) & (kpos > t_q - WINDOW), 0.0, NEG_INF)
        sw_ref[...] = _dot_nt(qa_ref[:, :LANES], kw_ref[0, pl.ds(w0, span), :])

    def window_out():
        o_w = _dot(pw_ref[...], vw_ref[0, pl.ds(w0, span), :])
        ow_ref[...] = o_w[:, :LANES] / o_w[:, LANES:]

    def scores(tile):
        k0 = pl.multiple_of(tile * tk, tk)
        return _dot_nt(qa_ref[...], ksa_ref[0, pl.ds(k0, tk), :])

    @pl.when(bi == 0)
    def _():
        stack_queries(q_ref, qn_ref)
        compressed_scores(qn_ref, start)
        masked_exp(sc_ref, bc_ref, mc_ref, pc_ref)
        compressed_out(start)
        select_blocks(candidate_scores(start))

    stack_queries(q_ref, qa_ref)
    for r in range(NSA_HEADS):
        g = r % NSA_KV_GROUPS
        qa_ref[head_rows(r), LANES:] = selb_ref[g * Q_BLOCK:(g + 1) * Q_BLOCK, :]
    oc_ref[...] = ocn_ref[...]
    stack_queries(qnext_ref, qn_ref)

    nxt = start + Q_BLOCK
    compressed_scores(qn_ref, nxt)
    window_scores()
    masked_exp(sc_ref, bc_ref, mc_ref, pc_ref)
    next_score = candidate_scores(nxt)
    sa_ref[...] = scores(0)
    masked_exp(sw_ref, bw_ref, mw_ref, pw_ref)
    compressed_out(nxt)
    window_out()
    g_hi, g_lo = _split_bf16(gate_ref[0], 2)
    gx_ref[...] = _dot(g_hi, gexp_ref[...]) + _dot(g_lo, gexp_ref[...])
    select_blocks(next_score)

    m_ref[...] = jnp.full(m_ref.shape, NEG_INF, jnp.float32)
    acc_ref[...] = jnp.zeros(acc_ref.shape, jnp.float32)

    def consume(buf_ref, tile, causal):
        k0 = pl.multiple_of(tile * tk, tk)
        s = buf_ref[...]
        if causal:
            kpos = k0 + lax.broadcasted_iota(jnp.int32, (1, tk), 1)
            tile_bias = jnp.where(kpos <= t_q, 0.0, NEG_INF)
            s = s + jnp.concatenate([tile_bias] * NSA_HEADS, axis=0)
        m_prev = m_ref[...]
        m_next = jnp.maximum(m_prev, jnp.max(s, axis=-1, keepdims=True))
        p = jnp.exp2(s - _tile_lanes(m_next, tk // LANES))
        alpha = jnp.exp2(m_prev - m_next)
        acc_ref[...] = (_tile_lanes(alpha, 2) * acc_ref[...]
                        + _dot(p.astype(bf16), vs_ref[0, pl.ds(k0, tk), :]))
        m_ref[...] = m_next

    diag = start // tk

    def tile_pair(first):
        sb_ref[...] = scores(first + 1)
        consume(sa_ref, first, False)
        sa_ref[...] = scores(first + 2)
        consume(sb_ref, first + 1, False)

    def tile_quad(j, carry):
        tile_pair(4 * j)
        tile_pair(4 * j + 2)
        return carry

    lax.fori_loop(0, diag // 4, tile_quad, 0)

    @pl.when(diag % 4 >= 2)
    def _():
        tile_pair((diag // 4) * 4)

    @pl.when(diag % 2 == 1)
    def _():
        sb_ref[...] = scores(diag)
        consume(sa_ref, diag - 1, False)
        consume(sb_ref, diag, True)

    @pl.when(diag % 2 == 0)
    def _():
        consume(sa_ref, diag, True)

    for i in range(NSA_HPG):
        lo_rows, hi_rows = head_rows(2 * i), head_rows(2 * i + 1)
        o_s = jnp.where(low, acc_ref[lo_rows, :LANES] / acc_ref[lo_rows, LANES:],
                        acc_ref[hi_rows, :LANES] / acc_ref[hi_rows, LANES:])
        branches = (jnp.where(low, oc_ref[lo_rows], oc_ref[hi_rows]), o_s,
                    jnp.where(low, ow_ref[lo_rows], ow_ref[hi_rows]))
        mixed = sum(o * gx_ref[:, (3 * i + c) * LANES:(3 * i + c + 1) * LANES]
                    for c, o in enumerate(branches))
        sl = slice(i * LANES, (i + 1) * LANES)
        out_ref[0, :, sl] = (mixed * nz_ref[0, :, sl]).astype(out_ref.dtype)


def _nsa(q, nz, gates, ksa, vs, kw, vw, kcmp, vcmp, ovl, gexp):
    batch, seq_len, _ = q.shape

    def q_spec(width):
        return pl.BlockSpec((1, Q_BLOCK, width), lambda b, i: (b, i, 0))

    def seq_spec(arr):
        return pl.BlockSpec((1,) + arr.shape[1:], lambda b, i: (b, 0, 0))

    rows = NSA_HEADS * Q_BLOCK
    n_cmp = kcmp.shape[1]
    span = WINDOW + Q_BLOCK
    last = seq_len // Q_BLOCK - 1
    next_q_spec = pl.BlockSpec((1, Q_BLOCK, NSA_WIDTH), lambda b, i: (b, jnp.minimum(i + 1, last), 0))
    return pl.pallas_call(
        functools.partial(_nsa_kernel, seq_len=seq_len),
        grid=(batch, seq_len // Q_BLOCK),
        in_specs=[q_spec(NSA_WIDTH), next_q_spec, q_spec(NSA_WIDTH), q_spec(LANES),
                  seq_spec(ksa), seq_spec(vs), seq_spec(kw), seq_spec(vw),
                  seq_spec(kcmp), seq_spec(vcmp),
                  pl.BlockSpec(ovl.shape, lambda b, i: (0, 0)),
                  pl.BlockSpec(gexp.shape, lambda b, i: (0, 0))],
        out_specs=q_spec(NSA_WIDTH),
        out_shape=jax.ShapeDtypeStruct((batch, seq_len, NSA_WIDTH), jnp.bfloat16),
        scratch_shapes=[pltpu.VMEM((rows, 2 * LANES), jnp.bfloat16),
                        pltpu.VMEM((rows, n_cmp), jnp.float32),
                        pltpu.VMEM((rows, span), jnp.float32),
                        pltpu.VMEM((rows, n_cmp), jnp.bfloat16),
                        pltpu.VMEM((rows, span), jnp.bfloat16),
                        pltpu.VMEM((Q_BLOCK, n_cmp), jnp.float32),
                        pltpu.VMEM((Q_BLOCK, span), jnp.float32),
                        pltpu.VMEM((rows, LANES), jnp.float32),
                        pltpu.VMEM((rows, 2 * LANES), jnp.float32),
                        pltpu.VMEM((rows, LANES), jnp.float32),
                        pltpu.VMEM((rows, LANES), jnp.float32),
                        pltpu.VMEM((rows, SEL_KEY_TILE), jnp.float32),
                        pltpu.VMEM((rows, SEL_KEY_TILE), jnp.float32),
                        pltpu.VMEM((rows, LANES), jnp.float32),
                        pltpu.VMEM((rows, LANES), jnp.float32),
                        pltpu.VMEM((rows, LANES), jnp.bfloat16),
                        pltpu.VMEM((NSA_KV_GROUPS * Q_BLOCK, LANES), jnp.bfloat16),
                        pltpu.VMEM((rows, LANES), jnp.float32),
                        pltpu.VMEM((Q_BLOCK, gexp.shape[1]), jnp.float32)],
        compiler_params=pltpu.CompilerParams(dimension_semantics=("arbitrary", "arbitrary"),
                                             vmem_limit_bytes=VMEM_LIMIT_BYTES),
        name="nsa_attention",
    )(q, q, nz, gates, ksa, vs, kw, vw, kcmp, vcmp, ovl, gexp)


def _outproj_kernel(x_ref, ygm_ref, ynsa_ref, ymem_ref, w_ref, g_ref, b_ref, o_ref, *, alpha):
    def mix_proj(rs):
        return (_dot(ygm_ref[rs, :], w_ref[:GM_WIDTH])
                + _dot(ynsa_ref[rs, :], w_ref[GM_WIDTH:GM_WIDTH + NSA_WIDTH])
                + _dot(ymem_ref[rs, :], w_ref[GM_WIDTH + NSA_WIDTH:]))

    subs = [slice(r0, r0 + PROJ_SUB_ROWS) for r0 in range(0, x_ref.shape[0], PROJ_SUB_ROWS)]
    y_next = mix_proj(subs[0])
    for j, rs in enumerate(subs):
        y = y_next
        y_next = mix_proj(subs[j + 1]) if j + 1 < len(subs) else None
        r = alpha * x_ref[rs, :] + y
        mu = jnp.mean(r, axis=-1, keepdims=True)
        d = r - mu
        var = jnp.mean(d * d, axis=-1, keepdims=True)
        o_ref[rs, :] = d * lax.rsqrt(var + LN_EPS) * g_ref[...] + b_ref[...]


def _outproj(x2d, ygm, ynsa, ymem, w_out, ln_g, ln_b, *, layer, alpha):
    n, d_model = x2d.shape
    rows = PROJ_ROWS

    def tok_spec(width):
        return pl.BlockSpec((rows, width), lambda i: (i, 0))

    def layer_spec(shape):
        return pl.BlockSpec((None,) + shape[1:], lambda i: (layer, 0, 0))

    return pl.pallas_call(
        functools.partial(_outproj_kernel, alpha=alpha),
        grid=(n // rows,),
        in_specs=[tok_spec(d_model), tok_spec(GM_WIDTH), tok_spec(NSA_WIDTH), tok_spec(MEM_WIDTH),
                  layer_spec(w_out.shape), layer_spec(ln_g.shape), layer_spec(ln_b.shape)],
        out_specs=tok_spec(d_model),
        out_shape=jax.ShapeDtypeStruct((n, d_model), jnp.float32),
        compiler_params=pltpu.CompilerParams(dimension_semantics=("arbitrary",),
                                             vmem_limit_bytes=VMEM_LIMIT_BYTES),
        name="out_proj_layernorm",
    )(x2d, ygm, ynsa, ymem, w_out, ln_g, ln_b)


def _pair_head_slices(w, start, axis):
    return [lax.slice_in_dim(w, start + h * HEAD_DIM, start + (h + 1) * HEAD_DIM, axis=axis)
            for h in PAIR_HEAD_ORDER]


def _w_in_pieces():
    o_gate = 2048
    o_nz = o_gate + GATE_COLS
    o_mq = o_nz + NSA_WIDTH
    end = o_mq + 2 * MEM_WIDTH
    heads = lambda start: [(start + h * HEAD_DIM, HEAD_DIM) for h in PAIR_HEAD_ORDER]
    return ([(0, 768)] + heads(768) + [(1280, 768)] + heads(o_nz) + [(o_mq, end - o_mq), (o_gate, GATE_COLS)])


def _permute_w_in_kernel(w_ref, o_ref):
    dst = 0
    for src, width in _w_in_pieces():
        o_ref[:, dst:dst + width] = w_ref[:, src:src + width].astype(o_ref.dtype)
        dst += width
    o_ref[:, dst:] = jnp.zeros((o_ref.shape[0], o_ref.shape[1] - dst), o_ref.dtype)


def _permute_w_in(w):
    depth, d_model, in_cols = w.shape
    rows = PROJ_SUB_ROWS
    return pl.pallas_call(
        _permute_w_in_kernel,
        grid=(depth, d_model // rows),
        in_specs=[pl.BlockSpec((None, rows, in_cols), lambda l, i: (l, i, 0))],
        out_specs=pl.BlockSpec((None, rows, N_COLS), lambda l, i: (l, i, 0)),
        out_shape=jax.ShapeDtypeStruct((depth, d_model, N_COLS), jnp.bfloat16),
        name="w_in_relayout",
    )(w)


def _permute_w_out(w):
    pieces = ([w[:, :GM_WIDTH]] + _pair_head_slices(w, GM_WIDTH, 1) + [w[:, GM_WIDTH + NSA_WIDTH:]])
    return jnp.concatenate([p.astype(jnp.bfloat16) for p in pieces], axis=1)


def _rope_tables(seq_len):
    half = HEAD_DIM // 2
    inv_freq = ROPE_THETA ** (-jnp.arange(half, dtype=jnp.float32) * 2.0 / HEAD_DIM)
    ang = jnp.arange(seq_len).astype(jnp.float32)[:, None] * inv_freq[None, :]
    cos, sin = jnp.cos(ang), jnp.sin(ang)
    reps = LANES // HEAD_DIM
    cos_t = jnp.tile(jnp.concatenate([cos, cos], axis=1), (1, reps))
    sin_t = jnp.tile(jnp.concatenate([-sin, sin], axis=1), (1, reps))
    rot_low = ((np.arange(LANES) % HEAD_DIM) < half).astype(np.float32)[None, :]
    return cos_t, sin_t, jnp.asarray(rot_low)


def _compress_weights(pos_k, w1_k, w2_k, pos_v, w1_v, w2_v):
    half = CMP_BLOCK // 2

    def block_diag2(w):
        z = jnp.zeros_like(w)
        return jnp.concatenate([jnp.concatenate([w, z], axis=-1),
                                jnp.concatenate([z, w], axis=-1)], axis=-2)

    def expand_w1(w1):
        w = w1.reshape(2, half, HEAD_DIM, CMP_HIDDEN)
        w = block_diag2(w)
        return w.reshape(2, half * NSA_KV_WIDTH, NSA_KV_GROUPS * CMP_HIDDEN)

    def expand_w2(w2):
        return block_diag2(w2)

    def expand_pos(pos):
        p = pos.reshape(2, half, 1, HEAD_DIM)
        p = jnp.broadcast_to(p, (2, half, NSA_KV_GROUPS, HEAD_DIM))
        return p.reshape(2, half * NSA_KV_WIDTH)

    pos = jnp.stack([expand_pos(pos_k), expand_pos(pos_v)])
    w1 = jnp.stack([expand_w1(w1_k), expand_w1(w1_v)]).astype(jnp.bfloat16)
    w2 = jnp.stack([expand_w2(w2_k), expand_w2(w2_v)]).astype(jnp.bfloat16)
    return pos, w1, w2


def _gate_expansion():
    out = np.zeros((LANES, NSA_HPG * 3 * LANES), np.float32)
    for i in range(NSA_HPG):
        for c in range(3):
            base = (3 * i + c) * LANES
            out[3 * i + c, base:base + HEAD_DIM] = 1.0
            out[3 * (i + NSA_HPG) + c, base + HEAD_DIM:base + LANES] = 1.0
    return jnp.asarray(out, dtype=jnp.bfloat16)


def _overlap_matrix(n_rows, n_sel):
    c_start = np.arange(n_rows) * CMP_STRIDE
    s_start = np.arange(LANES) * SEL_BLOCK
    ovl = ((c_start[:, None] < s_start[None, :] + SEL_BLOCK)
           & (c_start[:, None] + CMP_BLOCK > s_start[None, :])
           & (np.arange(LANES)[None, :] < n_sel))
    out = np.zeros((LANES + 16, n_rows), np.float32)
    out[:LANES] = ovl.T
    out[LANES] = 1.0
    return jnp.asarray(out, dtype=jnp.bfloat16)


def kernel(x, mem, w_in, gm_ln_g, gm_ln_b, gm_ws, gm_bs, cmp_pos_k, cmp_k_w1, cmp_k_w2,
           cmp_pos_v, cmp_v_w1, cmp_v_w2, w_mem_kv, w_out, ln_g, ln_b):
    batch, seq_len, d_model = x.shape
    depth = w_in.shape[0]
    assert seq_len % SEL_KEY_TILE == 0 and seq_len >= WINDOW + Q_BLOCK
    assert SEL_TOPK <= seq_len // SEL_BLOCK <= LANES
    alpha = (2.0 * depth) ** 0.25
    n_tok = batch * seq_len
    n_rows = seq_len // CMP_STRIDE

    cos_t, sin_t, rot_low = _rope_tables(seq_len)
    ovl = _overlap_matrix(n_rows, seq_len // SEL_BLOCK)
    gexp = _gate_expansion()
    tril = jnp.tril(jnp.ones((GM_CHUNK, GM_CHUNK), gm_ws.dtype))
    mk_all, mv_all = _memkv(mem.reshape(batch * mem.shape[1], d_model), w_mem_kv.astype(jnp.bfloat16))
    w_cat_all = _permute_w_in(w_in)
    w_out_all = _permute_w_out(w_out)

    h = x.reshape(n_tok, d_model)
    for l in range(depth):
        gws = (gm_ws[l] * tril[None]).astype(jnp.bfloat16)
        gbs = jnp.repeat(gm_bs[l].T, HEAD_DIM, axis=1)
        glg = gm_ln_g[l].reshape(1, GM_WIDTH)
        glb = gm_ln_b[l].reshape(1, GM_WIDTH)
        (ygm, ymem, q, kc, vc, ksa, vs, kw, vw, nz, gates) = _inproj(
            h, w_cat_all, cos_t, sin_t, rot_low, gws, gbs, glg, glb, mk_all, mv_all,
            layer=l, batch=batch, seq_len=seq_len)

        pos, w1, w2 = _compress_weights(cmp_pos_k[l], cmp_k_w1[l], cmp_k_w2[l],
                                        cmp_pos_v[l], cmp_v_w1[l], cmp_v_w2[l])
        row_shape = (batch, n_rows, CMP_STRIDE * NSA_KV_WIDTH)
        kcmp, vcmp = _compress(kc.reshape(row_shape), vc.reshape(row_shape), pos, w1, w2)

        def per_seq(a):
            return a.reshape(batch, seq_len, a.shape[-1])

        ynsa = _nsa(per_seq(q), per_seq(nz), per_seq(gates), per_seq(ksa), per_seq(vs),
                    per_seq(kw), per_seq(vw), kcmp, vcmp, ovl, gexp)

        h = _outproj(h, ygm, ynsa.reshape(n_tok, NSA_WIDTH), ymem, w_out_all,
                     ln_g.reshape(depth, 1, d_model), ln_b.reshape(depth, 1, d_model),
                     layer=l, alpha=alpha)
    return h.reshape(batch, seq_len, d_model)
```

```python
import functools

import numpy as np
import jax
import jax.numpy as jnp
from jax import lax
from jax.experimental import pallas as pl
from jax.experimental.pallas import tpu as pltpu

HEAD_DIM = 64
GM_GROUPS = 4
GM_WIDTH = GM_GROUPS * HEAD_DIM
GM_CHUNK = 128
NSA_HEADS = 8
NSA_KV_GROUPS = 2
NSA_HPG = NSA_HEADS // NSA_KV_GROUPS
NSA_WIDTH = NSA_HEADS * HEAD_DIM
NSA_KV_WIDTH = NSA_KV_GROUPS * HEAD_DIM
CMP_BLOCK = 32
CMP_STRIDE = 16
CMP_HIDDEN = 128
SEL_BLOCK = 64
SEL_TOPK = 16
N_LOCAL_SEL = 2
WINDOW = 512
Q_BLOCK = 128
MEM_HEADS = 4
MEM_WIDTH = MEM_HEADS * HEAD_DIM
ROPE_THETA = 10000.0
LN_EPS = 1e-5
NEG_INF = -1e30
FORCE_SCORE = 1e4
GATE_COLS = NSA_HEADS * 3

LANES = 128
VMEM_LIMIT_BYTES = 56 * 1024 * 1024

PROJ_ROWS = 1024
PROJ_SUB_ROWS = 256
SEL_KEY_TILE = 512
NSA_BLOCKS_PER_STEP = 2
SOFTMAX_ROWS = 32
REMOVED = -3.0e38

PAIR_HEAD_ORDER = tuple(h for i in range(NSA_HPG) for h in (i, i + NSA_HPG))

C_GU, C_GV, C_GZ = 0, 256, 512
C_Q = 768
C_KC, C_VC, C_KS, C_VS, C_KW, C_VW = 1280, 1408, 1536, 1664, 1792, 1920
C_NZ = 2048
C_MQ, C_MZ = 2560, 2816
C_GATE = 3072
N_COLS = 3200


def _dot(a, b):
    return jnp.dot(a, b, preferred_element_type=jnp.float32)


def _dot_nt(a, b):
    return lax.dot_general(a, b, (((1,), (1,)), ((), ())), preferred_element_type=jnp.float32)


def _gelu(x):
    return 0.5 * x * (1.0 + lax.erf(x * np.float32(np.sqrt(0.5))))


def _silu(x):
    return x * jax.nn.sigmoid(x)


def _lane_iota(shape):
    return lax.broadcasted_iota(jnp.int32, shape, len(shape) - 1)


def _low_half(shape):
    return (_lane_iota(shape) % LANES) < HEAD_DIM


def _tile_lanes(x, reps):
    return jnp.concatenate([x] * reps, axis=-1) if reps > 1 else x


def _memkv_kernel(mem_ref, w_ref, k_ref, v_ref):
    kv = _dot(mem_ref[...].astype(jnp.bfloat16), w_ref[0])
    k_ref[0] = kv[:, :MEM_WIDTH].astype(jnp.bfloat16)
    v_ref[0] = kv[:, MEM_WIDTH:].astype(jnp.bfloat16)


def _memkv(mem2d, w_mem_kv_bf16):
    depth = w_mem_kv_bf16.shape[0]
    rows, d_model = mem2d.shape
    out = jax.ShapeDtypeStruct((depth, rows, MEM_WIDTH), jnp.bfloat16)
    return pl.pallas_call(
        _memkv_kernel,
        grid=(depth,),
        in_specs=[pl.BlockSpec((rows, d_model), lambda l: (0, 0)),
                  pl.BlockSpec((1, d_model, 2 * MEM_WIDTH), lambda l: (l, 0, 0))],
        out_specs=[pl.BlockSpec((1, rows, MEM_WIDTH), lambda l: (l, 0, 0)),
                   pl.BlockSpec((1, rows, MEM_WIDTH), lambda l: (l, 0, 0))],
        out_shape=[out, out],
        name="mem_kv_proj",
    )(mem2d, w_mem_kv_bf16)


def _rope(x, cos, sin_signed, low):
    width = x.shape[-1]
    swapped = jnp.where(low, pltpu.roll(x, width - HEAD_DIM // 2, 1), pltpu.roll(x, HEAD_DIM // 2, 1))
    return x * cos + swapped * sin_signed


def _group_layer_norm(v, g, b, low):
    inv = np.float32(1.0 / HEAD_DIM)
    s_lo = jnp.sum(jnp.where(low, v, 0.0), axis=-1, keepdims=True)
    s_hi = jnp.sum(jnp.where(low, 0.0, v), axis=-1, keepdims=True)
    mu = jnp.where(low, s_lo, s_hi) * inv
    d = v - mu
    d2 = d * d
    q_lo = jnp.sum(jnp.where(low, d2, 0.0), axis=-1, keepdims=True)
    q_hi = jnp.sum(jnp.where(low, 0.0, d2), axis=-1, keepdims=True)
    var = jnp.where(low, q_lo, q_hi) * inv
    return d * lax.rsqrt(var + LN_EPS) * g + b


def _inproj_kernel(x_ref, w_ref, cos_ref, sin_ref, rot_low_ref, gws_ref, gbs_ref, glg_ref, glb_ref,
                   mk_ref, mv_ref,
                   ygm_ref, ymem_ref, q_ref, kc_ref, vc_ref, ksa_ref, vs_ref, kw_ref, vw_ref,
                   nz_ref, gate_ref, stage_ref, *, seq_len):
    subs = [slice(r0, r0 + PROJ_SUB_ROWS) for r0 in range(0, x_ref.shape[0], PROJ_SUB_ROWS)]
    mixers = _project(x_ref, w_ref, subs[0], MIXER_SECTIONS)
    others = _project(x_ref, w_ref, subs[0], OTHER_SECTIONS)
    for j, rs in enumerate(subs):
        upcoming = subs[j + 1] if j + 1 < len(subs) else None
        next_mixers = _project(x_ref, w_ref, upcoming, MIXER_SECTIONS) if upcoming else None
        epilogue = _inproj_rows(rs, pl.program_id(0) * x_ref.shape[0] + rs.start, mixers, others,
                                cos_ref, sin_ref, rot_low_ref, gws_ref, gbs_ref, glg_ref, glb_ref,
                                mk_ref, mv_ref, ygm_ref, ymem_ref, q_ref, kc_ref, vc_ref, ksa_ref,
                                vs_ref, kw_ref, vw_ref, nz_ref, gate_ref, stage_ref, seq_len=seq_len)
        next(epilogue)
        next_others = _project(x_ref, w_ref, upcoming, OTHER_SECTIONS) if upcoming else None
        for _ in epilogue:
            pass
        mixers, others = next_mixers, next_others


MIXER_SECTIONS = ((C_GU, 3 * GM_WIDTH), (C_MQ, 2 * MEM_WIDTH))
OTHER_SECTIONS = ((C_Q, NSA_WIDTH), (C_KC, 6 * NSA_KV_WIDTH), (C_NZ, NSA_WIDTH), (C_GATE, LANES))


def _project(x_ref, w_ref, rs, sections):
    xb = x_ref[rs, :].astype(jnp.bfloat16)
    return tuple(_dot(xb, w_ref[:, c0:c0 + width]) for c0, width in sections)


def _inproj_rows(rs, row0, mixers, others, cos_ref, sin_ref, rot_low_ref, gws_ref, gbs_ref, glg_ref,
                 glb_ref, mk_ref, mv_ref, ygm_ref, ymem_ref, q_ref, kc_ref, vc_ref, ksa_ref, vs_ref,
                 kw_ref, vw_ref, nz_ref, gate_ref, stage_ref, *, seq_len):
    rows = rs.stop - rs.start
    gm, mem = mixers
    qh, kv, nz_raw, gate_raw = others
    low = _low_half((rows, LANES))
    rot_low = rot_low_ref[...] > 0.5
    rot_low = jnp.broadcast_to(rot_low, (rows, LANES))
    cos = cos_ref[rs, :]
    sin = sin_ref[rs, :]

    def slab(h, i):
        return h[:, i * LANES:(i + 1) * LANES]

    qscale = np.float32(HEAD_DIM ** -0.5)

    u = _gelu(gm[:, :GM_WIDTH])
    v = _gelu(gm[:, GM_WIDTH:2 * GM_WIDTH])
    z = gm[:, 2 * GM_WIDTH:]
    spatial = {}
    for pair in range(GM_GROUPS // 2):
        sl = slice(pair * LANES, (pair + 1) * LANES)
        vln = _group_layer_norm(v[:, sl], glg_ref[:, sl], glb_ref[:, sl], low).astype(jnp.bfloat16)
        for c in range(rows // GM_CHUNK):
            cs = slice(c * GM_CHUNK, (c + 1) * GM_CHUNK)
            spatial[pair, c] = (_dot(gws_ref[2 * pair], vln[cs]), _dot(gws_ref[2 * pair + 1], vln[cs]))
    mq = mem[:, :MEM_WIDTH] * qscale
    mz = mem[:, MEM_WIDTH:]
    mem_scores = {}
    for pair in range(MEM_HEADS // 2):
        sl = slice(pair * LANES, (pair + 1) * LANES)
        for keep_low in (True, False):
            qm = jnp.where(low == keep_low, mq[:, sl], 0.0).astype(jnp.bfloat16)
            mem_scores[pair, keep_low] = _dot_nt(qm, mk_ref[0, :, sl])
    yield

    for pair in range(GM_GROUPS // 2):
        sl = slice(pair * LANES, (pair + 1) * LANES)
        for c in range(rows // GM_CHUNK):
            cs = slice(c * GM_CHUNK, (c + 1) * GM_CHUNK)
            out_rows = slice(rs.start + c * GM_CHUNK, rs.start + (c + 1) * GM_CHUNK)
            s_lo, s_hi = spatial[pair, c]
            s = jnp.where(_low_half((GM_CHUNK, LANES)), s_lo, s_hi) + gbs_ref[:, sl]
            ygm_ref[out_rows, sl] = (u[cs, sl] * s * _silu(z[cs, sl])).astype(ygm_ref.dtype)

    qscale2 = np.float32(HEAD_DIM ** -0.5 * np.log2(np.e))
    ones = jnp.ones((rows, LANES), vs_ref.dtype)
    for i in range(NSA_WIDTH // LANES):
        qi = _rope(slab(qh, i), cos, sin, rot_low) * qscale2
        q_ref[rs, i * LANES:(i + 1) * LANES] = qi.astype(q_ref.dtype)
    stage_ref[0, rs, :] = _rope(slab(kv, 0), cos, sin, rot_low)
    stage_ref[1, rs, :] = slab(kv, 1)
    out_rows = slice(rs.start // CMP_STRIDE, rs.stop // CMP_STRIDE)
    for j, dst in enumerate((kc_ref, vc_ref)):
        for l in range(CMP_STRIDE):
            token_l = stage_ref[j, pl.ds(rs.start + l, rows // CMP_STRIDE, stride=CMP_STRIDE), :]
            dst[out_rows, l * LANES:(l + 1) * LANES] = token_l
    ksa_ref[rs, :LANES] = _rope(slab(kv, 2), cos, sin, rot_low).astype(ksa_ref.dtype)
    tok = row0 % seq_len + lax.broadcasted_iota(jnp.int32, (rows, LANES), 0)
    onehot = (tok // SEL_BLOCK) == _lane_iota((rows, LANES))
    ksa_ref[rs, LANES:] = jnp.where(onehot, 1.0, 0.0).astype(ksa_ref.dtype)
    vs_ref[rs, :LANES] = slab(kv, 3).astype(vs_ref.dtype)
    vs_ref[rs, LANES:] = ones
    kw_ref[rs, :] = _rope(slab(kv, 4), cos, sin, rot_low).astype(kw_ref.dtype)
    vw_ref[rs, :LANES] = slab(kv, 5).astype(vw_ref.dtype)
    vw_ref[rs, LANES:] = ones
    nz_ref[rs, :] = _silu(nz_raw)
    gate_ref[rs, :] = jax.nn.sigmoid(gate_raw)

    for pair in range(MEM_HEADS // 2):
        sl = slice(pair * LANES, (pair + 1) * LANES)
        vp = mv_ref[0, :, sl]
        outs = []
        for keep_low in (True, False):
            s = mem_scores[pair, keep_low]
            e = jnp.exp(s - jnp.max(s, axis=-1, keepdims=True))
            p = e / jnp.sum(e, axis=-1, keepdims=True)
            outs.append(_dot(p.astype(jnp.bfloat16), vp))
        o = jnp.where(low, outs[0], outs[1])
        ymem_ref[rs, sl] = (o * _silu(mz[:, sl])).astype(ymem_ref.dtype)


def _inproj(x2d, w_cat, cos_t, sin_t, rot_low, gws, gbs, glg, glb, mk, mv, *, layer, batch, seq_len):
    n, d_model = x2d.shape
    rows = PROJ_ROWS
    steps_per_seq = seq_len // rows
    mem_len = mk.shape[1] // batch

    def tok_spec(width):
        return pl.BlockSpec((rows, width), lambda i: (i, 0))

    def const_spec(shape):
        return pl.BlockSpec(shape, lambda i: (0,) * len(shape))

    def layer_spec(shape):
        return pl.BlockSpec((None,) + shape[1:], lambda i: (layer,) + (0,) * (len(shape) - 1))

    tab_spec = pl.BlockSpec((rows, LANES), lambda i: (i % steps_per_seq, 0))
    mem_spec = pl.BlockSpec((None, 1, mem_len, MEM_WIDTH), lambda i: (layer, i // steps_per_seq, 0, 0))
    bf16, f32 = jnp.bfloat16, jnp.float32
    outs = [(1, GM_WIDTH, bf16), (1, MEM_WIDTH, bf16), (1, NSA_WIDTH, bf16),
            (CMP_STRIDE, CMP_STRIDE * LANES, f32), (CMP_STRIDE, CMP_STRIDE * LANES, f32),
            (1, 2 * LANES, bf16), (1, 2 * LANES, bf16), (1, LANES, bf16), (1, 2 * LANES, bf16),
            (1, NSA_WIDTH, f32), (1, LANES, f32)]
    return pl.pallas_call(
        functools.partial(_inproj_kernel, seq_len=seq_len),
        grid=(n // rows,),
        in_specs=[tok_spec(d_model), layer_spec(w_cat.shape), tab_spec, tab_spec,
                  const_spec(rot_low.shape), const_spec(gws.shape), const_spec(gbs.shape),
                  const_spec(glg.shape), const_spec(glb.shape), mem_spec, mem_spec],
        out_specs=[pl.BlockSpec((rows // d, w), lambda i: (i, 0)) for d, w, _ in outs],
        out_shape=[jax.ShapeDtypeStruct((n // d, w), dt) for d, w, dt in outs],
        scratch_shapes=[pltpu.VMEM((2, rows, LANES), f32)],
        compiler_params=pltpu.CompilerParams(dimension_semantics=("arbitrary",),
                                             vmem_limit_bytes=VMEM_LIMIT_BYTES),
        name="in_proj_mixers",
    )(x2d, w_cat, cos_t, sin_t, rot_low, gws, gbs, glg, glb,
      mk.reshape(-1, batch, mem_len, MEM_WIDTH), mv.reshape(-1, batch, mem_len, MEM_WIDTH))


def _compress_kernel(k_ref, v_ref, pos_ref, w1_ref, w2_ref, kcmp_ref, vcmp_ref):
    n_rows = k_ref.shape[1]
    for idx, (src, dst) in enumerate(((k_ref, kcmp_ref), (v_ref, vcmp_ref))):
        xr = src[0]
        top = _dot((xr + pos_ref[idx, 0:1]).astype(jnp.bfloat16), w1_ref[idx, 0])
        bot = _dot((xr + pos_ref[idx, 1:2]).astype(jnp.bfloat16), w1_ref[idx, 1])
        hidden = top + pltpu.roll(bot, n_rows - 1, 0)
        act = jax.nn.gelu(hidden, approximate=True)
        dst[0, :, :LANES] = _dot(act.astype(jnp.bfloat16), w2_ref[idx]).astype(dst.dtype)
    vcmp_ref[0, :, LANES:] = jnp.ones((n_rows, LANES), vcmp_ref.dtype)


def _compress(kc_rows, vc_rows, pos, w1, w2):
    batch, n_rows, width = kc_rows.shape
    row_spec = pl.BlockSpec((1, n_rows, width), lambda b: (b, 0, 0))
    def out_spec(width):
        return pl.BlockSpec((1, n_rows, width), lambda b: (b, 0, 0))

    def out(width):
        return jax.ShapeDtypeStruct((batch, n_rows, width), jnp.bfloat16)

    return pl.pallas_call(
        _compress_kernel,
        grid=(batch,),
        in_specs=[row_spec, row_spec,
                  pl.BlockSpec(pos.shape, lambda b: (0, 0, 0)),
                  pl.BlockSpec(w1.shape, lambda b: (0, 0, 0, 0)),
                  pl.BlockSpec(w2.shape, lambda b: (0, 0, 0))],
        out_specs=[out_spec(LANES), out_spec(2 * LANES)],
        out_shape=[out(LANES), out(2 * LANES)],
        compiler_params=pltpu.CompilerParams(dimension_semantics=("arbitrary",),
                                             vmem_limit_bytes=VMEM_LIMIT_BYTES),
        name="nsa_compress",
    )(kc_rows, vc_rows, pos, w1, w2)


def _split_bf16(x, parts):
    out = []
    for _ in range(parts):
        hi = x.astype(jnp.bfloat16)
        out.append(hi)
        x = x - hi.astype(jnp.float32)
    return out


def _topk_columns(score):
    row = lax.broadcasted_iota(jnp.int32, score.shape, 0).astype(jnp.float32)
    picked = jnp.zeros(score.shape, jnp.float32)
    for _ in range(SEL_TOPK):
        best = jnp.max(score, axis=0, keepdims=True)
        first = jnp.min(jnp.where(score == best, row, np.float32(score.shape[0])),
                        axis=0, keepdims=True)
        hit = row == first
        picked = jnp.where(hit, 1.0, picked)
        score = jnp.where(hit, REMOVED, score)
    return picked


def _nsa_kernel(*refs, seq_len):
    for j in range(NSA_BLOCKS_PER_STEP):
        _nsa_block(j, *refs, seq_len=seq_len)


def _nsa_block(j, q_ref, qnext_ref, nz_ref, gate_ref, ksa_ref, vs_ref, kw_ref, vw_ref, kcmp_ref,
               vcmp_ref, ovl_ref, gexp_ref, out_ref, qa_ref, sc_ref, sw_ref, pc_ref, pw_ref, bc_ref,
               bw_ref, m_ref, acc_ref, oc_ref, ow_ref, sa_ref, sb_ref, mc_ref, mw_ref, qn_ref,
               selb_ref, ocn_ref, gx_ref, *, seq_len):
    bi = pl.program_id(1) * NSA_BLOCKS_PER_STEP + j
    start = bi * Q_BLOCK
    blk_rows = slice(j * Q_BLOCK, (j + 1) * Q_BLOCK)
    if j + 1 < NSA_BLOCKS_PER_STEP:
        next_src, next_rows = q_ref, slice((j + 1) * Q_BLOCK, (j + 2) * Q_BLOCK)
    else:
        next_src, next_rows = qnext_ref, slice(0, Q_BLOCK)
    n_sel = seq_len // SEL_BLOCK
    n_cmp = kcmp_ref.shape[1]
    span = WINDOW + Q_BLOCK
    tk = SEL_KEY_TILE
    bf16 = jnp.bfloat16

    def head_rows(r):
        return slice(r * Q_BLOCK, (r + 1) * Q_BLOCK)

    low = _low_half((Q_BLOCK, LANES))
    chunk = SOFTMAX_ROWS
    n_chunks = Q_BLOCK // chunk

    def stack_queries(src_ref, src_rows, dst_ref):
        for i in range(NSA_HPG):
            qi = src_ref[0, src_rows, i * LANES:(i + 1) * LANES]
            zero = jnp.zeros_like(qi)
            dst_ref[head_rows(2 * i), :LANES] = jnp.where(low, qi, zero)
            dst_ref[head_rows(2 * i + 1), :LANES] = jnp.where(low, zero, qi)

    def compressed_scores(qs_ref, blk_start):
        t_b = blk_start + lax.broadcasted_iota(jnp.int32, (Q_BLOCK, 1), 0)
        c_end = lax.broadcasted_iota(jnp.int32, (1, n_cmp), 1) * CMP_STRIDE + (CMP_BLOCK - 1)
        bc_ref[...] = jnp.where(c_end <= t_b, 0.0, NEG_INF)
        sc_ref[...] = _dot_nt(qs_ref[:, :LANES], kcmp_ref[0])

    def masked_exp(s_ref, b_ref, m_ref_, p_ref):
        width_tiles = s_ref.shape[1] // LANES
        for r in range(NSA_HEADS):
            for c in range(n_chunks):
                crow = slice(c * chunk, (c + 1) * chunk)
                rows = slice(r * Q_BLOCK + c * chunk, r * Q_BLOCK + (c + 1) * chunk)
                row_max = jnp.max(s_ref[rows, :] + b_ref[crow, :], axis=-1, keepdims=True)
                m_ref_[rows, :] = jnp.broadcast_to(row_max, (chunk, LANES))
        for r in range(NSA_HEADS):
            for c in range(n_chunks):
                crow = slice(c * chunk, (c + 1) * chunk)
                rows = slice(r * Q_BLOCK + c * chunk, r * Q_BLOCK + (c + 1) * chunk)
                s = s_ref[rows, :] + b_ref[crow, :]
                p_ref[rows, :] = jnp.exp2(s - _tile_lanes(m_ref_[rows, :], width_tiles)).astype(bf16)

    def compressed_out(blk_start):
        t_col = blk_start + lax.broadcasted_iota(jnp.int32, (Q_BLOCK, 1), 0)
        seen_col = jnp.concatenate([t_col >= CMP_BLOCK - 1] * NSA_HEADS, axis=0)
        o_c = _dot(pc_ref[...], vcmp_ref[0])
        ocn_ref[...] = jnp.where(seen_col, o_c[:, :LANES] / o_c[:, LANES:], 0.0)

    def candidate_scores(blk_start):
        parts = _dot_nt(ovl_ref[...], pc_ref[...])
        t_lane = blk_start + _lane_iota((1, parts.shape[1])) % Q_BLOCK
        inv = jnp.where(t_lane >= CMP_BLOCK - 1, 1.0 / parts[LANES:LANES + 1, :], 0.0)
        weighted = parts[:LANES, :] * inv
        imp = jnp.concatenate(
            [sum(weighted[:, (2 * i + g) * Q_BLOCK:(2 * i + g + 1) * Q_BLOCK] for i in range(NSA_HPG))
             for g in range(NSA_KV_GROUPS)], axis=1)
        blk = lax.broadcasted_iota(jnp.int32, imp.shape, 0)
        t_blk = (blk_start + _lane_iota((1, imp.shape[1])) % Q_BLOCK) // SEL_BLOCK
        valid = blk <= t_blk
        forced = (blk == 0) | (valid & (blk > t_blk - N_LOCAL_SEL))
        score = jnp.where(forced, FORCE_SCORE, jnp.where(valid, imp, -1.0))
        if n_sel < LANES:
            score = jnp.where(blk < n_sel, score, REMOVED)
        return score

    def select_blocks(score):
        picked = _topk_columns(score).astype(bf16)
        eye = (lax.broadcasted_iota(jnp.int32, (Q_BLOCK, Q_BLOCK), 0)
               == lax.broadcasted_iota(jnp.int32, (Q_BLOCK, Q_BLOCK), 1)).astype(bf16)
        for g in range(NSA_KV_GROUPS):
            picked_q = _dot_nt(eye, picked[:, g * Q_BLOCK:(g + 1) * Q_BLOCK])
            selb_ref[g * Q_BLOCK:(g + 1) * Q_BLOCK, :] = ((1.0 - picked_q) * NEG_INF).astype(bf16)

    t_q = start + lax.broadcasted_iota(jnp.int32, (Q_BLOCK, 1), 0)
    w0 = pl.multiple_of(jnp.maximum(start - WINDOW, 0), Q_BLOCK)

    def window_scores():
        kpos = w0 + lax.broadcasted_iota(jnp.int32, (1, span), 1)
        bw_ref[...] = jnp.where((kpos <= t_q) & (kpos > t_q - WINDOW), 0.0, NEG_INF)
        sw_ref[...] = _dot_nt(qa_ref[:, :LANES], kw_ref[0, pl.ds(w0, span), :])

    def window_out():
        o_w = _dot(pw_ref[...], vw_ref[0, pl.ds(w0, span), :])
        ow_ref[...] = o_w[:, :LANES] / o_w[:, LANES:]

    def scores(tile):
        k0 = pl.multiple_of(tile * tk, tk)
        return _dot_nt(qa_ref[...], ksa_ref[0, pl.ds(k0, tk), :])

    if j == 0:
        @pl.when(bi == 0)
        def _():
            stack_queries(q_ref, blk_rows, qn_ref)
            compressed_scores(qn_ref, start)
            masked_exp(sc_ref, bc_ref, mc_ref, pc_ref)
            compressed_out(start)
            select_blocks(candidate_scores(start))

    stack_queries(q_ref, blk_rows, qa_ref)
    for r in range(NSA_HEADS):
        g = r % NSA_KV_GROUPS
        qa_ref[head_rows(r), LANES:] = selb_ref[g * Q_BLOCK:(g + 1) * Q_BLOCK, :]
    oc_ref[...] = ocn_ref[...]
    stack_queries(next_src, next_rows, qn_ref)

    nxt = start + Q_BLOCK
    compressed_scores(qn_ref, nxt)
    window_scores()
    masked_exp(sc_ref, bc_ref, mc_ref, pc_ref)
    next_score = candidate_scores(nxt)
    sa_ref[...] = scores(0)
    masked_exp(sw_ref, bw_ref, mw_ref, pw_ref)
    compressed_out(nxt)
    window_out()
    g_hi, g_lo = _split_bf16(gate_ref[0, blk_rows, :], 2)
    gx_ref[...] = _dot(g_hi, gexp_ref[...]) + _dot(g_lo, gexp_ref[...])
    select_blocks(next_score)

    m_ref[...] = jnp.full(m_ref.shape, NEG_INF, jnp.float32)
    acc_ref[...] = jnp.zeros(acc_ref.shape, jnp.float32)

    def consume(buf_ref, tile, causal):
        k0 = pl.multiple_of(tile * tk, tk)
        s = buf_ref[...]
        if causal:
            kpos = k0 + lax.broadcasted_iota(jnp.int32, (1, tk), 1)
            tile_bias = jnp.where(kpos <= t_q, 0.0, NEG_INF)
            s = s + jnp.concatenate([tile_bias] * NSA_HEADS, axis=0)
        m_prev = m_ref[...]
        m_next = jnp.maximum(m_prev, jnp.max(s, axis=-1, keepdims=True))
        p = jnp.exp2(s - _tile_lanes(m_next, tk // LANES))
        alpha = jnp.exp2(m_prev - m_next)
        acc_ref[...] = (_tile_lanes(alpha, 2) * acc_ref[...]
                        + _dot(p.astype(bf16), vs_ref[0, pl.ds(k0, tk), :]))
        m_ref[...] = m_next

    diag = start // tk

    def tile_pair(first):
        sb_ref[...] = scores(first + 1)
        consume(sa_ref, first, False)
        sa_ref[...] = scores(first + 2)
        consume(sb_ref, first + 1, False)

    def tile_quad(j, carry):
        tile_pair(4 * j)
        tile_pair(4 * j + 2)
        return carry

    lax.fori_loop(0, diag // 4, tile_quad, 0)

    @pl.when(diag % 4 >= 2)
    def _():
        tile_pair((diag // 4) * 4)

    @pl.when(diag % 2 == 1)
    def _():
        sb_ref[...] = scores(diag)
        consume(sa_ref, diag - 1, False)
        consume(sb_ref, diag, True)

    @pl.when(diag % 2 == 0)
    def _():
        consume(sa_ref, diag, True)

    for i in range(NSA_HPG):
        lo_rows, hi_rows = head_rows(2 * i), head_rows(2 * i + 1)
        o_s = jnp.where(low, acc_ref[lo_rows, :LANES] / acc_ref[lo_rows, LANES:],
                        acc_ref[hi_rows, :LANES] / acc_ref[hi_rows, LANES:])
        branches = (jnp.where(low, oc_ref[lo_rows], oc_ref[hi_rows]), o_s,
                    jnp.where(low, ow_ref[lo_rows], ow_ref[hi_rows]))
        mixed = sum(o * gx_ref[:, (3 * i + c) * LANES:(3 * i + c + 1) * LANES]
                    for c, o in enumerate(branches))
        sl = slice(i * LANES, (i + 1) * LANES)
        out_ref[0, blk_rows, sl] = (mixed * nz_ref[0, blk_rows, sl]).astype(out_ref.dtype)


def _nsa(q, nz, gates, ksa, vs, kw, vw, kcmp, vcmp, ovl, gexp):
    batch, seq_len, _ = q.shape

    per_step = NSA_BLOCKS_PER_STEP

    def q_spec(width):
        return pl.BlockSpec((1, per_step * Q_BLOCK, width), lambda b, i: (b, i, 0))

    def seq_spec(arr):
        return pl.BlockSpec((1,) + arr.shape[1:], lambda b, i: (b, 0, 0))

    rows = NSA_HEADS * Q_BLOCK
    n_cmp = kcmp.shape[1]
    span = WINDOW + Q_BLOCK
    last = seq_len // Q_BLOCK - 1
    next_q_spec = pl.BlockSpec((1, Q_BLOCK, NSA_WIDTH),
                               lambda b, i: (b, jnp.minimum(per_step * (i + 1), last), 0))
    return pl.pallas_call(
        functools.partial(_nsa_kernel, seq_len=seq_len),
        grid=(batch, seq_len // (per_step * Q_BLOCK)),
        in_specs=[q_spec(NSA_WIDTH), next_q_spec, q_spec(NSA_WIDTH), q_spec(LANES),
                  seq_spec(ksa), seq_spec(vs), seq_spec(kw), seq_spec(vw),
                  seq_spec(kcmp), seq_spec(vcmp),
                  pl.BlockSpec(ovl.shape, lambda b, i: (0, 0)),
                  pl.BlockSpec(gexp.shape, lambda b, i: (0, 0))],
        out_specs=q_spec(NSA_WIDTH),
        out_shape=jax.ShapeDtypeStruct((batch, seq_len, NSA_WIDTH), jnp.bfloat16),
        scratch_shapes=[pltpu.VMEM((rows, 2 * LANES), jnp.bfloat16),
                        pltpu.VMEM((rows, n_cmp), jnp.float32),
                        pltpu.VMEM((rows, span), jnp.float32),
                        pltpu.VMEM((rows, n_cmp), jnp.bfloat16),
                        pltpu.VMEM((rows, span), jnp.bfloat16),
                        pltpu.VMEM((Q_BLOCK, n_cmp), jnp.float32),
                        pltpu.VMEM((Q_BLOCK, span), jnp.float32),
                        pltpu.VMEM((rows, LANES), jnp.float32),
                        pltpu.VMEM((rows, 2 * LANES), jnp.float32),
                        pltpu.VMEM((rows, LANES), jnp.float32),
                        pltpu.VMEM((rows, LANES), jnp.float32),
                        pltpu.VMEM((rows, SEL_KEY_TILE), jnp.float32),
                        pltpu.VMEM((rows, SEL_KEY_TILE), jnp.float32),
                        pltpu.VMEM((rows, LANES), jnp.float32),
                        pltpu.VMEM((rows, LANES), jnp.float32),
                        pltpu.VMEM((rows, LANES), jnp.bfloat16),
                        pltpu.VMEM((NSA_KV_GROUPS * Q_BLOCK, LANES), jnp.bfloat16),
                        pltpu.VMEM((rows, LANES), jnp.float32),
                        pltpu.VMEM((Q_BLOCK, gexp.shape[1]), jnp.float32)],
        compiler_params=pltpu.CompilerParams(dimension_semantics=("arbitrary", "arbitrary"),
                                             vmem_limit_bytes=VMEM_LIMIT_BYTES),
        name="nsa_attention",
    )(q, q, nz, gates, ksa, vs, kw, vw, kcmp, vcmp, ovl, gexp)


def _outproj_kernel(x_ref, ygm_ref, ynsa_ref, ymem_ref, w_ref, g_ref, b_ref, o_ref, *, alpha):
    def mix_proj(rs):
        return (_dot(ygm_ref[rs, :], w_ref[:GM_WIDTH])
                + _dot(ynsa_ref[rs, :], w_ref[GM_WIDTH:GM_WIDTH + NSA_WIDTH])
                + _dot(ymem_ref[rs, :], w_ref[GM_WIDTH + NSA_WIDTH:]))

    subs = [slice(r0, r0 + PROJ_SUB_ROWS) for r0 in range(0, x_ref.shape[0], PROJ_SUB_ROWS)]
    y_next = mix_proj(subs[0])
    for j, rs in enumerate(subs):
        y = y_next
        y_next = mix_proj(subs[j + 1]) if j + 1 < len(subs) else None
        r = alpha * x_ref[rs, :] + y
        mu = jnp.mean(r, axis=-1, keepdims=True)
        d = r - mu
        var = jnp.mean(d * d, axis=-1, keepdims=True)
        o_ref[rs, :] = d * lax.rsqrt(var + LN_EPS) * g_ref[...] + b_ref[...]


def _outproj(x2d, ygm, ynsa, ymem, w_out, ln_g, ln_b, *, layer, alpha):
    n, d_model = x2d.shape
    rows = PROJ_ROWS

    def tok_spec(width):
        return pl.BlockSpec((rows, width), lambda i: (i, 0))

    def layer_spec(shape):
        return pl.BlockSpec((None,) + shape[1:], lambda i: (layer, 0, 0))

    return pl.pallas_call(
        functools.partial(_outproj_kernel, alpha=alpha),
        grid=(n // rows,),
        in_specs=[tok_spec(d_model), tok_spec(GM_WIDTH), tok_spec(NSA_WIDTH), tok_spec(MEM_WIDTH),
                  layer_spec(w_out.shape), layer_spec(ln_g.shape), layer_spec(ln_b.shape)],
        out_specs=tok_spec(d_model),
        out_shape=jax.ShapeDtypeStruct((n, d_model), jnp.float32),
        compiler_params=pltpu.CompilerParams(dimension_semantics=("arbitrary",),
                                             vmem_limit_bytes=VMEM_LIMIT_BYTES),
        name="out_proj_layernorm",
    )(x2d, ygm, ynsa, ymem, w_out, ln_g, ln_b)


def _pair_head_slices(w, start, axis):
    return [lax.slice_in_dim(w, start + h * HEAD_DIM, start + (h + 1) * HEAD_DIM, axis=axis)
            for h in PAIR_HEAD_ORDER]


def _w_in_pieces():
    o_gate = 2048
    o_nz = o_gate + GATE_COLS
    o_mq = o_nz + NSA_WIDTH
    end = o_mq + 2 * MEM_WIDTH
    heads = lambda start: [(start + h * HEAD_DIM, HEAD_DIM) for h in PAIR_HEAD_ORDER]
    return ([(0, 768)] + heads(768) + [(1280, 768)] + heads(o_nz) + [(o_mq, end - o_mq), (o_gate, GATE_COLS)])


def _permute_w_in_kernel(w_ref, o_ref):
    dst = 0
    for src, width in _w_in_pieces():
        o_ref[:, dst:dst + width] = w_ref[:, src:src + width].astype(o_ref.dtype)
        dst += width
    o_ref[:, dst:] = jnp.zeros((o_ref.shape[0], o_ref.shape[1] - dst), o_ref.dtype)


def _permute_w_in(w):
    depth, d_model, in_cols = w.shape
    rows = PROJ_SUB_ROWS
    return pl.pallas_call(
        _permute_w_in_kernel,
        grid=(depth, d_model // rows),
        in_specs=[pl.BlockSpec((None, rows, in_cols), lambda l, i: (l, i, 0))],
        out_specs=pl.BlockSpec((None, rows, N_COLS), lambda l, i: (l, i, 0)),
        out_shape=jax.ShapeDtypeStruct((depth, d_model, N_COLS), jnp.bfloat16),
        name="w_in_relayout",
    )(w)


def _permute_w_out(w):
    pieces = ([w[:, :GM_WIDTH]] + _pair_head_slices(w, GM_WIDTH, 1) + [w[:, GM_WIDTH + NSA_WIDTH:]])
    return jnp.concatenate([p.astype(jnp.bfloat16) for p in pieces], axis=1)


def _rope_tables(seq_len):
    half = HEAD_DIM // 2
    inv_freq = ROPE_THETA ** (-jnp.arange(half, dtype=jnp.float32) * 2.0 / HEAD_DIM)
    ang = jnp.arange(seq_len).astype(jnp.float32)[:, None] * inv_freq[None, :]
    cos, sin = jnp.cos(ang), jnp.sin(ang)
    reps = LANES // HEAD_DIM
    cos_t = jnp.tile(jnp.concatenate([cos, cos], axis=1), (1, reps))
    sin_t = jnp.tile(jnp.concatenate([-sin, sin], axis=1), (1, reps))
    rot_low = ((np.arange(LANES) % HEAD_DIM) < half).astype(np.float32)[None, :]
    return cos_t, sin_t, jnp.asarray(rot_low)


def _compress_weights(pos_k, w1_k, w2_k, pos_v, w1_v, w2_v):
    half = CMP_BLOCK // 2

    def block_diag2(w):
        z = jnp.zeros_like(w)
        return jnp.concatenate([jnp.concatenate([w, z], axis=-1),
                                jnp.concatenate([z, w], axis=-1)], axis=-2)

    def expand_w1(w1):
        w = w1.reshape(2, half, HEAD_DIM, CMP_HIDDEN)
        w = block_diag2(w)
        return w.reshape(2, half * NSA_KV_WIDTH, NSA_KV_GROUPS * CMP_HIDDEN)

    def expand_w2(w2):
        return block_diag2(w2)

    def expand_pos(pos):
        p = pos.reshape(2, half, 1, HEAD_DIM)
        p = jnp.broadcast_to(p, (2, half, NSA_KV_GROUPS, HEAD_DIM))
        return p.reshape(2, half * NSA_KV_WIDTH)

    pos = jnp.stack([expand_pos(pos_k), expand_pos(pos_v)])
    w1 = jnp.stack([expand_w1(w1_k), expand_w1(w1_v)]).astype(jnp.bfloat16)
    w2 = jnp.stack([expand_w2(w2_k), expand_w2(w2_v)]).astype(jnp.bfloat16)
    return pos, w1, w2


def _gate_expansion():
    out = np.zeros((LANES, NSA_HPG * 3 * LANES), np.float32)
    for i in range(NSA_HPG):
        for c in range(3):
            base = (3 * i + c) * LANES
            out[3 * i + c, base:base + HEAD_DIM] = 1.0
            out[3 * (i + NSA_HPG) + c, base + HEAD_DIM:base + LANES] = 1.0
    return jnp.asarray(out, dtype=jnp.bfloat16)


def _overlap_matrix(n_rows, n_sel):
    c_start = np.arange(n_rows) * CMP_STRIDE
    s_start = np.arange(LANES) * SEL_BLOCK
    ovl = ((c_start[:, None] < s_start[None, :] + SEL_BLOCK)
           & (c_start[:, None] + CMP_BLOCK > s_start[None, :])
           & (np.arange(LANES)[None, :] < n_sel))
    out = np.zeros((LANES + 16, n_rows), np.float32)
    out[:LANES] = ovl.T
    out[LANES] = 1.0
    return jnp.asarray(out, dtype=jnp.bfloat16)


def kernel(x, mem, w_in, gm_ln_g, gm_ln_b, gm_ws, gm_bs, cmp_pos_k, cmp_k_w1, cmp_k_w2,
           cmp_pos_v, cmp_v_w1, cmp_v_w2, w_mem_kv, w_out, ln_g, ln_b):
    batch, seq_len, d_model = x.shape
    depth = w_in.shape[0]
    assert seq_len % SEL_KEY_TILE == 0 and seq_len >= WINDOW + Q_BLOCK
    assert SEL_TOPK <= seq_len // SEL_BLOCK <= LANES
    alpha = (2.0 * depth) ** 0.25
    n_tok = batch * seq_len
    n_rows = seq_len // CMP_STRIDE

    cos_t, sin_t, rot_low = _rope_tables(seq_len)
    ovl = _overlap_matrix(n_rows, seq_len // SEL_BLOCK)
    gexp = _gate_expansion()
    tril = jnp.tril(jnp.ones((GM_CHUNK, GM_CHUNK), gm_ws.dtype))
    mk_all, mv_all = _memkv(mem.reshape(batch * mem.shape[1], d_model), w_mem_kv.astype(jnp.bfloat16))
    w_cat_all = _permute_w_in(w_in)
    w_out_all = _permute_w_out(w_out)

    h = x.reshape(n_tok, d_model)
    for l in range(depth):
        gws = (gm_ws[l] * tril[None]).astype(jnp.bfloat16)
        gbs = jnp.repeat(gm_bs[l].T, HEAD_DIM, axis=1)
        glg = gm_ln_g[l].reshape(1, GM_WIDTH)
        glb = gm_ln_b[l].reshape(1, GM_WIDTH)
        (ygm, ymem, q, kc, vc, ksa, vs, kw, vw, nz, gates) = _inproj(
            h, w_cat_all, cos_t, sin_t, rot_low, gws, gbs, glg, glb, mk_all, mv_all,
            layer=l, batch=batch, seq_len=seq_len)

        pos, w1, w2 = _compress_weights(cmp_pos_k[l], cmp_k_w1[l], cmp_k_w2[l],
                                        cmp_pos_v[l], cmp_v_w1[l], cmp_v_w2[l])
        row_shape = (batch, n_rows, CMP_STRIDE * NSA_KV_WIDTH)
        kcmp, vcmp = _compress(kc.reshape(row_shape), vc.reshape(row_shape), pos, w1, w2)

        def per_seq(a):
            return a.reshape(batch, seq_len, a.shape[-1])

        ynsa = _nsa(per_seq(q), per_seq(nz), per_seq(gates), per_seq(ksa), per_seq(vs),
                    per_seq(kw), per_seq(vw), kcmp, vcmp, ovl, gexp)

        h = _outproj(h, ygm, ynsa.reshape(n_tok, NSA_WIDTH), ymem, w_out_all,
                     ln_g.reshape(depth, 1, d_model), ln_b.reshape(depth, 1, d_model),
                     layer=l, alpha=alpha)
    return h.reshape(batch, seq_len, d_model)
```

```python
import functools

import numpy as np
import jax
import jax.numpy as jnp
from jax import lax
from jax.experimental import pallas as pl
from jax.experimental.pallas import tpu as pltpu

HEAD_DIM = 64
GM_GROUPS = 4
GM_WIDTH = GM_GROUPS * HEAD_DIM
GM_CHUNK = 128
NSA_HEADS = 8
NSA_KV_GROUPS = 2
NSA_HPG = NSA_HEADS // NSA_KV_GROUPS
NSA_WIDTH = NSA_HEADS * HEAD_DIM
NSA_KV_WIDTH = NSA_KV_GROUPS * HEAD_DIM
CMP_BLOCK = 32
CMP_STRIDE = 16
CMP_HIDDEN = 128
SEL_BLOCK = 64
SEL_TOPK = 16
N_LOCAL_SEL = 2
WINDOW = 512
Q_BLOCK = 128
MEM_HEADS = 4
MEM_WIDTH = MEM_HEADS * HEAD_DIM
ROPE_THETA = 10000.0
LN_EPS = 1e-5
NEG_INF = -1e30
FORCE_SCORE = 1e4
GATE_COLS = NSA_HEADS * 3

LANES = 128
VMEM_LIMIT_BYTES = 56 * 1024 * 1024

PROJ_ROWS = 1024
PROJ_SUB_ROWS = 256
SEL_KEY_TILE = 512
SOFTMAX_ROWS = 32
REMOVED = -3.0e38

PAIR_HEAD_ORDER = tuple(h for i in range(NSA_HPG) for h in (i, i + NSA_HPG))

C_GU, C_GV, C_GZ = 0, 256, 512
C_Q = 768
C_KC, C_VC, C_KS, C_VS, C_KW, C_VW = 1280, 1408, 1536, 1664, 1792, 1920
C_NZ = 2048
C_MQ, C_MZ = 2560, 2816
C_GATE = 3072
N_COLS = 3200


def _dot(a, b):
    return jnp.dot(a, b, preferred_element_type=jnp.float32)


def _dot_nt(a, b):
    return lax.dot_general(a, b, (((1,), (1,)), ((), ())), preferred_element_type=jnp.float32)


def _gelu(x):
    return 0.5 * x * (1.0 + lax.erf(x * np.float32(np.sqrt(0.5))))


def _silu(x):
    return x * jax.nn.sigmoid(x)


def _lane_iota(shape):
    return lax.broadcasted_iota(jnp.int32, shape, len(shape) - 1)


def _low_half(shape):
    return (_lane_iota(shape) % LANES) < HEAD_DIM


def _tile_lanes(x, reps):
    return jnp.concatenate([x] * reps, axis=-1) if reps > 1 else x


def _memkv_kernel(mem_ref, w_ref, k_ref, v_ref):
    kv = _dot(mem_ref[...].astype(jnp.bfloat16), w_ref[0])
    k_ref[0] = kv[:, :MEM_WIDTH].astype(jnp.bfloat16)
    v_ref[0] = kv[:, MEM_WIDTH:].astype(jnp.bfloat16)


def _memkv(mem2d, w_mem_kv_bf16):
    depth = w_mem_kv_bf16.shape[0]
    rows, d_model = mem2d.shape
    out = jax.ShapeDtypeStruct((depth, rows, MEM_WIDTH), jnp.bfloat16)
    return pl.pallas_call(
        _memkv_kernel,
        grid=(depth,),
        in_specs=[pl.BlockSpec((rows, d_model), lambda l: (0, 0)),
                  pl.BlockSpec((1, d_model, 2 * MEM_WIDTH), lambda l: (l, 0, 0))],
        out_specs=[pl.BlockSpec((1, rows, MEM_WIDTH), lambda l: (l, 0, 0)),
                   pl.BlockSpec((1, rows, MEM_WIDTH), lambda l: (l, 0, 0))],
        out_shape=[out, out],
        name="mem_kv_proj",
    )(mem2d, w_mem_kv_bf16)


def _rope(x, cos, sin_signed, low):
    width = x.shape[-1]
    swapped = jnp.where(low, pltpu.roll(x, width - HEAD_DIM // 2, 1), pltpu.roll(x, HEAD_DIM // 2, 1))
    return x * cos + swapped * sin_signed


def _group_layer_norm(v, g, b, low):
    inv = np.float32(1.0 / HEAD_DIM)
    s_lo = jnp.sum(jnp.where(low, v, 0.0), axis=-1, keepdims=True)
    s_hi = jnp.sum(jnp.where(low, 0.0, v), axis=-1, keepdims=True)
    mu = jnp.where(low, s_lo, s_hi) * inv
    d = v - mu
    d2 = d * d
    q_lo = jnp.sum(jnp.where(low, d2, 0.0), axis=-1, keepdims=True)
    q_hi = jnp.sum(jnp.where(low, 0.0, d2), axis=-1, keepdims=True)
    var = jnp.where(low, q_lo, q_hi) * inv
    return d * lax.rsqrt(var + LN_EPS) * g + b


def _inproj_kernel(x_ref, w_ref, cos_ref, sin_ref, rot_low_ref, gws_ref, gbs_ref, glg_ref, glb_ref,
                   mk_ref, mv_ref,
                   ygm_ref, ymem_ref, q_ref, kc_ref, vc_ref, ksa_ref, vs_ref, kw_ref, vw_ref,
                   nz_ref, gate_ref, stage_ref, *, seq_len):
    subs = [slice(r0, r0 + PROJ_SUB_ROWS) for r0 in range(0, x_ref.shape[0], PROJ_SUB_ROWS)]
    mixers = _project(x_ref, w_ref, subs[0], MIXER_SECTIONS)
    others = _project(x_ref, w_ref, subs[0], OTHER_SECTIONS)
    for j, rs in enumerate(subs):
        upcoming = subs[j + 1] if j + 1 < len(subs) else None
        next_mixers = _project(x_ref, w_ref, upcoming, MIXER_SECTIONS) if upcoming else None
        epilogue = _inproj_rows(rs, pl.program_id(0) * x_ref.shape[0] + rs.start, mixers, others,
                                cos_ref, sin_ref, rot_low_ref, gws_ref, gbs_ref, glg_ref, glb_ref,
                                mk_ref, mv_ref, ygm_ref, ymem_ref, q_ref, kc_ref, vc_ref, ksa_ref,
                                vs_ref, kw_ref, vw_ref, nz_ref, gate_ref, stage_ref, seq_len=seq_len)
        next(epilogue)
        next_others = _project(x_ref, w_ref, upcoming, OTHER_SECTIONS) if upcoming else None
        for _ in epilogue:
            pass
        mixers, others = next_mixers, next_others


MIXER_SECTIONS = ((C_GU, 3 * GM_WIDTH), (C_MQ, 2 * MEM_WIDTH))
OTHER_SECTIONS = ((C_Q, NSA_WIDTH), (C_KC, 6 * NSA_KV_WIDTH), (C_NZ, NSA_WIDTH), (C_GATE, LANES))


def _project(x_ref, w_ref, rs, sections):
    xb = x_ref[rs, :].astype(jnp.bfloat16)
    return tuple(_dot(xb, w_ref[:, c0:c0 + width]) for c0, width in sections)


def _inproj_rows(rs, row0, mixers, others, cos_ref, sin_ref, rot_low_ref, gws_ref, gbs_ref, glg_ref,
                 glb_ref, mk_ref, mv_ref, ygm_ref, ymem_ref, q_ref, kc_ref, vc_ref, ksa_ref, vs_ref,
                 kw_ref, vw_ref, nz_ref, gate_ref, stage_ref, *, seq_len):
    rows = rs.stop - rs.start
    gm, mem = mixers
    qh, kv, nz_raw, gate_raw = others
    low = _low_half((rows, LANES))
    rot_low = rot_low_ref[...] > 0.5
    rot_low = jnp.broadcast_to(rot_low, (rows, LANES))
    cos = cos_ref[rs, :]
    sin = sin_ref[rs, :]

    def slab(h, i):
        return h[:, i * LANES:(i + 1) * LANES]

    qscale = np.float32(HEAD_DIM ** -0.5)

    u = _gelu(gm[:, :GM_WIDTH])
    v = _gelu(gm[:, GM_WIDTH:2 * GM_WIDTH])
    z = gm[:, 2 * GM_WIDTH:]
    spatial = {}
    for pair in range(GM_GROUPS // 2):
        sl = slice(pair * LANES, (pair + 1) * LANES)
        vln = _group_layer_norm(v[:, sl], glg_ref[:, sl], glb_ref[:, sl], low).astype(jnp.bfloat16)
        for c in range(rows // GM_CHUNK):
            cs = slice(c * GM_CHUNK, (c + 1) * GM_CHUNK)
            spatial[pair, c] = (_dot(gws_ref[2 * pair], vln[cs]), _dot(gws_ref[2 * pair + 1], vln[cs]))
    mq = mem[:, :MEM_WIDTH] * qscale
    mz = mem[:, MEM_WIDTH:]
    mem_scores = {}
    for pair in range(MEM_HEADS // 2):
        sl = slice(pair * LANES, (pair + 1) * LANES)
        for keep_low in (True, False):
            qm = jnp.where(low == keep_low, mq[:, sl], 0.0).astype(jnp.bfloat16)
            mem_scores[pair, keep_low] = _dot_nt(qm, mk_ref[0, :, sl])
    yield

    for pair in range(GM_GROUPS // 2):
        sl = slice(pair * LANES, (pair + 1) * LANES)
        for c in range(rows // GM_CHUNK):
            cs = slice(c * GM_CHUNK, (c + 1) * GM_CHUNK)
            out_rows = slice(rs.start + c * GM_CHUNK, rs.start + (c + 1) * GM_CHUNK)
            s_lo, s_hi = spatial[pair, c]
            s = jnp.where(_low_half((GM_CHUNK, LANES)), s_lo, s_hi) + gbs_ref[:, sl]
            ygm_ref[out_rows, sl] = (u[cs, sl] * s * _silu(z[cs, sl])).astype(ygm_ref.dtype)

    qscale2 = np.float32(HEAD_DIM ** -0.5 * np.log2(np.e))
    ones = jnp.ones((rows, LANES), vs_ref.dtype)
    for i in range(NSA_WIDTH // LANES):
        qi = _rope(slab(qh, i), cos, sin, rot_low) * qscale2
        q_ref[rs, i * LANES:(i + 1) * LANES] = qi.astype(q_ref.dtype)
    stage_ref[0, rs, :] = _rope(slab(kv, 0), cos, sin, rot_low)
    stage_ref[1, rs, :] = slab(kv, 1)
    out_rows = slice(rs.start // CMP_STRIDE, rs.stop // CMP_STRIDE)
    for j, dst in enumerate((kc_ref, vc_ref)):
        for l in range(CMP_STRIDE):
            token_l = stage_ref[j, pl.ds(rs.start + l, rows // CMP_STRIDE, stride=CMP_STRIDE), :]
            dst[out_rows, l * LANES:(l + 1) * LANES] = token_l
    ksa_ref[rs, :LANES] = _rope(slab(kv, 2), cos, sin, rot_low).astype(ksa_ref.dtype)
    tok = row0 % seq_len + lax.broadcasted_iota(jnp.int32, (rows, LANES), 0)
    onehot = (tok // SEL_BLOCK) == _lane_iota((rows, LANES))
    ksa_ref[rs, LANES:] = jnp.where(onehot, 1.0, 0.0).astype(ksa_ref.dtype)
    vs_ref[rs, :LANES] = slab(kv, 3).astype(vs_ref.dtype)
    vs_ref[rs, LANES:] = ones
    kw_ref[rs, :] = _rope(slab(kv, 4), cos, sin, rot_low).astype(kw_ref.dtype)
    vw_ref[rs, :LANES] = slab(kv, 5).astype(vw_ref.dtype)
    vw_ref[rs, LANES:] = ones
    nz_ref[rs, :] = _silu(nz_raw)
    gate_ref[rs, :] = jax.nn.sigmoid(gate_raw)

    for pair in range(MEM_HEADS // 2):
        sl = slice(pair * LANES, (pair + 1) * LANES)
        vp = mv_ref[0, :, sl]
        outs = []
        for keep_low in (True, False):
            s = mem_scores[pair, keep_low]
            e = jnp.exp(s - jnp.max(s, axis=-1, keepdims=True))
            p = e / jnp.sum(e, axis=-1, keepdims=True)
            outs.append(_dot(p.astype(jnp.bfloat16), vp))
        o = jnp.where(low, outs[0], outs[1])
        ymem_ref[rs, sl] = (o * _silu(mz[:, sl])).astype(ymem_ref.dtype)


def _inproj(x2d, w_cat, cos_t, sin_t, rot_low, gws, gbs, glg, glb, mk, mv, *, layer, batch, seq_len):
    n, d_model = x2d.shape
    rows = PROJ_ROWS
    steps_per_seq = seq_len // rows
    mem_len = mk.shape[1] // batch

    def tok_spec(width):
        return pl.BlockSpec((rows, width), lambda i: (i, 0))

    def const_spec(shape):
        return pl.BlockSpec(shape, lambda i: (0,) * len(shape))

    def layer_spec(shape):
        return pl.BlockSpec((None,) + shape[1:], lambda i: (layer,) + (0,) * (len(shape) - 1))

    tab_spec = pl.BlockSpec((rows, LANES), lambda i: (i % steps_per_seq, 0))
    mem_spec = pl.BlockSpec((None, 1, mem_len, MEM_WIDTH), lambda i: (layer, i // steps_per_seq, 0, 0))
    bf16, f32 = jnp.bfloat16, jnp.float32
    outs = [(1, GM_WIDTH, bf16), (1, MEM_WIDTH, bf16), (1, NSA_WIDTH, bf16),
            (CMP_STRIDE, CMP_STRIDE * LANES, f32), (CMP_STRIDE, CMP_STRIDE * LANES, f32),
            (1, 2 * LANES, bf16), (1, 2 * LANES, bf16), (1, LANES, bf16), (1, 2 * LANES, bf16),
            (1, NSA_WIDTH, f32), (1, LANES, f32)]
    return pl.pallas_call(
        functools.partial(_inproj_kernel, seq_len=seq_len),
        grid=(n // rows,),
        in_specs=[tok_spec(d_model), layer_spec(w_cat.shape), tab_spec, tab_spec,
                  const_spec(rot_low.shape), const_spec(gws.shape), const_spec(gbs.shape),
                  const_spec(glg.shape), const_spec(glb.shape), mem_spec, mem_spec],
        out_specs=[pl.BlockSpec((rows // d, w), lambda i: (i, 0)) for d, w, _ in outs],
        out_shape=[jax.ShapeDtypeStruct((n // d, w), dt) for d, w, dt in outs],
        scratch_shapes=[pltpu.VMEM((2, rows, LANES), f32)],
        compiler_params=pltpu.CompilerParams(dimension_semantics=("arbitrary",),
                                             vmem_limit_bytes=VMEM_LIMIT_BYTES),
        name="in_proj_mixers",
    )(x2d, w_cat, cos_t, sin_t, rot_low, gws, gbs, glg, glb,
      mk.reshape(-1, batch, mem_len, MEM_WIDTH), mv.reshape(-1, batch, mem_len, MEM_WIDTH))


def _compress_kernel(k_ref, v_ref, pos_ref, w1_ref, w2_ref, kcmp_ref, vcmp_ref):
    n_rows = k_ref.shape[1]
    for idx, (src, dst) in enumerate(((k_ref, kcmp_ref), (v_ref, vcmp_ref))):
        xr = src[0]
        top = _dot((xr + pos_ref[idx, 0:1]).astype(jnp.bfloat16), w1_ref[idx, 0])
        bot = _dot((xr + pos_ref[idx, 1:2]).astype(jnp.bfloat16), w1_ref[idx, 1])
        hidden = top + pltpu.roll(bot, n_rows - 1, 0)
        act = jax.nn.gelu(hidden, approximate=True)
        dst[0, :, :LANES] = _dot(act.astype(jnp.bfloat16), w2_ref[idx]).astype(dst.dtype)
    vcmp_ref[0, :, LANES:] = jnp.ones((n_rows, LANES), vcmp_ref.dtype)


def _compress(kc_rows, vc_rows, pos, w1, w2):
    batch, n_rows, width = kc_rows.shape
    row_spec = pl.BlockSpec((1, n_rows, width), lambda b: (b, 0, 0))
    def out_spec(width):
        return pl.BlockSpec((1, n_rows, width), lambda b: (b, 0, 0))

    def out(width):
        return jax.ShapeDtypeStruct((batch, n_rows, width), jnp.bfloat16)

    return pl.pallas_call(
        _compress_kernel,
        grid=(batch,),
        in_specs=[row_spec, row_spec,
                  pl.BlockSpec(pos.shape, lambda b: (0, 0, 0)),
                  pl.BlockSpec(w1.shape, lambda b: (0, 0, 0, 0)),
                  pl.BlockSpec(w2.shape, lambda b: (0, 0, 0))],
        out_specs=[out_spec(LANES), out_spec(2 * LANES)],
        out_shape=[out(LANES), out(2 * LANES)],
        compiler_params=pltpu.CompilerParams(dimension_semantics=("arbitrary",),
                                             vmem_limit_bytes=VMEM_LIMIT_BYTES),
        name="nsa_compress",
    )(kc_rows, vc_rows, pos, w1, w2)


def _split_bf16(x, parts):
    out = []
    for _ in range(parts):
        hi = x.astype(jnp.bfloat16)
        out.append(hi)
        x = x - hi.astype(jnp.float32)
    return out


def _topk_columns(score):
    row = lax.broadcasted_iota(jnp.int32, score.shape, 0).astype(jnp.float32)
    picked = jnp.zeros(score.shape, jnp.float32)
    for _ in range(SEL_TOPK):
        best = jnp.max(score, axis=0, keepdims=True)
        first = jnp.min(jnp.where(score == best, row, np.float32(score.shape[0])),
                        axis=0, keepdims=True)
        hit = row == first
        picked = jnp.where(hit, 1.0, picked)
        score = jnp.where(hit, REMOVED, score)
    return picked


def _nsa_kernel(q_ref, qnext_ref, nz_ref, gate_ref, ksa_ref, vs_ref, kw_ref, vw_ref, kcmp_ref,
                vcmp_ref, ovl_ref, gexp_ref, out_ref, qa_ref, sc_ref, sw_ref, pc_ref, pw_ref, bc_ref,
                bw_ref, m_ref, acc_ref, oc_ref, ow_ref, sa_ref, sb_ref, mc_ref, mw_ref, qn_ref,
                selb_ref, ocn_ref, gx_ref, *, seq_len):
    bi = pl.program_id(1)
    start = bi * Q_BLOCK
    n_sel = seq_len // SEL_BLOCK
    n_cmp = kcmp_ref.shape[1]
    span = WINDOW + Q_BLOCK
    tk = SEL_KEY_TILE
    bf16 = jnp.bfloat16

    def head_rows(r):
        return slice(r * Q_BLOCK, (r + 1) * Q_BLOCK)

    low = _low_half((Q_BLOCK, LANES))
    chunk = SOFTMAX_ROWS
    n_chunks = Q_BLOCK // chunk

    def stack_queries(src_ref, dst_ref):
        for i in range(NSA_HPG):
            qi = src_ref[0, :, i * LANES:(i + 1) * LANES]
            zero = jnp.zeros_like(qi)
            dst_ref[head_rows(2 * i), :LANES] = jnp.where(low, qi, zero)
            dst_ref[head_rows(2 * i + 1), :LANES] = jnp.where(low, zero, qi)

    def compressed_scores(qs_ref, blk_start):
        t_b = blk_start + lax.broadcasted_iota(jnp.int32, (Q_BLOCK, 1), 0)
        c_end = lax.broadcasted_iota(jnp.int32, (1, n_cmp), 1) * CMP_STRIDE + (CMP_BLOCK - 1)
        bc_ref[...] = jnp.where(c_end <= t_b, 0.0, NEG_INF)
        sc_ref[...] = _dot_nt(qs_ref[:, :LANES], kcmp_ref[0])

    def masked_exp(s_ref, b_ref, m_ref_, p_ref):
        width_tiles = s_ref.shape[1] // LANES
        for r in range(NSA_HEADS):
            for c in range(n_chunks):
                crow = slice(c * chunk, (c + 1) * chunk)
                rows = slice(r * Q_BLOCK + c * chunk, r * Q_BLOCK + (c + 1) * chunk)
                row_max = jnp.max(s_ref[rows, :] + b_ref[crow, :], axis=-1, keepdims=True)
                m_ref_[rows, :] = jnp.broadcast_to(row_max, (chunk, LANES))
        for r in range(NSA_HEADS):
            for c in range(n_chunks):
                crow = slice(c * chunk, (c + 1) * chunk)
                rows = slice(r * Q_BLOCK + c * chunk, r * Q_BLOCK + (c + 1) * chunk)
                s = s_ref[rows, :] + b_ref[crow, :]
                p_ref[rows, :] = jnp.exp2(s - _tile_lanes(m_ref_[rows, :], width_tiles)).astype(bf16)

    def compressed_out(blk_start):
        t_col = blk_start + lax.broadcasted_iota(jnp.int32, (Q_BLOCK, 1), 0)
        seen_col = jnp.concatenate([t_col >= CMP_BLOCK - 1] * NSA_HEADS, axis=0)
        o_c = _dot(pc_ref[...], vcmp_ref[0])
        ocn_ref[...] = jnp.where(seen_col, o_c[:, :LANES] / o_c[:, LANES:], 0.0)

    def candidate_scores(blk_start):
        parts = _dot_nt(ovl_ref[...], pc_ref[...])
        t_lane = blk_start + _lane_iota((1, parts.shape[1])) % Q_BLOCK
        inv = jnp.where(t_lane >= CMP_BLOCK - 1, 1.0 / parts[LANES:LANES + 1, :], 0.0)
        weighted = parts[:LANES, :] * inv
        imp = jnp.concatenate(
            [sum(weighted[:, (2 * i + g) * Q_BLOCK:(2 * i + g + 1) * Q_BLOCK] for i in range(NSA_HPG))
             for g in range(NSA_KV_GROUPS)], axis=1)
        blk = lax.broadcasted_iota(jnp.int32, imp.shape, 0)
        t_blk = (blk_start + _lane_iota((1, imp.shape[1])) % Q_BLOCK) // SEL_BLOCK
        valid = blk <= t_blk
        forced = (blk == 0) | (valid & (blk > t_blk - N_LOCAL_SEL))
        score = jnp.where(forced, FORCE_SCORE, jnp.where(valid, imp, -1.0))
        if n_sel < LANES:
            score = jnp.where(blk < n_sel, score, REMOVED)
        return score

    def select_blocks(score):
        picked = _topk_columns(score).astype(bf16)
        eye = (lax.broadcasted_iota(jnp.int32, (Q_BLOCK, Q_BLOCK), 0)
               == lax.broadcasted_iota(jnp.int32, (Q_BLOCK, Q_BLOCK), 1)).astype(bf16)
        for g in range(NSA_KV_GROUPS):
            picked_q = _dot_nt(eye, picked[:, g * Q_BLOCK:(g + 1) * Q_BLOCK])
            selb_ref[g * Q_BLOCK:(g + 1) * Q_BLOCK, :] = ((1.0 - picked_q) * NEG_INF).astype(bf16)

    t_q = start + lax.broadcasted_iota(jnp.int32, (Q_BLOCK, 1), 0)
    w0 = pl.multiple_of(jnp.maximum(start - WINDOW, 0), Q_BLOCK)

    def window_scores():
        kpos = w0 + lax.broadcasted_iota(jnp.int32, (1, span), 1)
        bw_ref[...] = jnp.where((kpos <= t_q) & (kpos > t_q - WINDOW), 0.0, NEG_INF)
        sw_ref[...] = _dot_nt(qa_ref[:, :LANES], kw_ref[0, pl.ds(w0, span), :])

    def window_out():
        o_w = _dot(pw_ref[...], vw_ref[0, pl.ds(w0, span), :])
        ow_ref[...] = o_w[:, :LANES] / o_w[:, LANES:]

    def scores(tile):
        k0 = pl.multiple_of(tile * tk, tk)
        return _dot(qa_ref[...], ksa_ref[0, :, pl.ds(k0, tk)])

    @pl.when(bi == 0)
    def _():
        stack_queries(q_ref, qn_ref)
        compressed_scores(qn_ref, start)
        masked_exp(sc_ref, bc_ref, mc_ref, pc_ref)
        compressed_out(start)
        select_blocks(candidate_scores(start))

    stack_queries(q_ref, qa_ref)
    for r in range(NSA_HEADS):
        g = r % NSA_KV_GROUPS
        qa_ref[head_rows(r), LANES:] = selb_ref[g * Q_BLOCK:(g + 1) * Q_BLOCK, :]
    oc_ref[...] = ocn_ref[...]
    stack_queries(qnext_ref, qn_ref)

    nxt = start + Q_BLOCK
    compressed_scores(qn_ref, nxt)
    window_scores()
    masked_exp(sc_ref, bc_ref, mc_ref, pc_ref)
    next_score = candidate_scores(nxt)
    sa_ref[...] = scores(0)
    masked_exp(sw_ref, bw_ref, mw_ref, pw_ref)
    compressed_out(nxt)
    window_out()
    g_hi, g_lo = _split_bf16(gate_ref[0], 2)
    gx_ref[...] = _dot(g_hi, gexp_ref[...]) + _dot(g_lo, gexp_ref[...])
    select_blocks(next_score)

    m_ref[...] = jnp.full(m_ref.shape, NEG_INF, jnp.float32)
    acc_ref[...] = jnp.zeros(acc_ref.shape, jnp.float32)

    def consume(buf_ref, tile, causal):
        k0 = pl.multiple_of(tile * tk, tk)
        s = buf_ref[...]
        if causal:
            kpos = k0 + lax.broadcasted_iota(jnp.int32, (1, tk), 1)
            tile_bias = jnp.where(kpos <= t_q, 0.0, NEG_INF)
            s = s + jnp.concatenate([tile_bias] * NSA_HEADS, axis=0)
        m_prev = m_ref[...]
        m_next = jnp.maximum(m_prev, jnp.max(s, axis=-1, keepdims=True))
        p = jnp.exp2(s - _tile_lanes(m_next, tk // LANES))
        alpha = jnp.exp2(m_prev - m_next)
        acc_ref[...] = (_tile_lanes(alpha, 2) * acc_ref[...]
                        + _dot(p.astype(bf16), vs_ref[0, pl.ds(k0, tk), :]))
        m_ref[...] = m_next

    diag = start // tk

    def tile_pair(first):
        sb_ref[...] = scores(first + 1)
        consume(sa_ref, first, False)
        sa_ref[...] = scores(first + 2)
        consume(sb_ref, first + 1, False)

    def tile_quad(j, carry):
        tile_pair(4 * j)
        tile_pair(4 * j + 2)
        return carry

    lax.fori_loop(0, diag // 4, tile_quad, 0)

    @pl.when(diag % 4 >= 2)
    def _():
        tile_pair((diag // 4) * 4)

    @pl.when(diag % 2 == 1)
    def _():
        sb_ref[...] = scores(diag)
        consume(sa_ref, diag - 1, False)
        consume(sb_ref, diag, True)

    @pl.when(diag % 2 == 0)
    def _():
        consume(sa_ref, diag, True)

    for i in range(NSA_HPG):
        lo_rows, hi_rows = head_rows(2 * i), head_rows(2 * i + 1)
        o_s = jnp.where(low, acc_ref[lo_rows, :LANES] / acc_ref[lo_rows, LANES:],
                        acc_ref[hi_rows, :LANES] / acc_ref[hi_rows, LANES:])
        branches = (jnp.where(low, oc_ref[lo_rows], oc_ref[hi_rows]), o_s,
                    jnp.where(low, ow_ref[lo_rows], ow_ref[hi_rows]))
        mixed = sum(o * gx_ref[:, (3 * i + c) * LANES:(3 * i + c + 1) * LANES]
                    for c, o in enumerate(branches))
        sl = slice(i * LANES, (i + 1) * LANES)
        out_ref[0, :, sl] = (mixed * nz_ref[0, :, sl]).astype(out_ref.dtype)


def _nsa(q, nz, gates, ksa, vs, kw, vw, kcmp, vcmp, ovl, gexp):
    batch, seq_len, _ = q.shape

    def q_spec(width):
        return pl.BlockSpec((1, Q_BLOCK, width), lambda b, i: (b, i, 0))

    def seq_spec(arr):
        return pl.BlockSpec((1,) + arr.shape[1:], lambda b, i: (b, 0, 0))

    rows = NSA_HEADS * Q_BLOCK
    n_cmp = kcmp.shape[1]
    span = WINDOW + Q_BLOCK
    last = seq_len // Q_BLOCK - 1
    next_q_spec = pl.BlockSpec((1, Q_BLOCK, NSA_WIDTH), lambda b, i: (b, jnp.minimum(i + 1, last), 0))
    return pl.pallas_call(
        functools.partial(_nsa_kernel, seq_len=seq_len),
        grid=(batch, seq_len // Q_BLOCK),
        in_specs=[q_spec(NSA_WIDTH), next_q_spec, q_spec(NSA_WIDTH), q_spec(LANES),
                  seq_spec(ksa), seq_spec(vs), seq_spec(kw), seq_spec(vw),
                  seq_spec(kcmp), seq_spec(vcmp),
                  pl.BlockSpec(ovl.shape, lambda b, i: (0, 0)),
                  pl.BlockSpec(gexp.shape, lambda b, i: (0, 0))],
        out_specs=q_spec(NSA_WIDTH),
        out_shape=jax.ShapeDtypeStruct((batch, seq_len, NSA_WIDTH), jnp.bfloat16),
        scratch_shapes=[pltpu.VMEM((rows, 2 * LANES), jnp.bfloat16),
                        pltpu.VMEM((rows, n_cmp), jnp.float32),
                        pltpu.VMEM((rows, span), jnp.float32),
                        pltpu.VMEM((rows, n_cmp), jnp.bfloat16),
                        pltpu.VMEM((rows, span), jnp.bfloat16),
                        pltpu.VMEM((Q_BLOCK, n_cmp), jnp.float32),
                        pltpu.VMEM((Q_BLOCK, span), jnp.float32),
                        pltpu.VMEM((rows, LANES), jnp.float32),
                        pltpu.VMEM((rows, 2 * LANES), jnp.float32),
                        pltpu.VMEM((rows, LANES), jnp.float32),
                        pltpu.VMEM((rows, LANES), jnp.float32),
                        pltpu.VMEM((rows, SEL_KEY_TILE), jnp.float32),
                        pltpu.VMEM((rows, SEL_KEY_TILE), jnp.float32),
                        pltpu.VMEM((rows, LANES), jnp.float32),
                        pltpu.VMEM((rows, LANES), jnp.float32),
                        pltpu.VMEM((rows, LANES), jnp.bfloat16),
                        pltpu.VMEM((NSA_KV_GROUPS * Q_BLOCK, LANES), jnp.bfloat16),
                        pltpu.VMEM((rows, LANES), jnp.float32),
                        pltpu.VMEM((Q_BLOCK, gexp.shape[1]), jnp.float32)],
        compiler_params=pltpu.CompilerParams(dimension_semantics=("arbitrary", "arbitrary"),
                                             vmem_limit_bytes=VMEM_LIMIT_BYTES),
        name="nsa_attention",
    )(q, q, nz, gates, ksa, vs, kw, vw, kcmp, vcmp, ovl, gexp)


def _outproj_kernel(x_ref, ygm_ref, ynsa_ref, ymem_ref, w_ref, g_ref, b_ref, o_ref, *, alpha):
    def mix_proj(rs):
        return (_dot(ygm_ref[rs, :], w_ref[:GM_WIDTH])
                + _dot(ynsa_ref[rs, :], w_ref[GM_WIDTH:GM_WIDTH + NSA_WIDTH])
                + _dot(ymem_ref[rs, :], w_ref[GM_WIDTH + NSA_WIDTH:]))

    subs = [slice(r0, r0 + PROJ_SUB_ROWS) for r0 in range(0, x_ref.shape[0], PROJ_SUB_ROWS)]
    y_next = mix_proj(subs[0])
    for j, rs in enumerate(subs):
        y = y_next
        y_next = mix_proj(subs[j + 1]) if j + 1 < len(subs) else None
        r = alpha * x_ref[rs, :] + y
        mu = jnp.mean(r, axis=-1, keepdims=True)
        d = r - mu
        var = jnp.mean(d * d, axis=-1, keepdims=True)
        o_ref[rs, :] = d * lax.rsqrt(var + LN_EPS) * g_ref[...] + b_ref[...]


def _outproj(x2d, ygm, ynsa, ymem, w_out, ln_g, ln_b, *, layer, alpha):
    n, d_model = x2d.shape
    rows = PROJ_ROWS

    def tok_spec(width):
        return pl.BlockSpec((rows, width), lambda i: (i, 0))

    def layer_spec(shape):
        return pl.BlockSpec((None,) + shape[1:], lambda i: (layer, 0, 0))

    return pl.pallas_call(
        functools.partial(_outproj_kernel, alpha=alpha),
        grid=(n // rows,),
        in_specs=[tok_spec(d_model), tok_spec(GM_WIDTH), tok_spec(NSA_WIDTH), tok_spec(MEM_WIDTH),
                  layer_spec(w_out.shape), layer_spec(ln_g.shape), layer_spec(ln_b.shape)],
        out_specs=tok_spec(d_model),
        out_shape=jax.ShapeDtypeStruct((n, d_model), jnp.float32),
        compiler_params=pltpu.CompilerParams(dimension_semantics=("arbitrary",),
                                             vmem_limit_bytes=VMEM_LIMIT_BYTES),
        name="out_proj_layernorm",
    )(x2d, ygm, ynsa, ymem, w_out, ln_g, ln_b)


def _pair_head_slices(w, start, axis):
    return [lax.slice_in_dim(w, start + h * HEAD_DIM, start + (h + 1) * HEAD_DIM, axis=axis)
            for h in PAIR_HEAD_ORDER]


def _w_in_pieces():
    o_gate = 2048
    o_nz = o_gate + GATE_COLS
    o_mq = o_nz + NSA_WIDTH
    end = o_mq + 2 * MEM_WIDTH
    heads = lambda start: [(start + h * HEAD_DIM, HEAD_DIM) for h in PAIR_HEAD_ORDER]
    return ([(0, 768)] + heads(768) + [(1280, 768)] + heads(o_nz) + [(o_mq, end - o_mq), (o_gate, GATE_COLS)])


def _permute_w_in_kernel(w_ref, o_ref):
    dst = 0
    for src, width in _w_in_pieces():
        o_ref[:, dst:dst + width] = w_ref[:, src:src + width].astype(o_ref.dtype)
        dst += width
    o_ref[:, dst:] = jnp.zeros((o_ref.shape[0], o_ref.shape[1] - dst), o_ref.dtype)


def _permute_w_in(w):
    depth, d_model, in_cols = w.shape
    rows = PROJ_SUB_ROWS
    return pl.pallas_call(
        _permute_w_in_kernel,
        grid=(depth, d_model // rows),
        in_specs=[pl.BlockSpec((None, rows, in_cols), lambda l, i: (l, i, 0))],
        out_specs=pl.BlockSpec((None, rows, N_COLS), lambda l, i: (l, i, 0)),
        out_shape=jax.ShapeDtypeStruct((depth, d_model, N_COLS), jnp.bfloat16),
        name="w_in_relayout",
    )(w)


def _permute_w_out(w):
    pieces = ([w[:, :GM_WIDTH]] + _pair_head_slices(w, GM_WIDTH, 1) + [w[:, GM_WIDTH + NSA_WIDTH:]])
    return jnp.concatenate([p.astype(jnp.bfloat16) for p in pieces], axis=1)


def _rope_tables(seq_len):
    half = HEAD_DIM // 2
    inv_freq = ROPE_THETA ** (-jnp.arange(half, dtype=jnp.float32) * 2.0 / HEAD_DIM)
    ang = jnp.arange(seq_len).astype(jnp.float32)[:, None] * inv_freq[None, :]
    cos, sin = jnp.cos(ang), jnp.sin(ang)
    reps = LANES // HEAD_DIM
    cos_t = jnp.tile(jnp.concatenate([cos, cos], axis=1), (1, reps))
    sin_t = jnp.tile(jnp.concatenate([-sin, sin], axis=1), (1, reps))
    rot_low = ((np.arange(LANES) % HEAD_DIM) < half).astype(np.float32)[None, :]
    return cos_t, sin_t, jnp.asarray(rot_low)


def _compress_weights(pos_k, w1_k, w2_k, pos_v, w1_v, w2_v):
    half = CMP_BLOCK // 2

    def block_diag2(w):
        z = jnp.zeros_like(w)
        return jnp.concatenate([jnp.concatenate([w, z], axis=-1),
                                jnp.concatenate([z, w], axis=-1)], axis=-2)

    def expand_w1(w1):
        w = w1.reshape(2, half, HEAD_DIM, CMP_HIDDEN)
        w = block_diag2(w)
        return w.reshape(2, half * NSA_KV_WIDTH, NSA_KV_GROUPS * CMP_HIDDEN)

    def expand_w2(w2):
        return block_diag2(w2)

    def expand_pos(pos):
        p = pos.reshape(2, half, 1, HEAD_DIM)
        p = jnp.broadcast_to(p, (2, half, NSA_KV_GROUPS, HEAD_DIM))
        return p.reshape(2, half * NSA_KV_WIDTH)

    pos = jnp.stack([expand_pos(pos_k), expand_pos(pos_v)])
    w1 = jnp.stack([expand_w1(w1_k), expand_w1(w1_v)]).astype(jnp.bfloat16)
    w2 = jnp.stack([expand_w2(w2_k), expand_w2(w2_v)]).astype(jnp.bfloat16)
    return pos, w1, w2


def _gate_expansion():
    out = np.zeros((LANES, NSA_HPG * 3 * LANES), np.float32)
    for i in range(NSA_HPG):
        for c in range(3):
            base = (3 * i + c) * LANES
            out[3 * i + c, base:base + HEAD_DIM] = 1.0
            out[3 * (i + NSA_HPG) + c, base + HEAD_DIM:base + LANES] = 1.0
    return jnp.asarray(out, dtype=jnp.bfloat16)


def _overlap_matrix(n_rows, n_sel):
    c_start = np.arange(n_rows) * CMP_STRIDE
    s_start = np.arange(LANES) * SEL_BLOCK
    ovl = ((c_start[:, None] < s_start[None, :] + SEL_BLOCK)
           & (c_start[:, None] + CMP_BLOCK > s_start[None, :])
           & (np.arange(LANES)[None, :] < n_sel))
    out = np.zeros((LANES + 16, n_rows), np.float32)
    out[:LANES] = ovl.T
    out[LANES] = 1.0
    return jnp.asarray(out, dtype=jnp.bfloat16)


def kernel(x, mem, w_in, gm_ln_g, gm_ln_b, gm_ws, gm_bs, cmp_pos_k, cmp_k_w1, cmp_k_w2,
           cmp_pos_v, cmp_v_w1, cmp_v_w2, w_mem_kv, w_out, ln_g, ln_b):
    batch, seq_len, d_model = x.shape
    depth = w_in.shape[0]
    assert seq_len % SEL_KEY_TILE == 0 and seq_len >= WINDOW + Q_BLOCK
    assert SEL_TOPK <= seq_len // SEL_BLOCK <= LANES
    alpha = (2.0 * depth) ** 0.25
    n_tok = batch * seq_len
    n_rows = seq_len // CMP_STRIDE

    cos_t, sin_t, rot_low = _rope_tables(seq_len)
    ovl = _overlap_matrix(n_rows, seq_len // SEL_BLOCK)
    gexp = _gate_expansion()
    tril = jnp.tril(jnp.ones((GM_CHUNK, GM_CHUNK), gm_ws.dtype))
    mk_all, mv_all = _memkv(mem.reshape(batch * mem.shape[1], d_model), w_mem_kv.astype(jnp.bfloat16))
    w_cat_all = _permute_w_in(w_in)
    w_out_all = _permute_w_out(w_out)

    h = x.reshape(n_tok, d_model)
    for l in range(depth):
        gws = (gm_ws[l] * tril[None]).astype(jnp.bfloat16)
        gbs = jnp.repeat(gm_bs[l].T, HEAD_DIM, axis=1)
        glg = gm_ln_g[l].reshape(1, GM_WIDTH)
        glb = gm_ln_b[l].reshape(1, GM_WIDTH)
        (ygm, ymem, q, kc, vc, ksa, vs, kw, vw, nz, gates) = _inproj(
            h, w_cat_all, cos_t, sin_t, rot_low, gws, gbs, glg, glb, mk_all, mv_all,
            layer=l, batch=batch, seq_len=seq_len)

        pos, w1, w2 = _compress_weights(cmp_pos_k[l], cmp_k_w1[l], cmp_k_w2[l],
                                        cmp_pos_v[l], cmp_v_w1[l], cmp_v_w2[l])
        row_shape = (batch, n_rows, CMP_STRIDE * NSA_KV_WIDTH)
        kcmp, vcmp = _compress(kc.reshape(row_shape), vc.reshape(row_shape), pos, w1, w2)

        def per_seq(a):
            return a.reshape(batch, seq_len, a.shape[-1])

        ynsa = _nsa(per_seq(q), per_seq(nz), per_seq(gates), jnp.swapaxes(per_seq(ksa), 1, 2), per_seq(vs),
                    per_seq(kw), per_seq(vw), kcmp, vcmp, ovl, gexp)

        h = _outproj(h, ygm, ynsa.reshape(n_tok, NSA_WIDTH), ymem, w_out_all,
                     ln_g.reshape(depth, 1, d_model), ln_b.reshape(depth, 1, d_model),
                     layer=l, alpha=alpha)
    return h.reshape(batch, seq_len, d_model)
```

```python
import functools

import numpy as np
import jax
import jax.numpy as jnp
from jax import lax
from jax.experimental import pallas as pl
from jax.experimental.pallas import tpu as pltpu

HEAD_DIM = 64
GM_GROUPS = 4
GM_WIDTH = GM_GROUPS * HEAD_DIM
GM_CHUNK = 128
NSA_HEADS = 8
NSA_KV_GROUPS = 2
NSA_HPG = NSA_HEADS // NSA_KV_GROUPS
NSA_WIDTH = NSA_HEADS * HEAD_DIM
NSA_KV_WIDTH = NSA_KV_GROUPS * HEAD_DIM
CMP_BLOCK = 32
CMP_STRIDE = 16
CMP_HIDDEN = 128
SEL_BLOCK = 64
SEL_TOPK = 16
N_LOCAL_SEL = 2
WINDOW = 512
Q_BLOCK = 128
MEM_HEADS = 4
MEM_WIDTH = MEM_HEADS * HEAD_DIM
ROPE_THETA = 10000.0
LN_EPS = 1e-5
NEG_INF = -1e30
FORCE_SCORE = 1e4
GATE_COLS = NSA_HEADS * 3

LANES = 128
VMEM_LIMIT_BYTES = 56 * 1024 * 1024

PROJ_ROWS = 1024
OUT_PROJ_ROWS = 2048
PROJ_SUB_ROWS = 256
SEL_KEY_TILE = 512
SOFTMAX_ROWS = 32
REMOVED = -3.0e38

PAIR_HEAD_ORDER = tuple(h for i in range(NSA_HPG) for h in (i, i + NSA_HPG))

C_GU, C_GV, C_GZ = 0, 256, 512
C_Q = 768
C_KC, C_VC, C_KS, C_VS, C_KW, C_VW = 1280, 1408, 1536, 1664, 1792, 1920
C_NZ = 2048
C_MQ, C_MZ = 2560, 2816
C_GATE = 3072
N_COLS = 3200


def _dot(a, b):
    return jnp.dot(a, b, preferred_element_type=jnp.float32)


def _dot_nt(a, b):
    return lax.dot_general(a, b, (((1,), (1,)), ((), ())), preferred_element_type=jnp.float32)


def _gelu(x):
    return 0.5 * x * (1.0 + lax.erf(x * np.float32(np.sqrt(0.5))))


def _silu(x):
    return x * jax.nn.sigmoid(x)


def _lane_iota(shape):
    return lax.broadcasted_iota(jnp.int32, shape, len(shape) - 1)


def _low_half(shape):
    return (_lane_iota(shape) % LANES) < HEAD_DIM


def _tile_lanes(x, reps):
    return jnp.concatenate([x] * reps, axis=-1) if reps > 1 else x


def _memkv_kernel(mem_ref, w_ref, k_ref, v_ref):
    kv = _dot(mem_ref[...].astype(jnp.bfloat16), w_ref[0])
    k_ref[0] = kv[:, :MEM_WIDTH].astype(jnp.bfloat16)
    v_ref[0] = kv[:, MEM_WIDTH:].astype(jnp.bfloat16)


def _memkv(mem2d, w_mem_kv_bf16):
    depth = w_mem_kv_bf16.shape[0]
    rows, d_model = mem2d.shape
    out = jax.ShapeDtypeStruct((depth, rows, MEM_WIDTH), jnp.bfloat16)
    return pl.pallas_call(
        _memkv_kernel,
        grid=(depth,),
        in_specs=[pl.BlockSpec((rows, d_model), lambda l: (0, 0)),
                  pl.BlockSpec((1, d_model, 2 * MEM_WIDTH), lambda l: (l, 0, 0))],
        out_specs=[pl.BlockSpec((1, rows, MEM_WIDTH), lambda l: (l, 0, 0)),
                   pl.BlockSpec((1, rows, MEM_WIDTH), lambda l: (l, 0, 0))],
        out_shape=[out, out],
        name="mem_kv_proj",
    )(mem2d, w_mem_kv_bf16)


def _rope(x, cos, sin_signed, low):
    width = x.shape[-1]
    swapped = jnp.where(low, pltpu.roll(x, width - HEAD_DIM // 2, 1), pltpu.roll(x, HEAD_DIM // 2, 1))
    return x * cos + swapped * sin_signed


def _group_layer_norm(v, g, b, low):
    inv = np.float32(1.0 / HEAD_DIM)
    s_lo = jnp.sum(jnp.where(low, v, 0.0), axis=-1, keepdims=True)
    s_hi = jnp.sum(jnp.where(low, 0.0, v), axis=-1, keepdims=True)
    mu = jnp.where(low, s_lo, s_hi) * inv
    d = v - mu
    d2 = d * d
    q_lo = jnp.sum(jnp.where(low, d2, 0.0), axis=-1, keepdims=True)
    q_hi = jnp.sum(jnp.where(low, 0.0, d2), axis=-1, keepdims=True)
    var = jnp.where(low, q_lo, q_hi) * inv
    return d * lax.rsqrt(var + LN_EPS) * g + b


def _inproj_kernel(x_ref, w_ref, cos_ref, sin_ref, rot_low_ref, gws_ref, gbs_ref, glg_ref, glb_ref,
                   mk_ref, mv_ref,
                   ygm_ref, ymem_ref, q_ref, kc_ref, vc_ref, ksa_ref, vs_ref, kw_ref, vw_ref,
                   nz_ref, gate_ref, stage_ref, *, seq_len):
    subs = [slice(r0, r0 + PROJ_SUB_ROWS) for r0 in range(0, x_ref.shape[0], PROJ_SUB_ROWS)]
    mixers = _project(x_ref, w_ref, subs[0], MIXER_SECTIONS)
    others = _project(x_ref, w_ref, subs[0], OTHER_SECTIONS)
    for j, rs in enumerate(subs):
        upcoming = subs[j + 1] if j + 1 < len(subs) else None
        next_mixers = _project(x_ref, w_ref, upcoming, MIXER_SECTIONS) if upcoming else None
        epilogue = _inproj_rows(rs, pl.program_id(0) * x_ref.shape[0] + rs.start, mixers, others,
                                cos_ref, sin_ref, rot_low_ref, gws_ref, gbs_ref, glg_ref, glb_ref,
                                mk_ref, mv_ref, ygm_ref, ymem_ref, q_ref, kc_ref, vc_ref, ksa_ref,
                                vs_ref, kw_ref, vw_ref, nz_ref, gate_ref, stage_ref, seq_len=seq_len)
        next(epilogue)
        next_others = _project(x_ref, w_ref, upcoming, OTHER_SECTIONS) if upcoming else None
        for _ in epilogue:
            pass
        mixers, others = next_mixers, next_others


MIXER_SECTIONS = ((C_GU, 3 * GM_WIDTH), (C_MQ, 2 * MEM_WIDTH))
OTHER_SECTIONS = ((C_Q, NSA_WIDTH), (C_KC, 6 * NSA_KV_WIDTH), (C_NZ, NSA_WIDTH), (C_GATE, LANES))


def _project(x_ref, w_ref, rs, sections):
    xb = x_ref[rs, :].astype(jnp.bfloat16)
    return tuple(_dot(xb, w_ref[:, c0:c0 + width]) for c0, width in sections)


def _inproj_rows(rs, row0, mixers, others, cos_ref, sin_ref, rot_low_ref, gws_ref, gbs_ref, glg_ref,
                 glb_ref, mk_ref, mv_ref, ygm_ref, ymem_ref, q_ref, kc_ref, vc_ref, ksa_ref, vs_ref,
                 kw_ref, vw_ref, nz_ref, gate_ref, stage_ref, *, seq_len):
    rows = rs.stop - rs.start
    gm, mem = mixers
    qh, kv, nz_raw, gate_raw = others
    low = _low_half((rows, LANES))
    rot_low = rot_low_ref[...] > 0.5
    rot_low = jnp.broadcast_to(rot_low, (rows, LANES))
    cos = cos_ref[rs, :]
    sin = sin_ref[rs, :]

    def slab(h, i):
        return h[:, i * LANES:(i + 1) * LANES]

    qscale = np.float32(HEAD_DIM ** -0.5)

    u = _gelu(gm[:, :GM_WIDTH])
    v = _gelu(gm[:, GM_WIDTH:2 * GM_WIDTH])
    z = gm[:, 2 * GM_WIDTH:]
    spatial = {}
    for pair in range(GM_GROUPS // 2):
        sl = slice(pair * LANES, (pair + 1) * LANES)
        vln = _group_layer_norm(v[:, sl], glg_ref[:, sl], glb_ref[:, sl], low).astype(jnp.bfloat16)
        for c in range(rows // GM_CHUNK):
            cs = slice(c * GM_CHUNK, (c + 1) * GM_CHUNK)
            spatial[pair, c] = (_dot(gws_ref[2 * pair], vln[cs]), _dot(gws_ref[2 * pair + 1], vln[cs]))
    mq = mem[:, :MEM_WIDTH] * qscale
    mz = mem[:, MEM_WIDTH:]
    mem_scores = {}
    for pair in range(MEM_HEADS // 2):
        sl = slice(pair * LANES, (pair + 1) * LANES)
        for keep_low in (True, False):
            qm = jnp.where(low == keep_low, mq[:, sl], 0.0).astype(jnp.bfloat16)
            mem_scores[pair, keep_low] = _dot_nt(qm, mk_ref[0, :, sl])
    yield

    for pair in range(GM_GROUPS // 2):
        sl = slice(pair * LANES, (pair + 1) * LANES)
        for c in range(rows // GM_CHUNK):
            cs = slice(c * GM_CHUNK, (c + 1) * GM_CHUNK)
            out_rows = slice(rs.start + c * GM_CHUNK, rs.start + (c + 1) * GM_CHUNK)
            s_lo, s_hi = spatial[pair, c]
            s = jnp.where(_low_half((GM_CHUNK, LANES)), s_lo, s_hi) + gbs_ref[:, sl]
            ygm_ref[out_rows, sl] = (u[cs, sl] * s * _silu(z[cs, sl])).astype(ygm_ref.dtype)

    qscale2 = np.float32(HEAD_DIM ** -0.5 * np.log2(np.e))
    ones = jnp.ones((rows, LANES), vs_ref.dtype)
    for i in range(NSA_WIDTH // LANES):
        qi = _rope(slab(qh, i), cos, sin, rot_low) * qscale2
        q_ref[rs, i * LANES:(i + 1) * LANES] = qi.astype(q_ref.dtype)
    stage_ref[0, rs, :] = _rope(slab(kv, 0), cos, sin, rot_low)
    stage_ref[1, rs, :] = slab(kv, 1)
    out_rows = slice(rs.start // CMP_STRIDE, rs.stop // CMP_STRIDE)
    for j, dst in enumerate((kc_ref, vc_ref)):
        for l in range(CMP_STRIDE):
            token_l = stage_ref[j, pl.ds(rs.start + l, rows // CMP_STRIDE, stride=CMP_STRIDE), :]
            dst[out_rows, l * LANES:(l + 1) * LANES] = token_l
    ksa_ref[rs, :LANES] = _rope(slab(kv, 2), cos, sin, rot_low).astype(ksa_ref.dtype)
    tok = row0 % seq_len + lax.broadcasted_iota(jnp.int32, (rows, LANES), 0)
    onehot = (tok // SEL_BLOCK) == _lane_iota((rows, LANES))
    ksa_ref[rs, LANES:] = jnp.where(onehot, 1.0, 0.0).astype(ksa_ref.dtype)
    vs_ref[rs, :LANES] = slab(kv, 3).astype(vs_ref.dtype)
    vs_ref[rs, LANES:] = ones
    kw_ref[rs, :] = _rope(slab(kv, 4), cos, sin, rot_low).astype(kw_ref.dtype)
    vw_ref[rs, :LANES] = slab(kv, 5).astype(vw_ref.dtype)
    vw_ref[rs, LANES:] = ones
    nz_ref[rs, :] = _silu(nz_raw)
    gate_ref[rs, :] = jax.nn.sigmoid(gate_raw)

    for pair in range(MEM_HEADS // 2):
        sl = slice(pair * LANES, (pair + 1) * LANES)
        vp = mv_ref[0, :, sl]
        outs = []
        for keep_low in (True, False):
            s = mem_scores[pair, keep_low]
            e = jnp.exp(s - jnp.max(s, axis=-1, keepdims=True))
            p = e / jnp.sum(e, axis=-1, keepdims=True)
            outs.append(_dot(p.astype(jnp.bfloat16), vp))
        o = jnp.where(low, outs[0], outs[1])
        ymem_ref[rs, sl] = (o * _silu(mz[:, sl])).astype(ymem_ref.dtype)


def _inproj(x2d, w_cat, cos_t, sin_t, rot_low, gws, gbs, glg, glb, mk, mv, *, layer, batch, seq_len):
    n, d_model = x2d.shape
    rows = PROJ_ROWS
    steps_per_seq = seq_len // rows
    mem_len = mk.shape[1] // batch

    def tok_spec(width):
        return pl.BlockSpec((rows, width), lambda i: (i, 0))

    def const_spec(shape):
        return pl.BlockSpec(shape, lambda i: (0,) * len(shape))

    def layer_spec(shape):
        return pl.BlockSpec((None,) + shape[1:], lambda i: (layer,) + (0,) * (len(shape) - 1))

    tab_spec = pl.BlockSpec((rows, LANES), lambda i: (i % steps_per_seq, 0))
    mem_spec = pl.BlockSpec((None, 1, mem_len, MEM_WIDTH), lambda i: (layer, i // steps_per_seq, 0, 0))
    bf16, f32 = jnp.bfloat16, jnp.float32
    outs = [(1, GM_WIDTH, bf16), (1, MEM_WIDTH, bf16), (1, NSA_WIDTH, bf16),
            (CMP_STRIDE, CMP_STRIDE * LANES, f32), (CMP_STRIDE, CMP_STRIDE * LANES, f32),
            (1, 2 * LANES, bf16), (1, 2 * LANES, bf16), (1, LANES, bf16), (1, 2 * LANES, bf16),
            (1, NSA_WIDTH, f32), (1, LANES, f32)]
    return pl.pallas_call(
        functools.partial(_inproj_kernel, seq_len=seq_len),
        grid=(n // rows,),
        in_specs=[tok_spec(d_model), layer_spec(w_cat.shape), tab_spec, tab_spec,
                  const_spec(rot_low.shape), const_spec(gws.shape), const_spec(gbs.shape),
                  const_spec(glg.shape), const_spec(glb.shape), mem_spec, mem_spec],
        out_specs=[pl.BlockSpec((rows // d, w), lambda i: (i, 0)) for d, w, _ in outs],
        out_shape=[jax.ShapeDtypeStruct((n // d, w), dt) for d, w, dt in outs],
        scratch_shapes=[pltpu.VMEM((2, rows, LANES), f32)],
        compiler_params=pltpu.CompilerParams(dimension_semantics=("arbitrary",),
                                             vmem_limit_bytes=VMEM_LIMIT_BYTES),
        name="in_proj_mixers",
    )(x2d, w_cat, cos_t, sin_t, rot_low, gws, gbs, glg, glb,
      mk.reshape(-1, batch, mem_len, MEM_WIDTH), mv.reshape(-1, batch, mem_len, MEM_WIDTH))


def _compress_kernel(k_ref, v_ref, pos_ref, w1_ref, w2_ref, kcmp_ref, vcmp_ref):
    n_rows = k_ref.shape[1]
    for idx, (src, dst) in enumerate(((k_ref, kcmp_ref), (v_ref, vcmp_ref))):
        xr = src[0]
        top = _dot((xr + pos_ref[idx, 0:1]).astype(jnp.bfloat16), w1_ref[idx, 0])
        bot = _dot((xr + pos_ref[idx, 1:2]).astype(jnp.bfloat16), w1_ref[idx, 1])
        hidden = top + pltpu.roll(bot, n_rows - 1, 0)
        act = jax.nn.gelu(hidden, approximate=True)
        dst[0, :, :LANES] = _dot(act.astype(jnp.bfloat16), w2_ref[idx]).astype(dst.dtype)
    vcmp_ref[0, :, LANES:] = jnp.ones((n_rows, LANES), vcmp_ref.dtype)


def _compress(kc_rows, vc_rows, pos, w1, w2):
    batch, n_rows, width = kc_rows.shape
    row_spec = pl.BlockSpec((1, n_rows, width), lambda b: (b, 0, 0))
    def out_spec(width):
        return pl.BlockSpec((1, n_rows, width), lambda b: (b, 0, 0))

    def out(width):
        return jax.ShapeDtypeStruct((batch, n_rows, width), jnp.bfloat16)

    return pl.pallas_call(
        _compress_kernel,
        grid=(batch,),
        in_specs=[row_spec, row_spec,
                  pl.BlockSpec(pos.shape, lambda b: (0, 0, 0)),
                  pl.BlockSpec(w1.shape, lambda b: (0, 0, 0, 0)),
                  pl.BlockSpec(w2.shape, lambda b: (0, 0, 0))],
        out_specs=[out_spec(LANES), out_spec(2 * LANES)],
        out_shape=[out(LANES), out(2 * LANES)],
        compiler_params=pltpu.CompilerParams(dimension_semantics=("arbitrary",),
                                             vmem_limit_bytes=VMEM_LIMIT_BYTES),
        name="nsa_compress",
    )(kc_rows, vc_rows, pos, w1, w2)


def _split_bf16(x, parts):
    out = []
    for _ in range(parts):
        hi = x.astype(jnp.bfloat16)
        out.append(hi)
        x = x - hi.astype(jnp.float32)
    return out


def _topk_columns(score):
    row = lax.broadcasted_iota(jnp.int32, score.shape, 0).astype(jnp.float32)
    picked = jnp.zeros(score.shape, jnp.float32)
    for _ in range(SEL_TOPK):
        best = jnp.max(score, axis=0, keepdims=True)
        first = jnp.min(jnp.where(score == best, row, np.float32(score.shape[0])),
                        axis=0, keepdims=True)
        hit = row == first
        picked = jnp.where(hit, 1.0, picked)
        score = jnp.where(hit, REMOVED, score)
    return picked


def _nsa_kernel(q_ref, qnext_ref, nz_ref, gate_ref, ksa_ref, vs_ref, kw_ref, vw_ref, kcmp_ref,
                vcmp_ref, ovl_ref, gexp_ref, out_ref, qa_ref, sc_ref, sw_ref, pc_ref, pw_ref, bc_ref,
                bw_ref, m_ref, acc_ref, oc_ref, ow_ref, sa_ref, sb_ref, mc_ref, mw_ref, qn_ref,
                selb_ref, ocn_ref, gx_ref, *, seq_len):
    bi = pl.program_id(1)
    start = bi * Q_BLOCK
    n_sel = seq_len // SEL_BLOCK
    n_cmp = kcmp_ref.shape[1]
    span = WINDOW + Q_BLOCK
    tk = SEL_KEY_TILE
    bf16 = jnp.bfloat16

    def head_rows(r):
        return slice(r * Q_BLOCK, (r + 1) * Q_BLOCK)

    low = _low_half((Q_BLOCK, LANES))
    chunk = SOFTMAX_ROWS
    n_chunks = Q_BLOCK // chunk

    def stack_queries(src_ref, dst_ref):
        for i in range(NSA_HPG):
            qi = src_ref[0, :, i * LANES:(i + 1) * LANES]
            zero = jnp.zeros_like(qi)
            dst_ref[head_rows(2 * i), :LANES] = jnp.where(low, qi, zero)
            dst_ref[head_rows(2 * i + 1), :LANES] = jnp.where(low, zero, qi)

    def compressed_scores(qs_ref, blk_start):
        t_b = blk_start + lax.broadcasted_iota(jnp.int32, (Q_BLOCK, 1), 0)
        c_end = lax.broadcasted_iota(jnp.int32, (1, n_cmp), 1) * CMP_STRIDE + (CMP_BLOCK - 1)
        bc_ref[...] = jnp.where(c_end <= t_b, 0.0, NEG_INF)
        sc_ref[...] = _dot_nt(qs_ref[:, :LANES], kcmp_ref[0])

    def masked_exp(s_ref, b_ref, m_ref_, p_ref):
        width_tiles = s_ref.shape[1] // LANES
        for r in range(NSA_HEADS):
            for c in range(n_chunks):
                crow = slice(c * chunk, (c + 1) * chunk)
                rows = slice(r * Q_BLOCK + c * chunk, r * Q_BLOCK + (c + 1) * chunk)
                row_max = jnp.max(s_ref[rows, :] + b_ref[crow, :], axis=-1, keepdims=True)
                m_ref_[rows, :] = jnp.broadcast_to(row_max, (chunk, LANES))
        for r in range(NSA_HEADS):
            for c in range(n_chunks):
                crow = slice(c * chunk, (c + 1) * chunk)
                rows = slice(r * Q_BLOCK + c * chunk, r * Q_BLOCK + (c + 1) * chunk)
                s = s_ref[rows, :] + b_ref[crow, :]
                p_ref[rows, :] = jnp.exp2(s - _tile_lanes(m_ref_[rows, :], width_tiles)).astype(bf16)

    def compressed_out(blk_start):
        t_col = blk_start + lax.broadcasted_iota(jnp.int32, (Q_BLOCK, 1), 0)
        seen_col = jnp.concatenate([t_col >= CMP_BLOCK - 1] * NSA_HEADS, axis=0)
        o_c = _dot(pc_ref[...], vcmp_ref[0])
        ocn_ref[...] = jnp.where(seen_col, o_c[:, :LANES] / o_c[:, LANES:], 0.0)

    def candidate_scores(blk_start):
        parts = _dot_nt(ovl_ref[...], pc_ref[...])
        t_lane = blk_start + _lane_iota((1, parts.shape[1])) % Q_BLOCK
        inv = jnp.where(t_lane >= CMP_BLOCK - 1, 1.0 / parts[LANES:LANES + 1, :], 0.0)
        weighted = parts[:LANES, :] * inv
        imp = jnp.concatenate(
            [sum(weighted[:, (2 * i + g) * Q_BLOCK:(2 * i + g + 1) * Q_BLOCK] for i in range(NSA_HPG))
             for g in range(NSA_KV_GROUPS)], axis=1)
        blk = lax.broadcasted_iota(jnp.int32, imp.shape, 0)
        t_blk = (blk_start + _lane_iota((1, imp.shape[1])) % Q_BLOCK) // SEL_BLOCK
        valid = blk <= t_blk
        forced = (blk == 0) | (valid & (blk > t_blk - N_LOCAL_SEL))
        score = jnp.where(forced, FORCE_SCORE, jnp.where(valid, imp, -1.0))
        if n_sel < LANES:
            score = jnp.where(blk < n_sel, score, REMOVED)
        return score

    def select_blocks(score):
        picked = _topk_columns(score).astype(bf16)
        eye = (lax.broadcasted_iota(jnp.int32, (Q_BLOCK, Q_BLOCK), 0)
               == lax.broadcasted_iota(jnp.int32, (Q_BLOCK, Q_BLOCK), 1)).astype(bf16)
        for g in range(NSA_KV_GROUPS):
            picked_q = _dot_nt(eye, picked[:, g * Q_BLOCK:(g + 1) * Q_BLOCK])
            selb_ref[g * Q_BLOCK:(g + 1) * Q_BLOCK, :] = ((1.0 - picked_q) * NEG_INF).astype(bf16)

    t_q = start + lax.broadcasted_iota(jnp.int32, (Q_BLOCK, 1), 0)
    w0 = pl.multiple_of(jnp.maximum(start - WINDOW, 0), Q_BLOCK)

    def window_scores():
        kpos = w0 + lax.broadcasted_iota(jnp.int32, (1, span), 1)
        bw_ref[...] = jnp.where((kpos <= t_q) & (kpos > t_q - WINDOW), 0.0, NEG_INF)
        sw_ref[...] = _dot_nt(qa_ref[:, :LANES], kw_ref[0, pl.ds(w0, span), :])

    def window_out():
        o_w = _dot(pw_ref[...], vw_ref[0, pl.ds(w0, span), :])
        ow_ref[...] = o_w[:, :LANES] / o_w[:, LANES:]

    def scores(tile):
        k0 = pl.multiple_of(tile * tk, tk)
        return _dot_nt(qa_ref[...], ksa_ref[0, pl.ds(k0, tk), :])

    @pl.when(bi == 0)
    def _():
        stack_queries(q_ref, qn_ref)
        compressed_scores(qn_ref, start)
        masked_exp(sc_ref, bc_ref, mc_ref, pc_ref)
        compressed_out(start)
        select_blocks(candidate_scores(start))

    stack_queries(q_ref, qa_ref)
    for r in range(NSA_HEADS):
        g = r % NSA_KV_GROUPS
        qa_ref[head_rows(r), LANES:] = selb_ref[g * Q_BLOCK:(g + 1) * Q_BLOCK, :]
    oc_ref[...] = ocn_ref[...]
    stack_queries(qnext_ref, qn_ref)

    nxt = start + Q_BLOCK
    compressed_scores(qn_ref, nxt)
    window_scores()
    masked_exp(sc_ref, bc_ref, mc_ref, pc_ref)
    next_score = candidate_scores(nxt)
    sa_ref[...] = scores(0)
    masked_exp(sw_ref, bw_ref, mw_ref, pw_ref)
    compressed_out(nxt)
    window_out()
    g_hi, g_lo = _split_bf16(gate_ref[0], 2)
    gx_ref[...] = _dot(g_hi, gexp_ref[...]) + _dot(g_lo, gexp_ref[...])
    select_blocks(next_score)

    m_ref[...] = jnp.full(m_ref.shape, NEG_INF, jnp.float32)
    acc_ref[...] = jnp.zeros(acc_ref.shape, jnp.float32)

    def consume(buf_ref, tile, causal):
        k0 = pl.multiple_of(tile * tk, tk)
        s = buf_ref[...]
        if causal:
            kpos = k0 + lax.broadcasted_iota(jnp.int32, (1, tk), 1)
            tile_bias = jnp.where(kpos <= t_q, 0.0, NEG_INF)
            s = s + jnp.concatenate([tile_bias] * NSA_HEADS, axis=0)
        m_prev = m_ref[...]
        m_next = jnp.maximum(m_prev, jnp.max(s, axis=-1, keepdims=True))
        p = jnp.exp2(s - _tile_lanes(m_next, tk // LANES))
        alpha = jnp.exp2(m_prev - m_next)
        acc_ref[...] = (_tile_lanes(alpha, 2) * acc_ref[...]
                        + _dot(p.astype(bf16), vs_ref[0, pl.ds(k0, tk), :]))
        m_ref[...] = m_next

    diag = start // tk

    def tile_pair(first):
        sb_ref[...] = scores(first + 1)
        consume(sa_ref, first, False)
        sa_ref[...] = scores(first + 2)
        consume(sb_ref, first + 1, False)

    def tile_quad(j, carry):
        tile_pair(4 * j)
        tile_pair(4 * j + 2)
        return carry

    lax.fori_loop(0, diag // 4, tile_quad, 0)

    @pl.when(diag % 4 >= 2)
    def _():
        tile_pair((diag // 4) * 4)

    @pl.when(diag % 2 == 1)
    def _():
        sb_ref[...] = scores(diag)
        consume(sa_ref, diag - 1, False)
        consume(sb_ref, diag, True)

    @pl.when(diag % 2 == 0)
    def _():
        consume(sa_ref, diag, True)

    for i in range(NSA_HPG):
        lo_rows, hi_rows = head_rows(2 * i), head_rows(2 * i + 1)
        o_s = jnp.where(low, acc_ref[lo_rows, :LANES] / acc_ref[lo_rows, LANES:],
                        acc_ref[hi_rows, :LANES] / acc_ref[hi_rows, LANES:])
        branches = (jnp.where(low, oc_ref[lo_rows], oc_ref[hi_rows]), o_s,
                    jnp.where(low, ow_ref[lo_rows], ow_ref[hi_rows]))
        mixed = sum(o * gx_ref[:, (3 * i + c) * LANES:(3 * i + c + 1) * LANES]
                    for c, o in enumerate(branches))
        sl = slice(i * LANES, (i + 1) * LANES)
        out_ref[0, :, sl] = (mixed * nz_ref[0, :, sl]).astype(out_ref.dtype)


def _nsa(q, nz, gates, ksa, vs, kw, vw, kcmp, vcmp, ovl, gexp):
    batch, seq_len, _ = q.shape

    def q_spec(width):
        return pl.BlockSpec((1, Q_BLOCK, width), lambda b, i: (b, i, 0))

    def seq_spec(arr):
        return pl.BlockSpec((1,) + arr.shape[1:], lambda b, i: (b, 0, 0))

    rows = NSA_HEADS * Q_BLOCK
    n_cmp = kcmp.shape[1]
    span = WINDOW + Q_BLOCK
    last = seq_len // Q_BLOCK - 1
    next_q_spec = pl.BlockSpec((1, Q_BLOCK, NSA_WIDTH), lambda b, i: (b, jnp.minimum(i + 1, last), 0))
    return pl.pallas_call(
        functools.partial(_nsa_kernel, seq_len=seq_len),
        grid=(batch, seq_len // Q_BLOCK),
        in_specs=[q_spec(NSA_WIDTH), next_q_spec, q_spec(NSA_WIDTH), q_spec(LANES),
                  seq_spec(ksa), seq_spec(vs), seq_spec(kw), seq_spec(vw),
                  seq_spec(kcmp), seq_spec(vcmp),
                  pl.BlockSpec(ovl.shape, lambda b, i: (0, 0)),
                  pl.BlockSpec(gexp.shape, lambda b, i: (0, 0))],
        out_specs=q_spec(NSA_WIDTH),
        out_shape=jax.ShapeDtypeStruct((batch, seq_len, NSA_WIDTH), jnp.bfloat16),
        scratch_shapes=[pltpu.VMEM((rows, 2 * LANES), jnp.bfloat16),
                        pltpu.VMEM((rows, n_cmp), jnp.float32),
                        pltpu.VMEM((rows, span), jnp.float32),
                        pltpu.VMEM((rows, n_cmp), jnp.bfloat16),
                        pltpu.VMEM((rows, span), jnp.bfloat16),
                        pltpu.VMEM((Q_BLOCK, n_cmp), jnp.float32),
                        pltpu.VMEM((Q_BLOCK, span), jnp.float32),
                        pltpu.VMEM((rows, LANES), jnp.float32),
                        pltpu.VMEM((rows, 2 * LANES), jnp.float32),
                        pltpu.VMEM((rows, LANES), jnp.float32),
                        pltpu.VMEM((rows, LANES), jnp.float32),
                        pltpu.VMEM((rows, SEL_KEY_TILE), jnp.float32),
                        pltpu.VMEM((rows, SEL_KEY_TILE), jnp.float32),
                        pltpu.VMEM((rows, LANES), jnp.float32),
                        pltpu.VMEM((rows, LANES), jnp.float32),
                        pltpu.VMEM((rows, LANES), jnp.bfloat16),
                        pltpu.VMEM((NSA_KV_GROUPS * Q_BLOCK, LANES), jnp.bfloat16),
                        pltpu.VMEM((rows, LANES), jnp.float32),
                        pltpu.VMEM((Q_BLOCK, gexp.shape[1]), jnp.float32)],
        compiler_params=pltpu.CompilerParams(dimension_semantics=("arbitrary", "arbitrary"),
                                             vmem_limit_bytes=VMEM_LIMIT_BYTES),
        name="nsa_attention",
    )(q, q, nz, gates, ksa, vs, kw, vw, kcmp, vcmp, ovl, gexp)


def _outproj_kernel(x_ref, ygm_ref, ynsa_ref, ymem_ref, w_ref, g_ref, b_ref, o_ref, *, alpha):
    def mix_proj(rs):
        return (_dot(ygm_ref[rs, :], w_ref[:GM_WIDTH])
                + _dot(ynsa_ref[rs, :], w_ref[GM_WIDTH:GM_WIDTH + NSA_WIDTH])
                + _dot(ymem_ref[rs, :], w_ref[GM_WIDTH + NSA_WIDTH:]))

    subs = [slice(r0, r0 + PROJ_SUB_ROWS) for r0 in range(0, x_ref.shape[0], PROJ_SUB_ROWS)]
    y_next = mix_proj(subs[0])
    for j, rs in enumerate(subs):
        y = y_next
        y_next = mix_proj(subs[j + 1]) if j + 1 < len(subs) else None
        r = alpha * x_ref[rs, :] + y
        mu = jnp.mean(r, axis=-1, keepdims=True)
        d = r - mu
        var = jnp.mean(d * d, axis=-1, keepdims=True)
        o_ref[rs, :] = d * lax.rsqrt(var + LN_EPS) * g_ref[...] + b_ref[...]


def _outproj(x2d, ygm, ynsa, ymem, w_out, ln_g, ln_b, *, layer, alpha):
    n, d_model = x2d.shape
    rows = OUT_PROJ_ROWS

    def tok_spec(width):
        return pl.BlockSpec((rows, width), lambda i: (i, 0))

    def layer_spec(shape):
        return pl.BlockSpec((None,) + shape[1:], lambda i: (layer, 0, 0))

    return pl.pallas_call(
        functools.partial(_outproj_kernel, alpha=alpha),
        grid=(n // rows,),
        in_specs=[tok_spec(d_model), tok_spec(GM_WIDTH), tok_spec(NSA_WIDTH), tok_spec(MEM_WIDTH),
                  layer_spec(w_out.shape), layer_spec(ln_g.shape), layer_spec(ln_b.shape)],
        out_specs=tok_spec(d_model),
        out_shape=jax.ShapeDtypeStruct((n, d_model), jnp.float32),
        compiler_params=pltpu.CompilerParams(dimension_semantics=("arbitrary",),
                                             vmem_limit_bytes=VMEM_LIMIT_BYTES),
        name="out_proj_layernorm",
    )(x2d, ygm, ynsa, ymem, w_out, ln_g, ln_b)


def _pair_head_slices(w, start, axis):
    return [lax.slice_in_dim(w, start + h * HEAD_DIM, start + (h + 1) * HEAD_DIM, axis=axis)
            for h in PAIR_HEAD_ORDER]


def _w_in_pieces():
    o_gate = 2048
    o_nz = o_gate + GATE_COLS
    o_mq = o_nz + NSA_WIDTH
    end = o_mq + 2 * MEM_WIDTH
    heads = lambda start: [(start + h * HEAD_DIM, HEAD_DIM) for h in PAIR_HEAD_ORDER]
    return ([(0, 768)] + heads(768) + [(1280, 768)] + heads(o_nz) + [(o_mq, end - o_mq), (o_gate, GATE_COLS)])


def _permute_w_in_kernel(w_ref, o_ref):
    dst = 0
    for src, width in _w_in_pieces():
        o_ref[:, dst:dst + width] = w_ref[:, src:src + width].astype(o_ref.dtype)
        dst += width
    o_ref[:, dst:] = jnp.zeros((o_ref.shape[0], o_ref.shape[1] - dst), o_ref.dtype)


def _permute_w_in(w):
    depth, d_model, in_cols = w.shape
    rows = PROJ_SUB_ROWS
    return pl.pallas_call(
        _permute_w_in_kernel,
        grid=(depth, d_model // rows),
        in_specs=[pl.BlockSpec((None, rows, in_cols), lambda l, i: (l, i, 0))],
        out_specs=pl.BlockSpec((None, rows, N_COLS), lambda l, i: (l, i, 0)),
        out_shape=jax.ShapeDtypeStruct((depth, d_model, N_COLS), jnp.bfloat16),
        name="w_in_relayout",
    )(w)


def _permute_w_out(w):
    pieces = ([w[:, :GM_WIDTH]] + _pair_head_slices(w, GM_WIDTH, 1) + [w[:, GM_WIDTH + NSA_WIDTH:]])
    return jnp.concatenate([p.astype(jnp.bfloat16) for p in pieces], axis=1)


def _rope_tables(seq_len):
    half = HEAD_DIM // 2
    inv_freq = ROPE_THETA ** (-jnp.arange(half, dtype=jnp.float32) * 2.0 / HEAD_DIM)
    ang = jnp.arange(seq_len).astype(jnp.float32)[:, None] * inv_freq[None, :]
    cos, sin = jnp.cos(ang), jnp.sin(ang)
    reps = LANES // HEAD_DIM
    cos_t = jnp.tile(jnp.concatenate([cos, cos], axis=1), (1, reps))
    sin_t = jnp.tile(jnp.concatenate([-sin, sin], axis=1), (1, reps))
    rot_low = ((np.arange(LANES) % HEAD_DIM) < half).astype(np.float32)[None, :]
    return cos_t, sin_t, jnp.asarray(rot_low)


def _compress_weights(pos_k, w1_k, w2_k, pos_v, w1_v, w2_v):
    half = CMP_BLOCK // 2

    def block_diag2(w):
        z = jnp.zeros_like(w)
        return jnp.concatenate([jnp.concatenate([w, z], axis=-1),
                                jnp.concatenate([z, w], axis=-1)], axis=-2)

    def expand_w1(w1):
        w = w1.reshape(2, half, HEAD_DIM, CMP_HIDDEN)
        w = block_diag2(w)
        return w.reshape(2, half * NSA_KV_WIDTH, NSA_KV_GROUPS * CMP_HIDDEN)

    def expand_w2(w2):
        return block_diag2(w2)

    def expand_pos(pos):
        p = pos.reshape(2, half, 1, HEAD_DIM)
        p = jnp.broadcast_to(p, (2, half, NSA_KV_GROUPS, HEAD_DIM))
        return p.reshape(2, half * NSA_KV_WIDTH)

    pos = jnp.stack([expand_pos(pos_k), expand_pos(pos_v)])
    w1 = jnp.stack([expand_w1(w1_k), expand_w1(w1_v)]).astype(jnp.bfloat16)
    w2 = jnp.stack([expand_w2(w2_k), expand_w2(w2_v)]).astype(jnp.bfloat16)
    return pos, w1, w2


def _gate_expansion():
    out = np.zeros((LANES, NSA_HPG * 3 * LANES), np.float32)
    for i in range(NSA_HPG):
        for c in range(3):
            base = (3 * i + c) * LANES
            out[3 * i + c, base:base + HEAD_DIM] = 1.0
            out[3 * (i + NSA_HPG) + c, base + HEAD_DIM:base + LANES] = 1.0
    return jnp.asarray(out, dtype=jnp.bfloat16)


def _overlap_matrix(n_rows, n_sel):
    c_start = np.arange(n_rows) * CMP_STRIDE
    s_start = np.arange(LANES) * SEL_BLOCK
    ovl = ((c_start[:, None] < s_start[None, :] + SEL_BLOCK)
           & (c_start[:, None] + CMP_BLOCK > s_start[None, :])
           & (np.arange(LANES)[None, :] < n_sel))
    out = np.zeros((LANES + 16, n_rows), np.float32)
    out[:LANES] = ovl.T
    out[LANES] = 1.0
    return jnp.asarray(out, dtype=jnp.bfloat16)


def kernel(x, mem, w_in, gm_ln_g, gm_ln_b, gm_ws, gm_bs, cmp_pos_k, cmp_k_w1, cmp_k_w2,
           cmp_pos_v, cmp_v_w1, cmp_v_w2, w_mem_kv, w_out, ln_g, ln_b):
    batch, seq_len, d_model = x.shape
    depth = w_in.shape[0]
    assert seq_len % SEL_KEY_TILE == 0 and seq_len >= WINDOW + Q_BLOCK
    assert SEL_TOPK <= seq_len // SEL_BLOCK <= LANES
    alpha = (2.0 * depth) ** 0.25
    n_tok = batch * seq_len
    n_rows = seq_len // CMP_STRIDE

    cos_t, sin_t, rot_low = _rope_tables(seq_len)
    ovl = _overlap_matrix(n_rows, seq_len // SEL_BLOCK)
    gexp = _gate_expansion()
    tril = jnp.tril(jnp.ones((GM_CHUNK, GM_CHUNK), gm_ws.dtype))
    mk_all, mv_all = _memkv(mem.reshape(batch * mem.shape[1], d_model), w_mem_kv.astype(jnp.bfloat16))
    w_cat_all = _permute_w_in(w_in)
    w_out_all = _permute_w_out(w_out)

    h = x.reshape(n_tok, d_model)
    for l in range(depth):
        gws = (gm_ws[l] * tril[None]).astype(jnp.bfloat16)
        gbs = jnp.repeat(gm_bs[l].T, HEAD_DIM, axis=1)
        glg = gm_ln_g[l].reshape(1, GM_WIDTH)
        glb = gm_ln_b[l].reshape(1, GM_WIDTH)
        (ygm, ymem, q, kc, vc, ksa, vs, kw, vw, nz, gates) = _inproj(
            h, w_cat_all, cos_t, sin_t, rot_low, gws, gbs, glg, glb, mk_all, mv_all,
            layer=l, batch=batch, seq_len=seq_len)

        pos, w1, w2 = _compress_weights(cmp_pos_k[l], cmp_k_w1[l], cmp_k_w2[l],
                                        cmp_pos_v[l], cmp_v_w1[l], cmp_v_w2[l])
        row_shape = (batch, n_rows, CMP_STRIDE * NSA_KV_WIDTH)
        kcmp, vcmp = _compress(kc.reshape(row_shape), vc.reshape(row_shape), pos, w1, w2)

        def per_seq(a):
            return a.reshape(batch, seq_len, a.shape[-1])

        ynsa = _nsa(per_seq(q), per_seq(nz), per_seq(gates), per_seq(ksa), per_seq(vs),
                    per_seq(kw), per_seq(vw), kcmp, vcmp, ovl, gexp)

        h = _outproj(h, ygm, ynsa.reshape(n_tok, NSA_WIDTH), ymem, w_out_all,
                     ln_g.reshape(depth, 1, d_model), ln_b.reshape(depth, 1, d_model),
                     layer=l, alpha=alpha)
    return h.reshape(batch, seq_len, d_model)
```

```python
import functools

import numpy as np
import jax
import jax.numpy as jnp
from jax import lax
from jax.experimental import pallas as pl
from jax.experimental.pallas import tpu as pltpu

HEAD_DIM = 64
GM_GROUPS = 4
GM_WIDTH = GM_GROUPS * HEAD_DIM
GM_CHUNK = 128
NSA_HEADS = 8
NSA_KV_GROUPS = 2
NSA_HPG = NSA_HEADS // NSA_KV_GROUPS
NSA_WIDTH = NSA_HEADS * HEAD_DIM
NSA_KV_WIDTH = NSA_KV_GROUPS * HEAD_DIM
CMP_BLOCK = 32
CMP_STRIDE = 16
CMP_HIDDEN = 128
SEL_BLOCK = 64
SEL_TOPK = 16
N_LOCAL_SEL = 2
WINDOW = 512
Q_BLOCK = 128
MEM_HEADS = 4
MEM_WIDTH = MEM_HEADS * HEAD_DIM
ROPE_THETA = 10000.0
LN_EPS = 1e-5
NEG_INF = -1e30
FORCE_SCORE = 1e4
GATE_COLS = NSA_HEADS * 3

LANES = 128
VMEM_LIMIT_BYTES = 56 * 1024 * 1024

PROJ_ROWS = 1024
OUT_PROJ_ROWS = 2048
PROJ_SUB_ROWS = 256
SEL_KEY_TILE = 512
SOFTMAX_ROWS = 32
REMOVED = -3.0e38

PAIR_HEAD_ORDER = tuple(h for i in range(NSA_HPG) for h in (i, i + NSA_HPG))

C_GU, C_GV, C_GZ = 0, 256, 512
C_Q = 768
C_KC, C_VC, C_KS, C_VS, C_KW, C_VW = 1280, 1408, 1536, 1664, 1792, 1920
C_NZ = 2048
C_MQ, C_MZ = 2560, 2816
C_GATE = 3072
N_COLS = 3200


def _dot(a, b):
    return jnp.dot(a, b, preferred_element_type=jnp.float32)


def _dot_nt(a, b):
    return lax.dot_general(a, b, (((1,), (1,)), ((), ())), preferred_element_type=jnp.float32)


def _gelu(x):
    return 0.5 * x * (1.0 + lax.erf(x * np.float32(np.sqrt(0.5))))


def _silu(x):
    return x * jax.nn.sigmoid(x)


def _lane_iota(shape):
    return lax.broadcasted_iota(jnp.int32, shape, len(shape) - 1)


def _low_half(shape):
    return (_lane_iota(shape) % LANES) < HEAD_DIM


def _tile_lanes(x, reps):
    return jnp.concatenate([x] * reps, axis=-1) if reps > 1 else x


def _memkv_kernel(mem_ref, w_ref, k_ref, v_ref):
    kv = _dot(mem_ref[...].astype(jnp.bfloat16), w_ref[0])
    k_ref[0] = kv[:, :MEM_WIDTH].astype(jnp.bfloat16)
    v_ref[0] = kv[:, MEM_WIDTH:].astype(jnp.bfloat16)


def _memkv(mem2d, w_mem_kv_bf16):
    depth = w_mem_kv_bf16.shape[0]
    rows, d_model = mem2d.shape
    out = jax.ShapeDtypeStruct((depth, rows, MEM_WIDTH), jnp.bfloat16)
    return pl.pallas_call(
        _memkv_kernel,
        grid=(depth,),
        in_specs=[pl.BlockSpec((rows, d_model), lambda l: (0, 0)),
                  pl.BlockSpec((1, d_model, 2 * MEM_WIDTH), lambda l: (l, 0, 0))],
        out_specs=[pl.BlockSpec((1, rows, MEM_WIDTH), lambda l: (l, 0, 0)),
                   pl.BlockSpec((1, rows, MEM_WIDTH), lambda l: (l, 0, 0))],
        out_shape=[out, out],
        name="mem_kv_proj",
    )(mem2d, w_mem_kv_bf16)


def _rope(x, cos, sin_signed, low):
    width = x.shape[-1]
    swapped = jnp.where(low, pltpu.roll(x, width - HEAD_DIM // 2, 1), pltpu.roll(x, HEAD_DIM // 2, 1))
    return x * cos + swapped * sin_signed


def _group_layer_norm(v, g, b, low):
    inv = np.float32(1.0 / HEAD_DIM)
    s_lo = jnp.sum(jnp.where(low, v, 0.0), axis=-1, keepdims=True)
    s_hi = jnp.sum(jnp.where(low, 0.0, v), axis=-1, keepdims=True)
    mu = jnp.where(low, s_lo, s_hi) * inv
    d = v - mu
    d2 = d * d
    q_lo = jnp.sum(jnp.where(low, d2, 0.0), axis=-1, keepdims=True)
    q_hi = jnp.sum(jnp.where(low, 0.0, d2), axis=-1, keepdims=True)
    var = jnp.where(low, q_lo, q_hi) * inv
    return d * lax.rsqrt(var + LN_EPS) * g + b


def _inproj_kernel(x_ref, w_ref, cos_ref, sin_ref, rot_low_ref, gws_ref, gbs_ref, glg_ref, glb_ref,
                   mk_ref, mv_ref,
                   ygm_ref, ymem_ref, q_ref, kc_ref, vc_ref, ksa_ref, vs_ref, kw_ref, vw_ref,
                   nz_ref, gate_ref, stage_ref, *, seq_len):
    subs = [slice(r0, r0 + PROJ_SUB_ROWS) for r0 in range(0, x_ref.shape[0], PROJ_SUB_ROWS)]
    mixers = _project(x_ref, w_ref, subs[0], MIXER_SECTIONS)
    others = _project(x_ref, w_ref, subs[0], OTHER_SECTIONS)
    for j, rs in enumerate(subs):
        upcoming = subs[j + 1] if j + 1 < len(subs) else None
        next_mixers = _project(x_ref, w_ref, upcoming, MIXER_SECTIONS) if upcoming else None
        epilogue = _inproj_rows(rs, pl.program_id(0) * x_ref.shape[0] + rs.start, mixers, others,
                                cos_ref, sin_ref, rot_low_ref, gws_ref, gbs_ref, glg_ref, glb_ref,
                                mk_ref, mv_ref, ygm_ref, ymem_ref, q_ref, kc_ref, vc_ref, ksa_ref,
                                vs_ref, kw_ref, vw_ref, nz_ref, gate_ref, stage_ref, seq_len=seq_len)
        next(epilogue)
        next_others = _project(x_ref, w_ref, upcoming, OTHER_SECTIONS) if upcoming else None
        for _ in epilogue:
            pass
        mixers, others = next_mixers, next_others


MIXER_SECTIONS = ((C_GU, 3 * GM_WIDTH), (C_MQ, 2 * MEM_WIDTH))
OTHER_SECTIONS = ((C_Q, NSA_WIDTH), (C_KC, 6 * NSA_KV_WIDTH), (C_NZ, NSA_WIDTH), (C_GATE, LANES))


def _project(x_ref, w_ref, rs, sections):
    xb = x_ref[rs, :].astype(jnp.bfloat16)
    return tuple(_dot(xb, w_ref[:, c0:c0 + width]) for c0, width in sections)


def _inproj_rows(rs, row0, mixers, others, cos_ref, sin_ref, rot_low_ref, gws_ref, gbs_ref, glg_ref,
                 glb_ref, mk_ref, mv_ref, ygm_ref, ymem_ref, q_ref, kc_ref, vc_ref, ksa_ref, vs_ref,
                 kw_ref, vw_ref, nz_ref, gate_ref, stage_ref, *, seq_len):
    rows = rs.stop - rs.start
    gm, mem = mixers
    qh, kv, nz_raw, gate_raw = others
    low = _low_half((rows, LANES))
    rot_low = rot_low_ref[...] > 0.5
    rot_low = jnp.broadcast_to(rot_low, (rows, LANES))
    cos = cos_ref[rs, :]
    sin = sin_ref[rs, :]

    def slab(h, i):
        return h[:, i * LANES:(i + 1) * LANES]

    qscale = np.float32(HEAD_DIM ** -0.5)

    u = _gelu(gm[:, :GM_WIDTH])
    v = _gelu(gm[:, GM_WIDTH:2 * GM_WIDTH])
    z = gm[:, 2 * GM_WIDTH:]
    spatial = {}
    for pair in range(GM_GROUPS // 2):
        sl = slice(pair * LANES, (pair + 1) * LANES)
        vln = _group_layer_norm(v[:, sl], glg_ref[:, sl], glb_ref[:, sl], low).astype(jnp.bfloat16)
        for c in range(rows // GM_CHUNK):
            cs = slice(c * GM_CHUNK, (c + 1) * GM_CHUNK)
            spatial[pair, c] = (_dot(gws_ref[2 * pair], vln[cs]), _dot(gws_ref[2 * pair + 1], vln[cs]))
    mq = mem[:, :MEM_WIDTH] * qscale
    mz = mem[:, MEM_WIDTH:]
    mem_scores = {}
    for pair in range(MEM_HEADS // 2):
        sl = slice(pair * LANES, (pair + 1) * LANES)
        for keep_low in (True, False):
            qm = jnp.where(low == keep_low, mq[:, sl], 0.0).astype(jnp.bfloat16)
            mem_scores[pair, keep_low] = _dot_nt(qm, mk_ref[0, :, sl])
    yield

    for pair in range(GM_GROUPS // 2):
        sl = slice(pair * LANES, (pair + 1) * LANES)
        for c in range(rows // GM_CHUNK):
            cs = slice(c * GM_CHUNK, (c + 1) * GM_CHUNK)
            out_rows = slice(rs.start + c * GM_CHUNK, rs.start + (c + 1) * GM_CHUNK)
            s_lo, s_hi = spatial[pair, c]
            s = jnp.where(_low_half((GM_CHUNK, LANES)), s_lo, s_hi) + gbs_ref[:, sl]
            ygm_ref[out_rows, sl] = (u[cs, sl] * s * _silu(z[cs, sl])).astype(ygm_ref.dtype)

    qscale2 = np.float32(HEAD_DIM ** -0.5 * np.log2(np.e))
    ones = jnp.ones((rows, LANES), vs_ref.dtype)
    for i in range(NSA_WIDTH // LANES):
        qi = _rope(slab(qh, i), cos, sin, rot_low) * qscale2
        q_ref[rs, i * LANES:(i + 1) * LANES] = qi.astype(q_ref.dtype)
    stage_ref[0, rs, :] = _rope(slab(kv, 0), cos, sin, rot_low)
    stage_ref[1, rs, :] = slab(kv, 1)
    out_rows = slice(rs.start // CMP_STRIDE, rs.stop // CMP_STRIDE)
    for j, dst in enumerate((kc_ref, vc_ref)):
        for l in range(CMP_STRIDE):
            token_l = stage_ref[j, pl.ds(rs.start + l, rows // CMP_STRIDE, stride=CMP_STRIDE), :]
            dst[out_rows, l * LANES:(l + 1) * LANES] = token_l
    ksa_ref[rs, :LANES] = _rope(slab(kv, 2), cos, sin, rot_low).astype(ksa_ref.dtype)
    tok = row0 % seq_len + lax.broadcasted_iota(jnp.int32, (rows, LANES), 0)
    onehot = (tok // SEL_BLOCK) == _lane_iota((rows, LANES))
    ksa_ref[rs, LANES:] = jnp.where(onehot, 1.0, 0.0).astype(ksa_ref.dtype)
    vs_ref[rs, :LANES] = slab(kv, 3).astype(vs_ref.dtype)
    vs_ref[rs, LANES:] = ones
    kw_ref[rs, :] = _rope(slab(kv, 4), cos, sin, rot_low).astype(kw_ref.dtype)
    vw_ref[rs, :LANES] = slab(kv, 5).astype(vw_ref.dtype)
    vw_ref[rs, LANES:] = ones
    nz_ref[rs, :] = _silu(nz_raw).astype(nz_ref.dtype)
    gate_ref[rs, :] = jax.nn.sigmoid(gate_raw)

    for pair in range(MEM_HEADS // 2):
        sl = slice(pair * LANES, (pair + 1) * LANES)
        vp = mv_ref[0, :, sl]
        outs = []
        for keep_low in (True, False):
            s = mem_scores[pair, keep_low]
            e = jnp.exp(s - jnp.max(s, axis=-1, keepdims=True))
            p = e / jnp.sum(e, axis=-1, keepdims=True)
            outs.append(_dot(p.astype(jnp.bfloat16), vp))
        o = jnp.where(low, outs[0], outs[1])
        ymem_ref[rs, sl] = (o * _silu(mz[:, sl])).astype(ymem_ref.dtype)


def _inproj(x2d, w_cat, cos_t, sin_t, rot_low, gws, gbs, glg, glb, mk, mv, *, layer, batch, seq_len):
    n, d_model = x2d.shape
    rows = PROJ_ROWS
    steps_per_seq = seq_len // rows
    mem_len = mk.shape[1] // batch

    def tok_spec(width):
        return pl.BlockSpec((rows, width), lambda i: (i, 0))

    def const_spec(shape):
        return pl.BlockSpec(shape, lambda i: (0,) * len(shape))

    def layer_spec(shape):
        return pl.BlockSpec((None,) + shape[1:], lambda i: (layer,) + (0,) * (len(shape) - 1))

    tab_spec = pl.BlockSpec((rows, LANES), lambda i: (i % steps_per_seq, 0))
    mem_spec = pl.BlockSpec((None, 1, mem_len, MEM_WIDTH), lambda i: (layer, i // steps_per_seq, 0, 0))
    bf16, f32 = jnp.bfloat16, jnp.float32
    outs = [(1, GM_WIDTH, bf16), (1, MEM_WIDTH, bf16), (1, NSA_WIDTH, bf16),
            (CMP_STRIDE, CMP_STRIDE * LANES, f32), (CMP_STRIDE, CMP_STRIDE * LANES, f32),
            (1, 2 * LANES, bf16), (1, 2 * LANES, bf16), (1, LANES, bf16), (1, 2 * LANES, bf16),
            (1, NSA_WIDTH, bf16), (1, LANES, f32)]
    return pl.pallas_call(
        functools.partial(_inproj_kernel, seq_len=seq_len),
        grid=(n // rows,),
        in_specs=[tok_spec(d_model), layer_spec(w_cat.shape), tab_spec, tab_spec,
                  const_spec(rot_low.shape), const_spec(gws.shape), const_spec(gbs.shape),
                  const_spec(glg.shape), const_spec(glb.shape), mem_spec, mem_spec],
        out_specs=[pl.BlockSpec((rows // d, w), lambda i: (i, 0)) for d, w, _ in outs],
        out_shape=[jax.ShapeDtypeStruct((n // d, w), dt) for d, w, dt in outs],
        scratch_shapes=[pltpu.VMEM((2, rows, LANES), f32)],
        compiler_params=pltpu.CompilerParams(dimension_semantics=("arbitrary",),
                                             vmem_limit_bytes=VMEM_LIMIT_BYTES),
        name="in_proj_mixers",
    )(x2d, w_cat, cos_t, sin_t, rot_low, gws, gbs, glg, glb,
      mk.reshape(-1, batch, mem_len, MEM_WIDTH), mv.reshape(-1, batch, mem_len, MEM_WIDTH))


def _compress_kernel(k_ref, v_ref, pos_ref, w1_ref, w2_ref, kcmp_ref, vcmp_ref):
    n_rows = k_ref.shape[1]
    for idx, (src, dst) in enumerate(((k_ref, kcmp_ref), (v_ref, vcmp_ref))):
        xr = src[0]
        top = _dot((xr + pos_ref[idx, 0:1]).astype(jnp.bfloat16), w1_ref[idx, 0])
        bot = _dot((xr + pos_ref[idx, 1:2]).astype(jnp.bfloat16), w1_ref[idx, 1])
        hidden = top + pltpu.roll(bot, n_rows - 1, 0)
        act = jax.nn.gelu(hidden, approximate=True)
        dst[0, :, :LANES] = _dot(act.astype(jnp.bfloat16), w2_ref[idx]).astype(dst.dtype)
    vcmp_ref[0, :, LANES:] = jnp.ones((n_rows, LANES), vcmp_ref.dtype)


def _compress(kc_rows, vc_rows, pos, w1, w2):
    batch, n_rows, width = kc_rows.shape
    row_spec = pl.BlockSpec((1, n_rows, width), lambda b: (b, 0, 0))
    def out_spec(width):
        return pl.BlockSpec((1, n_rows, width), lambda b: (b, 0, 0))

    def out(width):
        return jax.ShapeDtypeStruct((batch, n_rows, width), jnp.bfloat16)

    return pl.pallas_call(
        _compress_kernel,
        grid=(batch,),
        in_specs=[row_spec, row_spec,
                  pl.BlockSpec(pos.shape, lambda b: (0, 0, 0)),
                  pl.BlockSpec(w1.shape, lambda b: (0, 0, 0, 0)),
                  pl.BlockSpec(w2.shape, lambda b: (0, 0, 0))],
        out_specs=[out_spec(LANES), out_spec(2 * LANES)],
        out_shape=[out(LANES), out(2 * LANES)],
        compiler_params=pltpu.CompilerParams(dimension_semantics=("arbitrary",),
                                             vmem_limit_bytes=VMEM_LIMIT_BYTES),
        name="nsa_compress",
    )(kc_rows, vc_rows, pos, w1, w2)


def _split_bf16(x, parts):
    out = []
    for _ in range(parts):
        hi = x.astype(jnp.bfloat16)
        out.append(hi)
        x = x - hi.astype(jnp.float32)
    return out


def _topk_columns(score):
    row = lax.broadcasted_iota(jnp.int32, score.shape, 0).astype(jnp.float32)
    picked = jnp.zeros(score.shape, jnp.float32)
    for _ in range(SEL_TOPK):
        best = jnp.max(score, axis=0, keepdims=True)
        first = jnp.min(jnp.where(score == best, row, np.float32(score.shape[0])),
                        axis=0, keepdims=True)
        hit = row == first
        picked = jnp.where(hit, 1.0, picked)
        score = jnp.where(hit, REMOVED, score)
    return picked


def _nsa_kernel(q_ref, qnext_ref, nz_ref, gate_ref, ksa_ref, vs_ref, kw_ref, vw_ref, kcmp_ref,
                vcmp_ref, ovl_ref, gexp_ref, out_ref, qa_ref, sc_ref, sw_ref, pc_ref, pw_ref, bc_ref,
                bw_ref, m_ref, acc_ref, oc_ref, ow_ref, sa_ref, sb_ref, mc_ref, mw_ref, qn_ref,
                selb_ref, ocn_ref, gx_ref, *, seq_len):
    bi = pl.program_id(1)
    start = bi * Q_BLOCK
    n_sel = seq_len // SEL_BLOCK
    n_cmp = kcmp_ref.shape[1]
    span = WINDOW + Q_BLOCK
    tk = SEL_KEY_TILE
    bf16 = jnp.bfloat16

    def head_rows(r):
        return slice(r * Q_BLOCK, (r + 1) * Q_BLOCK)

    low = _low_half((Q_BLOCK, LANES))
    chunk = SOFTMAX_ROWS
    n_chunks = Q_BLOCK // chunk

    def stack_queries(src_ref, dst_ref):
        for i in range(NSA_HPG):
            qi = src_ref[0, :, i * LANES:(i + 1) * LANES]
            zero = jnp.zeros_like(qi)
            dst_ref[head_rows(2 * i), :LANES] = jnp.where(low, qi, zero)
            dst_ref[head_rows(2 * i + 1), :LANES] = jnp.where(low, zero, qi)

    def compressed_scores(qs_ref, blk_start):
        t_b = blk_start + lax.broadcasted_iota(jnp.int32, (Q_BLOCK, 1), 0)
        c_end = lax.broadcasted_iota(jnp.int32, (1, n_cmp), 1) * CMP_STRIDE + (CMP_BLOCK - 1)
        bc_ref[...] = jnp.where(c_end <= t_b, 0.0, NEG_INF)
        sc_ref[...] = _dot_nt(qs_ref[:, :LANES], kcmp_ref[0])

    def masked_exp(s_ref, b_ref, m_ref_, p_ref):
        width_tiles = s_ref.shape[1] // LANES
        for r in range(NSA_HEADS):
            for c in range(n_chunks):
                crow = slice(c * chunk, (c + 1) * chunk)
                rows = slice(r * Q_BLOCK + c * chunk, r * Q_BLOCK + (c + 1) * chunk)
                row_max = jnp.max(s_ref[rows, :] + b_ref[crow, :], axis=-1, keepdims=True)
                m_ref_[rows, :] = jnp.broadcast_to(row_max, (chunk, LANES))
        for r in range(NSA_HEADS):
            for c in range(n_chunks):
                crow = slice(c * chunk, (c + 1) * chunk)
                rows = slice(r * Q_BLOCK + c * chunk, r * Q_BLOCK + (c + 1) * chunk)
                s = s_ref[rows, :] + b_ref[crow, :]
                p_ref[rows, :] = jnp.exp2(s - _tile_lanes(m_ref_[rows, :], width_tiles)).astype(bf16)

    def compressed_out(blk_start):
        t_col = blk_start + lax.broadcasted_iota(jnp.int32, (Q_BLOCK, 1), 0)
        seen_col = jnp.concatenate([t_col >= CMP_BLOCK - 1] * NSA_HEADS, axis=0)
        o_c = _dot(pc_ref[...], vcmp_ref[0])
        ocn_ref[...] = jnp.where(seen_col, o_c[:, :LANES] / o_c[:, LANES:], 0.0)

    def candidate_scores(blk_start):
        parts = _dot_nt(ovl_ref[...], pc_ref[...])
        t_lane = blk_start + _lane_iota((1, parts.shape[1])) % Q_BLOCK
        inv = jnp.where(t_lane >= CMP_BLOCK - 1, 1.0 / parts[LANES:LANES + 1, :], 0.0)
        weighted = parts[:LANES, :] * inv
        imp = jnp.concatenate(
            [sum(weighted[:, (2 * i + g) * Q_BLOCK:(2 * i + g + 1) * Q_BLOCK] for i in range(NSA_HPG))
             for g in range(NSA_KV_GROUPS)], axis=1)
        blk = lax.broadcasted_iota(jnp.int32, imp.shape, 0)
        t_blk = (blk_start + _lane_iota((1, imp.shape[1])) % Q_BLOCK) // SEL_BLOCK
        valid = blk <= t_blk
        forced = (blk == 0) | (valid & (blk > t_blk - N_LOCAL_SEL))
        score = jnp.where(forced, FORCE_SCORE, jnp.where(valid, imp, -1.0))
        if n_sel < LANES:
            score = jnp.where(blk < n_sel, score, REMOVED)
        return score

    def select_blocks(score):
        picked = _topk_columns(score).astype(bf16)
        eye = (lax.broadcasted_iota(jnp.int32, (Q_BLOCK, Q_BLOCK), 0)
               == lax.broadcasted_iota(jnp.int32, (Q_BLOCK, Q_BLOCK), 1)).astype(bf16)
        for g in range(NSA_KV_GROUPS):
            picked_q = _dot_nt(eye, picked[:, g * Q_BLOCK:(g + 1) * Q_BLOCK])
            selb_ref[g * Q_BLOCK:(g + 1) * Q_BLOCK, :] = ((1.0 - picked_q) * NEG_INF).astype(bf16)

    t_q = start + lax.broadcasted_iota(jnp.int32, (Q_BLOCK, 1), 0)
    w0 = pl.multiple_of(jnp.maximum(start - WINDOW, 0), Q_BLOCK)

    def window_scores():
        kpos = w0 + lax.broadcasted_iota(jnp.int32, (1, span), 1)
        bw_ref[...] = jnp.where((kpos <= t_q) & (kpos > t_q - WINDOW), 0.0, NEG_INF)
        sw_ref[...] = _dot_nt(qa_ref[:, :LANES], kw_ref[0, pl.ds(w0, span), :])

    def window_out():
        o_w = _dot(pw_ref[...], vw_ref[0, pl.ds(w0, span), :])
        ow_ref[...] = o_w[:, :LANES] / o_w[:, LANES:]

    def scores(tile):
        k0 = pl.multiple_of(tile * tk, tk)
        return _dot_nt(qa_ref[...], ksa_ref[0, pl.ds(k0, tk), :])

    @pl.when(bi == 0)
    def _():
        stack_queries(q_ref, qn_ref)
        compressed_scores(qn_ref, start)
        masked_exp(sc_ref, bc_ref, mc_ref, pc_ref)
        compressed_out(start)
        select_blocks(candidate_scores(start))

    stack_queries(q_ref, qa_ref)
    for r in range(NSA_HEADS):
        g = r % NSA_KV_GROUPS
        qa_ref[head_rows(r), LANES:] = selb_ref[g * Q_BLOCK:(g + 1) * Q_BLOCK, :]
    oc_ref[...] = ocn_ref[...]
    stack_queries(qnext_ref, qn_ref)

    nxt = start + Q_BLOCK
    compressed_scores(qn_ref, nxt)
    window_scores()
    masked_exp(sc_ref, bc_ref, mc_ref, pc_ref)
    next_score = candidate_scores(nxt)
    sa_ref[...] = scores(0)
    masked_exp(sw_ref, bw_ref, mw_ref, pw_ref)
    compressed_out(nxt)
    window_out()
    g_hi, g_lo = _split_bf16(gate_ref[0], 2)
    gx_ref[...] = _dot(g_hi, gexp_ref[...]) + _dot(g_lo, gexp_ref[...])
    select_blocks(next_score)

    m_ref[...] = jnp.full(m_ref.shape, NEG_INF, jnp.float32)
    acc_ref[...] = jnp.zeros(acc_ref.shape, jnp.float32)

    def consume(buf_ref, tile, causal):
        k0 = pl.multiple_of(tile * tk, tk)
        s = buf_ref[...]
        if causal:
            kpos = k0 + lax.broadcasted_iota(jnp.int32, (1, tk), 1)
            tile_bias = jnp.where(kpos <= t_q, 0.0, NEG_INF)
            s = s + jnp.concatenate([tile_bias] * NSA_HEADS, axis=0)
        m_prev = m_ref[...]
        m_next = jnp.maximum(m_prev, jnp.max(s, axis=-1, keepdims=True))
        p = jnp.exp2(s - _tile_lanes(m_next, tk // LANES))
        alpha = jnp.exp2(m_prev - m_next)
        acc_ref[...] = (_tile_lanes(alpha, 2) * acc_ref[...]
                        + _dot(p.astype(bf16), vs_ref[0, pl.ds(k0, tk), :]))
        m_ref[...] = m_next

    diag = start // tk

    def tile_pair(first):
        sb_ref[...] = scores(first + 1)
        consume(sa_ref, first, False)
        sa_ref[...] = scores(first + 2)
        consume(sb_ref, first + 1, False)

    def tile_quad(j, carry):
        tile_pair(4 * j)
        tile_pair(4 * j + 2)
        return carry

    lax.fori_loop(0, diag // 4, tile_quad, 0)

    @pl.when(diag % 4 >= 2)
    def _():
        tile_pair((diag // 4) * 4)

    @pl.when(diag % 2 == 1)
    def _():
        sb_ref[...] = scores(diag)
        consume(sa_ref, diag - 1, False)
        consume(sb_ref, diag, True)

    @pl.when(diag % 2 == 0)
    def _():
        consume(sa_ref, diag, True)

    for i in range(NSA_HPG):
        lo_rows, hi_rows = head_rows(2 * i), head_rows(2 * i + 1)
        o_s = jnp.where(low, acc_ref[lo_rows, :LANES] / acc_ref[lo_rows, LANES:],
                        acc_ref[hi_rows, :LANES] / acc_ref[hi_rows, LANES:])
        branches = (jnp.where(low, oc_ref[lo_rows], oc_ref[hi_rows]), o_s,
                    jnp.where(low, ow_ref[lo_rows], ow_ref[hi_rows]))
        mixed = sum(o * gx_ref[:, (3 * i + c) * LANES:(3 * i + c + 1) * LANES]
                    for c, o in enumerate(branches))
        sl = slice(i * LANES, (i + 1) * LANES)
        out_ref[0, :, sl] = (mixed * nz_ref[0, :, sl]).astype(out_ref.dtype)


def _nsa(q, nz, gates, ksa, vs, kw, vw, kcmp, vcmp, ovl, gexp):
    batch, seq_len, _ = q.shape

    def q_spec(width):
        return pl.BlockSpec((1, Q_BLOCK, width), lambda b, i: (b, i, 0))

    def seq_spec(arr):
        return pl.BlockSpec((1,) + arr.shape[1:], lambda b, i: (b, 0, 0))

    rows = NSA_HEADS * Q_BLOCK
    n_cmp = kcmp.shape[1]
    span = WINDOW + Q_BLOCK
    last = seq_len // Q_BLOCK - 1
    next_q_spec = pl.BlockSpec((1, Q_BLOCK, NSA_WIDTH), lambda b, i: (b, jnp.minimum(i + 1, last), 0))
    return pl.pallas_call(
        functools.partial(_nsa_kernel, seq_len=seq_len),
        grid=(batch, seq_len // Q_BLOCK),
        in_specs=[q_spec(NSA_WIDTH), next_q_spec, q_spec(NSA_WIDTH), q_spec(LANES),
                  seq_spec(ksa), seq_spec(vs), seq_spec(kw), seq_spec(vw),
                  seq_spec(kcmp), seq_spec(vcmp),
                  pl.BlockSpec(ovl.shape, lambda b, i: (0, 0)),
                  pl.BlockSpec(gexp.shape, lambda b, i: (0, 0))],
        out_specs=q_spec(NSA_WIDTH),
        out_shape=jax.ShapeDtypeStruct((batch, seq_len, NSA_WIDTH), jnp.bfloat16),
        scratch_shapes=[pltpu.VMEM((rows, 2 * LANES), jnp.bfloat16),
                        pltpu.VMEM((rows, n_cmp), jnp.float32),
                        pltpu.VMEM((rows, span), jnp.float32),
                        pltpu.VMEM((rows, n_cmp), jnp.bfloat16),
                        pltpu.VMEM((rows, span), jnp.bfloat16),
                        pltpu.VMEM((Q_BLOCK, n_cmp), jnp.float32),
                        pltpu.VMEM((Q_BLOCK, span), jnp.float32),
                        pltpu.VMEM((rows, LANES), jnp.float32),
                        pltpu.VMEM((rows, 2 * LANES), jnp.float32),
                        pltpu.VMEM((rows, LANES), jnp.float32),
                        pltpu.VMEM((rows, LANES), jnp.float32),
                        pltpu.VMEM((rows, SEL_KEY_TILE), jnp.float32),
                        pltpu.VMEM((rows, SEL_KEY_TILE), jnp.float32),
                        pltpu.VMEM((rows, LANES), jnp.float32),
                        pltpu.VMEM((rows, LANES), jnp.float32),
                        pltpu.VMEM((rows, LANES), jnp.bfloat16),
                        pltpu.VMEM((NSA_KV_GROUPS * Q_BLOCK, LANES), jnp.bfloat16),
                        pltpu.VMEM((rows, LANES), jnp.float32),
                        pltpu.VMEM((Q_BLOCK, gexp.shape[1]), jnp.float32)],
        compiler_params=pltpu.CompilerParams(dimension_semantics=("arbitrary", "arbitrary"),
                                             vmem_limit_bytes=VMEM_LIMIT_BYTES),
        name="nsa_attention",
    )(q, q, nz, gates, ksa, vs, kw, vw, kcmp, vcmp, ovl, gexp)


def _outproj_kernel(x_ref, ygm_ref, ynsa_ref, ymem_ref, w_ref, g_ref, b_ref, o_ref, *, alpha):
    def mix_proj(rs):
        return (_dot(ygm_ref[rs, :], w_ref[:GM_WIDTH])
                + _dot(ynsa_ref[rs, :], w_ref[GM_WIDTH:GM_WIDTH + NSA_WIDTH])
                + _dot(ymem_ref[rs, :], w_ref[GM_WIDTH + NSA_WIDTH:]))

    subs = [slice(r0, r0 + PROJ_SUB_ROWS) for r0 in range(0, x_ref.shape[0], PROJ_SUB_ROWS)]
    y_next = mix_proj(subs[0])
    for j, rs in enumerate(subs):
        y = y_next
        y_next = mix_proj(subs[j + 1]) if j + 1 < len(subs) else None
        r = alpha * x_ref[rs, :] + y
        mu = jnp.mean(r, axis=-1, keepdims=True)
        d = r - mu
        var = jnp.mean(d * d, axis=-1, keepdims=True)
        o_ref[rs, :] = d * lax.rsqrt(var + LN_EPS) * g_ref[...] + b_ref[...]


def _outproj(x2d, ygm, ynsa, ymem, w_out, ln_g, ln_b, *, layer, alpha):
    n, d_model = x2d.shape
    rows = OUT_PROJ_ROWS

    def tok_spec(width):
        return pl.BlockSpec((rows, width), lambda i: (i, 0))

    def layer_spec(shape):
        return pl.BlockSpec((None,) + shape[1:], lambda i: (layer, 0, 0))

    return pl.pallas_call(
        functools.partial(_outproj_kernel, alpha=alpha),
        grid=(n // rows,),
        in_specs=[tok_spec(d_model), tok_spec(GM_WIDTH), tok_spec(NSA_WIDTH), tok_spec(MEM_WIDTH),
                  layer_spec(w_out.shape), layer_spec(ln_g.shape), layer_spec(ln_b.shape)],
        out_specs=tok_spec(d_model),
        out_shape=jax.ShapeDtypeStruct((n, d_model), jnp.float32),
        compiler_params=pltpu.CompilerParams(dimension_semantics=("arbitrary",),
                                             vmem_limit_bytes=VMEM_LIMIT_BYTES),
        name="out_proj_layernorm",
    )(x2d, ygm, ynsa, ymem, w_out, ln_g, ln_b)


def _pair_head_slices(w, start, axis):
    return [lax.slice_in_dim(w, start + h * HEAD_DIM, start + (h + 1) * HEAD_DIM, axis=axis)
            for h in PAIR_HEAD_ORDER]


def _w_in_pieces():
    o_gate = 2048
    o_nz = o_gate + GATE_COLS
    o_mq = o_nz + NSA_WIDTH
    end = o_mq + 2 * MEM_WIDTH
    heads = lambda start: [(start + h * HEAD_DIM, HEAD_DIM) for h in PAIR_HEAD_ORDER]
    return ([(0, 768)] + heads(768) + [(1280, 768)] + heads(o_nz) + [(o_mq, end - o_mq), (o_gate, GATE_COLS)])


def _permute_w_in_kernel(w_ref, o_ref):
    dst = 0
    for src, width in _w_in_pieces():
        o_ref[:, dst:dst + width] = w_ref[:, src:src + width].astype(o_ref.dtype)
        dst += width
    o_ref[:, dst:] = jnp.zeros((o_ref.shape[0], o_ref.shape[1] - dst), o_ref.dtype)


def _permute_w_in(w):
    depth, d_model, in_cols = w.shape
    rows = PROJ_SUB_ROWS
    return pl.pallas_call(
        _permute_w_in_kernel,
        grid=(depth, d_model // rows),
        in_specs=[pl.BlockSpec((None, rows, in_cols), lambda l, i: (l, i, 0))],
        out_specs=pl.BlockSpec((None, rows, N_COLS), lambda l, i: (l, i, 0)),
        out_shape=jax.ShapeDtypeStruct((depth, d_model, N_COLS), jnp.bfloat16),
        name="w_in_relayout",
    )(w)


def _permute_w_out(w):
    pieces = ([w[:, :GM_WIDTH]] + _pair_head_slices(w, GM_WIDTH, 1) + [w[:, GM_WIDTH + NSA_WIDTH:]])
    return jnp.concatenate([p.astype(jnp.bfloat16) for p in pieces], axis=1)


def _rope_tables(seq_len):
    half = HEAD_DIM // 2
    inv_freq = ROPE_THETA ** (-jnp.arange(half, dtype=jnp.float32) * 2.0 / HEAD_DIM)
    ang = jnp.arange(seq_len).astype(jnp.float32)[:, None] * inv_freq[None, :]
    cos, sin = jnp.cos(ang), jnp.sin(ang)
    reps = LANES // HEAD_DIM
    cos_t = jnp.tile(jnp.concatenate([cos, cos], axis=1), (1, reps))
    sin_t = jnp.tile(jnp.concatenate([-sin, sin], axis=1), (1, reps))
    rot_low = ((np.arange(LANES) % HEAD_DIM) < half).astype(np.float32)[None, :]
    return cos_t, sin_t, jnp.asarray(rot_low)


def _compress_weights(pos_k, w1_k, w2_k, pos_v, w1_v, w2_v):
    half = CMP_BLOCK // 2

    def block_diag2(w):
        z = jnp.zeros_like(w)
        return jnp.concatenate([jnp.concatenate([w, z], axis=-1),
                                jnp.concatenate([z, w], axis=-1)], axis=-2)

    def expand_w1(w1):
        w = w1.reshape(2, half, HEAD_DIM, CMP_HIDDEN)
        w = block_diag2(w)
        return w.reshape(2, half * NSA_KV_WIDTH, NSA_KV_GROUPS * CMP_HIDDEN)

    def expand_w2(w2):
        return block_diag2(w2)

    def expand_pos(pos):
        p = pos.reshape(2, half, 1, HEAD_DIM)
        p = jnp.broadcast_to(p, (2, half, NSA_KV_GROUPS, HEAD_DIM))
        return p.reshape(2, half * NSA_KV_WIDTH)

    pos = jnp.stack([expand_pos(pos_k), expand_pos(pos_v)])
    w1 = jnp.stack([expand_w1(w1_k), expand_w1(w1_v)]).astype(jnp.bfloat16)
    w2 = jnp.stack([expand_w2(w2_k), expand_w2(w2_v)]).astype(jnp.bfloat16)
    return pos, w1, w2


def _gate_expansion():
    out = np.zeros((LANES, NSA_HPG * 3 * LANES), np.float32)
    for i in range(NSA_HPG):
        for c in range(3):
            base = (3 * i + c) * LANES
            out[3 * i + c, base:base + HEAD_DIM] = 1.0
            out[3 * (i + NSA_HPG) + c, base + HEAD_DIM:base + LANES] = 1.0
    return jnp.asarray(out, dtype=jnp.bfloat16)


def _overlap_matrix(n_rows, n_sel):
    c_start = np.arange(n_rows) * CMP_STRIDE
    s_start = np.arange(LANES) * SEL_BLOCK
    ovl = ((c_start[:, None] < s_start[None, :] + SEL_BLOCK)
           & (c_start[:, None] + CMP_BLOCK > s_start[None, :])
           & (np.arange(LANES)[None, :] < n_sel))
    out = np.zeros((LANES + 16, n_rows), np.float32)
    out[:LANES] = ovl.T
    out[LANES] = 1.0
    return jnp.asarray(out, dtype=jnp.bfloat16)


def kernel(x, mem, w_in, gm_ln_g, gm_ln_b, gm_ws, gm_bs, cmp_pos_k, cmp_k_w1, cmp_k_w2,
           cmp_pos_v, cmp_v_w1, cmp_v_w2, w_mem_kv, w_out, ln_g, ln_b):
    batch, seq_len, d_model = x.shape
    depth = w_in.shape[0]
    assert seq_len % SEL_KEY_TILE == 0 and seq_len >= WINDOW + Q_BLOCK
    assert SEL_TOPK <= seq_len // SEL_BLOCK <= LANES
    alpha = (2.0 * depth) ** 0.25
    n_tok = batch * seq_len
    n_rows = seq_len // CMP_STRIDE

    cos_t, sin_t, rot_low = _rope_tables(seq_len)
    ovl = _overlap_matrix(n_rows, seq_len // SEL_BLOCK)
    gexp = _gate_expansion()
    tril = jnp.tril(jnp.ones((GM_CHUNK, GM_CHUNK), gm_ws.dtype))
    mk_all, mv_all = _memkv(mem.reshape(batch * mem.shape[1], d_model), w_mem_kv.astype(jnp.bfloat16))
    w_cat_all = _permute_w_in(w_in)
    w_out_all = _permute_w_out(w_out)

    h = x.reshape(n_tok, d_model)
    for l in range(depth):
        gws = (gm_ws[l] * tril[None]).astype(jnp.bfloat16)
        gbs = jnp.repeat(gm_bs[l].T, HEAD_DIM, axis=1)
        glg = gm_ln_g[l].reshape(1, GM_WIDTH)
        glb = gm_ln_b[l].reshape(1, GM_WIDTH)
        (ygm, ymem, q, kc, vc, ksa, vs, kw, vw, nz, gates) = _inproj(
            h, w_cat_all, cos_t, sin_t, rot_low, gws, gbs, glg, glb, mk_all, mv_all,
            layer=l, batch=batch, seq_len=seq_len)

        pos, w1, w2 = _compress_weights(cmp_pos_k[l], cmp_k_w1[l], cmp_k_w2[l],
                                        cmp_pos_v[l], cmp_v_w1[l], cmp_v_w2[l])
        row_shape = (batch, n_rows, CMP_STRIDE * NSA_KV_WIDTH)
        kcmp, vcmp = _compress(kc.reshape(row_shape), vc.reshape(row_shape), pos, w1, w2)

        def per_seq(a):
            return a.reshape(batch, seq_len, a.shape[-1])

        ynsa = _nsa(per_seq(q), per_seq(nz), per_seq(gates), per_seq(ksa), per_seq(vs),
                    per_seq(kw), per_seq(vw), kcmp, vcmp, ovl, gexp)

        h = _outproj(h, ygm, ynsa.reshape(n_tok, NSA_WIDTH), ymem, w_out_all,
                     ln_g.reshape(depth, 1, d_model), ln_b.reshape(depth, 1, d_model),
                     layer=l, alpha=alpha)
    return h.reshape(batch, seq_len, d_model)
```

```python
import functools

import numpy as np
import jax
import jax.numpy as jnp
from jax import lax
from jax.experimental import pallas as pl
from jax.experimental.pallas import tpu as pltpu

HEAD_DIM = 64
GM_GROUPS = 4
GM_WIDTH = GM_GROUPS * HEAD_DIM
GM_CHUNK = 128
NSA_HEADS = 8
NSA_KV_GROUPS = 2
NSA_HPG = NSA_HEADS // NSA_KV_GROUPS
NSA_WIDTH = NSA_HEADS * HEAD_DIM
NSA_KV_WIDTH = NSA_KV_GROUPS * HEAD_DIM
CMP_BLOCK = 32
CMP_STRIDE = 16
CMP_HIDDEN = 128
SEL_BLOCK = 64
SEL_TOPK = 16
N_LOCAL_SEL = 2
WINDOW = 512
Q_BLOCK = 128
MEM_HEADS = 4
MEM_WIDTH = MEM_HEADS * HEAD_DIM
ROPE_THETA = 10000.0
LN_EPS = 1e-5
NEG_INF = -1e30
FORCE_SCORE = 1e4
GATE_COLS = NSA_HEADS * 3

LANES = 128
VMEM_LIMIT_BYTES = 56 * 1024 * 1024

PROJ_ROWS = 1024
OUT_PROJ_ROWS = 2048
PROJ_SUB_ROWS = 256
SEL_KEY_TILE = 512
SOFTMAX_ROWS = 32
REMOVED = -3.0e38

PAIR_HEAD_ORDER = tuple(h for i in range(NSA_HPG) for h in (i, i + NSA_HPG))

C_GU, C_GV, C_GZ = 0, 256, 512
C_Q = 768
C_KC, C_VC, C_KS, C_VS, C_KW, C_VW = 1280, 1408, 1536, 1664, 1792, 1920
C_NZ = 2048
C_MQ, C_MZ = 2560, 2816
C_GATE = 3072
N_COLS = 3200


def _dot(a, b):
    return jnp.dot(a, b, preferred_element_type=jnp.float32)


def _dot_nt(a, b):
    return lax.dot_general(a, b, (((1,), (1,)), ((), ())), preferred_element_type=jnp.float32)


def _gelu(x):
    return 0.5 * x * (1.0 + lax.erf(x * np.float32(np.sqrt(0.5))))


def _silu(x):
    return x * jax.nn.sigmoid(x)


def _lane_iota(shape):
    return lax.broadcasted_iota(jnp.int32, shape, len(shape) - 1)


def _low_half(shape):
    return (_lane_iota(shape) % LANES) < HEAD_DIM


def _tile_lanes(x, reps):
    return jnp.concatenate([x] * reps, axis=-1) if reps > 1 else x


def _memkv_kernel(mem_ref, w_ref, k_ref, v_ref):
    kv = _dot(mem_ref[...].astype(jnp.bfloat16), w_ref[0])
    k_ref[0] = kv[:, :MEM_WIDTH].astype(jnp.bfloat16)
    v_ref[0] = kv[:, MEM_WIDTH:].astype(jnp.bfloat16)


def _memkv(mem2d, w_mem_kv_bf16):
    depth = w_mem_kv_bf16.shape[0]
    rows, d_model = mem2d.shape
    out = jax.ShapeDtypeStruct((depth, rows, MEM_WIDTH), jnp.bfloat16)
    return pl.pallas_call(
        _memkv_kernel,
        grid=(depth,),
        in_specs=[pl.BlockSpec((rows, d_model), lambda l: (0, 0)),
                  pl.BlockSpec((1, d_model, 2 * MEM_WIDTH), lambda l: (l, 0, 0))],
        out_specs=[pl.BlockSpec((1, rows, MEM_WIDTH), lambda l: (l, 0, 0)),
                   pl.BlockSpec((1, rows, MEM_WIDTH), lambda l: (l, 0, 0))],
        out_shape=[out, out],
        name="mem_kv_proj",
    )(mem2d, w_mem_kv_bf16)


def _rope(x, cos, sin_signed, low):
    width = x.shape[-1]
    swapped = jnp.where(low, pltpu.roll(x, width - HEAD_DIM // 2, 1), pltpu.roll(x, HEAD_DIM // 2, 1))
    return x * cos + swapped * sin_signed


def _group_layer_norm(v, g, b, low):
    inv = np.float32(1.0 / HEAD_DIM)
    s_lo = jnp.sum(jnp.where(low, v, 0.0), axis=-1, keepdims=True)
    s_hi = jnp.sum(jnp.where(low, 0.0, v), axis=-1, keepdims=True)
    mu = jnp.where(low, s_lo, s_hi) * inv
    d = v - mu
    d2 = d * d
    q_lo = jnp.sum(jnp.where(low, d2, 0.0), axis=-1, keepdims=True)
    q_hi = jnp.sum(jnp.where(low, 0.0, d2), axis=-1, keepdims=True)
    var = jnp.where(low, q_lo, q_hi) * inv
    return d * lax.rsqrt(var + LN_EPS) * g + b


def _inproj_kernel(x_ref, w_ref, cos_ref, sin_ref, rot_low_ref, gws_ref, gbs_ref, glg_ref, glb_ref,
                   mk_ref, mv_ref,
                   ygm_ref, ymem_ref, q_ref, kc_ref, vc_ref, ksa_ref, vs_ref, kw_ref, vw_ref,
                   nz_ref, gate_ref, stage_ref, *, seq_len):
    subs = [slice(r0, r0 + PROJ_SUB_ROWS) for r0 in range(0, x_ref.shape[0], PROJ_SUB_ROWS)]
    mixers = _project(x_ref, w_ref, subs[0], MIXER_SECTIONS)
    others = _project(x_ref, w_ref, subs[0], OTHER_SECTIONS)
    for j, rs in enumerate(subs):
        upcoming = subs[j + 1] if j + 1 < len(subs) else None
        next_mixers = _project(x_ref, w_ref, upcoming, MIXER_SECTIONS) if upcoming else None
        epilogue = _inproj_rows(rs, pl.program_id(0) * x_ref.shape[0] + rs.start, mixers, others,
                                cos_ref, sin_ref, rot_low_ref, gws_ref, gbs_ref, glg_ref, glb_ref,
                                mk_ref, mv_ref, ygm_ref, ymem_ref, q_ref, kc_ref, vc_ref, ksa_ref,
                                vs_ref, kw_ref, vw_ref, nz_ref, gate_ref, stage_ref, seq_len=seq_len)
        next(epilogue)
        next_others = _project(x_ref, w_ref, upcoming, OTHER_SECTIONS) if upcoming else None
        for _ in epilogue:
            pass
        mixers, others = next_mixers, next_others


MIXER_SECTIONS = ((C_GU, 3 * GM_WIDTH), (C_MQ, 2 * MEM_WIDTH))
OTHER_SECTIONS = ((C_Q, NSA_WIDTH), (C_KC, 6 * NSA_KV_WIDTH), (C_NZ, NSA_WIDTH), (C_GATE, LANES))


def _project(x_ref, w_ref, rs, sections):
    xb = x_ref[rs, :].astype(jnp.bfloat16)
    return tuple(_dot(xb, w_ref[:, c0:c0 + width]) for c0, width in sections)


def _inproj_rows(rs, row0, mixers, others, cos_ref, sin_ref, rot_low_ref, gws_ref, gbs_ref, glg_ref,
                 glb_ref, mk_ref, mv_ref, ygm_ref, ymem_ref, q_ref, kc_ref, vc_ref, ksa_ref, vs_ref,
                 kw_ref, vw_ref, nz_ref, gate_ref, stage_ref, *, seq_len):
    rows = rs.stop - rs.start
    gm, mem = mixers
    qh, kv, nz_raw, gate_raw = others
    low = _low_half((rows, LANES))
    rot_low = rot_low_ref[...] > 0.5
    rot_low = jnp.broadcast_to(rot_low, (rows, LANES))
    cos = cos_ref[rs, :]
    sin = sin_ref[rs, :]

    def slab(h, i):
        return h[:, i * LANES:(i + 1) * LANES]

    qscale = np.float32(HEAD_DIM ** -0.5)

    u = _gelu(gm[:, :GM_WIDTH])
    v = _gelu(gm[:, GM_WIDTH:2 * GM_WIDTH])
    z = gm[:, 2 * GM_WIDTH:]
    spatial = {}
    for pair in range(GM_GROUPS // 2):
        sl = slice(pair * LANES, (pair + 1) * LANES)
        vln = _group_layer_norm(v[:, sl], glg_ref[:, sl], glb_ref[:, sl], low).astype(jnp.bfloat16)
        for c in range(rows // GM_CHUNK):
            cs = slice(c * GM_CHUNK, (c + 1) * GM_CHUNK)
            spatial[pair, c] = (_dot(gws_ref[2 * pair], vln[cs]), _dot(gws_ref[2 * pair + 1], vln[cs]))
    mq = mem[:, :MEM_WIDTH] * qscale
    mz = mem[:, MEM_WIDTH:]
    mem_scores = {}
    for pair in range(MEM_HEADS // 2):
        sl = slice(pair * LANES, (pair + 1) * LANES)
        for keep_low in (True, False):
            qm = jnp.where(low == keep_low, mq[:, sl], 0.0).astype(jnp.bfloat16)
            mem_scores[pair, keep_low] = _dot_nt(qm, mk_ref[0, :, sl])
    yield

    for pair in range(GM_GROUPS // 2):
        sl = slice(pair * LANES, (pair + 1) * LANES)
        for c in range(rows // GM_CHUNK):
            cs = slice(c * GM_CHUNK, (c + 1) * GM_CHUNK)
            out_rows = slice(rs.start + c * GM_CHUNK, rs.start + (c + 1) * GM_CHUNK)
            s_lo, s_hi = spatial[pair, c]
            s = jnp.where(_low_half((GM_CHUNK, LANES)), s_lo, s_hi) + gbs_ref[:, sl]
            ygm_ref[out_rows, sl] = (u[cs, sl] * s * _silu(z[cs, sl])).astype(ygm_ref.dtype)

    qscale2 = np.float32(HEAD_DIM ** -0.5 * np.log2(np.e))
    ones = jnp.ones((rows, LANES), vs_ref.dtype)
    for i in range(NSA_WIDTH // LANES):
        qi = _rope(slab(qh, i), cos, sin, rot_low) * qscale2
        q_ref[rs, i * LANES:(i + 1) * LANES] = qi.astype(q_ref.dtype)
    stage_ref[0, rs, :] = _rope(slab(kv, 0), cos, sin, rot_low)
    stage_ref[1, rs, :] = slab(kv, 1)
    out_rows = slice(rs.start // CMP_STRIDE, rs.stop // CMP_STRIDE)
    for j, dst in enumerate((kc_ref, vc_ref)):
        for l in range(CMP_STRIDE):
            token_l = stage_ref[j, pl.ds(rs.start + l, rows // CMP_STRIDE, stride=CMP_STRIDE), :]
            dst[out_rows, l * LANES:(l + 1) * LANES] = token_l
    ksa_ref[rs, :LANES] = _rope(slab(kv, 2), cos, sin, rot_low).astype(ksa_ref.dtype)
    tok = row0 % seq_len + lax.broadcasted_iota(jnp.int32, (rows, LANES), 0)
    onehot = (tok // SEL_BLOCK) == _lane_iota((rows, LANES))
    ksa_ref[rs, LANES:] = jnp.where(onehot, 1.0, 0.0).astype(ksa_ref.dtype)
    vs_ref[rs, :LANES] = slab(kv, 3).astype(vs_ref.dtype)
    vs_ref[rs, LANES:] = ones
    kw_ref[rs, :] = _rope(slab(kv, 4), cos, sin, rot_low).astype(kw_ref.dtype)
    vw_ref[rs, :LANES] = slab(kv, 5).astype(vw_ref.dtype)
    vw_ref[rs, LANES:] = ones
    nz_ref[rs, :] = _silu(nz_raw)
    gate_ref[rs, :] = jax.nn.sigmoid(gate_raw)

    for pair in range(MEM_HEADS // 2):
        sl = slice(pair * LANES, (pair + 1) * LANES)
        vp = mv_ref[0, :, sl]
        outs = []
        for keep_low in (True, False):
            s = mem_scores[pair, keep_low]
            e = jnp.exp(s - jnp.max(s, axis=-1, keepdims=True))
            p = e / jnp.sum(e, axis=-1, keepdims=True)
            outs.append(_dot(p.astype(jnp.bfloat16), vp))
        o = jnp.where(low, outs[0], outs[1])
        ymem_ref[rs, sl] = (o * _silu(mz[:, sl])).astype(ymem_ref.dtype)


def _inproj(x2d, w_cat, cos_t, sin_t, rot_low, gws, gbs, glg, glb, mk, mv, *, layer, batch, seq_len):
    n, d_model = x2d.shape
    rows = PROJ_ROWS
    steps_per_seq = seq_len // rows
    mem_len = mk.shape[1] // batch

    def tok_spec(width):
        return pl.BlockSpec((rows, width), lambda i: (i, 0))

    def const_spec(shape):
        return pl.BlockSpec(shape, lambda i: (0,) * len(shape))

    def layer_spec(shape):
        return pl.BlockSpec((None,) + shape[1:], lambda i: (layer,) + (0,) * (len(shape) - 1))

    tab_spec = pl.BlockSpec((rows, LANES), lambda i: (i % steps_per_seq, 0))
    mem_spec = pl.BlockSpec((None, 1, mem_len, MEM_WIDTH), lambda i: (layer, i // steps_per_seq, 0, 0))
    bf16, f32 = jnp.bfloat16, jnp.float32
    outs = [(1, GM_WIDTH, bf16), (1, MEM_WIDTH, bf16), (1, NSA_WIDTH, bf16),
            (CMP_STRIDE, CMP_STRIDE * LANES, f32), (CMP_STRIDE, CMP_STRIDE * LANES, f32),
            (1, 2 * LANES, bf16), (1, 2 * LANES, bf16), (1, LANES, bf16), (1, 2 * LANES, bf16),
            (1, NSA_WIDTH, f32), (1, LANES, f32)]
    return pl.pallas_call(
        functools.partial(_inproj_kernel, seq_len=seq_len),
        grid=(n // rows,),
        in_specs=[tok_spec(d_model), layer_spec(w_cat.shape), tab_spec, tab_spec,
                  const_spec(rot_low.shape), const_spec(gws.shape), const_spec(gbs.shape),
                  const_spec(glg.shape), const_spec(glb.shape), mem_spec, mem_spec],
        out_specs=[pl.BlockSpec((rows // d, w), lambda i: (i, 0)) for d, w, _ in outs],
        out_shape=[jax.ShapeDtypeStruct((n // d, w), dt) for d, w, dt in outs],
        scratch_shapes=[pltpu.VMEM((2, rows, LANES), f32)],
        compiler_params=pltpu.CompilerParams(dimension_semantics=("arbitrary",),
                                             vmem_limit_bytes=VMEM_LIMIT_BYTES),
        name="in_proj_mixers",
    )(x2d, w_cat, cos_t, sin_t, rot_low, gws, gbs, glg, glb,
      mk.reshape(-1, batch, mem_len, MEM_WIDTH), mv.reshape(-1, batch, mem_len, MEM_WIDTH))


def _compress_kernel(k_ref, v_ref, pos_ref, w1_ref, w2_ref, kcmp_ref, vcmp_ref):
    n_rows = k_ref.shape[1]
    for idx, (src, dst) in enumerate(((k_ref, kcmp_ref), (v_ref, vcmp_ref))):
        xr = src[0]
        top = _dot((xr + pos_ref[idx, 0:1]).astype(jnp.bfloat16), w1_ref[idx, 0])
        bot = _dot((xr + pos_ref[idx, 1:2]).astype(jnp.bfloat16), w1_ref[idx, 1])
        hidden = top + pltpu.roll(bot, n_rows - 1, 0)
        act = jax.nn.gelu(hidden, approximate=True)
        dst[0, :, :LANES] = _dot(act.astype(jnp.bfloat16), w2_ref[idx]).astype(dst.dtype)
    vcmp_ref[0, :, LANES:] = jnp.ones((n_rows, LANES), vcmp_ref.dtype)


def _compress(kc_rows, vc_rows, pos, w1, w2):
    batch, n_rows, width = kc_rows.shape
    row_spec = pl.BlockSpec((1, n_rows, width), lambda b: (b, 0, 0))
    def out_spec(width):
        return pl.BlockSpec((1, n_rows, width), lambda b: (b, 0, 0))

    def out(width):
        return jax.ShapeDtypeStruct((batch, n_rows, width), jnp.bfloat16)

    return pl.pallas_call(
        _compress_kernel,
        grid=(batch,),
        in_specs=[row_spec, row_spec,
                  pl.BlockSpec(pos.shape, lambda b: (0, 0, 0)),
                  pl.BlockSpec(w1.shape, lambda b: (0, 0, 0, 0)),
                  pl.BlockSpec(w2.shape, lambda b: (0, 0, 0))],
        out_specs=[out_spec(LANES), out_spec(2 * LANES)],
        out_shape=[out(LANES), out(2 * LANES)],
        compiler_params=pltpu.CompilerParams(dimension_semantics=("arbitrary",),
                                             vmem_limit_bytes=VMEM_LIMIT_BYTES),
        name="nsa_compress",
    )(kc_rows, vc_rows, pos, w1, w2)


def _split_bf16(x, parts):
    out = []
    for _ in range(parts):
        hi = x.astype(jnp.bfloat16)
        out.append(hi)
        x = x - hi.astype(jnp.float32)
    return out


def _topk_columns(score):
    row = lax.broadcasted_iota(jnp.int32, score.shape, 0).astype(jnp.float32)
    picked = jnp.zeros(score.shape, jnp.float32)
    for _ in range(SEL_TOPK):
        best = jnp.max(score, axis=0, keepdims=True)
        first = jnp.min(jnp.where(score == best, row, np.float32(score.shape[0])),
                        axis=0, keepdims=True)
        hit = row == first
        picked = jnp.where(hit, 1.0, picked)
        score = jnp.where(hit, REMOVED, score)
    return picked


def _nsa_kernel(q_ref, qnext_ref, nz_ref, gate_ref, ksa_ref, vs_ref, kw_ref, vw_ref, kcmp_ref,
                vcmp_ref, ovl_ref, gexp_ref, out_ref, qa_ref, sc_ref, sw_ref, pc_ref, pw_ref, bc_ref,
                bw_ref, m_ref, acc_ref, oc_ref, ow_ref, sa_ref, sb_ref, mc_ref, mw_ref, qn_ref,
                selb_ref, ocn_ref, gx_ref, *, seq_len):
    bi = pl.program_id(1)
    start = bi * Q_BLOCK
    n_sel = seq_len // SEL_BLOCK
    n_cmp = kcmp_ref.shape[1]
    span = WINDOW + Q_BLOCK
    tk = SEL_KEY_TILE
    bf16 = jnp.bfloat16

    def head_rows(r):
        return slice(r * Q_BLOCK, (r + 1) * Q_BLOCK)

    low = _low_half((Q_BLOCK, LANES))
    chunk = SOFTMAX_ROWS
    n_chunks = Q_BLOCK // chunk

    def stack_queries(src_ref, dst_ref):
        for i in range(NSA_HPG):
            qi = src_ref[0, :, i * LANES:(i + 1) * LANES]
            zero = jnp.zeros_like(qi)
            dst_ref[head_rows(2 * i), :LANES] = jnp.where(low, qi, zero)
            dst_ref[head_rows(2 * i + 1), :LANES] = jnp.where(low, zero, qi)

    def compressed_scores(qs_ref, blk_start):
        t_b = blk_start + lax.broadcasted_iota(jnp.int32, (Q_BLOCK, 1), 0)
        c_end = lax.broadcasted_iota(jnp.int32, (1, n_cmp), 1) * CMP_STRIDE + (CMP_BLOCK - 1)
        bc_ref[...] = jnp.where(c_end <= t_b, 0.0, NEG_INF)
        sc_ref[...] = _dot_nt(qs_ref[:, :LANES], kcmp_ref[0])

    def masked_exp(s_ref, b_ref, m_ref_, p_ref):
        width_tiles = s_ref.shape[1] // LANES
        for r in range(NSA_HEADS):
            for c in range(n_chunks):
                crow = slice(c * chunk, (c + 1) * chunk)
                rows = slice(r * Q_BLOCK + c * chunk, r * Q_BLOCK + (c + 1) * chunk)
                row_max = jnp.max(s_ref[rows, :] + b_ref[crow, :], axis=-1, keepdims=True)
                m_ref_[rows, :] = jnp.broadcast_to(row_max, (chunk, LANES))
        for r in range(NSA_HEADS):
            for c in range(n_chunks):
                crow = slice(c * chunk, (c + 1) * chunk)
                rows = slice(r * Q_BLOCK + c * chunk, r * Q_BLOCK + (c + 1) * chunk)
                s = s_ref[rows, :] + b_ref[crow, :]
                p_ref[rows, :] = jnp.exp2(s - _tile_lanes(m_ref_[rows, :], width_tiles)).astype(bf16)

    def compressed_out(blk_start):
        t_col = blk_start + lax.broadcasted_iota(jnp.int32, (Q_BLOCK, 1), 0)
        seen_col = jnp.concatenate([t_col >= CMP_BLOCK - 1] * NSA_HEADS, axis=0)
        o_c = _dot(pc_ref[...], vcmp_ref[0])
        ocn_ref[...] = jnp.where(seen_col, o_c[:, :LANES] / o_c[:, LANES:], 0.0)

    def candidate_scores(blk_start):
        parts = _dot_nt(ovl_ref[...], pc_ref[...])
        t_lane = blk_start + _lane_iota((1, parts.shape[1])) % Q_BLOCK
        inv = jnp.where(t_lane >= CMP_BLOCK - 1, 1.0 / parts[LANES:LANES + 1, :], 0.0)
        weighted = parts[:LANES, :] * inv
        imp = jnp.concatenate(
            [sum(weighted[:, (2 * i + g) * Q_BLOCK:(2 * i + g + 1) * Q_BLOCK] for i in range(NSA_HPG))
             for g in range(NSA_KV_GROUPS)], axis=1)
        blk = lax.broadcasted_iota(jnp.int32, imp.shape, 0)
        t_blk = (blk_start + _lane_iota((1, imp.shape[1])) % Q_BLOCK) // SEL_BLOCK
        valid = blk <= t_blk
        forced = (blk == 0) | (valid & (blk > t_blk - N_LOCAL_SEL))
        score = jnp.where(forced, FORCE_SCORE, jnp.where(valid, imp, -1.0))
        if n_sel < LANES:
            score = jnp.where(blk < n_sel, score, REMOVED)
        return score

    def select_blocks(score):
        picked = _topk_columns(score).astype(bf16)
        eye = (lax.broadcasted_iota(jnp.int32, (Q_BLOCK, Q_BLOCK), 0)
               == lax.broadcasted_iota(jnp.int32, (Q_BLOCK, Q_BLOCK), 1)).astype(bf16)
        for g in range(NSA_KV_GROUPS):
            picked_q = _dot_nt(eye, picked[:, g * Q_BLOCK:(g + 1) * Q_BLOCK])
            selb_ref[g * Q_BLOCK:(g + 1) * Q_BLOCK, :] = ((1.0 - picked_q) * NEG_INF).astype(bf16)

    t_q = start + lax.broadcasted_iota(jnp.int32, (Q_BLOCK, 1), 0)
    w0 = pl.multiple_of(jnp.maximum(start - WINDOW, 0), Q_BLOCK)

    def window_scores():
        kpos = w0 + lax.broadcasted_iota(jnp.int32, (1, span), 1)
        bw_ref[...] = jnp.where((kpos <= t_q) & (kpos > t_q - WINDOW), 0.0, NEG_INF)
        sw_ref[...] = _dot_nt(qa_ref[:, :LANES], kw_ref[0, pl.ds(w0, span), :])

    def window_out():
        o_w = _dot(pw_ref[...], vw_ref[0, pl.ds(w0, span), :])
        ow_ref[...] = o_w[:, :LANES] / o_w[:, LANES:]

    def scores(tile):
        k0 = pl.multiple_of(tile * tk, tk)
        return _dot_nt(qa_ref[...], ksa_ref[0, pl.ds(k0, tk), :])

    @pl.when(bi == 0)
    def _():
        stack_queries(q_ref, qn_ref)
        compressed_scores(qn_ref, start)
        masked_exp(sc_ref, bc_ref, mc_ref, pc_ref)
        compressed_out(start)
        select_blocks(candidate_scores(start))

    stack_queries(q_ref, qa_ref)
    for r in range(NSA_HEADS):
        g = r % NSA_KV_GROUPS
        qa_ref[head_rows(r), LANES:] = selb_ref[g * Q_BLOCK:(g + 1) * Q_BLOCK, :]
    oc_ref[...] = ocn_ref[...]
    stack_queries(qnext_ref, qn_ref)

    nxt = start + Q_BLOCK
    compressed_scores(qn_ref, nxt)
    window_scores()
    masked_exp(sc_ref, bc_ref, mc_ref, pc_ref)
    next_score = candidate_scores(nxt)
    sa_ref[...] = scores(0)
    masked_exp(sw_ref, bw_ref, mw_ref, pw_ref)
    compressed_out(nxt)
    window_out()
    gx_ref[...] = _dot(gate_ref[0].astype(bf16), gexp_ref[...])
    select_blocks(next_score)

    m_ref[...] = jnp.full(m_ref.shape, NEG_INF, jnp.float32)
    acc_ref[...] = jnp.zeros(acc_ref.shape, jnp.float32)

    def consume(buf_ref, tile, causal):
        k0 = pl.multiple_of(tile * tk, tk)
        s = buf_ref[...]
        if causal:
            kpos = k0 + lax.broadcasted_iota(jnp.int32, (1, tk), 1)
            tile_bias = jnp.where(kpos <= t_q, 0.0, NEG_INF)
            s = s + jnp.concatenate([tile_bias] * NSA_HEADS, axis=0)
        m_prev = m_ref[...]
        m_next = jnp.maximum(m_prev, jnp.max(s, axis=-1, keepdims=True))
        p = jnp.exp2(s - _tile_lanes(m_next, tk // LANES))
        alpha = jnp.exp2(m_prev - m_next)
        acc_ref[...] = (_tile_lanes(alpha, 2) * acc_ref[...]
                        + _dot(p.astype(bf16), vs_ref[0, pl.ds(k0, tk), :]))
        m_ref[...] = m_next

    diag = start // tk

    def tile_pair(first):
        sb_ref[...] = scores(first + 1)
        consume(sa_ref, first, False)
        sa_ref[...] = scores(first + 2)
        consume(sb_ref, first + 1, False)

    def tile_quad(j, carry):
        tile_pair(4 * j)
        tile_pair(4 * j + 2)
        return carry

    lax.fori_loop(0, diag // 4, tile_quad, 0)

    @pl.when(diag % 4 >= 2)
    def _():
        tile_pair((diag // 4) * 4)

    @pl.when(diag % 2 == 1)
    def _():
        sb_ref[...] = scores(diag)
        consume(sa_ref, diag - 1, False)
        consume(sb_ref, diag, True)

    @pl.when(diag % 2 == 0)
    def _():
        consume(sa_ref, diag, True)

    for i in range(NSA_HPG):
        lo_rows, hi_rows = head_rows(2 * i), head_rows(2 * i + 1)
        o_s = jnp.where(low, acc_ref[lo_rows, :LANES] / acc_ref[lo_rows, LANES:],
                        acc_ref[hi_rows, :LANES] / acc_ref[hi_rows, LANES:])
        branches = (jnp.where(low, oc_ref[lo_rows], oc_ref[hi_rows]), o_s,
                    jnp.where(low, ow_ref[lo_rows], ow_ref[hi_rows]))
        mixed = sum(o * gx_ref[:, (3 * i + c) * LANES:(3 * i + c + 1) * LANES]
                    for c, o in enumerate(branches))
        sl = slice(i * LANES, (i + 1) * LANES)
        out_ref[0, :, sl] = (mixed * nz_ref[0, :, sl]).astype(out_ref.dtype)


def _nsa(q, nz, gates, ksa, vs, kw, vw, kcmp, vcmp, ovl, gexp):
    batch, seq_len, _ = q.shape

    def q_spec(width):
        return pl.BlockSpec((1, Q_BLOCK, width), lambda b, i: (b, i, 0))

    def seq_spec(arr):
        return pl.BlockSpec((1,) + arr.shape[1:], lambda b, i: (b, 0, 0))

    rows = NSA_HEADS * Q_BLOCK
    n_cmp = kcmp.shape[1]
    span = WINDOW + Q_BLOCK
    last = seq_len // Q_BLOCK - 1
    next_q_spec = pl.BlockSpec((1, Q_BLOCK, NSA_WIDTH), lambda b, i: (b, jnp.minimum(i + 1, last), 0))
    return pl.pallas_call(
        functools.partial(_nsa_kernel, seq_len=seq_len),
        grid=(batch, seq_len // Q_BLOCK),
        in_specs=[q_spec(NSA_WIDTH), next_q_spec, q_spec(NSA_WIDTH), q_spec(LANES),
                  seq_spec(ksa), seq_spec(vs), seq_spec(kw), seq_spec(vw),
                  seq_spec(kcmp), seq_spec(vcmp),
                  pl.BlockSpec(ovl.shape, lambda b, i: (0, 0)),
                  pl.BlockSpec(gexp.shape, lambda b, i: (0, 0))],
        out_specs=q_spec(NSA_WIDTH),
        out_shape=jax.ShapeDtypeStruct((batch, seq_len, NSA_WIDTH), jnp.bfloat16),
        scratch_shapes=[pltpu.VMEM((rows, 2 * LANES), jnp.bfloat16),
                        pltpu.VMEM((rows, n_cmp), jnp.float32),
                        pltpu.VMEM((rows, span), jnp.float32),
                        pltpu.VMEM((rows, n_cmp), jnp.bfloat16),
                        pltpu.VMEM((rows, span), jnp.bfloat16),
                        pltpu.VMEM((Q_BLOCK, n_cmp), jnp.float32),
                        pltpu.VMEM((Q_BLOCK, span), jnp.float32),
                        pltpu.VMEM((rows, LANES), jnp.float32),
                        pltpu.VMEM((rows, 2 * LANES), jnp.float32),
                        pltpu.VMEM((rows, LANES), jnp.float32),
                        pltpu.VMEM((rows, LANES), jnp.float32),
                        pltpu.VMEM((rows, SEL_KEY_TILE), jnp.float32),
                        pltpu.VMEM((rows, SEL_KEY_TILE), jnp.float32),
                        pltpu.VMEM((rows, LANES), jnp.float32),
                        pltpu.VMEM((rows, LANES), jnp.float32),
                        pltpu.VMEM((rows, LANES), jnp.bfloat16),
                        pltpu.VMEM((NSA_KV_GROUPS * Q_BLOCK, LANES), jnp.bfloat16),
                        pltpu.VMEM((rows, LANES), jnp.float32),
                        pltpu.VMEM((Q_BLOCK, gexp.shape[1]), jnp.float32)],
        compiler_params=pltpu.CompilerParams(dimension_semantics=("arbitrary", "arbitrary"),
                                             vmem_limit_bytes=VMEM_LIMIT_BYTES),
        name="nsa_attention",
    )(q, q, nz, gates, ksa, vs, kw, vw, kcmp, vcmp, ovl, gexp)


def _outproj_kernel(x_ref, ygm_ref, ynsa_ref, ymem_ref, w_ref, g_ref, b_ref, o_ref, *, alpha):
    def mix_proj(rs):
        return (_dot(ygm_ref[rs, :], w_ref[:GM_WIDTH])
                + _dot(ynsa_ref[rs, :], w_ref[GM_WIDTH:GM_WIDTH + NSA_WIDTH])
                + _dot(ymem_ref[rs, :], w_ref[GM_WIDTH + NSA_WIDTH:]))

    subs = [slice(r0, r0 + PROJ_SUB_ROWS) for r0 in range(0, x_ref.shape[0], PROJ_SUB_ROWS)]
    y_next = mix_proj(subs[0])
    for j, rs in enumerate(subs):
        y = y_next
        y_next = mix_proj(subs[j + 1]) if j + 1 < len(subs) else None
        r = alpha * x_ref[rs, :] + y
        mu = jnp.mean(r, axis=-1, keepdims=True)
        d = r - mu
        var = jnp.mean(d * d, axis=-1, keepdims=True)
        o_ref[rs, :] = d * lax.rsqrt(var + LN_EPS) * g_ref[...] + b_ref[...]


def _outproj(x2d, ygm, ynsa, ymem, w_out, ln_g, ln_b, *, layer, alpha):
    n, d_model = x2d.shape
    rows = OUT_PROJ_ROWS

    def tok_spec(width):
        return pl.BlockSpec((rows, width), lambda i: (i, 0))

    def layer_spec(shape):
        return pl.BlockSpec((None,) + shape[1:], lambda i: (layer, 0, 0))

    return pl.pallas_call(
        functools.partial(_outproj_kernel, alpha=alpha),
        grid=(n // rows,),
        in_specs=[tok_spec(d_model), tok_spec(GM_WIDTH), tok_spec(NSA_WIDTH), tok_spec(MEM_WIDTH),
                  layer_spec(w_out.shape), layer_spec(ln_g.shape), layer_spec(ln_b.shape)],
        out_specs=tok_spec(d_model),
        out_shape=jax.ShapeDtypeStruct((n, d_model), jnp.float32),
        compiler_params=pltpu.CompilerParams(dimension_semantics=("arbitrary",),
                                             vmem_limit_bytes=VMEM_LIMIT_BYTES),
        name="out_proj_layernorm",
    )(x2d, ygm, ynsa, ymem, w_out, ln_g, ln_b)


def _pair_head_slices(w, start, axis):
    return [lax.slice_in_dim(w, start + h * HEAD_DIM, start + (h + 1) * HEAD_DIM, axis=axis)
            for h in PAIR_HEAD_ORDER]


def _w_in_pieces():
    o_gate = 2048
    o_nz = o_gate + GATE_COLS
    o_mq = o_nz + NSA_WIDTH
    end = o_mq + 2 * MEM_WIDTH
    heads = lambda start: [(start + h * HEAD_DIM, HEAD_DIM) for h in PAIR_HEAD_ORDER]
    return ([(0, 768)] + heads(768) + [(1280, 768)] + heads(o_nz) + [(o_mq, end - o_mq), (o_gate, GATE_COLS)])


def _permute_w_in_kernel(w_ref, o_ref):
    dst = 0
    for src, width in _w_in_pieces():
        o_ref[:, dst:dst + width] = w_ref[:, src:src + width].astype(o_ref.dtype)
        dst += width
    o_ref[:, dst:] = jnp.zeros((o_ref.shape[0], o_ref.shape[1] - dst), o_ref.dtype)


def _permute_w_in(w):
    depth, d_model, in_cols = w.shape
    rows = PROJ_SUB_ROWS
    return pl.pallas_call(
        _permute_w_in_kernel,
        grid=(depth, d_model // rows),
        in_specs=[pl.BlockSpec((None, rows, in_cols), lambda l, i: (l, i, 0))],
        out_specs=pl.BlockSpec((None, rows, N_COLS), lambda l, i: (l, i, 0)),
        out_shape=jax.ShapeDtypeStruct((depth, d_model, N_COLS), jnp.bfloat16),
        name="w_in_relayout",
    )(w)


def _permute_w_out(w):
    pieces = ([w[:, :GM_WIDTH]] + _pair_head_slices(w, GM_WIDTH, 1) + [w[:, GM_WIDTH + NSA_WIDTH:]])
    return jnp.concatenate([p.astype(jnp.bfloat16) for p in pieces], axis=1)


def _rope_tables(seq_len):
    half = HEAD_DIM // 2
    inv_freq = ROPE_THETA ** (-jnp.arange(half, dtype=jnp.float32) * 2.0 / HEAD_DIM)
    ang = jnp.arange(seq_len).astype(jnp.float32)[:, None] * inv_freq[None, :]
    cos, sin = jnp.cos(ang), jnp.sin(ang)
    reps = LANES // HEAD_DIM
    cos_t = jnp.tile(jnp.concatenate([cos, cos], axis=1), (1, reps))
    sin_t = jnp.tile(jnp.concatenate([-sin, sin], axis=1), (1, reps))
    rot_low = ((np.arange(LANES) % HEAD_DIM) < half).astype(np.float32)[None, :]
    return cos_t, sin_t, jnp.asarray(rot_low)


def _compress_weights(pos_k, w1_k, w2_k, pos_v, w1_v, w2_v):
    half = CMP_BLOCK // 2

    def block_diag2(w):
        z = jnp.zeros_like(w)
        return jnp.concatenate([jnp.concatenate([w, z], axis=-1),
                                jnp.concatenate([z, w], axis=-1)], axis=-2)

    def expand_w1(w1):
        w = w1.reshape(2, half, HEAD_DIM, CMP_HIDDEN)
        w = block_diag2(w)
        return w.reshape(2, half * NSA_KV_WIDTH, NSA_KV_GROUPS * CMP_HIDDEN)

    def expand_w2(w2):
        return block_diag2(w2)

    def expand_pos(pos):
        p = pos.reshape(2, half, 1, HEAD_DIM)
        p = jnp.broadcast_to(p, (2, half, NSA_KV_GROUPS, HEAD_DIM))
        return p.reshape(2, half * NSA_KV_WIDTH)

    pos = jnp.stack([expand_pos(pos_k), expand_pos(pos_v)])
    w1 = jnp.stack([expand_w1(w1_k), expand_w1(w1_v)]).astype(jnp.bfloat16)
    w2 = jnp.stack([expand_w2(w2_k), expand_w2(w2_v)]).astype(jnp.bfloat16)
    return pos, w1, w2


def _gate_expansion():
    out = np.zeros((LANES, NSA_HPG * 3 * LANES), np.float32)
    for i in range(NSA_HPG):
        for c in range(3):
            base = (3 * i + c) * LANES
            out[3 * i + c, base:base + HEAD_DIM] = 1.0
            out[3 * (i + NSA_HPG) + c, base + HEAD_DIM:base + LANES] = 1.0
    return jnp.asarray(out, dtype=jnp.bfloat16)


def _overlap_matrix(n_rows, n_sel):
    c_start = np.arange(n_rows) * CMP_STRIDE
    s_start = np.arange(LANES) * SEL_BLOCK
    ovl = ((c_start[:, None] < s_start[None, :] + SEL_BLOCK)
           & (c_start[:, None] + CMP_BLOCK > s_start[None, :])
           & (np.arange(LANES)[None, :] < n_sel))
    out = np.zeros((LANES + 16, n_rows), np.float32)
    out[:LANES] = ovl.T
    out[LANES] = 1.0
    return jnp.asarray(out, dtype=jnp.bfloat16)


def kernel(x, mem, w_in, gm_ln_g, gm_ln_b, gm_ws, gm_bs, cmp_pos_k, cmp_k_w1, cmp_k_w2,
           cmp_pos_v, cmp_v_w1, cmp_v_w2, w_mem_kv, w_out, ln_g, ln_b):
    batch, seq_len, d_model = x.shape
    depth = w_in.shape[0]
    assert seq_len % SEL_KEY_TILE == 0 and seq_len >= WINDOW + Q_BLOCK
    assert SEL_TOPK <= seq_len // SEL_BLOCK <= LANES
    alpha = (2.0 * depth) ** 0.25
    n_tok = batch * seq_len
    n_rows = seq_len // CMP_STRIDE

    cos_t, sin_t, rot_low = _rope_tables(seq_len)
    ovl = _overlap_matrix(n_rows, seq_len // SEL_BLOCK)
    gexp = _gate_expansion()
    tril = jnp.tril(jnp.ones((GM_CHUNK, GM_CHUNK), gm_ws.dtype))
    mk_all, mv_all = _memkv(mem.reshape(batch * mem.shape[1], d_model), w_mem_kv.astype(jnp.bfloat16))
    w_cat_all = _permute_w_in(w_in)
    w_out_all = _permute_w_out(w_out)

    h = x.reshape(n_tok, d_model)
    for l in range(depth):
        gws = (gm_ws[l] * tril[None]).astype(jnp.bfloat16)
        gbs = jnp.repeat(gm_bs[l].T, HEAD_DIM, axis=1)
        glg = gm_ln_g[l].reshape(1, GM_WIDTH)
        glb = gm_ln_b[l].reshape(1, GM_WIDTH)
        (ygm, ymem, q, kc, vc, ksa, vs, kw, vw, nz, gates) = _inproj(
            h, w_cat_all, cos_t, sin_t, rot_low, gws, gbs, glg, glb, mk_all, mv_all,
            layer=l, batch=batch, seq_len=seq_len)

        pos, w1, w2 = _compress_weights(cmp_pos_k[l], cmp_k_w1[l], cmp_k_w2[l],
                                        cmp_pos_v[l], cmp_v_w1[l], cmp_v_w2[l])
        row_shape = (batch, n_rows, CMP_STRIDE * NSA_KV_WIDTH)
        kcmp, vcmp = _compress(kc.reshape(row_shape), vc.reshape(row_shape), pos, w1, w2)

        def per_seq(a):
            return a.reshape(batch, seq_len, a.shape[-1])

        ynsa = _nsa(per_seq(q), per_seq(nz), per_seq(gates), per_seq(ksa), per_seq(vs),
                    per_seq(kw), per_seq(vw), kcmp, vcmp, ovl, gexp)

        h = _outproj(h, ygm, ynsa.reshape(n_tok, NSA_WIDTH), ymem, w_out_all,
                     ln_g.reshape(depth, 1, d_model), ln_b.reshape(depth, 1, d_model),
                     layer=l, alpha=alpha)
    return h.reshape(batch, seq_len, d_model)
```
